```python
import jax, jax.numpy as jnp
from jax import lax
import numpy as np

D_MODEL = 1024
BATCH = 8
SEQ = 8192
DEPTH = 2

N_META = 16
GLA_HEADS = 4
GLA_DK = D_MODEL // 2
GLA_DV = D_MODEL
GLA_HK = GLA_DK // GLA_HEADS
GLA_HV = GLA_DV // GLA_HEADS
GLA_RANK = 16
GLA_TAU = 16.0
CHUNK = 64
POOL_WINDOWS = (2, 4, 8, 16)
POOL_GROUPS = 4
POOL_DIM = D_MODEL
POOL_GDIM = POOL_DIM // POOL_GROUPS
D_FF = 2816
CONV_W = 3
EPS = 1e-6
IN_WIDTH = 2 * GLA_DK + 2 * GLA_DV + GLA_RANK + POOL_DIM + 2 * D_MODEL

kernel_name = "hybrid_gla_pool_gated_block"


def rmsnorm(x, g):
    xf = x.astype(jnp.float32)
    y = xf * lax.rsqrt(jnp.mean(xf * xf, axis=-1, keepdims=True) + EPS)
    return (y * g.astype(jnp.float32)).astype(x.dtype)


def gla_chunked(q, k, v, log_a):
    B, L, H, DKH = q.shape
    DVH = v.shape[-1]
    pad = CHUNK - N_META
    n_chunks = (L + pad) // CHUNK

    def to_chunks(t):
        t = jnp.pad(t.astype(jnp.float32), ((0, 0), (pad, 0), (0, 0), (0, 0)))
        return t.reshape(B, n_chunks, CHUNK, H, t.shape[-1]).transpose(1, 0, 3, 2, 4)

    q, k, v, g = (to_chunks(t) for t in (q, k, v, log_a))
    b = jnp.cumsum(g, axis=3)
    b_last = b[:, :, :, -1:, :]
    q_dec = q * jnp.exp(b)
    k_inv = k * jnp.exp(-b)
    k_end = k * jnp.exp(b_last - b)
    causal = jnp.tril(jnp.ones((CHUNK, CHUNK), dtype=bool))
    att = jnp.where(causal, jnp.einsum('nbhcd,nbhsd->nbhcs', q_dec, k_inv), 0.0)
    o_intra = jnp.einsum('nbhcs,nbhse->nbhce', att, v)

    def step(state, inp):
        q_c, k_c, v_c, dec_c = inp
        o_c = jnp.einsum('bhcd,bhde->bhce', q_c, state)
        state = dec_c[..., None] * state + jnp.einsum('bhsd,bhse->bhde', k_c, v_c)
        return state, o_c

    s0 = jnp.zeros((B, H, DKH, DVH), jnp.float32)
    _, o_inter = lax.scan(step, s0, (q_dec, k_end, v, jnp.exp(b_last[:, :, :, 0, :])))
    o = (o_intra + o_inter).transpose(1, 0, 3, 2, 4).reshape(B, n_chunks * CHUNK, H, DVH)
    return o[:, pad:]


def multiscale_pool(u):
    B, L, _ = u.shape
    ug = u.astype(jnp.float32).reshape(B, L, POOL_GROUPS, POOL_GDIM)
    csp = jnp.pad(jnp.cumsum(ug, axis=1), ((0, 0), (1, 0), (0, 0), (0, 0)))
    pos = jnp.arange(L)
    outs = []
    for gi, w in enumerate(POOL_WINDOWS):
        c = csp[:, :, gi]
        lagged = jnp.pad(c[:, :L - w + 1], ((0, 0), (w - 1, 0), (0, 0)))
        cnt = jnp.minimum(pos + 1, w).astype(jnp.float32)[None, :, None]
        outs.append((c[:, 1:] - lagged) / cnt)
    return jnp.stack(outs, axis=2) - ug


def causal_dwconv(h, w, b):
    C = h.shape[-1]
    out = lax.conv_general_dilated(h, w[:, None, :].astype(h.dtype), window_strides=(1,),
                                   padding=((CONV_W - 1, 0),),
                                   dimension_numbers=('NWC', 'WIO', 'NWC'),
                                   feature_group_count=C)
    return out + b


def _fwd_setup_inputs(seed: int = 0) -> dict:
    key = jax.random.key(seed)
    ks = jax.random.split(key, 20)
    nrm = lambda k, shape, s: jax.random.normal(k, shape, jnp.float32) * s
    F2 = 2 * D_FF
    return {
        'x': nrm(ks[0], (BATCH, SEQ, D_MODEL), 1.0),
        'meta_tokens': nrm(ks[1], (N_META, D_MODEL), 1.0),
        'norm1_g': 1.0 + nrm(ks[2], (DEPTH, D_MODEL), 0.02),
        'w_in': nrm(ks[3], (DEPTH, D_MODEL, IN_WIDTH), D_MODEL ** -0.5),
        'w_gk': nrm(ks[4], (DEPTH, GLA_RANK, GLA_DK), GLA_RANK ** -0.5),
        'b_gk': nrm(ks[5], (DEPTH, GLA_DK), 0.1),
        'gla_norm_g': 1.0 + nrm(ks[6], (DEPTH, GLA_HV), 0.02),
        'w_a': nrm(ks[7], (DEPTH, GLA_DV, D_MODEL), GLA_DV ** -0.5),
        'w_pool_grp': nrm(ks[8], (DEPTH, POOL_GROUPS, POOL_GDIM, POOL_GDIM), POOL_GDIM ** -0.5),
        'pool_scale': 1.0 + nrm(ks[9], (DEPTH, POOL_DIM), 0.02),
        'w_b': nrm(ks[10], (DEPTH, POOL_DIM, D_MODEL), POOL_DIM ** -0.5),
        'b_gates': nrm(ks[11], (DEPTH, 2 * D_MODEL), 0.02),
        'w_o': nrm(ks[12], (DEPTH, D_MODEL, D_MODEL), D_MODEL ** -0.5),
        'norm2_g': 1.0 + nrm(ks[13], (DEPTH, D_MODEL), 0.02),
        'w_up': nrm(ks[14], (DEPTH, D_MODEL, F2), D_MODEL ** -0.5),
        'conv_w': nrm(ks[15], (DEPTH, CONV_W, F2), CONV_W ** -0.5),
        'conv_b': nrm(ks[16], (DEPTH, F2), 0.02),
        'w_down': nrm(ks[17], (DEPTH, D_FF, D_MODEL), D_FF ** -0.5),
        'final_norm_g': 1.0 + nrm(ks[18], (D_MODEL,), 0.02),
    }


def _fwd_reference(x, meta_tokens, norm1_g, w_in, w_gk, b_gk, gla_norm_g, w_a, w_pool_grp, pool_scale,
              w_b, b_gates, w_o, norm2_g, w_up, conv_w, conv_b, w_down, final_norm_g):
    B = x.shape[0]
    dt = x.dtype
    meta = jnp.broadcast_to(meta_tokens.astype(dt)[None], (B, N_META, D_MODEL))
    h = jnp.concatenate([meta, x], axis=1)
    L = h.shape[1]
    sizes = (GLA_DK, GLA_DK, GLA_DV, GLA_RANK, GLA_DV, POOL_DIM, D_MODEL, D_MODEL)
    splits = np.cumsum(sizes)[:-1].tolist()

    for l in range(DEPTH):
        hn = rmsnorm(h, norm1_g[l])
        p = hn @ w_in[l]
        q, k, v, glr, r, u, ga, gb = jnp.split(p, splits, axis=-1)
        log_a = jax.nn.log_sigmoid((glr @ w_gk[l] + b_gk[l]).astype(jnp.float32)) / GLA_TAU
        q = q.reshape(B, L, GLA_HEADS, GLA_HK) * (GLA_HK ** -0.5)
        k = k.reshape(B, L, GLA_HEADS, GLA_HK)
        v = v.reshape(B, L, GLA_HEADS, GLA_HV)
        log_a = log_a.reshape(B, L, GLA_HEADS, GLA_HK)
        o = gla_chunked(q, k, v, log_a)
        o = rmsnorm(o, gla_norm_g[l]).reshape(B, L, GLA_DV).astype(dt)
        y_a = (o * jax.nn.silu(r)) @ w_a[l]
        pooled = multiscale_pool(u).astype(dt)
        y_b = jnp.einsum('blgc,gcd->blgd', pooled, w_pool_grp[l]).reshape(B, L, POOL_DIM)
        y_b = (y_b * pool_scale[l]) @ w_b[l]
        gate_a = jax.nn.sigmoid(ga + b_gates[l, :D_MODEL])
        gate_b = jax.nn.sigmoid(gb + b_gates[l, D_MODEL:])
        h = h + (gate_a * y_a + gate_b * y_b) @ w_o[l]
        hn = rmsnorm(h, norm2_g[l])
        up = causal_dwconv(hn @ w_up[l], conv_w[l], conv_b[l])
        a, bv = jnp.split(up, 2, axis=-1)
        h = h + (jax.nn.silu(a) * bv) @ w_down[l]

    return rmsnorm(h, final_norm_g)[:, N_META:]


import jax as _jax
import jax.numpy as _jnp

TWIN_FORMAT = 'train_step'
FWD_PARAMS = ['x', 'meta_tokens', 'norm1_g', 'w_in', 'w_gk', 'b_gk', 'gla_norm_g', 'w_a', 'w_pool_grp', 'pool_scale', 'w_b', 'b_gates', 'w_o', 'norm2_g', 'w_up', 'conv_w', 'conv_b', 'w_down', 'final_norm_g']
TWIN_WEIGHTS = ['meta_tokens', 'norm1_g', 'w_in', 'w_gk', 'b_gk', 'gla_norm_g', 'w_a', 'w_pool_grp', 'pool_scale', 'w_b', 'b_gates', 'w_o', 'norm2_g', 'w_up', 'conv_w', 'conv_b', 'w_down', 'final_norm_g']
TWIN_DIFF_INPUT = 'x'
TWIN_INPUTS = ['x', 'meta_tokens', 'norm1_g', 'w_in', 'w_gk', 'b_gk', 'gla_norm_g', 'w_a', 'w_pool_grp', 'pool_scale', 'w_b', 'b_gates', 'w_o', 'norm2_g', 'w_up', 'conv_w', 'conv_b', 'w_down', 'final_norm_g', 'loss_target', 'm_meta_tokens', 'm_norm1_g', 'm_w_in', 'm_w_gk', 'm_b_gk', 'm_gla_norm_g', 'm_w_a', 'm_w_pool_grp', 'm_pool_scale', 'm_w_b', 'm_b_gates', 'm_w_o', 'm_norm2_g', 'm_w_up', 'm_conv_w', 'm_conv_b', 'm_w_down', 'm_final_norm_g', 'v_meta_tokens', 'v_norm1_g', 'v_w_in', 'v_w_gk', 'v_b_gk', 'v_gla_norm_g', 'v_w_a', 'v_w_pool_grp', 'v_pool_scale', 'v_w_b', 'v_b_gates', 'v_w_o', 'v_norm2_g', 'v_w_up', 'v_conv_w', 'v_conv_b', 'v_w_down', 'v_final_norm_g']
TWIN_OUTPUTS = ['loss', 'grad_x', 'grad_meta_tokens', 'grad_norm1_g', 'grad_w_in', 'grad_w_gk', 'grad_b_gk', 'grad_gla_norm_g', 'grad_w_a', 'grad_w_pool_grp', 'grad_pool_scale', 'grad_w_b', 'grad_b_gates', 'grad_w_o', 'grad_norm2_g', 'grad_w_up', 'grad_conv_w', 'grad_conv_b', 'grad_w_down', 'grad_final_norm_g', 'delta_meta_tokens', 'delta_norm1_g', 'delta_w_in', 'delta_w_gk', 'delta_b_gk', 'delta_gla_norm_g', 'delta_w_a', 'delta_w_pool_grp', 'delta_pool_scale', 'delta_w_b', 'delta_b_gates', 'delta_w_o', 'delta_norm2_g', 'delta_w_up', 'delta_conv_w', 'delta_conv_b', 'delta_w_down', 'delta_final_norm_g', 'new_m_meta_tokens', 'new_m_norm1_g', 'new_m_w_in', 'new_m_w_gk', 'new_m_b_gk', 'new_m_gla_norm_g', 'new_m_w_a', 'new_m_w_pool_grp', 'new_m_pool_scale', 'new_m_w_b', 'new_m_b_gates', 'new_m_w_o', 'new_m_norm2_g', 'new_m_w_up', 'new_m_conv_w', 'new_m_conv_b', 'new_m_w_down', 'new_m_final_norm_g', 'new_v_meta_tokens', 'new_v_norm1_g', 'new_v_w_in', 'new_v_w_gk', 'new_v_b_gk', 'new_v_gla_norm_g', 'new_v_w_a', 'new_v_w_pool_grp', 'new_v_pool_scale', 'new_v_w_b', 'new_v_b_gates', 'new_v_w_o', 'new_v_norm2_g', 'new_v_w_up', 'new_v_conv_w', 'new_v_conv_b', 'new_v_w_down', 'new_v_final_norm_g']
TWIN_LEAF_KINDS = {'loss': 'loss', 'grad_x': 'grad_x', 'grad_meta_tokens': 'grad_w', 'grad_norm1_g': 'grad_w', 'grad_w_in': 'grad_w', 'grad_w_gk': 'grad_w', 'grad_b_gk': 'grad_w', 'grad_gla_norm_g': 'grad_w', 'grad_w_a': 'grad_w', 'grad_w_pool_grp': 'grad_w', 'grad_pool_scale': 'grad_w', 'grad_w_b': 'grad_w', 'grad_b_gates': 'grad_w', 'grad_w_o': 'grad_w', 'grad_norm2_g': 'grad_w', 'grad_w_up': 'grad_w', 'grad_conv_w': 'grad_w', 'grad_conv_b': 'grad_w', 'grad_w_down': 'grad_w', 'grad_final_norm_g': 'grad_w', 'delta_meta_tokens': 'delta_w', 'delta_norm1_g': 'delta_w', 'delta_w_in': 'delta_w', 'delta_w_gk': 'delta_w', 'delta_b_gk': 'delta_w', 'delta_gla_norm_g': 'delta_w', 'delta_w_a': 'delta_w', 'delta_w_pool_grp': 'delta_w', 'delta_pool_scale': 'delta_w', 'delta_w_b': 'delta_w', 'delta_b_gates': 'delta_w', 'delta_w_o': 'delta_w', 'delta_norm2_g': 'delta_w', 'delta_w_up': 'delta_w', 'delta_conv_w': 'delta_w', 'delta_conv_b': 'delta_w', 'delta_w_down': 'delta_w', 'delta_final_norm_g': 'delta_w', 'new_m_meta_tokens': 'new_m', 'new_m_norm1_g': 'new_m', 'new_m_w_in': 'new_m', 'new_m_w_gk': 'new_m', 'new_m_b_gk': 'new_m', 'new_m_gla_norm_g': 'new_m', 'new_m_w_a': 'new_m', 'new_m_w_pool_grp': 'new_m', 'new_m_pool_scale': 'new_m', 'new_m_w_b': 'new_m', 'new_m_b_gates': 'new_m', 'new_m_w_o': 'new_m', 'new_m_norm2_g': 'new_m', 'new_m_w_up': 'new_m', 'new_m_conv_w': 'new_m', 'new_m_conv_b': 'new_m', 'new_m_w_down': 'new_m', 'new_m_final_norm_g': 'new_m', 'new_v_meta_tokens': 'new_v', 'new_v_norm1_g': 'new_v', 'new_v_w_in': 'new_v', 'new_v_w_gk': 'new_v', 'new_v_b_gk': 'new_v', 'new_v_gla_norm_g': 'new_v', 'new_v_w_a': 'new_v', 'new_v_w_pool_grp': 'new_v', 'new_v_pool_scale': 'new_v', 'new_v_w_b': 'new_v', 'new_v_b_gates': 'new_v', 'new_v_w_o': 'new_v', 'new_v_norm2_g': 'new_v', 'new_v_w_up': 'new_v', 'new_v_conv_w': 'new_v', 'new_v_conv_b': 'new_v', 'new_v_w_down': 'new_v', 'new_v_final_norm_g': 'new_v'}


def _forward(args):
    return _fwd_reference(*[args[k] for k in FWD_PARAMS])


def _output_shape():
    def fwd():
        inp = _fwd_setup_inputs(0)
        return _fwd_reference(*[inp[k] for k in FWD_PARAMS])
    out = _jax.eval_shape(fwd)
    return out.shape, out.dtype

N_MICROBATCH = 1
ADAM_LR = 0.001
ADAM_B1 = 0.9
ADAM_B2 = 0.999
ADAM_EPS = 1e-08
ADAM_WD = 0.01
ADAM_STEP = 10
PER_EXAMPLE_BATCH_AXIS = {'x': 0, 'loss_target': 0}
SHARED_INPUTS = []
_WEIGHT_DTYPES = {'meta_tokens': _jnp.float32, 'norm1_g': _jnp.float32, 'w_in': _jnp.float32, 'w_gk': _jnp.float32, 'b_gk': _jnp.float32, 'gla_norm_g': _jnp.float32, 'w_a': _jnp.float32, 'w_pool_grp': _jnp.float32, 'pool_scale': _jnp.float32, 'w_b': _jnp.float32, 'b_gates': _jnp.float32, 'w_o': _jnp.float32, 'norm2_g': _jnp.float32, 'w_up': _jnp.float32, 'conv_w': _jnp.float32, 'conv_b': _jnp.float32, 'w_down': _jnp.float32, 'final_norm_g': _jnp.float32}
MOMENT_SCALE = {'meta_tokens': 9.357420e-03, 'norm1_g': 2.017476e-01, 'w_in': 8.341915e-02, 'w_gk': 1.245987e-02, 'b_gk': 5.168664e-02, 'gla_norm_g': 1.614933e-01, 'w_a': 7.809266e-02, 'w_pool_grp': 1.152908e-01, 'pool_scale': 1.154438e-01, 'w_b': 1.152316e-01, 'b_gates': 4.039397e-02, 'w_o': 1.387744e-01, 'norm2_g': 1.745677e-01, 'w_up': 7.398998e-02, 'conv_w': 7.324481e-02, 'conv_b': 7.199250e-02, 'w_down': 1.208826e-01, 'final_norm_g': 6.402361e+01}


def _to_microbatches(a, axis):
    t = _jnp.moveaxis(a, axis, 0)
    t = t.reshape((N_MICROBATCH, t.shape[0] // N_MICROBATCH) + t.shape[1:])
    return _jnp.moveaxis(t, 1, axis + 1)


def setup_inputs(seed: int = 0) -> dict:
    inp = _fwd_setup_inputs(seed)
    key = _jax.random.fold_in(_jax.random.key(seed), 7919)
    shape, _ = _output_shape()
    out = dict(inp)
    out["loss_target"] = _jax.random.normal(_jax.random.fold_in(key, 0), shape, _jnp.float32)
    for i, name in enumerate(TWIN_WEIGHTS):
        w = inp[name].astype(_jnp.float32)
        if MOMENT_SCALE is None:
            s = _jnp.sqrt(_jnp.mean(_jnp.square(w)) + 1e-30)
        else:
            s = MOMENT_SCALE[name]
        km, kv = _jax.random.split(_jax.random.fold_in(key, i + 1))
        out[name] = w
        out["m_" + name] = s * _jax.random.normal(km, w.shape, _jnp.float32)
        out["v_" + name] = (s * s) * _jax.random.uniform(kv, w.shape, _jnp.float32, 0.5, 1.5)
    if N_MICROBATCH > 1:
        for name, axis in PER_EXAMPLE_BATCH_AXIS.items():
            out[name] = _to_microbatches(out[name], axis)
    return {'x': out['x'], 'meta_tokens': out['meta_tokens'], 'norm1_g': out['norm1_g'], 'w_in': out['w_in'], 'w_gk': out['w_gk'], 'b_gk': out['b_gk'], 'gla_norm_g': out['gla_norm_g'], 'w_a': out['w_a'], 'w_pool_grp': out['w_pool_grp'], 'pool_scale': out['pool_scale'], 'w_b': out['w_b'], 'b_gates': out['b_gates'], 'w_o': out['w_o'], 'norm2_g': out['norm2_g'], 'w_up': out['w_up'], 'conv_w': out['conv_w'], 'conv_b': out['conv_b'], 'w_down': out['w_down'], 'final_norm_g': out['final_norm_g'], 'loss_target': out['loss_target'], 'm_meta_tokens': out['m_meta_tokens'], 'm_norm1_g': out['m_norm1_g'], 'm_w_in': out['m_w_in'], 'm_w_gk': out['m_w_gk'], 'm_b_gk': out['m_b_gk'], 'm_gla_norm_g': out['m_gla_norm_g'], 'm_w_a': out['m_w_a'], 'm_w_pool_grp': out['m_w_pool_grp'], 'm_pool_scale': out['m_pool_scale'], 'm_w_b': out['m_w_b'], 'm_b_gates': out['m_b_gates'], 'm_w_o': out['m_w_o'], 'm_norm2_g': out['m_norm2_g'], 'm_w_up': out['m_w_up'], 'm_conv_w': out['m_conv_w'], 'm_conv_b': out['m_conv_b'], 'm_w_down': out['m_w_down'], 'm_final_norm_g': out['m_final_norm_g'], 'v_meta_tokens': out['v_meta_tokens'], 'v_norm1_g': out['v_norm1_g'], 'v_w_in': out['v_w_in'], 'v_w_gk': out['v_w_gk'], 'v_b_gk': out['v_b_gk'], 'v_gla_norm_g': out['v_gla_norm_g'], 'v_w_a': out['v_w_a'], 'v_w_pool_grp': out['v_w_pool_grp'], 'v_pool_scale': out['v_pool_scale'], 'v_w_b': out['v_w_b'], 'v_b_gates': out['v_b_gates'], 'v_w_o': out['v_w_o'], 'v_norm2_g': out['v_norm2_g'], 'v_w_up': out['v_w_up'], 'v_conv_w': out['v_conv_w'], 'v_conv_b': out['v_conv_b'], 'v_w_down': out['v_w_down'], 'v_final_norm_g': out['v_final_norm_g']}


def _loss(weights, diff, rest, loss_target):
    with _jax.named_scope("forward"):
        args = {**rest, TWIN_DIFF_INPUT: diff, **{k: w.astype(_WEIGHT_DTYPES[k]) for k, w in weights.items()}}
        y = _forward(args)
    with _jax.named_scope("loss_head"):
        err = _jnp.square(y.astype(_jnp.float32) - loss_target)
        return 0.5 * _jnp.sum(_jnp.mean(err, axis=-1)) if err.ndim else 0.5 * err


def _adamw(w, g, m, v):
    m = ADAM_B1 * m + (1.0 - ADAM_B1) * g
    v = ADAM_B2 * v + (1.0 - ADAM_B2) * _jnp.square(g)
    m_hat = m / (1.0 - ADAM_B1 ** ADAM_STEP)
    v_hat = v / (1.0 - ADAM_B2 ** ADAM_STEP)
    delta = -ADAM_LR * (m_hat / (_jnp.sqrt(v_hat) + ADAM_EPS) + ADAM_WD * w)
    return delta, m, v


def reference(x, meta_tokens, norm1_g, w_in, w_gk, b_gk, gla_norm_g, w_a, w_pool_grp, pool_scale, w_b, b_gates, w_o, norm2_g, w_up, conv_w, conv_b, w_down, final_norm_g, loss_target, m_meta_tokens, m_norm1_g, m_w_in, m_w_gk, m_b_gk, m_gla_norm_g, m_w_a, m_w_pool_grp, m_pool_scale, m_w_b, m_b_gates, m_w_o, m_norm2_g, m_w_up, m_conv_w, m_conv_b, m_w_down, m_final_norm_g, v_meta_tokens, v_norm1_g, v_w_in, v_w_gk, v_b_gk, v_gla_norm_g, v_w_a, v_w_pool_grp, v_pool_scale, v_w_b, v_b_gates, v_w_o, v_norm2_g, v_w_up, v_conv_w, v_conv_b, v_w_down, v_final_norm_g):
    given = dict(x=x, meta_tokens=meta_tokens, norm1_g=norm1_g, w_in=w_in, w_gk=w_gk, b_gk=b_gk, gla_norm_g=gla_norm_g, w_a=w_a, w_pool_grp=w_pool_grp, pool_scale=pool_scale, w_b=w_b, b_gates=b_gates, w_o=w_o, norm2_g=norm2_g, w_up=w_up, conv_w=conv_w, conv_b=conv_b, w_down=w_down, final_norm_g=final_norm_g, loss_target=loss_target, m_meta_tokens=m_meta_tokens, m_norm1_g=m_norm1_g, m_w_in=m_w_in, m_w_gk=m_w_gk, m_b_gk=m_b_gk, m_gla_norm_g=m_gla_norm_g, m_w_a=m_w_a, m_w_pool_grp=m_w_pool_grp, m_pool_scale=m_pool_scale, m_w_b=m_w_b, m_b_gates=m_b_gates, m_w_o=m_w_o, m_norm2_g=m_norm2_g, m_w_up=m_w_up, m_conv_w=m_conv_w, m_conv_b=m_conv_b, m_w_down=m_w_down, m_final_norm_g=m_final_norm_g, v_meta_tokens=v_meta_tokens, v_norm1_g=v_norm1_g, v_w_in=v_w_in, v_w_gk=v_w_gk, v_b_gk=v_b_gk, v_gla_norm_g=v_gla_norm_g, v_w_a=v_w_a, v_w_pool_grp=v_w_pool_grp, v_pool_scale=v_pool_scale, v_w_b=v_w_b, v_b_gates=v_b_gates, v_w_o=v_w_o, v_norm2_g=v_norm2_g, v_w_up=v_w_up, v_conv_w=v_conv_w, v_conv_b=v_conv_b, v_w_down=v_w_down, v_final_norm_g=v_final_norm_g)
    weights = {n: given[n] for n in TWIN_WEIGHTS}
    shared = {n: given[n] for n in SHARED_INPUTS}
    per_example = {n: given[n] for n in ['x']}
    grad_fn = _jax.value_and_grad(_loss, argnums=(0, 1))

    def one_microbatch(ex, loss_target):
        ex = dict(ex)
        diff = ex.pop(TWIN_DIFF_INPUT)
        return grad_fn(weights, diff, {**shared, **ex}, loss_target)

    if N_MICROBATCH == 1:
        loss, (grad_w, grad_x) = one_microbatch(per_example, given["loss_target"])
    else:
        def body(carry, xs):
            loss_sum, grad_sum = carry
            l_k, (gw_k, gx_k) = one_microbatch(xs[0], xs[1])
            with _jax.named_scope("update"):
                return (loss_sum + l_k, _jax.tree.map(_jnp.add, grad_sum, gw_k)), gx_k

        init = (_jnp.zeros((), _jnp.float32), _jax.tree.map(_jnp.zeros_like, weights))
        (loss, grad_w), grad_x = _jax.lax.scan(body, init, (per_example, given["loss_target"]))
    with _jax.named_scope("update"):
        delta_w, new_m, new_v = {}, {}, {}
        for n in TWIN_WEIGHTS:
            delta_w[n], new_m[n], new_v[n] = _adamw(weights[n], grad_w[n], given["m_" + n], given["v_" + n])
    return (loss, grad_x, *[grad_w[n] for n in TWIN_WEIGHTS], *[delta_w[n] for n in TWIN_WEIGHTS],
            *[new_m[n] for n in TWIN_WEIGHTS], *[new_v[n] for n in TWIN_WEIGHTS])
```

```python
import functools

import jax
import jax.numpy as jnp
from jax import lax
from jax.experimental import pallas as pl
from jax.experimental.pallas import tpu as pltpu

F32 = jnp.float32
BF16 = jnp.bfloat16

D = 1024
DEPTH = 2
N_META = 16
HEADS = 4
DK = 512
DV = 1024
HK = 128
HV = 256
RANK = 16
TAU = 16.0
CHUNK = 64
POOL_WINDOWS = (2, 4, 8, 16)
GROUPS = 4
GDIM = 256
D_FF = 2816
F2 = 2 * D_FF
EPS = 1e-6
IN_WIDTH = 6160
LR, B1, B2, ADAM_EPS, WD, STEP = 0.001, 0.9, 0.999, 1e-8, 0.01, 10

N_DEV = 8
PAD = CHUNK - N_META
X0 = CHUNK
IN_R = 6272
C_Q, C_K, C_V, C_R, C_U, C_GA, C_GB, C_GLR = 0, 512, 1024, 2048, 3072, 4096, 5120, 6144
VMEM_LIMIT = 56 * 1024 * 1024
LANES = 128


def _params(sem=None):
    return pltpu.CompilerParams(dimension_semantics=sem, vmem_limit_bytes=VMEM_LIMIT)


def _pick(n, prefs):
    for t in prefs:
        if n % t == 0:
            return t
    raise ValueError(f"no tile for {n} in {prefs}")


def _row_tile(lp):
    return _pick(lp, (688, 192, 128, 64))


def _ew_tile(lp):
    return _pick(lp, (192, 128, 64))


def _sigmoid(x):
    return 1.0 / (1.0 + jnp.exp(-x))


def _dot(a, b, dims):
    return lax.dot_general(a, b, (dims, ((), ())), preferred_element_type=F32)


def _nn(a, b):
    return _dot(a, b, ((1,), (0,)))


def _nt(a, b):
    return _dot(a, b, ((1,), (1,)))


def _tn(a, b):
    return _dot(a, b, ((0,), (0,)))


def mm_nn(a, b, *, out_dtype=BF16, tn=None, res=None, name):
    m, k = a.shape
    n = b.shape[1]
    tm = _row_tile(m)
    tn = tn or n
    has_res = res is not None

    def body(*refs):
        if has_res:
            a_ref, b_ref, r_ref, o_ref = refs
        else:
            a_ref, b_ref, o_ref = refs
        acc = _nn(a_ref[...], b_ref[...])
        if has_res:
            row = pl.program_id(1) * tm + lax.broadcasted_iota(jnp.int32, (tm, 1), 0)
            acc = jnp.where(row >= PAD, acc + r_ref[...], 0.0)
        o_ref[...] = acc.astype(o_ref.dtype)

    in_specs = [pl.BlockSpec((tm, k), lambda j, i: (i, 0)),
                pl.BlockSpec((k, tn), lambda j, i: (0, j))]
    args = [a, b]
    if has_res:
        in_specs.append(pl.BlockSpec((tm, tn), lambda j, i: (i, j)))
        args.append(res)
    return pl.pallas_call(
        body, name=name, grid=(n // tn, m // tm), in_specs=in_specs,
        out_specs=pl.BlockSpec((tm, tn), lambda j, i: (i, j)),
        out_shape=jax.ShapeDtypeStruct((m, n), out_dtype),
        compiler_params=_params(("parallel", "parallel")))(*args)


def mm_nt(a, b, *, out_dtype=BF16, tn=None, tk=None, name):
    m, k = a.shape
    n = b.shape[0]
    tm = _row_tile(m)
    tn = tn or n
    tk = tk or k
    nk = k // tk

    def body(a_ref, b_ref, o_ref, acc_ref):
        kk = pl.program_id(2)
        part = _nt(a_ref[...], b_ref[...])

        @pl.when(kk == 0)
        def _():
            acc_ref[...] = part

        @pl.when(kk > 0)
        def _():
            acc_ref[...] += part

        @pl.when(kk == nk - 1)
        def _():
            o_ref[...] = acc_ref[...].astype(o_ref.dtype)

    return pl.pallas_call(
        body, name=name, grid=(n // tn, m // tm, nk),
        in_specs=[pl.BlockSpec((tm, tk), lambda j, i, kk: (i, kk)),
                  pl.BlockSpec((tn, tk), lambda j, i, kk: (j, kk))],
        out_specs=pl.BlockSpec((tm, tn), lambda j, i, kk: (i, j)),
        out_shape=jax.ShapeDtypeStruct((m, n), out_dtype),
        scratch_shapes=[pltpu.VMEM((tm, tn), F32)],
        compiler_params=_params(("parallel", "parallel", "arbitrary")))(a, b)


def mm_tn(a, b, *, tk1=None, tn=None, name):
    m, k1 = a.shape
    n = b.shape[1]
    tm = _row_tile(m)
    tk1 = tk1 or k1
    tn = tn or n

    def body(a_ref, b_ref, o_ref):
        part = _tn(a_ref[...], b_ref[...])

        @pl.when(pl.program_id(2) == 0)
        def _():
            o_ref[...] = part

        @pl.when(pl.program_id(2) > 0)
        def _():
            o_ref[...] += part

    return pl.pallas_call(
        body, name=name, grid=(k1 // tk1, n // tn, m // tm),
        in_specs=[pl.BlockSpec((tm, tk1), lambda p, j, i: (i, p)),
                  pl.BlockSpec((tm, tn), lambda p, j, i: (i, j))],
        out_specs=pl.BlockSpec((tk1, tn), lambda p, j, i: (p, j)),
        out_shape=jax.ShapeDtypeStruct((k1, n), F32),
        compiler_params=_params(("parallel", "parallel", "arbitrary")))(a, b)


def pool_mm_fwd(pooled, wp, scale, *, name):
    m = pooled.shape[0]
    tm = _row_tile(m)

    def body(a_ref, w_ref, s_ref, y0_ref, y1_ref):
        acc = _nn(a_ref[...], w_ref[...])
        y0_ref[...] = acc.astype(BF16)
        y1_ref[...] = (acc * s_ref[...]).astype(BF16)

    blk = pl.BlockSpec((tm, GDIM), lambda g, i: (i, g))
    return pl.pallas_call(
        body, name=name, grid=(GROUPS, m // tm),
        in_specs=[blk, pl.BlockSpec((None, GDIM, GDIM), lambda g, i: (g, 0, 0)),
                  pl.BlockSpec((1, GDIM), lambda g, i: (0, g))],
        out_specs=[blk, blk],
        out_shape=[jax.ShapeDtypeStruct((m, D), BF16)] * 2,
        compiler_params=_params(("parallel", "parallel")))(pooled, wp, scale)


def pool_mm_bwd_x(dy0, wp, *, name):
    m = dy0.shape[0]
    tm = _row_tile(m)

    def body(a_ref, w_ref, o_ref):
        o_ref[...] = _nt(a_ref[...], w_ref[...]).astype(BF16)

    blk = pl.BlockSpec((tm, GDIM), lambda g, i: (i, g))
    return pl.pallas_call(
        body, name=name, grid=(GROUPS, m // tm),
        in_specs=[blk, pl.BlockSpec((None, GDIM, GDIM), lambda g, i: (g, 0, 0))],
        out_specs=blk, out_shape=jax.ShapeDtypeStruct((m, D), BF16),
        compiler_params=_params(("parallel", "parallel")))(dy0, wp)


def pool_mm_bwd_w(pooled, dy0, *, name):
    m = pooled.shape[0]
    tm = _row_tile(m)

    def body(a_ref, b_ref, o_ref):
        part = _tn(a_ref[...], b_ref[...])

        @pl.when(pl.program_id(1) == 0)
        def _():
            o_ref[...] = part

        @pl.when(pl.program_id(1) > 0)
        def _():
            o_ref[...] += part

    blk = pl.BlockSpec((tm, GDIM), lambda g, i: (i, g))
    return pl.pallas_call(
        body, name=name, grid=(GROUPS, m // tm), in_specs=[blk, blk],
        out_specs=pl.BlockSpec((None, GDIM, GDIM), lambda g, i: (g, 0, 0)),
        out_shape=jax.ShapeDtypeStruct((GROUPS, GDIM, GDIM), F32),
        compiler_params=_params(("parallel", "arbitrary")))(pooled, dy0)


def rmsnorm_fwd(x, g, *, name):
    m = x.shape[0]
    tm = _ew_tile(m)

    def body(x_ref, g_ref, o_ref):
        xv = x_ref[...]
        r = lax.rsqrt(jnp.mean(xv * xv, axis=-1, keepdims=True) + EPS)
        o_ref[...] = (xv * r * g_ref[...]).astype(BF16)

    return pl.pallas_call(
        body, name=name, grid=(m // tm,),
        in_specs=[pl.BlockSpec((tm, D), lambda i: (i, 0)), pl.BlockSpec((1, D), lambda i: (0, 0))],
        out_specs=pl.BlockSpec((tm, D), lambda i: (i, 0)),
        out_shape=jax.ShapeDtypeStruct((m, D), BF16),
        compiler_params=_params(("parallel",)))(x, g)


def rmsnorm_bwd(dy, x, g, dres, *, name):
    m = x.shape[0]
    tm = _ew_tile(m)

    def body(dy_ref, x_ref, g_ref, r_ref, dx_ref, dg_ref):
        i = pl.program_id(0)
        xv = x_ref[...]
        dyv = dy_ref[...].astype(F32)
        r = lax.rsqrt(jnp.mean(xv * xv, axis=-1, keepdims=True) + EPS)
        xh = xv * r
        dxh = dyv * g_ref[...]
        dx = r * (dxh - xh * jnp.mean(dxh * xh, axis=-1, keepdims=True))
        row = i * tm + lax.broadcasted_iota(jnp.int32, (tm, 1), 0)
        dx_ref[...] = jnp.where(row >= PAD, dx + r_ref[...], 0.0)

        @pl.when(i == 0)
        def _():
            dg_ref[...] = jnp.zeros_like(dg_ref)

        dg_ref[...] += jnp.sum(dyv * xh, axis=0, keepdims=True)

    blk = pl.BlockSpec((tm, D), lambda i: (i, 0))
    vec = pl.BlockSpec((1, D), lambda i: (0, 0))
    return pl.pallas_call(
        body, name=name, grid=(m // tm,), in_specs=[blk, blk, vec, blk],
        out_specs=[blk, vec],
        out_shape=[jax.ShapeDtypeStruct((m, D), F32), jax.ShapeDtypeStruct((1, D), F32)],
        compiler_params=_params(("arbitrary",)))(dy, x, g, dres)


def loss_head(h, gf, target, *, name):
    m = h.shape[0]
    t = X0
    inv_d = 1.0 / D

    def body(h_ref, g_ref, t_ref, dh_ref, dg_ref, ls_ref):
        i = pl.program_id(0)

        @pl.when(i == 0)
        def _():
            dg_ref[...] = jnp.zeros_like(dg_ref)
            ls_ref[...] = jnp.zeros_like(ls_ref)
            dh_ref[...] = jnp.zeros_like(dh_ref)

        @pl.when(i > 0)
        def _():
            xv = h_ref[...]
            r = lax.rsqrt(jnp.mean(xv * xv, axis=-1, keepdims=True) + EPS)
            xh = xv * r
            err = xh * g_ref[...] - t_ref[...]
            ls_ref[...] += jnp.sum(err * err, axis=0, keepdims=True)
            dy = err * inv_d
            dg_ref[...] += jnp.sum(dy * xh, axis=0, keepdims=True)
            dxh = dy * g_ref[...]
            dh_ref[...] = r * (dxh - xh * jnp.mean(dxh * xh, axis=-1, keepdims=True))

    blk = pl.BlockSpec((t, D), lambda i: (i, 0))
    vec = pl.BlockSpec((1, D), lambda i: (0, 0))
    return pl.pallas_call(
        body, name=name, grid=(m // t,),
        in_specs=[blk, vec, pl.BlockSpec((t, D), lambda i: (jnp.maximum(i - 1, 0), 0))],
        out_specs=[blk, vec, vec],
        out_shape=[jax.ShapeDtypeStruct((m, D), F32), jax.ShapeDtypeStruct((1, D), F32),
                   jax.ShapeDtypeStruct((1, D), F32)],
        compiler_params=_params(("arbitrary",)))(h, gf, target)


def _split3(x):
    x1 = x.astype(BF16)
    r1 = x - x1.astype(F32)
    x2 = r1.astype(BF16)
    x3 = (r1 - x2.astype(F32)).astype(BF16)
    return x1, x2, x3


def _tri_mm(tri, x):
    x1, x2, x3 = _split3(x)
    return _nn(tri, x1) + _nn(tri, x2) + _nn(tri, x3)


def _log_decay(glr, wgk, bgk, row0, rows):
    z = _nn(glr, wgk) + bgk
    la = (jnp.minimum(z, 0.0) - jnp.log(1.0 + jnp.exp(-jnp.abs(z)))) * (1.0 / TAU)
    row = row0 + lax.broadcasted_iota(jnp.int32, (rows, 1), 0)
    return z, jnp.where(row >= PAD, la, 0.0)


def _chunk_group(n_chunks):
    return _pick(n_chunks, (3, 2, 1))


def gla_fwd(p, wgk, bgk, *, name):
    m = p.shape[0]
    n_chunks = m // CHUNK
    cg = _chunk_group(n_chunks)
    t = cg * CHUNK
    scale = HK ** -0.5

    def body(q_ref, k_ref, v_ref, glr_ref, wgk_ref, bgk_ref, o_ref, st_ref, state):
        i = pl.program_id(0)

        @pl.when(i == 0)
        def _():
            state[...] = jnp.zeros_like(state)

        _, la = _log_decay(glr_ref[...], wgk_ref[...], bgk_ref[...], i * t, t)
        ri = lax.broadcasted_iota(jnp.int32, (CHUNK, CHUNK), 0)
        ci = lax.broadcasted_iota(jnp.int32, (CHUNK, CHUNK), 1)
        causal = ri >= ci
        tri = causal.astype(BF16)
        for c in range(cg):
            rows = pl.ds(c * CHUNK, CHUNK)
            b = _tri_mm(tri, la[c * CHUNK:(c + 1) * CHUNK])
            bl = b[CHUNK - 1:CHUNK, :]
            q = q_ref[rows, :].astype(F32) * scale
            k = k_ref[rows, :].astype(F32)
            qd = (q * jnp.exp(b)).astype(BF16)
            ki = (k * jnp.exp(-b)).astype(BF16)
            ke = (k * jnp.exp(bl - b)).astype(BF16)
            dec = jnp.exp(bl)
            for h in range(HEADS):
                ks = slice(h * HK, (h + 1) * HK)
                vs = pl.ds(h * HV, HV)
                vh = v_ref[rows, vs]
                s_t = state[h]
                st_ref[c, h] = s_t
                att = jnp.where(causal, _nt(qd[:, ks], ki[:, ks]), 0.0).astype(BF16)
                o_ref[rows, vs] = _nn(att, vh) + _nt(qd[:, ks], s_t.astype(BF16))
                state[h] = s_t * dec[:, ks] + _tn(vh, ke[:, ks])

    return pl.pallas_call(
        body, name=name, grid=(n_chunks // cg,),
        in_specs=[pl.BlockSpec((t, DK), lambda i: (i, C_Q // DK)),
                  pl.BlockSpec((t, DK), lambda i: (i, C_K // DK)),
                  pl.BlockSpec((t, DV), lambda i: (i, C_V // DV)),
                  pl.BlockSpec((t, LANES), lambda i: (i, C_GLR // LANES)),
                  pl.BlockSpec((LANES, DK), lambda i: (0, 0)),
                  pl.BlockSpec((1, DK), lambda i: (0, 0))],
        out_specs=[pl.BlockSpec((t, DV), lambda i: (i, 0)),
                   pl.BlockSpec((cg, HEADS, HV, HK), lambda i: (i, 0, 0, 0))],
        out_shape=[jax.ShapeDtypeStruct((m, DV), F32),
                   jax.ShapeDtypeStruct((n_chunks, HEADS, HV, HK), F32)],
        scratch_shapes=[pltpu.VMEM((HEADS, HV, HK), F32)],
        compiler_params=_params(("arbitrary",)))(p, p, p, p, wgk, bgk)


def gla_bwd(p, wgk, bgk, st, do, *, name):
    m = p.shape[0]
    n_chunks = m // CHUNK
    cg = _chunk_group(n_chunks)
    t = cg * CHUNK
    ns = n_chunks // cg
    scale = HK ** -0.5

    def body(q_ref, k_ref, v_ref, glr_ref, wgk_ref, bgk_ref, st_ref, do_ref,
             dqkv_ref, dglr_ref, dwgk_ref, dbgk_ref, dstate, dz_buf):
        i = pl.program_id(0)
        blk = ns - 1 - i

        @pl.when(i == 0)
        def _():
            dstate[...] = jnp.zeros_like(dstate)
            dwgk_ref[...] = jnp.zeros_like(dwgk_ref)
            dbgk_ref[...] = jnp.zeros_like(dbgk_ref)

        z, la = _log_decay(glr_ref[...], wgk_ref[...], bgk_ref[...], blk * t, t)
        ri = lax.broadcasted_iota(jnp.int32, (CHUNK, CHUNK), 0)
        ci = lax.broadcasted_iota(jnp.int32, (CHUNK, CHUNK), 1)
        causal = ri >= ci
        tri = causal.astype(BF16)
        tri_u = (ri <= ci).astype(BF16)
        for c in reversed(range(cg)):
            rows = pl.ds(c * CHUNK, CHUNK)
            b = _tri_mm(tri, la[c * CHUNK:(c + 1) * CHUNK])
            bl = b[CHUNK - 1:CHUNK, :]
            eb = jnp.exp(b)
            enb = jnp.exp(-b)
            ebl = jnp.exp(bl - b)
            dec = jnp.exp(bl)
            q = q_ref[rows, :].astype(F32) * scale
            k = k_ref[rows, :].astype(F32)
            qd32 = q * eb
            ki32 = k * enb
            ke32 = k * ebl
            qd = qd32.astype(BF16)
            ki = ki32.astype(BF16)
            ke = ke32.astype(BF16)
            dqd_parts, dki_parts, dke_parts, ddec_parts = [], [], [], []
            for h in range(HEADS):
                ks = slice(h * HK, (h + 1) * HK)
                vs = pl.ds(h * HV, HV)
                vh = v_ref[rows, vs]
                doh = do_ref[rows, vs].astype(BF16)
                s_t = st_ref[c, h]
                ds_t = dstate[h]
                ds_b = ds_t.astype(BF16)
                att = jnp.where(causal, _nt(qd[:, ks], ki[:, ks]), 0.0).astype(BF16)
                datt = jnp.where(causal, _nt(doh, vh), 0.0).astype(BF16)
                dvh = _tn(att, doh) + _nt(ke[:, ks], ds_b)
                dqkv_ref[rows, pl.ds(2 * DK + h * HV, HV)] = dvh.astype(BF16)
                dqd_parts.append(_nn(datt, ki[:, ks]) + _nn(doh, s_t.astype(BF16)))
                dki_parts.append(_tn(datt, qd[:, ks]))
                dke_parts.append(_nn(vh, ds_b))
                ddec_parts.append(jnp.sum(s_t * ds_t, axis=0, keepdims=True))
                dstate[h] = _tn(doh, qd[:, ks]) + ds_t * dec[:, ks]
            dqd = jnp.concatenate(dqd_parts, axis=1)
            dki = jnp.concatenate(dki_parts, axis=1)
            dke = jnp.concatenate(dke_parts, axis=1)
            ddec = jnp.concatenate(ddec_parts, axis=1)
            dqkv_ref[rows, pl.ds(0, DK)] = (dqd * eb * scale).astype(BF16)
            dqkv_ref[rows, pl.ds(DK, DK)] = (dki * enb + dke * ebl).astype(BF16)
            dke_ke = dke * ke32
            db = dqd * qd32 - dki * ki32 - dke_ke
            dbl = jnp.sum(dke_ke, axis=0, keepdims=True) + ddec * dec
            dg = _tri_mm(tri_u, db) + dbl
            row = blk * t + c * CHUNK + lax.broadcasted_iota(jnp.int32, (CHUNK, 1), 0)
            zc = z[c * CHUNK:(c + 1) * CHUNK]
            dz = jnp.where(row >= PAD, dg * (1.0 / TAU) * _sigmoid(-zc), 0.0)
            dz_buf[rows, :] = dz
        dz_all = dz_buf[...]
        dz_b = dz_all.astype(BF16)
        dbgk_ref[...] += jnp.sum(dz_all, axis=0, keepdims=True)
        dglr_ref[...] = _nt(dz_b, wgk_ref[...]).astype(BF16)
        dwgk_ref[...] += _tn(glr_ref[...], dz_b)

    rev = lambda i: ns - 1 - i
    return pl.pallas_call(
        body, name=name, grid=(ns,),
        in_specs=[pl.BlockSpec((t, DK), lambda i: (rev(i), C_Q // DK)),
                  pl.BlockSpec((t, DK), lambda i: (rev(i), C_K // DK)),
                  pl.BlockSpec((t, DV), lambda i: (rev(i), C_V // DV)),
                  pl.BlockSpec((t, LANES), lambda i: (rev(i), C_GLR // LANES)),
                  pl.BlockSpec((LANES, DK), lambda i: (0, 0)),
                  pl.BlockSpec((1, DK), lambda i: (0, 0)),
                  pl.BlockSpec((cg, HEADS, HV, HK), lambda i: (rev(i), 0, 0, 0)),
                  pl.BlockSpec((t, DV), lambda i: (rev(i), 0))],
        out_specs=[pl.BlockSpec((t, 2 * DK + DV), lambda i: (rev(i), 0)),
                   pl.BlockSpec((t, LANES), lambda i: (rev(i), 0)),
                   pl.BlockSpec((LANES, DK), lambda i: (0, 0)),
                   pl.BlockSpec((1, DK), lambda i: (0, 0))],
        out_shape=[jax.ShapeDtypeStruct((m, 2 * DK + DV), BF16),
                   jax.ShapeDtypeStruct((m, LANES), BF16),
                   jax.ShapeDtypeStruct((LANES, DK), F32),
                   jax.ShapeDtypeStruct((1, DK), F32)],
        scratch_shapes=[pltpu.VMEM((HEADS, HV, HK), F32), pltpu.VMEM((t, DK), F32)],
        compiler_params=_params(("arbitrary",)))(p, p, p, p, wgk, bgk, st, do)


HALO = 16


def _shift_down(xx, s):
    return pltpu.roll(xx, s, 0)


def _shift_up(xx, s):
    return pltpu.roll(xx, xx.shape[0] - s, 0)


def mix_pre(o, p, gn, *, name):
    m = o.shape[0]
    tm = _ew_tile(m)

    def body(o_ref, r_ref, u_ref, gn_ref, ya_ref, pooled_ref, halo):
        i = pl.program_id(0)

        @pl.when(i == 0)
        def _():
            halo[...] = jnp.zeros_like(halo)

        rv = r_ref[...].astype(F32)
        silu_r = rv * _sigmoid(rv)
        for h in range(HEADS):
            cs = pl.ds(h * HV, HV)
            ov = o_ref[:, cs]
            rs = lax.rsqrt(jnp.mean(ov * ov, axis=-1, keepdims=True) + EPS)
            ya_ref[:, cs] = (ov * rs * gn_ref[...] * silu_r[:, h * HV:(h + 1) * HV]).astype(BF16)

        row = i * tm + lax.broadcasted_iota(jnp.int32, (tm, 1), 0)
        pos1 = jnp.maximum(row - PAD + 1, 1).astype(F32)
        for g, w in enumerate(POOL_WINDOWS):
            cs = pl.ds(g * GDIM, GDIM)
            uv = u_ref[:, cs].astype(F32)
            xx = jnp.concatenate([halo[:, cs], uv], axis=0)
            s = xx
            span = 1
            while span < w:
                s = s + _shift_down(s, span)
                span *= 2
            inv = 1.0 / jnp.minimum(pos1, float(w))
            pooled_ref[:, cs] = (s[HALO:] * inv - uv).astype(BF16)
            halo[:, cs] = uv[tm - HALO:]

    blk = pl.BlockSpec((tm, D), lambda i: (i, 0))
    return pl.pallas_call(
        body, name=name, grid=(m // tm,),
        in_specs=[blk, pl.BlockSpec((tm, D), lambda i: (i, C_R // D)),
                  pl.BlockSpec((tm, D), lambda i: (i, C_U // D)),
                  pl.BlockSpec((1, HV), lambda i: (0, 0))],
        out_specs=[blk, blk],
        out_shape=[jax.ShapeDtypeStruct((m, D), BF16)] * 2,
        scratch_shapes=[pltpu.VMEM((HALO, D), F32)],
        compiler_params=_params(("arbitrary",)))(o, p, p, gn)


def mix_pre_bwd(dya, dpooled, o, p, gn, *, name):
    m = o.shape[0]
    tm = _ew_tile(m)
    nt = m // tm

    def body(dya_ref, dpl_ref, o_ref, r_ref, gn_ref, do_ref, dr_ref, du_ref, dgn_ref, halo):
        i = pl.program_id(0)
        blk_i = nt - 1 - i

        @pl.when(i == 0)
        def _():
            halo[...] = jnp.zeros_like(halo)
            dgn_ref[...] = jnp.zeros_like(dgn_ref)

        rv = r_ref[...].astype(F32)
        sg = _sigmoid(rv)
        silu_r = rv * sg
        dsilu = sg * (1.0 + rv * (1.0 - sg))
        dgn = jnp.zeros((1, HV), F32)
        for h in range(HEADS):
            cs = pl.ds(h * HV, HV)
            hs = slice(h * HV, (h + 1) * HV)
            ov = o_ref[:, cs]
            dy = dya_ref[:, cs].astype(F32)
            rs = lax.rsqrt(jnp.mean(ov * ov, axis=-1, keepdims=True) + EPS)
            xh = ov * rs
            on = xh * gn_ref[...]
            don = dy * silu_r[:, hs]
            dr_ref[:, cs] = (dy * on * dsilu[:, hs]).astype(BF16)
            dxh = don * gn_ref[...]
            do_ref[:, cs] = rs * (dxh - xh * jnp.mean(dxh * xh, axis=-1, keepdims=True))
            dgn = dgn + jnp.sum(don * xh, axis=0, keepdims=True)
        dgn_ref[...] += dgn

        row = blk_i * tm + lax.broadcasted_iota(jnp.int32, (tm, 1), 0)
        pos1 = jnp.maximum(row - PAD + 1, 1).astype(F32)
        for g, w in enumerate(POOL_WINDOWS):
            cs = pl.ds(g * GDIM, GDIM)
            dpv = dpl_ref[:, cs].astype(F32)
            e = dpv * (1.0 / jnp.minimum(pos1, float(w)))
            xx = jnp.concatenate([e, halo[:, cs]], axis=0)
            s = xx
            span = 1
            while span < w:
                s = s + _shift_up(s, span)
                span *= 2
            du_ref[:, cs] = (s[:tm] - dpv).astype(BF16)
            halo[:, cs] = e[:HALO]

    rev = lambda i: nt - 1 - i
    blk = pl.BlockSpec((tm, D), lambda i: (rev(i), 0))
    return pl.pallas_call(
        body, name=name, grid=(nt,),
        in_specs=[blk, blk, blk, pl.BlockSpec((tm, D), lambda i: (rev(i), C_R // D)),
                  pl.BlockSpec((1, HV), lambda i: (0, 0))],
        out_specs=[blk, blk, blk, pl.BlockSpec((1, HV), lambda i: (0, 0))],
        out_shape=[jax.ShapeDtypeStruct((m, D), F32), jax.ShapeDtypeStruct((m, D), BF16),
                   jax.ShapeDtypeStruct((m, D), BF16), jax.ShapeDtypeStruct((1, HV), F32)],
        scratch_shapes=[pltpu.VMEM((HALO, D), F32)],
        compiler_params=_params(("arbitrary",)))(dya, dpooled, o, p, gn)


def merge_fwd(p, ya, yb, bg, *, name):
    m = ya.shape[0]
    tm = _ew_tile(m)

    def body(ga_ref, gb_ref, ya_ref, yb_ref, ba_ref, bb_ref, o_ref):
        gate_a = _sigmoid(ga_ref[...].astype(F32) + ba_ref[...])
        gate_b = _sigmoid(gb_ref[...].astype(F32) + bb_ref[...])
        o_ref[...] = (gate_a * ya_ref[...].astype(F32) + gate_b * yb_ref[...].astype(F32)).astype(BF16)

    blk = pl.BlockSpec((tm, D), lambda i: (i, 0))
    return pl.pallas_call(
        body, name=name, grid=(m // tm,),
        in_specs=[pl.BlockSpec((tm, D), lambda i: (i, C_GA // D)),
                  pl.BlockSpec((tm, D), lambda i: (i, C_GB // D)), blk, blk,
                  pl.BlockSpec((1, D), lambda i: (0, 0)), pl.BlockSpec((1, D), lambda i: (0, 1))],
        out_specs=blk, out_shape=jax.ShapeDtypeStruct((m, D), BF16),
        compiler_params=_params(("parallel",)))(p, p, ya, yb, bg, bg)


def merge_bwd(dmrg, p, ya, yb, bg, *, name):
    m = ya.shape[0]
    tm = _ew_tile(m)

    def body(dm_ref, ga_ref, gb_ref, ya_ref, yb_ref, ba_ref, bb_ref,
             dya_ref, dyb_ref, dga_ref, dgb_ref, dbg_ref):
        @pl.when(pl.program_id(0) == 0)
        def _():
            dbg_ref[...] = jnp.zeros_like(dbg_ref)

        dm = dm_ref[...].astype(F32)
        gate_a = _sigmoid(ga_ref[...].astype(F32) + ba_ref[...])
        gate_b = _sigmoid(gb_ref[...].astype(F32) + bb_ref[...])
        dya_ref[...] = (dm * gate_a).astype(BF16)
        dyb_ref[...] = (dm * gate_b).astype(BF16)
        dga = dm * ya_ref[...].astype(F32) * gate_a * (1.0 - gate_a)
        dgb = dm * yb_ref[...].astype(F32) * gate_b * (1.0 - gate_b)
        dga_ref[...] = dga.astype(BF16)
        dgb_ref[...] = dgb.astype(BF16)
        dbg_ref[:, pl.ds(0, D)] += jnp.sum(dga, axis=0, keepdims=True)
        dbg_ref[:, pl.ds(D, D)] += jnp.sum(dgb, axis=0, keepdims=True)

    blk = pl.BlockSpec((tm, D), lambda i: (i, 0))
    return pl.pallas_call(
        body, name=name, grid=(m // tm,),
        in_specs=[blk, pl.BlockSpec((tm, D), lambda i: (i, C_GA // D)),
                  pl.BlockSpec((tm, D), lambda i: (i, C_GB // D)), blk, blk,
                  pl.BlockSpec((1, D), lambda i: (0, 0)), pl.BlockSpec((1, D), lambda i: (0, 1))],
        out_specs=[blk, blk, blk, blk, pl.BlockSpec((1, 2 * D), lambda i: (0, 0))],
        out_shape=[jax.ShapeDtypeStruct((m, D), BF16)] * 4 + [jax.ShapeDtypeStruct((1, 2 * D), F32)],
        compiler_params=_params(("arbitrary",)))(dmrg, p, p, ya, yb, bg, bg)


def scale_bwd(dy1, y0, scale, *, name):
    m = y0.shape[0]
    tm = _ew_tile(m)

    def body(dy_ref, y0_ref, s_ref, o_ref, ds_ref):
        @pl.when(pl.program_id(0) == 0)
        def _():
            ds_ref[...] = jnp.zeros_like(ds_ref)

        dy = dy_ref[...].astype(F32)
        o_ref[...] = (dy * s_ref[...]).astype(BF16)
        ds_ref[...] += jnp.sum(dy * y0_ref[...].astype(F32), axis=0, keepdims=True)

    blk = pl.BlockSpec((tm, D), lambda i: (i, 0))
    vec = pl.BlockSpec((1, D), lambda i: (0, 0))
    return pl.pallas_call(
        body, name=name, grid=(m // tm,), in_specs=[blk, blk, vec], out_specs=[blk, vec],
        out_shape=[jax.ShapeDtypeStruct((m, D), BF16), jax.ShapeDtypeStruct((1, D), F32)],
        compiler_params=_params(("arbitrary",)))(dy1, y0, scale)


CONV_BLK = 1408
N_CONV_BLK = D_FF // CONV_BLK


def conv_act_fwd(up, cw, cb, *, name):
    m = up.shape[0]
    tm = _ew_tile(m)

    def conv(x_ref, halo, w_ref, b_ref):
        xv = x_ref[...].astype(F32)
        xx = jnp.concatenate([halo[...], xv], axis=0)
        y = (w_ref[2:3, :] * xx + w_ref[1:2, :] * _shift_down(xx, 1)
             + w_ref[0:1, :] * _shift_down(xx, 2))[HALO:] + b_ref[...]
        halo[...] = xv[tm - HALO:]
        return y

    def body(xa_ref, xb_ref, wa_ref, wb_ref, ba_ref, bb_ref, upc_a_ref, upc_b_ref, act_ref, halo_a, halo_b):
        @pl.when(pl.program_id(1) == 0)
        def _():
            halo_a[...] = jnp.zeros_like(halo_a)
            halo_b[...] = jnp.zeros_like(halo_b)

        a = conv(xa_ref, halo_a, wa_ref, ba_ref)
        bv = conv(xb_ref, halo_b, wb_ref, bb_ref)
        upc_a_ref[...] = a.astype(BF16)
        upc_b_ref[...] = bv.astype(BF16)
        act_ref[...] = (a * _sigmoid(a) * bv).astype(BF16)

    nb = N_CONV_BLK
    xa = pl.BlockSpec((tm, CONV_BLK), lambda j, i: (i, j))
    xb = pl.BlockSpec((tm, CONV_BLK), lambda j, i: (i, j + nb))
    return pl.pallas_call(
        body, name=name, grid=(nb, m // tm),
        in_specs=[xa, xb,
                  pl.BlockSpec((3, CONV_BLK), lambda j, i: (0, j)),
                  pl.BlockSpec((3, CONV_BLK), lambda j, i: (0, j + nb)),
                  pl.BlockSpec((1, CONV_BLK), lambda j, i: (0, j)),
                  pl.BlockSpec((1, CONV_BLK), lambda j, i: (0, j + nb))],
        out_specs=[xa, xa, xa],
        out_shape=[jax.ShapeDtypeStruct((m, D_FF), BF16)] * 3,
        scratch_shapes=[pltpu.VMEM((HALO, CONV_BLK), F32)] * 2,
        compiler_params=_params(("parallel", "arbitrary")))(up, up, cw, cw, cb, cb)


def conv_act_bwd(dact, upc_a, upc_b, up, cw, *, name):
    m = up.shape[0]
    tm = _ew_tile(m)
    nt = m // tm

    def conv_t(d, halo, x_ref, w_ref, dup_ref, dw_ref, db_ref):
        xx = jnp.concatenate([d, halo[...]], axis=0)
        d1 = _shift_up(xx, 1)[:tm]
        d2 = _shift_up(xx, 2)[:tm]
        dup_ref[...] = (w_ref[2:3, :] * d + w_ref[1:2, :] * d1 + w_ref[0:1, :] * d2).astype(BF16)
        xv = x_ref[...].astype(F32)
        dw_ref[2:3, :] += jnp.sum(xv * d, axis=0, keepdims=True)
        dw_ref[1:2, :] += jnp.sum(xv * d1, axis=0, keepdims=True)
        dw_ref[0:1, :] += jnp.sum(xv * d2, axis=0, keepdims=True)
        db_ref[...] += jnp.sum(d, axis=0, keepdims=True)
        halo[...] = d[:HALO]

    def body(da_ref, a_ref, b_ref, xa_ref, xb_ref, wa_ref, wb_ref,
             dupa_ref, dupb_ref, dwa_ref, dwb_ref, dba_ref, dbb_ref, halo_a, halo_b):
        @pl.when(pl.program_id(1) == 0)
        def _():
            for r in (halo_a, halo_b, dwa_ref, dwb_ref, dba_ref, dbb_ref):
                r[...] = jnp.zeros_like(r)

        dact_v = da_ref[...].astype(F32)
        a = a_ref[...].astype(F32)
        bv = b_ref[...].astype(F32)
        sg = _sigmoid(a)
        d_a = dact_v * bv * sg * (1.0 + a * (1.0 - sg))
        d_b = dact_v * a * sg
        conv_t(d_a, halo_a, xa_ref, wa_ref, dupa_ref, dwa_ref, dba_ref)
        conv_t(d_b, halo_b, xb_ref, wb_ref, dupb_ref, dwb_ref, dbb_ref)

    nb = N_CONV_BLK
    rev = lambda i: nt - 1 - i
    half = pl.BlockSpec((tm, CONV_BLK), lambda j, i: (rev(i), j))
    xa = half
    xb = pl.BlockSpec((tm, CONV_BLK), lambda j, i: (rev(i), j + nb))
    wa = pl.BlockSpec((3, CONV_BLK), lambda j, i: (0, j))
    wb = pl.BlockSpec((3, CONV_BLK), lambda j, i: (0, j + nb))
    va = pl.BlockSpec((1, CONV_BLK), lambda j, i: (0, j))
    outs = pl.pallas_call(
        body, name=name, grid=(nb, nt),
        in_specs=[half, half, half, xa, xb, wa, wb],
        out_specs=[half, half, wa, wa, va, va],
        out_shape=[jax.ShapeDtypeStruct((m, D_FF), BF16)] * 2
                  + [jax.ShapeDtypeStruct((3, D_FF), F32)] * 2
                  + [jax.ShapeDtypeStruct((1, D_FF), F32)] * 2,
        scratch_shapes=[pltpu.VMEM((HALO, CONV_BLK), F32)] * 2,
        compiler_params=_params(("parallel", "arbitrary")))(dact, upc_a, upc_b, up, up, cw, cw)
    return outs


def local_step(x, target, w):
    seq = x.shape[0]
    h = jnp.concatenate([jnp.zeros((PAD, D), F32), w["meta"], x], axis=0)
    saved = []
    for l in range(DEPTH):
        ln = f"l{l}_"
        hn1 = rmsnorm_fwd(h, w["norm1_g"][l:l + 1], name=ln + "norm1")
        p = mm_nn(hn1, w["w_in"][l], tn=896, name=ln + "in_proj")
        o, st = gla_fwd(p, w["w_gk"][l], w["b_gk"][l:l + 1], name=ln + "gla_fwd")
        ya_in, pooled = mix_pre(o, p, w["gla_norm_g"][l:l + 1], name=ln + "mix_pre")
        ya = mm_nn(ya_in, w["w_a"][l], name=ln + "proj_a")
        yb0, yb1 = pool_mm_fwd(pooled, w["w_pool"][l], w["pool_scale"][l:l + 1], name=ln + "pool_mm")
        yb = mm_nn(yb1, w["w_b"][l], name=ln + "proj_b")
        mrg = merge_fwd(p, ya, yb, w["b_gates"][l:l + 1], name=ln + "merge")
        h2 = mm_nn(mrg, w["w_o"][l], out_dtype=F32, res=h, name=ln + "proj_o")
        hn2 = rmsnorm_fwd(h2, w["norm2_g"][l:l + 1], name=ln + "norm2")
        up = mm_nn(hn2, w["w_up"][l], tn=1408, name=ln + "up_proj")
        upc_a, upc_b, act = conv_act_fwd(up, w["conv_w"][l], w["conv_b"][l:l + 1], name=ln + "conv_act")
        h3 = mm_nn(act, w["w_down"][l], out_dtype=F32, res=h2, name=ln + "down_proj")
        saved.append(dict(h=h, hn1=hn1, p=p, o=o, st=st, ya_in=ya_in, pooled=pooled, ya=ya, yb0=yb0,
                          yb1=yb1, yb=yb, mrg=mrg, h2=h2, hn2=hn2, up=up, upc_a=upc_a, upc_b=upc_b, act=act))
        h = h3

    dh, dgf, loss_rows = loss_head(h, w["final_norm_g"], target, name="loss_head")
    g = {"final_norm_g": dgf}
    per_layer = []
    for l in reversed(range(DEPTH)):
        ln = f"l{l}_"
        s = saved[l]
        dh_b = dh.astype(BF16)
        dact = mm_nt(dh_b, w["w_down"][l], tn=1408, name=ln + "d_act")
        d_w_down = mm_tn(s["act"], dh_b, tk1=1408, name=ln + "dw_down")
        dup_a, dup_b, dcw_a, dcw_b, dcb_a, dcb_b = conv_act_bwd(
            dact, s["upc_a"], s["upc_b"], s["up"], w["conv_w"][l], name=ln + "conv_act_bwd")
        dup = jnp.concatenate([dup_a, dup_b], axis=1)
        dhn2 = mm_nt(dup, w["w_up"][l], out_dtype=F32, tk=1408, name=ln + "d_hn2")
        d_w_up = mm_tn(s["hn2"], dup, tn=1408, name=ln + "dw_up")
        dh2, dg2 = rmsnorm_bwd(dhn2, s["h2"], w["norm2_g"][l:l + 1], dh, name=ln + "norm2_bwd")
        dh2_b = dh2.astype(BF16)
        dmrg = mm_nt(dh2_b, w["w_o"][l], name=ln + "d_mrg")
        d_w_o = mm_tn(s["mrg"], dh2_b, name=ln + "dw_o")
        dya, dyb, dga, dgb, dbg = merge_bwd(dmrg, s["p"], s["ya"], s["yb"], w["b_gates"][l:l + 1],
                                            name=ln + "merge_bwd")
        dya_in = mm_nt(dya, w["w_a"][l], name=ln + "d_ya_in")
        d_w_a = mm_tn(s["ya_in"], dya, name=ln + "dw_a")
        dyb1 = mm_nt(dyb, w["w_b"][l], name=ln + "d_yb1")
        d_w_b = mm_tn(s["yb1"], dyb, name=ln + "dw_b")
        dyb0, dps = scale_bwd(dyb1, s["yb0"], w["pool_scale"][l:l + 1], name=ln + "scale_bwd")
        dpooled = pool_mm_bwd_x(dyb0, w["w_pool"][l], name=ln + "d_pooled")
        d_w_pool = pool_mm_bwd_w(s["pooled"], dyb0, name=ln + "dw_pool")
        do, dr, du, dgn = mix_pre_bwd(dya_in, dpooled, s["o"], s["p"], w["gla_norm_g"][l:l + 1],
                                      name=ln + "mix_pre_bwd")
        dqkv, dglr, dwgk, dbgk = gla_bwd(s["p"], w["w_gk"][l], w["b_gk"][l:l + 1], s["st"], do,
                                         name=ln + "gla_bwd")
        dp = jnp.concatenate([dqkv, dr, du, dga, dgb, dglr], axis=1)
        dhn1 = mm_nt(dp, w["w_in"][l], out_dtype=F32, tk=896, name=ln + "d_hn1")
        d_w_in = mm_tn(s["hn1"], dp, tn=896, name=ln + "dw_in")
        dh, dg1 = rmsnorm_bwd(dhn1, s["h"], w["norm1_g"][l:l + 1], dh2, name=ln + "norm1_bwd")
        per_layer.append(dict(
            norm1_g=dg1, w_in=d_w_in, w_gk=dwgk, b_gk=dbgk, gla_norm_g=dgn, w_a=d_w_a, w_pool=d_w_pool,
            pool_scale=dps, w_b=d_w_b, b_gates=dbg, w_o=d_w_o, norm2_g=dg2, w_up=d_w_up,
            conv_w=jnp.concatenate([dcw_a, dcw_b], axis=1), conv_b=jnp.concatenate([dcb_a, dcb_b], axis=1),
            w_down=d_w_down))
    per_layer.reverse()
    for k in per_layer[0]:
        g[k] = jnp.stack([per_layer[l][k] for l in range(DEPTH)])
    g["meta"] = dh[PAD:X0]
    return loss_rows, dh[X0:X0 + seq], g


def _my_place():
    return lax.axis_index("x"), lax.axis_index("y"), lax.axis_index("c")


def _peer(place, k):
    x, y, c = place
    return (1 - x if k & 4 else x, 1 - y if k & 2 else y, 1 - c if k & 1 else c)


def _index(place):
    x, y, c = place
    return 4 * x + 2 * y + c


def exchange(arrays, kinds, *, name):
    n = len(arrays)

    def body(*refs):
        ins, outs = refs[:n], refs[n:2 * n]
        send_sems, recv_sems, local_sems = refs[2 * n:]
        place = _my_place()
        me = _index(place)

        def src(a, dest):
            return ins[a] if kinds[a] == "gather" else ins[a].at[dest]

        def remote(a, k):
            peer = _peer(place, k)
            return pltpu.make_async_remote_copy(
                src_ref=src(a, _index(peer)), dst_ref=outs[a].at[me],
                send_sem=send_sems.at[a, k - 1], recv_sem=recv_sems.at[a, k - 1],
                device_id=peer, device_id_type=pl.DeviceIdType.MESH)

        def arrival(a, k):
            peer = _peer(place, k)
            return pltpu.make_async_remote_copy(
                src_ref=src(a, me), dst_ref=outs[a].at[_index(peer)],
                send_sem=send_sems.at[a, k - 1], recv_sem=recv_sems.at[a, k - 1],
                device_id=peer, device_id_type=pl.DeviceIdType.MESH)

        own = [pltpu.make_async_copy(src(a, me), outs[a].at[me], local_sems.at[a]) for a in range(n)]
        sends = [remote(a, k) for k in range(1, N_DEV) for a in range(n)]
        for cp in sends:
            cp.start()
        for cp in own:
            cp.start()
        for k in range(1, N_DEV):
            for a in range(n):
                arrival(a, k).wait_recv()
        for cp in sends:
            cp.wait_send()
        for cp in own:
            cp.wait()

    any_spec = pl.BlockSpec(memory_space=pl.ANY)
    out_shape = []
    for arr, kind in zip(arrays, kinds):
        shape = arr.shape if kind == "gather" else arr.shape[1:]
        out_shape.append(jax.ShapeDtypeStruct((N_DEV,) + tuple(shape), arr.dtype))
    return pl.pallas_call(
        body, name=name, in_specs=[any_spec] * n, out_specs=[any_spec] * n, out_shape=out_shape,
        scratch_shapes=[pltpu.SemaphoreType.DMA((n, N_DEV - 1)), pltpu.SemaphoreType.DMA((n, N_DEV - 1)),
                        pltpu.SemaphoreType.DMA((n,))],
    )(*arrays)


def reduce_adam(parts, w, m, v, *, name):
    r, c = w.shape
    tr = _pick(r, (256, 352, 192, 128, 72, 64, 32, 16, 8))

    def body(p_ref, w_ref, m_ref, v_ref, g_ref, d_ref, m2_ref, v2_ref):
        g = p_ref[0].astype(F32)
        for i in range(1, N_DEV):
            g = g + p_ref[i].astype(F32)
        wv = w_ref[...]
        m2 = B1 * m_ref[...] + (1.0 - B1) * g
        v2 = B2 * v_ref[...] + (1.0 - B2) * (g * g)
        m_hat = m2 / (1.0 - B1 ** STEP)
        v_hat = v2 / (1.0 - B2 ** STEP)
        g_ref[...] = g
        d_ref[...] = -LR * (m_hat / (jnp.sqrt(v_hat) + ADAM_EPS) + WD * wv)
        m2_ref[...] = m2
        v2_ref[...] = v2

    blk = pl.BlockSpec((tr, c), lambda i: (i, 0))
    return pl.pallas_call(
        body, name=name, grid=(r // tr,),
        in_specs=[pl.BlockSpec((N_DEV, tr, c), lambda i: (0, i, 0)), blk, blk, blk],
        out_specs=[blk] * 4, out_shape=[jax.ShapeDtypeStruct((r, c), F32)] * 4,
        compiler_params=_params(("parallel",)))(parts, w, m, v)


BIG = ("w_in", "w_a", "w_pool_grp", "w_b", "w_o", "w_up", "w_down")
SHARDED_SMALL = ("meta_tokens", "w_gk", "conv_w")
REPLICATED = ("norm1_g", "b_gk", "gla_norm_g", "pool_scale", "b_gates", "norm2_g", "conv_b", "final_norm_g")
CUT_AXIS = {"w_in": 2, "w_a": 1, "w_pool_grp": 2, "w_b": 1, "w_o": 1, "w_up": 2, "w_down": 1,
            "meta_tokens": 1, "w_gk": 2, "conv_w": 2}
WEIGHTS = ("meta_tokens", "norm1_g", "w_in", "w_gk", "b_gk", "gla_norm_g", "w_a", "w_pool_grp", "pool_scale",
           "w_b", "b_gates", "w_o", "norm2_g", "w_up", "conv_w", "conv_b", "w_down", "final_norm_g")


def _as_2d(a):
    return a.reshape(-1, a.shape[-1])


def _from_slots(slots, axis):
    full = jnp.moveaxis(slots, 0, axis)
    shape = list(full.shape)
    shape[axis:axis + 2] = [shape[axis] * shape[axis + 1]]
    return full.reshape(shape)


def _to_slots(full, axis):
    shape = list(full.shape)
    shape[axis:axis + 1] = [N_DEV, shape[axis] // N_DEV]
    return jnp.moveaxis(full.reshape(shape), axis, 0)


def _pack(vectors, rows):
    flat = jnp.concatenate([v.reshape(-1).astype(F32) for v in vectors])
    return jnp.pad(flat, (0, rows * LANES - flat.shape[0])).reshape(rows, LANES)


def _unpack(packed, shapes):
    flat = packed.reshape(-1)
    out, off = [], 0
    for s in shapes:
        size = 1
        for d in s:
            size *= d
        out.append(flat[off:off + size].reshape(s))
        off += size
    return out


def _rows_for(shapes, mult=8):
    total = 0
    for s in shapes:
        size = 1
        for d in s:
            size *= d
        total += size
    rows = -(-total // LANES)
    return -(-rows // mult) * mult


def _permute_in(w_in):
    pad = jnp.zeros(w_in.shape[:-1] + (IN_R - IN_WIDTH,), w_in.dtype)
    return jnp.concatenate([w_in[..., :2048], w_in[..., 2064:], w_in[..., 2048:2064], pad], axis=-1)


def _unpermute_in(w_r):
    return jnp.concatenate([w_r[..., :2048], w_r[..., C_GLR:C_GLR + RANK], w_r[..., 2048:C_GLR]], axis=-1)


def kernel(x, meta_tokens, norm1_g, w_in, w_gk, b_gk, gla_norm_g, w_a, w_pool_grp, pool_scale, w_b, b_gates, w_o, norm2_g, w_up, conv_w, conv_b, w_down, final_norm_g, loss_target, m_meta_tokens, m_norm1_g, m_w_in, m_w_gk, m_b_gk, m_gla_norm_g, m_w_a, m_w_pool_grp, m_pool_scale, m_w_b, m_b_gates, m_w_o, m_norm2_g, m_w_up, m_conv_w, m_conv_b, m_w_down, m_final_norm_g, v_meta_tokens, v_norm1_g, v_w_in, v_w_gk, v_b_gk, v_gla_norm_g, v_w_a, v_w_pool_grp, v_pool_scale, v_w_b, v_b_gates, v_w_o, v_norm2_g, v_w_up, v_conv_w, v_conv_b, v_w_down, v_final_norm_g):
    wts = dict(meta_tokens=meta_tokens, norm1_g=norm1_g, w_in=w_in, w_gk=w_gk, b_gk=b_gk, gla_norm_g=gla_norm_g,
               w_a=w_a, w_pool_grp=w_pool_grp, pool_scale=pool_scale, w_b=w_b, b_gates=b_gates, w_o=w_o,
               norm2_g=norm2_g, w_up=w_up, conv_w=conv_w, conv_b=conv_b, w_down=w_down, final_norm_g=final_norm_g)
    mom = dict(meta_tokens=m_meta_tokens, norm1_g=m_norm1_g, w_in=m_w_in, w_gk=m_w_gk, b_gk=m_b_gk,
               gla_norm_g=m_gla_norm_g, w_a=m_w_a, w_pool_grp=m_w_pool_grp, pool_scale=m_pool_scale, w_b=m_w_b,
               b_gates=m_b_gates, w_o=m_w_o, norm2_g=m_norm2_g, w_up=m_w_up, conv_w=m_conv_w, conv_b=m_conv_b,
               w_down=m_w_down, final_norm_g=m_final_norm_g)
    var = dict(meta_tokens=v_meta_tokens, norm1_g=v_norm1_g, w_in=v_w_in, w_gk=v_w_gk, b_gk=v_b_gk,
               gla_norm_g=v_gla_norm_g, w_a=v_w_a, w_pool_grp=v_w_pool_grp, pool_scale=v_pool_scale, w_b=v_w_b,
               b_gates=v_b_gates, w_o=v_w_o, norm2_g=v_norm2_g, w_up=v_w_up, conv_w=v_conv_w, conv_b=v_conv_b,
               w_down=v_w_down, final_norm_g=v_final_norm_g)

    small_shapes = [wts[n].shape for n in SHARDED_SMALL]
    small_rows = _rows_for(small_shapes)
    gathered = exchange(
        [_as_2d(wts[n].astype(BF16)) for n in BIG] + [_pack([wts[n] for n in SHARDED_SMALL], small_rows)],
        ["gather"] * (len(BIG) + 1), name="gather_weights")
    full = {}
    for n, slots in zip(BIG, gathered):
        full[n] = _from_slots(slots.reshape((N_DEV,) + wts[n].shape), CUT_AXIS[n])
    small_slots = [jnp.stack(parts) for parts in zip(*[_unpack(gathered[-1][i], small_shapes) for i in range(N_DEV)])]
    for n, slots in zip(SHARDED_SMALL, small_slots):
        full[n] = _from_slots(slots, CUT_AXIS[n])
    w_gk_pad = jnp.pad(full["w_gk"], ((0, 0), (0, LANES - RANK), (0, 0))).astype(BF16)
    w = dict(meta=full["meta_tokens"], norm1_g=norm1_g, w_in=_permute_in(full["w_in"]), w_gk=w_gk_pad, b_gk=b_gk,
             gla_norm_g=gla_norm_g, w_a=full["w_a"], w_pool=full["w_pool_grp"], pool_scale=pool_scale,
             w_b=full["w_b"], b_gates=b_gates, w_o=full["w_o"], norm2_g=norm2_g, w_up=full["w_up"],
             conv_w=full["conv_w"], conv_b=conv_b, w_down=full["w_down"], final_norm_g=final_norm_g[None])

    loss_rows, grad_x, g = local_step(x[0], loss_target[0], w)
    loss_part = 0.5 * jnp.sum(loss_rows) / D
    g_full = dict(g)
    g_full["w_in"] = _unpermute_in(g["w_in"])
    g_full["w_pool_grp"] = g["w_pool"]
    g_full["meta_tokens"] = g["meta"]
    g_full["w_gk"] = g["w_gk"][:, :RANK]
    for n in ("norm1_g", "b_gk", "gla_norm_g", "pool_scale", "b_gates", "norm2_g", "conv_b"):
        g_full[n] = g[n][:, 0]
    g_full["final_norm_g"] = g["final_norm_g"][0]

    rep_shapes = [wts[n].shape for n in REPLICATED] + [(1,)]
    rep_rows = _rows_for(rep_shapes)
    big_blocks = [_to_slots(g_full[n], CUT_AXIS[n]).astype(BF16) for n in BIG]
    big_blocks = [b.reshape(N_DEV, -1, b.shape[-1]) for b in big_blocks]
    small_blocks = jnp.stack([
        _pack([_to_slots(g_full[n], CUT_AXIS[n])[i] for n in SHARDED_SMALL], small_rows) for i in range(N_DEV)])
    rep_pack = _pack([g_full[n] for n in REPLICATED] + [loss_part.reshape(1)], rep_rows)
    received = exchange(big_blocks + [small_blocks, rep_pack],
                        ["scatter"] * (len(BIG) + 1) + ["gather"], name="exchange_grads")

    grads, delta, new_m, new_v = {}, {}, {}, {}
    for n, parts in zip(BIG, received):
        outs = reduce_adam(parts, _as_2d(wts[n]), _as_2d(mom[n]), _as_2d(var[n]), name="adam_" + n)
        grads[n], delta[n], new_m[n], new_v[n] = [o.reshape(wts[n].shape) for o in outs]
    outs = reduce_adam(received[-2], _pack([wts[n] for n in SHARDED_SMALL], small_rows),
                       _pack([mom[n] for n in SHARDED_SMALL], small_rows),
                       _pack([var[n] for n in SHARDED_SMALL], small_rows), name="adam_small")
    for d, o in zip((grads, delta, new_m, new_v), outs):
        for n, a in zip(SHARDED_SMALL, _unpack(o, small_shapes)):
            d[n] = a
    one = [jnp.zeros((1,), F32)]
    outs = reduce_adam(received[-1], _pack([wts[n] for n in REPLICATED] + one, rep_rows),
                       _pack([mom[n] for n in REPLICATED] + one, rep_rows),
                       _pack([var[n] for n in REPLICATED] + one, rep_rows), name="adam_replicated")
    for d, o in zip((grads, delta, new_m, new_v), outs):
        for n, a in zip(REPLICATED + ("loss",), _unpack(o, rep_shapes)):
            d[n] = a
    loss = grads["loss"][0]
    return (loss, grad_x[None], *[grads[n] for n in WEIGHTS], *[delta[n] for n in WEIGHTS],
            *[new_m[n] for n in WEIGHTS], *[new_v[n] for n in WEIGHTS])
```

```python
import functools

import jax
import jax.numpy as jnp
from jax import lax
from jax.experimental import pallas as pl
from jax.experimental.pallas import tpu as pltpu

F32 = jnp.float32
BF16 = jnp.bfloat16

D = 1024
DEPTH = 2
N_META = 16
HEADS = 4
DK = 512
DV = 1024
HK = 128
HV = 256
RANK = 16
TAU = 16.0
CHUNK = 64
POOL_WINDOWS = (2, 4, 8, 16)
GROUPS = 4
GDIM = 256
D_FF = 2816
F2 = 2 * D_FF
EPS = 1e-6
IN_WIDTH = 6160
LR, B1, B2, ADAM_EPS, WD, STEP = 0.001, 0.9, 0.999, 1e-8, 0.01, 10

N_DEV = 8
PAD = CHUNK - N_META
X0 = CHUNK
IN_R = 6272
C_Q, C_K, C_V, C_R, C_U, C_GA, C_GB, C_GLR = 0, 512, 1024, 2048, 3072, 4096, 5120, 6144
VMEM_LIMIT = 56 * 1024 * 1024
LANES = 128


def _params(sem=None):
    return pltpu.CompilerParams(dimension_semantics=sem, vmem_limit_bytes=VMEM_LIMIT)


def _pick(n, prefs):
    for t in prefs:
        if n % t == 0:
            return t
    raise ValueError(f"no tile for {n} in {prefs}")


def _row_tile(lp):
    return _pick(lp, (688, 192, 128, 64))


def _ew_tile(lp):
    return _pick(lp, (192, 128, 64))


def _sigmoid(x):
    return 1.0 / (1.0 + jnp.exp(-x))


def _dot(a, b, dims):
    return lax.dot_general(a, b, (dims, ((), ())), preferred_element_type=F32)


def _nn(a, b):
    return _dot(a, b, ((1,), (0,)))


def _nt(a, b):
    return _dot(a, b, ((1,), (1,)))


def _tn(a, b):
    return _dot(a, b, ((0,), (0,)))


def mm_nn(a, b, *, out_dtype=BF16, tn=None, res=None, name):
    m, k = a.shape
    n = b.shape[1]
    tm = _row_tile(m)
    tn = tn or n
    has_res = res is not None

    def body(*refs):
        if has_res:
            a_ref, b_ref, r_ref, o_ref = refs
        else:
            a_ref, b_ref, o_ref = refs
        acc = _nn(a_ref[...], b_ref[...])
        if has_res:
            row = pl.program_id(1) * tm + lax.broadcasted_iota(jnp.int32, (tm, 1), 0)
            acc = jnp.where(row >= PAD, acc + r_ref[...], 0.0)
        o_ref[...] = acc.astype(o_ref.dtype)

    in_specs = [pl.BlockSpec((tm, k), lambda j, i: (i, 0)),
                pl.BlockSpec((k, tn), lambda j, i: (0, j))]
    args = [a, b]
    if has_res:
        in_specs.append(pl.BlockSpec((tm, tn), lambda j, i: (i, j)))
        args.append(res)
    return pl.pallas_call(
        body, name=name, grid=(n // tn, m // tm), in_specs=in_specs,
        out_specs=pl.BlockSpec((tm, tn), lambda j, i: (i, j)),
        out_shape=jax.ShapeDtypeStruct((m, n), out_dtype),
        compiler_params=_params(("parallel", "parallel")))(*args)


def mm_nt(a, b, *, out_dtype=BF16, tn=None, tk=None, name):
    m, k = a.shape
    n = b.shape[0]
    tm = _row_tile(m)
    tn = tn or n
    tk = tk or k
    nk = k // tk

    def body(a_ref, b_ref, o_ref, acc_ref):
        kk = pl.program_id(2)
        part = _nt(a_ref[...], b_ref[...])

        @pl.when(kk == 0)
        def _():
            acc_ref[...] = part

        @pl.when(kk > 0)
        def _():
            acc_ref[...] += part

        @pl.when(kk == nk - 1)
        def _():
            o_ref[...] = acc_ref[...].astype(o_ref.dtype)

    return pl.pallas_call(
        body, name=name, grid=(n // tn, m // tm, nk),
        in_specs=[pl.BlockSpec((tm, tk), lambda j, i, kk: (i, kk)),
                  pl.BlockSpec((tn, tk), lambda j, i, kk: (j, kk))],
        out_specs=pl.BlockSpec((tm, tn), lambda j, i, kk: (i, j)),
        out_shape=jax.ShapeDtypeStruct((m, n), out_dtype),
        scratch_shapes=[pltpu.VMEM((tm, tn), F32)],
        compiler_params=_params(("parallel", "parallel", "arbitrary")))(a, b)


def mm_tn(a, b, *, tk1=None, tn=None, out_dtype=F32, name):
    m, k1 = a.shape
    n = b.shape[1]
    tm = _row_tile(m)
    tk1 = tk1 or k1
    tn = tn or n
    nm = m // tm

    def body(a_ref, b_ref, o_ref, acc_ref):
        i = pl.program_id(2)
        part = _tn(a_ref[...], b_ref[...])

        @pl.when(i == 0)
        def _():
            acc_ref[...] = part

        @pl.when(i > 0)
        def _():
            acc_ref[...] += part

        @pl.when(i == nm - 1)
        def _():
            o_ref[...] = acc_ref[...].astype(o_ref.dtype)

    return pl.pallas_call(
        body, name=name, grid=(k1 // tk1, n // tn, nm),
        in_specs=[pl.BlockSpec((tm, tk1), lambda p, j, i: (i, p)),
                  pl.BlockSpec((tm, tn), lambda p, j, i: (i, j))],
        out_specs=pl.BlockSpec((tk1, tn), lambda p, j, i: (p, j)),
        out_shape=jax.ShapeDtypeStruct((k1, n), out_dtype),
        scratch_shapes=[pltpu.VMEM((tk1, tn), F32)],
        compiler_params=_params(("parallel", "parallel", "arbitrary")))(a, b)


def pool_mm_fwd(pooled, wp, scale, *, name):
    m = pooled.shape[0]
    tm = _row_tile(m)

    def body(a_ref, w_ref, s_ref, y0_ref, y1_ref):
        acc = _nn(a_ref[...], w_ref[...])
        y0_ref[...] = acc.astype(BF16)
        y1_ref[...] = (acc * s_ref[...]).astype(BF16)

    blk = pl.BlockSpec((tm, GDIM), lambda g, i: (i, g))
    return pl.pallas_call(
        body, name=name, grid=(GROUPS, m // tm),
        in_specs=[blk, pl.BlockSpec((None, GDIM, GDIM), lambda g, i: (g, 0, 0)),
                  pl.BlockSpec((1, GDIM), lambda g, i: (0, g))],
        out_specs=[blk, blk],
        out_shape=[jax.ShapeDtypeStruct((m, D), BF16)] * 2,
        compiler_params=_params(("parallel", "parallel")))(pooled, wp, scale)


def pool_mm_bwd_x(dy0, wp, *, name):
    m = dy0.shape[0]
    tm = _row_tile(m)

    def body(a_ref, w_ref, o_ref):
        o_ref[...] = _nt(a_ref[...], w_ref[...]).astype(BF16)

    blk = pl.BlockSpec((tm, GDIM), lambda g, i: (i, g))
    return pl.pallas_call(
        body, name=name, grid=(GROUPS, m // tm),
        in_specs=[blk, pl.BlockSpec((None, GDIM, GDIM), lambda g, i: (g, 0, 0))],
        out_specs=blk, out_shape=jax.ShapeDtypeStruct((m, D), BF16),
        compiler_params=_params(("parallel", "parallel")))(dy0, wp)


def pool_mm_bwd_w(pooled, dy0, *, name):
    m = pooled.shape[0]
    tm = _row_tile(m)

    def body(a_ref, b_ref, o_ref):
        part = _tn(a_ref[...], b_ref[...])

        @pl.when(pl.program_id(1) == 0)
        def _():
            o_ref[...] = part

        @pl.when(pl.program_id(1) > 0)
        def _():
            o_ref[...] += part

    blk = pl.BlockSpec((tm, GDIM), lambda g, i: (i, g))
    return pl.pallas_call(
        body, name=name, grid=(GROUPS, m // tm), in_specs=[blk, blk],
        out_specs=pl.BlockSpec((None, GDIM, GDIM), lambda g, i: (g, 0, 0)),
        out_shape=jax.ShapeDtypeStruct((GROUPS, GDIM, GDIM), F32),
        compiler_params=_params(("parallel", "arbitrary")))(pooled, dy0)


def rmsnorm_fwd(x, g, *, name):
    m = x.shape[0]
    tm = _ew_tile(m)

    def body(x_ref, g_ref, o_ref):
        xv = x_ref[...]
        r = lax.rsqrt(jnp.mean(xv * xv, axis=-1, keepdims=True) + EPS)
        o_ref[...] = (xv * r * g_ref[...]).astype(BF16)

    return pl.pallas_call(
        body, name=name, grid=(m // tm,),
        in_specs=[pl.BlockSpec((tm, D), lambda i: (i, 0)), pl.BlockSpec((1, D), lambda i: (0, 0))],
        out_specs=pl.BlockSpec((tm, D), lambda i: (i, 0)),
        out_shape=jax.ShapeDtypeStruct((m, D), BF16),
        compiler_params=_params(("parallel",)))(x, g)


def rmsnorm_bwd(dy, x, g, dres, *, name):
    m = x.shape[0]
    tm = _ew_tile(m)

    def body(dy_ref, x_ref, g_ref, r_ref, dx_ref, dg_ref):
        i = pl.program_id(0)
        xv = x_ref[...]
        dyv = dy_ref[...].astype(F32)
        r = lax.rsqrt(jnp.mean(xv * xv, axis=-1, keepdims=True) + EPS)
        xh = xv * r
        dxh = dyv * g_ref[...]
        dx = r * (dxh - xh * jnp.mean(dxh * xh, axis=-1, keepdims=True))
        row = i * tm + lax.broadcasted_iota(jnp.int32, (tm, 1), 0)
        dx_ref[...] = jnp.where(row >= PAD, dx + r_ref[...], 0.0)

        @pl.when(i == 0)
        def _():
            dg_ref[...] = jnp.zeros_like(dg_ref)

        dg_ref[...] += jnp.sum(dyv * xh, axis=0, keepdims=True)

    blk = pl.BlockSpec((tm, D), lambda i: (i, 0))
    vec = pl.BlockSpec((1, D), lambda i: (0, 0))
    return pl.pallas_call(
        body, name=name, grid=(m // tm,), in_specs=[blk, blk, vec, blk],
        out_specs=[blk, vec],
        out_shape=[jax.ShapeDtypeStruct((m, D), F32), jax.ShapeDtypeStruct((1, D), F32)],
        compiler_params=_params(("arbitrary",)))(dy, x, g, dres)


def loss_head(h, gf, target, *, name):
    m = h.shape[0]
    t = X0
    inv_d = 1.0 / D

    def body(h_ref, g_ref, t_ref, dh_ref, dg_ref, ls_ref):
        i = pl.program_id(0)

        @pl.when(i == 0)
        def _():
            dg_ref[...] = jnp.zeros_like(dg_ref)
            ls_ref[...] = jnp.zeros_like(ls_ref)
            dh_ref[...] = jnp.zeros_like(dh_ref)

        @pl.when(i > 0)
        def _():
            xv = h_ref[...]
            r = lax.rsqrt(jnp.mean(xv * xv, axis=-1, keepdims=True) + EPS)
            xh = xv * r
            err = xh * g_ref[...] - t_ref[...]
            ls_ref[...] += jnp.sum(err * err, axis=0, keepdims=True)
            dy = err * inv_d
            dg_ref[...] += jnp.sum(dy * xh, axis=0, keepdims=True)
            dxh = dy * g_ref[...]
            dh_ref[...] = r * (dxh - xh * jnp.mean(dxh * xh, axis=-1, keepdims=True))

    blk = pl.BlockSpec((t, D), lambda i: (i, 0))
    vec = pl.BlockSpec((1, D), lambda i: (0, 0))
    return pl.pallas_call(
        body, name=name, grid=(m // t,),
        in_specs=[blk, vec, pl.BlockSpec((t, D), lambda i: (jnp.maximum(i - 1, 0), 0))],
        out_specs=[blk, vec, vec],
        out_shape=[jax.ShapeDtypeStruct((m, D), F32), jax.ShapeDtypeStruct((1, D), F32),
                   jax.ShapeDtypeStruct((1, D), F32)],
        compiler_params=_params(("arbitrary",)))(h, gf, target)


def _split3(x):
    x1 = x.astype(BF16)
    r1 = x - x1.astype(F32)
    x2 = r1.astype(BF16)
    x3 = (r1 - x2.astype(F32)).astype(BF16)
    return x1, x2, x3


def _tri_mm(tri, x):
    x1, x2, x3 = _split3(x)
    return _nn(tri, x1) + _nn(tri, x2) + _nn(tri, x3)


def _log_decay(glr, wgk, bgk, row0, rows):
    z = _nn(glr, wgk) + bgk
    la = (jnp.minimum(z, 0.0) - jnp.log(1.0 + jnp.exp(-jnp.abs(z)))) * (1.0 / TAU)
    row = row0 + lax.broadcasted_iota(jnp.int32, (rows, 1), 0)
    return z, jnp.where(row >= PAD, la, 0.0)


def _chunk_group(n_chunks):
    return _pick(n_chunks, (3, 2, 1))


def gla_fwd(p, wgk, bgk, *, name):
    m = p.shape[0]
    n_chunks = m // CHUNK
    cg = _chunk_group(n_chunks)
    t = cg * CHUNK
    scale = HK ** -0.5

    def body(q_ref, k_ref, v_ref, glr_ref, wgk_ref, bgk_ref, o_ref, st_ref, state):
        i = pl.program_id(0)

        @pl.when(i == 0)
        def _():
            state[...] = jnp.zeros_like(state)

        _, la = _log_decay(glr_ref[...], wgk_ref[...], bgk_ref[...], i * t, t)
        ri = lax.broadcasted_iota(jnp.int32, (CHUNK, CHUNK), 0)
        ci = lax.broadcasted_iota(jnp.int32, (CHUNK, CHUNK), 1)
        causal = ri >= ci
        tri = causal.astype(BF16)
        for c in range(cg):
            rows = pl.ds(c * CHUNK, CHUNK)
            b = _tri_mm(tri, la[c * CHUNK:(c + 1) * CHUNK])
            bl = b[CHUNK - 1:CHUNK, :]
            q = q_ref[rows, :].astype(F32) * scale
            k = k_ref[rows, :].astype(F32)
            qd = (q * jnp.exp(b)).astype(BF16)
            ki = (k * jnp.exp(-b)).astype(BF16)
            ke = (k * jnp.exp(bl - b)).astype(BF16)
            dec = jnp.exp(bl)
            for h in range(HEADS):
                ks = slice(h * HK, (h + 1) * HK)
                vs = pl.ds(h * HV, HV)
                vh = v_ref[rows, vs]
                s_t = state[h]
                st_ref[c, h] = s_t
                att = jnp.where(causal, _nt(qd[:, ks], ki[:, ks]), 0.0).astype(BF16)
                o_ref[rows, vs] = _nn(att, vh) + _nt(qd[:, ks], s_t.astype(BF16))
                state[h] = s_t * dec[:, ks] + _tn(vh, ke[:, ks])

    return pl.pallas_call(
        body, name=name, grid=(n_chunks // cg,),
        in_specs=[pl.BlockSpec((t, DK), lambda i: (i, C_Q // DK)),
                  pl.BlockSpec((t, DK), lambda i: (i, C_K // DK)),
                  pl.BlockSpec((t, DV), lambda i: (i, C_V // DV)),
                  pl.BlockSpec((t, LANES), lambda i: (i, C_GLR // LANES)),
                  pl.BlockSpec((LANES, DK), lambda i: (0, 0)),
                  pl.BlockSpec((1, DK), lambda i: (0, 0))],
        out_specs=[pl.BlockSpec((t, DV), lambda i: (i, 0)),
                   pl.BlockSpec((cg, HEADS, HV, HK), lambda i: (i, 0, 0, 0))],
        out_shape=[jax.ShapeDtypeStruct((m, DV), F32),
                   jax.ShapeDtypeStruct((n_chunks, HEADS, HV, HK), F32)],
        scratch_shapes=[pltpu.VMEM((HEADS, HV, HK), F32)],
        compiler_params=_params(("arbitrary",)))(p, p, p, p, wgk, bgk)


def gla_bwd(p, wgk, bgk, st, do, *, name):
    m = p.shape[0]
    n_chunks = m // CHUNK
    cg = _chunk_group(n_chunks)
    t = cg * CHUNK
    ns = n_chunks // cg
    scale = HK ** -0.5

    def body(q_ref, k_ref, v_ref, glr_ref, wgk_ref, bgk_ref, st_ref, do_ref,
             dqkv_ref, dglr_ref, dwgk_ref, dbgk_ref, dstate, dz_buf):
        i = pl.program_id(0)
        blk = ns - 1 - i

        @pl.when(i == 0)
        def _():
            dstate[...] = jnp.zeros_like(dstate)
            dwgk_ref[...] = jnp.zeros_like(dwgk_ref)
            dbgk_ref[...] = jnp.zeros_like(dbgk_ref)

        z, la = _log_decay(glr_ref[...], wgk_ref[...], bgk_ref[...], blk * t, t)
        ri = lax.broadcasted_iota(jnp.int32, (CHUNK, CHUNK), 0)
        ci = lax.broadcasted_iota(jnp.int32, (CHUNK, CHUNK), 1)
        causal = ri >= ci
        tri = causal.astype(BF16)
        tri_u = (ri <= ci).astype(BF16)
        for c in reversed(range(cg)):
            rows = pl.ds(c * CHUNK, CHUNK)
            b = _tri_mm(tri, la[c * CHUNK:(c + 1) * CHUNK])
            bl = b[CHUNK - 1:CHUNK, :]
            eb = jnp.exp(b)
            enb = jnp.exp(-b)
            ebl = jnp.exp(bl - b)
            dec = jnp.exp(bl)
            q = q_ref[rows, :].astype(F32) * scale
            k = k_ref[rows, :].astype(F32)
            qd32 = q * eb
            ki32 = k * enb
            ke32 = k * ebl
            qd = qd32.astype(BF16)
            ki = ki32.astype(BF16)
            ke = ke32.astype(BF16)
            dqd_parts, dki_parts, dke_parts, ddec_parts = [], [], [], []
            for h in range(HEADS):
                ks = slice(h * HK, (h + 1) * HK)
                vs = pl.ds(h * HV, HV)
                vh = v_ref[rows, vs]
                doh = do_ref[rows, vs].astype(BF16)
                s_t = st_ref[c, h]
                ds_t = dstate[h]
                ds_b = ds_t.astype(BF16)
                att = jnp.where(causal, _nt(qd[:, ks], ki[:, ks]), 0.0).astype(BF16)
                datt = jnp.where(causal, _nt(doh, vh), 0.0).astype(BF16)
                dvh = _tn(att, doh) + _nt(ke[:, ks], ds_b)
                dqkv_ref[rows, pl.ds(2 * DK + h * HV, HV)] = dvh.astype(BF16)
                dqd_parts.append(_nn(datt, ki[:, ks]) + _nn(doh, s_t.astype(BF16)))
                dki_parts.append(_tn(datt, qd[:, ks]))
                dke_parts.append(_nn(vh, ds_b))
                ddec_parts.append(jnp.sum(s_t * ds_t, axis=0, keepdims=True))
                dstate[h] = _tn(doh, qd[:, ks]) + ds_t * dec[:, ks]
            dqd = jnp.concatenate(dqd_parts, axis=1)
            dki = jnp.concatenate(dki_parts, axis=1)
            dke = jnp.concatenate(dke_parts, axis=1)
            ddec = jnp.concatenate(ddec_parts, axis=1)
            dqkv_ref[rows, pl.ds(0, DK)] = (dqd * eb * scale).astype(BF16)
            dqkv_ref[rows, pl.ds(DK, DK)] = (dki * enb + dke * ebl).astype(BF16)
            dke_ke = dke * ke32
            db = dqd * qd32 - dki * ki32 - dke_ke
            dbl = jnp.sum(dke_ke, axis=0, keepdims=True) + ddec * dec
            dg = _tri_mm(tri_u, db) + dbl
            row = blk * t + c * CHUNK + lax.broadcasted_iota(jnp.int32, (CHUNK, 1), 0)
            zc = z[c * CHUNK:(c + 1) * CHUNK]
            dz = jnp.where(row >= PAD, dg * (1.0 / TAU) * _sigmoid(-zc), 0.0)
            dz_buf[rows, :] = dz
        dz_all = dz_buf[...]
        dz_b = dz_all.astype(BF16)
        dbgk_ref[...] += jnp.sum(dz_all, axis=0, keepdims=True)
        dglr_ref[...] = _nt(dz_b, wgk_ref[...]).astype(BF16)
        dwgk_ref[...] += _tn(glr_ref[...], dz_b)

    rev = lambda i: ns - 1 - i
    return pl.pallas_call(
        body, name=name, grid=(ns,),
        in_specs=[pl.BlockSpec((t, DK), lambda i: (rev(i), C_Q // DK)),
                  pl.BlockSpec((t, DK), lambda i: (rev(i), C_K // DK)),
                  pl.BlockSpec((t, DV), lambda i: (rev(i), C_V // DV)),
                  pl.BlockSpec((t, LANES), lambda i: (rev(i), C_GLR // LANES)),
                  pl.BlockSpec((LANES, DK), lambda i: (0, 0)),
                  pl.BlockSpec((1, DK), lambda i: (0, 0)),
                  pl.BlockSpec((cg, HEADS, HV, HK), lambda i: (rev(i), 0, 0, 0)),
                  pl.BlockSpec((t, DV), lambda i: (rev(i), 0))],
        out_specs=[pl.BlockSpec((t, 2 * DK + DV), lambda i: (rev(i), 0)),
                   pl.BlockSpec((t, LANES), lambda i: (rev(i), 0)),
                   pl.BlockSpec((LANES, DK), lambda i: (0, 0)),
                   pl.BlockSpec((1, DK), lambda i: (0, 0))],
        out_shape=[jax.ShapeDtypeStruct((m, 2 * DK + DV), BF16),
                   jax.ShapeDtypeStruct((m, LANES), BF16),
                   jax.ShapeDtypeStruct((LANES, DK), F32),
                   jax.ShapeDtypeStruct((1, DK), F32)],
        scratch_shapes=[pltpu.VMEM((HEADS, HV, HK), F32), pltpu.VMEM((t, DK), F32)],
        compiler_params=_params(("arbitrary",)))(p, p, p, p, wgk, bgk, st, do)


HALO = 16


def _shift_down(xx, s):
    return pltpu.roll(xx, s, 0)


def _shift_up(xx, s):
    return pltpu.roll(xx, xx.shape[0] - s, 0)


def mix_pre(o, p, gn, *, name):
    m = o.shape[0]
    tm = _ew_tile(m)

    def body(o_ref, r_ref, u_ref, gn_ref, ya_ref, pooled_ref, halo):
        i = pl.program_id(0)

        @pl.when(i == 0)
        def _():
            halo[...] = jnp.zeros_like(halo)

        rv = r_ref[...].astype(F32)
        silu_r = rv * _sigmoid(rv)
        for h in range(HEADS):
            cs = pl.ds(h * HV, HV)
            ov = o_ref[:, cs]
            rs = lax.rsqrt(jnp.mean(ov * ov, axis=-1, keepdims=True) + EPS)
            ya_ref[:, cs] = (ov * rs * gn_ref[...] * silu_r[:, h * HV:(h + 1) * HV]).astype(BF16)

        row = i * tm + lax.broadcasted_iota(jnp.int32, (tm, 1), 0)
        pos1 = jnp.maximum(row - PAD + 1, 1).astype(F32)
        for g, w in enumerate(POOL_WINDOWS):
            cs = pl.ds(g * GDIM, GDIM)
            uv = u_ref[:, cs].astype(F32)
            xx = jnp.concatenate([halo[:, cs], uv], axis=0)
            s = xx
            span = 1
            while span < w:
                s = s + _shift_down(s, span)
                span *= 2
            inv = 1.0 / jnp.minimum(pos1, float(w))
            pooled_ref[:, cs] = (s[HALO:] * inv - uv).astype(BF16)
            halo[:, cs] = uv[tm - HALO:]

    blk = pl.BlockSpec((tm, D), lambda i: (i, 0))
    return pl.pallas_call(
        body, name=name, grid=(m // tm,),
        in_specs=[blk, pl.BlockSpec((tm, D), lambda i: (i, C_R // D)),
                  pl.BlockSpec((tm, D), lambda i: (i, C_U // D)),
                  pl.BlockSpec((1, HV), lambda i: (0, 0))],
        out_specs=[blk, blk],
        out_shape=[jax.ShapeDtypeStruct((m, D), BF16)] * 2,
        scratch_shapes=[pltpu.VMEM((HALO, D), F32)],
        compiler_params=_params(("arbitrary",)))(o, p, p, gn)


def mix_pre_bwd(dya, dpooled, o, p, gn, *, name):
    m = o.shape[0]
    tm = _ew_tile(m)
    nt = m // tm

    def body(dya_ref, dpl_ref, o_ref, r_ref, gn_ref, do_ref, dr_ref, du_ref, dgn_ref, halo):
        i = pl.program_id(0)
        blk_i = nt - 1 - i

        @pl.when(i == 0)
        def _():
            halo[...] = jnp.zeros_like(halo)
            dgn_ref[...] = jnp.zeros_like(dgn_ref)

        rv = r_ref[...].astype(F32)
        sg = _sigmoid(rv)
        silu_r = rv * sg
        dsilu = sg * (1.0 + rv * (1.0 - sg))
        dgn = jnp.zeros((1, HV), F32)
        for h in range(HEADS):
            cs = pl.ds(h * HV, HV)
            hs = slice(h * HV, (h + 1) * HV)
            ov = o_ref[:, cs]
            dy = dya_ref[:, cs].astype(F32)
            rs = lax.rsqrt(jnp.mean(ov * ov, axis=-1, keepdims=True) + EPS)
            xh = ov * rs
            on = xh * gn_ref[...]
            don = dy * silu_r[:, hs]
            dr_ref[:, cs] = (dy * on * dsilu[:, hs]).astype(BF16)
            dxh = don * gn_ref[...]
            do_ref[:, cs] = rs * (dxh - xh * jnp.mean(dxh * xh, axis=-1, keepdims=True))
            dgn = dgn + jnp.sum(don * xh, axis=0, keepdims=True)
        dgn_ref[...] += dgn

        row = blk_i * tm + lax.broadcasted_iota(jnp.int32, (tm, 1), 0)
        pos1 = jnp.maximum(row - PAD + 1, 1).astype(F32)
        for g, w in enumerate(POOL_WINDOWS):
            cs = pl.ds(g * GDIM, GDIM)
            dpv = dpl_ref[:, cs].astype(F32)
            e = dpv * (1.0 / jnp.minimum(pos1, float(w)))
            xx = jnp.concatenate([e, halo[:, cs]], axis=0)
            s = xx
            span = 1
            while span < w:
                s = s + _shift_up(s, span)
                span *= 2
            du_ref[:, cs] = (s[:tm] - dpv).astype(BF16)
            halo[:, cs] = e[:HALO]

    rev = lambda i: nt - 1 - i
    blk = pl.BlockSpec((tm, D), lambda i: (rev(i), 0))
    return pl.pallas_call(
        body, name=name, grid=(nt,),
        in_specs=[blk, blk, blk, pl.BlockSpec((tm, D), lambda i: (rev(i), C_R // D)),
                  pl.BlockSpec((1, HV), lambda i: (0, 0))],
        out_specs=[blk, blk, blk, pl.BlockSpec((1, HV), lambda i: (0, 0))],
        out_shape=[jax.ShapeDtypeStruct((m, D), F32), jax.ShapeDtypeStruct((m, D), BF16),
                   jax.ShapeDtypeStruct((m, D), BF16), jax.ShapeDtypeStruct((1, HV), F32)],
        scratch_shapes=[pltpu.VMEM((HALO, D), F32)],
        compiler_params=_params(("arbitrary",)))(dya, dpooled, o, p, gn)


def merge_fwd(p, ya, yb, bg, *, name):
    m = ya.shape[0]
    tm = _ew_tile(m)

    def body(ga_ref, gb_ref, ya_ref, yb_ref, ba_ref, bb_ref, o_ref):
        gate_a = _sigmoid(ga_ref[...].astype(F32) + ba_ref[...])
        gate_b = _sigmoid(gb_ref[...].astype(F32) + bb_ref[...])
        o_ref[...] = (gate_a * ya_ref[...].astype(F32) + gate_b * yb_ref[...].astype(F32)).astype(BF16)

    blk = pl.BlockSpec((tm, D), lambda i: (i, 0))
    return pl.pallas_call(
        body, name=name, grid=(m // tm,),
        in_specs=[pl.BlockSpec((tm, D), lambda i: (i, C_GA // D)),
                  pl.BlockSpec((tm, D), lambda i: (i, C_GB // D)), blk, blk,
                  pl.BlockSpec((1, D), lambda i: (0, 0)), pl.BlockSpec((1, D), lambda i: (0, 1))],
        out_specs=blk, out_shape=jax.ShapeDtypeStruct((m, D), BF16),
        compiler_params=_params(("parallel",)))(p, p, ya, yb, bg, bg)


def merge_bwd(dmrg, p, ya, yb, bg, *, name):
    m = ya.shape[0]
    tm = _ew_tile(m)

    def body(dm_ref, ga_ref, gb_ref, ya_ref, yb_ref, ba_ref, bb_ref,
             dya_ref, dyb_ref, dga_ref, dgb_ref, dbg_ref):
        @pl.when(pl.program_id(0) == 0)
        def _():
            dbg_ref[...] = jnp.zeros_like(dbg_ref)

        dm = dm_ref[...].astype(F32)
        gate_a = _sigmoid(ga_ref[...].astype(F32) + ba_ref[...])
        gate_b = _sigmoid(gb_ref[...].astype(F32) + bb_ref[...])
        dya_ref[...] = (dm * gate_a).astype(BF16)
        dyb_ref[...] = (dm * gate_b).astype(BF16)
        dga = dm * ya_ref[...].astype(F32) * gate_a * (1.0 - gate_a)
        dgb = dm * yb_ref[...].astype(F32) * gate_b * (1.0 - gate_b)
        dga_ref[...] = dga.astype(BF16)
        dgb_ref[...] = dgb.astype(BF16)
        dbg_ref[:, pl.ds(0, D)] += jnp.sum(dga, axis=0, keepdims=True)
        dbg_ref[:, pl.ds(D, D)] += jnp.sum(dgb, axis=0, keepdims=True)

    blk = pl.BlockSpec((tm, D), lambda i: (i, 0))
    return pl.pallas_call(
        body, name=name, grid=(m // tm,),
        in_specs=[blk, pl.BlockSpec((tm, D), lambda i: (i, C_GA // D)),
                  pl.BlockSpec((tm, D), lambda i: (i, C_GB // D)), blk, blk,
                  pl.BlockSpec((1, D), lambda i: (0, 0)), pl.BlockSpec((1, D), lambda i: (0, 1))],
        out_specs=[blk, blk, blk, blk, pl.BlockSpec((1, 2 * D), lambda i: (0, 0))],
        out_shape=[jax.ShapeDtypeStruct((m, D), BF16)] * 4 + [jax.ShapeDtypeStruct((1, 2 * D), F32)],
        compiler_params=_params(("arbitrary",)))(dmrg, p, p, ya, yb, bg, bg)


def scale_bwd(dy1, y0, scale, *, name):
    m = y0.shape[0]
    tm = _ew_tile(m)

    def body(dy_ref, y0_ref, s_ref, o_ref, ds_ref):
        @pl.when(pl.program_id(0) == 0)
        def _():
            ds_ref[...] = jnp.zeros_like(ds_ref)

        dy = dy_ref[...].astype(F32)
        o_ref[...] = (dy * s_ref[...]).astype(BF16)
        ds_ref[...] += jnp.sum(dy * y0_ref[...].astype(F32), axis=0, keepdims=True)

    blk = pl.BlockSpec((tm, D), lambda i: (i, 0))
    vec = pl.BlockSpec((1, D), lambda i: (0, 0))
    return pl.pallas_call(
        body, name=name, grid=(m // tm,), in_specs=[blk, blk, vec], out_specs=[blk, vec],
        out_shape=[jax.ShapeDtypeStruct((m, D), BF16), jax.ShapeDtypeStruct((1, D), F32)],
        compiler_params=_params(("arbitrary",)))(dy1, y0, scale)


CONV_BLK = 1408
N_CONV_BLK = D_FF // CONV_BLK


def conv_act_fwd(up, cw, cb, *, name):
    m = up.shape[0]
    tm = _ew_tile(m)

    def conv(x_ref, halo, w_ref, b_ref):
        xv = x_ref[...].astype(F32)
        xx = jnp.concatenate([halo[...], xv], axis=0)
        y = (w_ref[2:3, :] * xx + w_ref[1:2, :] * _shift_down(xx, 1)
             + w_ref[0:1, :] * _shift_down(xx, 2))[HALO:] + b_ref[...]
        halo[...] = xv[tm - HALO:]
        return y

    def body(xa_ref, xb_ref, wa_ref, wb_ref, ba_ref, bb_ref, upc_a_ref, upc_b_ref, act_ref, halo_a, halo_b):
        @pl.when(pl.program_id(1) == 0)
        def _():
            halo_a[...] = jnp.zeros_like(halo_a)
            halo_b[...] = jnp.zeros_like(halo_b)

        a = conv(xa_ref, halo_a, wa_ref, ba_ref)
        bv = conv(xb_ref, halo_b, wb_ref, bb_ref)
        upc_a_ref[...] = a.astype(BF16)
        upc_b_ref[...] = bv.astype(BF16)
        act_ref[...] = (a * _sigmoid(a) * bv).astype(BF16)

    nb = N_CONV_BLK
    xa = pl.BlockSpec((tm, CONV_BLK), lambda j, i: (i, j))
    xb = pl.BlockSpec((tm, CONV_BLK), lambda j, i: (i, j + nb))
    return pl.pallas_call(
        body, name=name, grid=(nb, m // tm),
        in_specs=[xa, xb,
                  pl.BlockSpec((3, CONV_BLK), lambda j, i: (0, j)),
                  pl.BlockSpec((3, CONV_BLK), lambda j, i: (0, j + nb)),
                  pl.BlockSpec((1, CONV_BLK), lambda j, i: (0, j)),
                  pl.BlockSpec((1, CONV_BLK), lambda j, i: (0, j + nb))],
        out_specs=[xa, xa, xa],
        out_shape=[jax.ShapeDtypeStruct((m, D_FF), BF16)] * 3,
        scratch_shapes=[pltpu.VMEM((HALO, CONV_BLK), F32)] * 2,
        compiler_params=_params(("parallel", "arbitrary")))(up, up, cw, cw, cb, cb)


def conv_act_bwd(dact, upc_a, upc_b, up, cw, *, name):
    m = up.shape[0]
    tm = _ew_tile(m)
    nt = m // tm

    def conv_t(d, halo, x_ref, w_ref, dup_ref, dw_ref, db_ref):
        xx = jnp.concatenate([d, halo[...]], axis=0)
        d1 = _shift_up(xx, 1)[:tm]
        d2 = _shift_up(xx, 2)[:tm]
        dup_ref[...] = (w_ref[2:3, :] * d + w_ref[1:2, :] * d1 + w_ref[0:1, :] * d2).astype(BF16)
        xv = x_ref[...].astype(F32)
        dw_ref[2:3, :] += jnp.sum(xv * d, axis=0, keepdims=True)
        dw_ref[1:2, :] += jnp.sum(xv * d1, axis=0, keepdims=True)
        dw_ref[0:1, :] += jnp.sum(xv * d2, axis=0, keepdims=True)
        db_ref[...] += jnp.sum(d, axis=0, keepdims=True)
        halo[...] = d[:HALO]

    def body(da_ref, a_ref, b_ref, xa_ref, xb_ref, wa_ref, wb_ref,
             dupa_ref, dupb_ref, dwa_ref, dwb_ref, dba_ref, dbb_ref, halo_a, halo_b):
        @pl.when(pl.program_id(1) == 0)
        def _():
            for r in (halo_a, halo_b, dwa_ref, dwb_ref, dba_ref, dbb_ref):
                r[...] = jnp.zeros_like(r)

        dact_v = da_ref[...].astype(F32)
        a = a_ref[...].astype(F32)
        bv = b_ref[...].astype(F32)
        sg = _sigmoid(a)
        d_a = dact_v * bv * sg * (1.0 + a * (1.0 - sg))
        d_b = dact_v * a * sg
        conv_t(d_a, halo_a, xa_ref, wa_ref, dupa_ref, dwa_ref, dba_ref)
        conv_t(d_b, halo_b, xb_ref, wb_ref, dupb_ref, dwb_ref, dbb_ref)

    nb = N_CONV_BLK
    rev = lambda i: nt - 1 - i
    half = pl.BlockSpec((tm, CONV_BLK), lambda j, i: (rev(i), j))
    xa = half
    xb = pl.BlockSpec((tm, CONV_BLK), lambda j, i: (rev(i), j + nb))
    wa = pl.BlockSpec((3, CONV_BLK), lambda j, i: (0, j))
    wb = pl.BlockSpec((3, CONV_BLK), lambda j, i: (0, j + nb))
    va = pl.BlockSpec((1, CONV_BLK), lambda j, i: (0, j))
    outs = pl.pallas_call(
        body, name=name, grid=(nb, nt),
        in_specs=[half, half, half, xa, xb, wa, wb],
        out_specs=[half, half, wa, wa, va, va],
        out_shape=[jax.ShapeDtypeStruct((m, D_FF), BF16)] * 2
                  + [jax.ShapeDtypeStruct((3, D_FF), F32)] * 2
                  + [jax.ShapeDtypeStruct((1, D_FF), F32)] * 2,
        scratch_shapes=[pltpu.VMEM((HALO, CONV_BLK), F32)] * 2,
        compiler_params=_params(("parallel", "arbitrary")))(dact, upc_a, upc_b, up, up, cw, cw)
    return outs


def local_step(x, target, w):
    seq = x.shape[0]
    h = jnp.concatenate([jnp.zeros((PAD, D), F32), w["meta"], x], axis=0)
    saved = []
    for l in range(DEPTH):
        wl = {k: (v[l:l + 1] if k in ROW_PARAMS else v[l]) for k, v in w.items() if k not in ("meta", "final_norm_g")}
        s = {"h": h}
        fwd_in(s, wl, f"l{l}_")
        fwd_mixer(s, wl, f"l{l}_")
        fwd_ffn(s, wl, f"l{l}_")
        saved.append(s)
        h = s["h3"]

    dh, dgf, loss_rows = loss_head(h, w["final_norm_g"], target, name="loss_head")
    g = {"final_norm_g": dgf}
    per_layer = []
    for l in reversed(range(DEPTH)):
        wl = {k: (v[l:l + 1] if k in ROW_PARAMS else v[l]) for k, v in w.items() if k not in ("meta", "final_norm_g")}
        s = saved[l]
        gl = {}
        dh2 = bwd_ffn(dh, dh.astype(BF16), s, wl, gl, f"l{l}_")
        dp = bwd_mixer(dh2, dh2.astype(BF16), s, wl, gl, f"l{l}_")
        gl["w_in"] = bwd_in_w(dp, s, f"l{l}_")
        dh = bwd_in_x(dp, dh2, s, wl, gl, f"l{l}_")
        per_layer.append(gl)
    per_layer.reverse()
    for k in per_layer[0]:
        g[k] = jnp.stack([per_layer[l][k].astype(F32) for l in range(DEPTH)])
    g["meta"] = dh[PAD:X0]
    return loss_rows, dh[X0:X0 + seq], g


ROW_PARAMS = ("norm1_g", "b_gk", "gla_norm_g", "pool_scale", "b_gates", "norm2_g", "conv_b")


def fwd_in(s, w, ln):
    s["hn1"] = rmsnorm_fwd(s["h"], w["norm1_g"], name=ln + "norm1")
    s["p"] = mm_nn(s["hn1"], w["w_in"], tn=896, name=ln + "in_proj")


def fwd_mixer(s, w, ln):
    p = s["p"]
    s["o"], s["st"] = gla_fwd(p, w["w_gk"], w["b_gk"], name=ln + "gla_fwd")
    s["ya_in"], s["pooled"] = mix_pre(s["o"], p, w["gla_norm_g"], name=ln + "mix_pre")
    s["ya"] = mm_nn(s["ya_in"], w["w_a"], name=ln + "proj_a")
    s["yb0"], s["yb1"] = pool_mm_fwd(s["pooled"], w["w_pool"], w["pool_scale"], name=ln + "pool_mm")
    s["yb"] = mm_nn(s["yb1"], w["w_b"], name=ln + "proj_b")
    s["mrg"] = merge_fwd(p, s["ya"], s["yb"], w["b_gates"], name=ln + "merge")
    s["h2"] = mm_nn(s["mrg"], w["w_o"], out_dtype=F32, res=s["h"], name=ln + "proj_o")


def fwd_ffn(s, w, ln):
    s["hn2"] = rmsnorm_fwd(s["h2"], w["norm2_g"], name=ln + "norm2")
    s["up"] = mm_nn(s["hn2"], w["w_up"], tn=1408, name=ln + "up_proj")
    s["upc_a"], s["upc_b"], s["act"] = conv_act_fwd(s["up"], w["conv_w"], w["conv_b"], name=ln + "conv_act")
    s["h3"] = mm_nn(s["act"], w["w_down"], out_dtype=F32, res=s["h2"], name=ln + "down_proj")


def bwd_ffn(dh, dh_b, s, w, g, ln):
    dact = mm_nt(dh_b, w["w_down"], tn=1408, name=ln + "d_act")
    g["w_down"] = mm_tn(s["act"], dh_b, tk1=1408, out_dtype=BF16, name=ln + "dw_down")
    dup_a, dup_b, dcw_a, dcw_b, dcb_a, dcb_b = conv_act_bwd(
        dact, s["upc_a"], s["upc_b"], s["up"], w["conv_w"], name=ln + "conv_act_bwd")
    dup = jnp.concatenate([dup_a, dup_b], axis=1)
    dhn2 = mm_nt(dup, w["w_up"], out_dtype=F32, tk=1408, name=ln + "d_hn2")
    g["w_up"] = mm_tn(s["hn2"], dup, tn=1408, out_dtype=BF16, name=ln + "dw_up")
    dh2, g["norm2_g"] = rmsnorm_bwd(dhn2, s["h2"], w["norm2_g"], dh, name=ln + "norm2_bwd")
    g["conv_w"] = jnp.concatenate([dcw_a, dcw_b], axis=1)
    g["conv_b"] = jnp.concatenate([dcb_a, dcb_b], axis=1)
    return dh2


def bwd_mixer(dh2, dh2_b, s, w, g, ln):
    dmrg = mm_nt(dh2_b, w["w_o"], name=ln + "d_mrg")
    g["w_o"] = mm_tn(s["mrg"], dh2_b, out_dtype=BF16, name=ln + "dw_o")
    dya, dyb, dga, dgb, g["b_gates"] = merge_bwd(dmrg, s["p"], s["ya"], s["yb"], w["b_gates"], name=ln + "merge_bwd")
    dya_in = mm_nt(dya, w["w_a"], name=ln + "d_ya_in")
    g["w_a"] = mm_tn(s["ya_in"], dya, out_dtype=BF16, name=ln + "dw_a")
    dyb1 = mm_nt(dyb, w["w_b"], name=ln + "d_yb1")
    g["w_b"] = mm_tn(s["yb1"], dyb, out_dtype=BF16, name=ln + "dw_b")
    dyb0, g["pool_scale"] = scale_bwd(dyb1, s["yb0"], w["pool_scale"], name=ln + "scale_bwd")
    dpooled = pool_mm_bwd_x(dyb0, w["w_pool"], name=ln + "d_pooled")
    g["w_pool"] = pool_mm_bwd_w(s["pooled"], dyb0, name=ln + "dw_pool")
    do, dr, du, g["gla_norm_g"] = mix_pre_bwd(dya_in, dpooled, s["o"], s["p"], w["gla_norm_g"],
                                              name=ln + "mix_pre_bwd")
    dqkv, dglr, g["w_gk"], g["b_gk"] = gla_bwd(s["p"], w["w_gk"], w["b_gk"], s["st"], do, name=ln + "gla_bwd")
    return jnp.concatenate([dqkv, dr, du, dga, dgb, dglr], axis=1)


def bwd_in_w(dp, s, ln):
    return mm_tn(s["hn1"], dp, tn=896, out_dtype=BF16, name=ln + "dw_in")


def bwd_in_x(dp, dh2, s, w, g, ln):
    dhn1 = mm_nt(dp, w["w_in"], out_dtype=F32, tk=896, name=ln + "d_hn1")
    dh, g["norm1_g"] = rmsnorm_bwd(dhn1, s["h"], w["norm1_g"], dh2, name=ln + "norm1_bwd")
    return dh


def _my_place():
    return lax.axis_index("x"), lax.axis_index("y"), lax.axis_index("c")


def _peer(place, k):
    x, y, c = place
    return (1 - x if k & 4 else x, 1 - y if k & 2 else y, 1 - c if k & 1 else c)


def _index(place):
    x, y, c = place
    return 4 * x + 2 * y + c


def exchange(arrays, kinds, *, name):
    n = len(arrays)

    def body(*refs):
        ins, outs = refs[:n], refs[n:2 * n]
        send_sems, recv_sems, local_sems = refs[2 * n:]
        place = _my_place()
        me = _index(place)

        def src(a, dest):
            return ins[a] if kinds[a] == "gather" else ins[a].at[dest]

        def remote(a, k):
            peer = _peer(place, k)
            return pltpu.make_async_remote_copy(
                src_ref=src(a, _index(peer)), dst_ref=outs[a].at[me],
                send_sem=send_sems.at[a, k - 1], recv_sem=recv_sems.at[a, k - 1],
                device_id=peer, device_id_type=pl.DeviceIdType.MESH)

        def arrival(a, k):
            peer = _peer(place, k)
            return pltpu.make_async_remote_copy(
                src_ref=src(a, me), dst_ref=outs[a].at[_index(peer)],
                send_sem=send_sems.at[a, k - 1], recv_sem=recv_sems.at[a, k - 1],
                device_id=peer, device_id_type=pl.DeviceIdType.MESH)

        own = [pltpu.make_async_copy(src(a, me), outs[a].at[me], local_sems.at[a]) for a in range(n)]
        sends = [remote(a, k) for k in range(1, N_DEV) for a in range(n)]
        for cp in sends:
            cp.start()
        for cp in own:
            cp.start()
        for k in range(1, N_DEV):
            for a in range(n):
                arrival(a, k).wait_recv()
        for cp in sends:
            cp.wait_send()
        for cp in own:
            cp.wait()

    any_spec = pl.BlockSpec(memory_space=pl.ANY)
    out_shape = []
    for arr, kind in zip(arrays, kinds):
        shape = arr.shape if kind == "gather" else arr.shape[1:]
        out_shape.append(jax.ShapeDtypeStruct((N_DEV,) + tuple(shape), arr.dtype))
    return pl.pallas_call(
        body, name=name, in_specs=[any_spec] * n, out_specs=[any_spec] * n, out_shape=out_shape,
        scratch_shapes=[pltpu.SemaphoreType.DMA((n, N_DEV - 1)), pltpu.SemaphoreType.DMA((n, N_DEV - 1)),
                        pltpu.SemaphoreType.DMA((n,))],
    )(*arrays)


def _sem_slot(a, k):
    return a * (N_DEV - 1) + k - 1


_HBM = pl.BlockSpec(memory_space=pltpu.HBM)
_SEM = pl.BlockSpec(memory_space=pltpu.SEMAPHORE)
_DATAFLOW = pltpu.SideEffectType.DATAFLOW_SIDE_EFFECTING


def exchange_start(arrays, kinds, *, name):
    n = len(arrays)
    zones = []
    for arr, kind in zip(arrays, kinds):
        shape = arr.shape if kind == "gather" else arr.shape[1:]
        zones.append(lax.empty((N_DEV,) + tuple(shape), arr.dtype))

    def body(*refs):
        ins, lands = refs[:n], refs[n:2 * n]
        send_sems, recv_sems = refs[2 * n], refs[2 * n + 1]
        token, local_sems = refs[4 * n + 2], refs[4 * n + 3]
        place = _my_place()
        me = _index(place)

        def src(a, dest):
            return ins[a] if kinds[a] == "gather" else ins[a].at[dest]

        for a in range(n):
            for k in range(1, N_DEV):
                peer = _peer(place, k)
                pltpu.make_async_remote_copy(
                    src_ref=src(a, _index(peer)), dst_ref=lands[a].at[me],
                    send_sem=send_sems.at[_sem_slot(a, k)], recv_sem=recv_sems.at[_sem_slot(a, k)],
                    device_id=peer, device_id_type=pl.DeviceIdType.MESH).start()
        own = [pltpu.make_async_copy(src(a, me), lands[a].at[me], local_sems.at[a]) for a in range(n)]
        for cp in own:
            cp.start()
        for cp in own:
            cp.wait()
        token[...] = jnp.zeros_like(token)

    sems = pltpu.SemaphoreType.DMA((n * (N_DEV - 1),))
    hbm = lambda a: pltpu.HBM(a.shape, a.dtype)
    outs = pl.pallas_call(
        body, name=name,
        out_shape=(sems, sems, *[hbm(a) for a in arrays], *[hbm(z) for z in zones],
                   jax.ShapeDtypeStruct((8, LANES), F32)),
        in_specs=[_HBM] * (2 * n),
        out_specs=(_SEM, _SEM, *[_HBM] * (2 * n), pl.BlockSpec(memory_space=pltpu.VMEM)),
        input_output_aliases={i: 2 + i for i in range(2 * n)},
        scratch_shapes=[pltpu.SemaphoreType.DMA((n,))],
        compiler_params=pltpu.CompilerParams(has_side_effects=_DATAFLOW),
    )(*[pltpu.with_memory_space_constraint(a, pltpu.HBM) for a in arrays],
      *[pltpu.with_memory_space_constraint(z, pltpu.HBM) for z in zones])
    return dict(send=outs[0], recv=outs[1], srcs=outs[2:2 + n], zones=outs[2 + n:2 + 2 * n],
                token=outs[2 + 2 * n], kinds=kinds)


def exchange_wait(handle, after, *, name):
    kinds = handle["kinds"]
    n = len(kinds)

    def body(*refs):
        ins, lands = refs[:n], refs[n:2 * n]
        send_sems, recv_sems = refs[2 * n], refs[2 * n + 1]
        place = _my_place()
        me = _index(place)
        for a in range(n):
            for k in range(1, N_DEV):
                peer = _peer(place, k)
                src = ins[a] if kinds[a] == "gather" else ins[a].at[_index(peer)]
                copy = pltpu.make_async_remote_copy(
                    src_ref=src, dst_ref=lands[a].at[_index(peer)],
                    send_sem=send_sems.at[_sem_slot(a, k)], recv_sem=recv_sems.at[_sem_slot(a, k)],
                    device_id=peer, device_id_type=pl.DeviceIdType.MESH)
                copy.wait_send()
                copy.wait_recv()

    srcs, zones = handle["srcs"], handle["zones"]
    hbm = lambda a: pltpu.HBM(a.shape, a.dtype)
    outs = pl.pallas_call(
        body, name=name,
        out_shape=(*[hbm(a) for a in srcs], *[hbm(z) for z in zones]),
        in_specs=[_HBM] * (2 * n) + [_SEM, _SEM, pl.BlockSpec(memory_space=pl.ANY)],
        out_specs=[_HBM] * (2 * n),
        input_output_aliases={i: i for i in range(2 * n)},
        compiler_params=pltpu.CompilerParams(has_side_effects=_DATAFLOW),
    )(*srcs, *zones, handle["send"], handle["recv"], after)
    return outs[n:]


def reduce_adam_layer(parts, w, m, v, layer, prev, *, name):
    _, r, c = w.shape
    tr = _pick(r, (256, 352, 128))

    def body(*refs):
        p_ref, w_ref, m_ref, v_ref = refs[:4]
        g_ref, d_ref, m2_ref, v2_ref = refs[-4:]
        g = p_ref[0].astype(F32)
        for i in range(1, N_DEV):
            g = g + p_ref[i].astype(F32)
        m2 = B1 * m_ref[...] + (1.0 - B1) * g
        v2 = B2 * v_ref[...] + (1.0 - B2) * (g * g)
        m_hat = m2 / (1.0 - B1 ** STEP)
        v_hat = v2 / (1.0 - B2 ** STEP)
        g_ref[...] = g
        d_ref[...] = -LR * (m_hat / (jnp.sqrt(v_hat) + ADAM_EPS) + WD * w_ref[...])
        m2_ref[...] = m2
        v2_ref[...] = v2

    blk = pl.BlockSpec((None, tr, c), lambda i: (layer, i, 0))
    in_specs = [pl.BlockSpec((N_DEV, tr, c), lambda i: (0, i, 0)), blk, blk, blk]
    args = [parts, w, m, v]
    aliases = {}
    if prev is not None:
        in_specs += [pl.BlockSpec(memory_space=pl.ANY)] * 4
        args += list(prev)
        aliases = {4 + j: j for j in range(4)}
    return pl.pallas_call(
        body, name=name, grid=(r // tr,), in_specs=in_specs, out_specs=[blk] * 4,
        out_shape=[jax.ShapeDtypeStruct(w.shape, F32)] * 4, input_output_aliases=aliases,
        compiler_params=_params(("parallel",)))(*args)


def reduce_adam(parts, w, m, v, *, name):
    r, c = w.shape
    tr = _pick(r, (256, 352, 192, 128, 72, 64, 32, 16, 8))

    def body(p_ref, w_ref, m_ref, v_ref, g_ref, d_ref, m2_ref, v2_ref):
        g = p_ref[0].astype(F32)
        for i in range(1, N_DEV):
            g = g + p_ref[i].astype(F32)
        wv = w_ref[...]
        m2 = B1 * m_ref[...] + (1.0 - B1) * g
        v2 = B2 * v_ref[...] + (1.0 - B2) * (g * g)
        m_hat = m2 / (1.0 - B1 ** STEP)
        v_hat = v2 / (1.0 - B2 ** STEP)
        g_ref[...] = g
        d_ref[...] = -LR * (m_hat / (jnp.sqrt(v_hat) + ADAM_EPS) + WD * wv)
        m2_ref[...] = m2
        v2_ref[...] = v2

    blk = pl.BlockSpec((tr, c), lambda i: (i, 0))
    return pl.pallas_call(
        body, name=name, grid=(r // tr,),
        in_specs=[pl.BlockSpec((N_DEV, tr, c), lambda i: (0, i, 0)), blk, blk, blk],
        out_specs=[blk] * 4, out_shape=[jax.ShapeDtypeStruct((r, c), F32)] * 4,
        compiler_params=_params(("parallel",)))(parts, w, m, v)


BIG = ("w_in", "w_a", "w_pool_grp", "w_b", "w_o", "w_up", "w_down")
SHARDED_SMALL = ("meta_tokens", "w_gk", "conv_w")
REPLICATED = ("norm1_g", "b_gk", "gla_norm_g", "pool_scale", "b_gates", "norm2_g", "conv_b", "final_norm_g")
CUT_AXIS = {"w_in": 2, "w_a": 1, "w_pool_grp": 2, "w_b": 1, "w_o": 1, "w_up": 2, "w_down": 1,
            "meta_tokens": 1, "w_gk": 2, "conv_w": 2}
WEIGHTS = ("meta_tokens", "norm1_g", "w_in", "w_gk", "b_gk", "gla_norm_g", "w_a", "w_pool_grp", "pool_scale",
           "w_b", "b_gates", "w_o", "norm2_g", "w_up", "conv_w", "conv_b", "w_down", "final_norm_g")


def _as_2d(a):
    return a.reshape(-1, a.shape[-1])


def _from_slots(slots, axis):
    full = jnp.moveaxis(slots, 0, axis)
    shape = list(full.shape)
    shape[axis:axis + 2] = [shape[axis] * shape[axis + 1]]
    return full.reshape(shape)


def _to_slots(full, axis):
    shape = list(full.shape)
    shape[axis:axis + 1] = [N_DEV, shape[axis] // N_DEV]
    return jnp.moveaxis(full.reshape(shape), axis, 0)


def _pack(vectors, rows):
    flat = jnp.concatenate([v.reshape(-1).astype(F32) for v in vectors])
    return jnp.pad(flat, (0, rows * LANES - flat.shape[0])).reshape(rows, LANES)


def _unpack(packed, shapes):
    flat = packed.reshape(-1)
    out, off = [], 0
    for s in shapes:
        size = 1
        for d in s:
            size *= d
        out.append(flat[off:off + size].reshape(s))
        off += size
    return out


def _rows_for(shapes, mult=8):
    total = 0
    for s in shapes:
        size = 1
        for d in s:
            size *= d
        total += size
    rows = -(-total // LANES)
    return -(-rows // mult) * mult


def _permute_in(w_in):
    pad = jnp.zeros(w_in.shape[:-1] + (IN_R - IN_WIDTH,), w_in.dtype)
    return jnp.concatenate([w_in[..., :2048], w_in[..., 2064:], w_in[..., 2048:2064], pad], axis=-1)


def _unpermute_in(w_r):
    return jnp.concatenate([w_r[..., :2048], w_r[..., C_GLR:C_GLR + RANK], w_r[..., 2048:C_GLR]], axis=-1)


def kernel(x, meta_tokens, norm1_g, w_in, w_gk, b_gk, gla_norm_g, w_a, w_pool_grp, pool_scale, w_b, b_gates, w_o, norm2_g, w_up, conv_w, conv_b, w_down, final_norm_g, loss_target, m_meta_tokens, m_norm1_g, m_w_in, m_w_gk, m_b_gk, m_gla_norm_g, m_w_a, m_w_pool_grp, m_pool_scale, m_w_b, m_b_gates, m_w_o, m_norm2_g, m_w_up, m_conv_w, m_conv_b, m_w_down, m_final_norm_g, v_meta_tokens, v_norm1_g, v_w_in, v_w_gk, v_b_gk, v_gla_norm_g, v_w_a, v_w_pool_grp, v_pool_scale, v_w_b, v_b_gates, v_w_o, v_norm2_g, v_w_up, v_conv_w, v_conv_b, v_w_down, v_final_norm_g):
    wts = dict(meta_tokens=meta_tokens, norm1_g=norm1_g, w_in=w_in, w_gk=w_gk, b_gk=b_gk, gla_norm_g=gla_norm_g,
               w_a=w_a, w_pool_grp=w_pool_grp, pool_scale=pool_scale, w_b=w_b, b_gates=b_gates, w_o=w_o,
               norm2_g=norm2_g, w_up=w_up, conv_w=conv_w, conv_b=conv_b, w_down=w_down, final_norm_g=final_norm_g)
    mom = dict(meta_tokens=m_meta_tokens, norm1_g=m_norm1_g, w_in=m_w_in, w_gk=m_w_gk, b_gk=m_b_gk,
               gla_norm_g=m_gla_norm_g, w_a=m_w_a, w_pool_grp=m_w_pool_grp, pool_scale=m_pool_scale, w_b=m_w_b,
               b_gates=m_b_gates, w_o=m_w_o, norm2_g=m_norm2_g, w_up=m_w_up, conv_w=m_conv_w, conv_b=m_conv_b,
               w_down=m_w_down, final_norm_g=m_final_norm_g)
    var = dict(meta_tokens=v_meta_tokens, norm1_g=v_norm1_g, w_in=v_w_in, w_gk=v_w_gk, b_gk=v_b_gk,
               gla_norm_g=v_gla_norm_g, w_a=v_w_a, w_pool_grp=v_w_pool_grp, pool_scale=v_pool_scale, w_b=v_w_b,
               b_gates=v_b_gates, w_o=v_w_o, norm2_g=v_norm2_g, w_up=v_w_up, conv_w=v_conv_w, conv_b=v_conv_b,
               w_down=v_w_down, final_norm_g=v_final_norm_g)

    small_shapes = [wts[n].shape for n in SHARDED_SMALL]
    small_rows = _rows_for(small_shapes)

    def shard3(a):
        return a.reshape(DEPTH, -1, a.shape[-1])

    def layer_shards(l, names):
        return [shard3(wts[n])[l].astype(BF16) for n in names]

    groups = [("w_in",), ("w_a", "w_pool_grp", "w_b", "w_o"), ("w_up", "w_down")]
    starts = [exchange_start([_pack([wts[n] for n in SHARDED_SMALL], small_rows)] + layer_shards(0, groups[0]),
                             ["gather"] * 2, name="gather_in0_start")]
    starts.append(exchange_start(layer_shards(0, groups[1]), ["gather"] * 4, name="gather_mix0_start"))
    starts.append(exchange_start(layer_shards(0, groups[2]), ["gather"] * 2, name="gather_ffn0_start"))
    starts.append(exchange_start(layer_shards(1, BIG), ["gather"] * len(BIG), name="gather_l1_start"))

    def full_weight(n, zone):
        if n == "w_in":
            return _permute_in(jnp.moveaxis(zone, 0, 1).reshape(D, IN_WIDTH))
        if n == "w_up":
            return jnp.moveaxis(zone, 0, 1).reshape(D, F2)
        if n == "w_pool_grp":
            return jnp.moveaxis(zone.reshape(N_DEV, GROUPS, GDIM // N_DEV, GDIM), 0, 1).reshape(GROUPS, GDIM, GDIM)
        return zone.reshape(-1, zone.shape[-1])

    rows = dict(norm1_g=norm1_g, b_gk=b_gk, gla_norm_g=gla_norm_g, pool_scale=pool_scale, b_gates=b_gates,
                norm2_g=norm2_g, conv_b=conv_b)
    key = {"w_pool_grp": "w_pool"}
    zones = exchange_wait(starts[0], starts[-1]["token"], name="gather_in0_wait")
    small_slots = [jnp.stack(parts) for parts in zip(*[_unpack(zones[0][i], small_shapes) for i in range(N_DEV)])]
    small_full = {n: _from_slots(slots, CUT_AXIS[n]) for n, slots in zip(SHARDED_SMALL, small_slots)}
    w_gk_pad = jnp.pad(small_full["w_gk"], ((0, 0), (0, LANES - RANK), (0, 0))).astype(BF16)
    wl = []
    for l in range(DEPTH):
        d = {n: v[l:l + 1] for n, v in rows.items()}
        d["w_gk"] = w_gk_pad[l]
        d["conv_w"] = small_full["conv_w"][l]
        wl.append(d)
    wl[0]["w_in"] = full_weight("w_in", zones[1])

    h = jnp.concatenate([jnp.zeros((PAD, D), F32), small_full["meta_tokens"], x[0]], axis=0)
    s0 = {"h": h}
    fwd_in(s0, wl[0], "l0_")
    zones = exchange_wait(starts[1], s0["p"], name="gather_mix0_wait")
    for n, z in zip(groups[1], zones):
        wl[0][key.get(n, n)] = full_weight(n, z)
    fwd_mixer(s0, wl[0], "l0_")
    zones = exchange_wait(starts[2], s0["h2"], name="gather_ffn0_wait")
    for n, z in zip(groups[2], zones):
        wl[0][n] = full_weight(n, z)
    fwd_ffn(s0, wl[0], "l0_")
    zones = exchange_wait(starts[3], s0["h3"], name="gather_l1_wait")
    for n, z in zip(BIG, zones):
        wl[1][key.get(n, n)] = full_weight(n, z)
    s1 = {"h": s0["h3"]}
    fwd_in(s1, wl[1], "l1_")
    fwd_mixer(s1, wl[1], "l1_")
    fwd_ffn(s1, wl[1], "l1_")
    dh, dgf, loss_rows = loss_head(s1["h3"], final_norm_g[None], loss_target[0], name="loss_head")
    loss_part = 0.5 * jnp.sum(loss_rows) / D

    def blocks(n, gw):
        if n == "w_in":
            return jnp.moveaxis(_unpermute_in(gw).reshape(D, N_DEV, IN_WIDTH // N_DEV), 1, 0)
        if n == "w_up":
            return jnp.moveaxis(gw.reshape(D, N_DEV, F2 // N_DEV), 1, 0)
        if n == "w_pool_grp":
            gw = gw.astype(BF16).reshape(GROUPS, N_DEV, GDIM // N_DEV, GDIM)
            return jnp.moveaxis(gw, 1, 0).reshape(N_DEV, GROUPS * GDIM // N_DEV, GDIM)
        return gw.reshape(N_DEV, gw.shape[0] // N_DEV, gw.shape[1])

    g1 = {}
    dh2 = bwd_ffn(dh, dh.astype(BF16), s1, wl[1], g1, "l1_")
    dp = bwd_mixer(dh2, dh2.astype(BF16), s1, wl[1], g1, "l1_")
    g1["w_in"] = bwd_in_w(dp, s1, "l1_")
    dh = bwd_in_x(dp, dh2, s1, wl[1], g1, "l1_")
    sent_l1 = exchange_start([blocks(n, g1[key.get(n, n)]) for n in BIG], ["scatter"] * len(BIG),
                             name="scatter_l1_start")
    g0 = {}
    dh_b = (dh + sent_l1["token"][0, 0]).astype(BF16)
    dh2 = bwd_ffn(dh, dh_b, s0, wl[0], g0, "l0_")
    sent_ffn0 = exchange_start([blocks(n, g0[n]) for n in groups[2]], ["scatter"] * 2, name="scatter_ffn0_start")
    dh2_b = (dh2 + sent_ffn0["token"][0, 0]).astype(BF16)
    dp = bwd_mixer(dh2, dh2_b, s0, wl[0], g0, "l0_")
    recv_l1 = exchange_wait(sent_l1, dp, name="scatter_l1_wait")
    g0["w_in"] = bwd_in_w(dp, s0, "l0_")
    rest0 = groups[0] + groups[1]
    sent_rest0 = exchange_start([blocks(n, g0[key.get(n, n)]) for n in rest0], ["scatter"] * len(rest0),
                                name="scatter_rest0_start")
    dh = bwd_in_x(dp, dh2, s0, wl[0], g0, "l0_")
    recv_ffn0 = exchange_wait(sent_ffn0, dh, name="scatter_ffn0_wait")
    recv_rest0 = exchange_wait(sent_rest0, dh, name="scatter_rest0_wait")
    grad_x = dh[X0:]
    recv0 = dict(zip(rest0 + groups[2], list(recv_rest0) + list(recv_ffn0)))

    grads, delta, new_m, new_v = {}, {}, {}, {}
    for n, parts1 in zip(BIG, recv_l1):
        w3, m3, v3 = shard3(wts[n]), shard3(mom[n]), shard3(var[n])
        first = reduce_adam_layer(parts1, w3, m3, v3, 1, None, name="adam_l1_" + n)
        outs = reduce_adam_layer(recv0[n], w3, m3, v3, 0, first, name="adam_l0_" + n)
        grads[n], delta[n], new_m[n], new_v[n] = [o.reshape(wts[n].shape) for o in outs]

    g_full = {n: jnp.stack([g0[n], g1[n]])[:, 0] for n in rows}
    g_full["final_norm_g"] = dgf[0]
    g_full["meta_tokens"] = dh[PAD:X0]
    g_full["w_gk"] = jnp.stack([g0["w_gk"], g1["w_gk"]])[:, :RANK]
    g_full["conv_w"] = jnp.stack([g0["conv_w"], g1["conv_w"]])
    rep_shapes = [wts[n].shape for n in REPLICATED] + [(1,)]
    rep_rows = _rows_for(rep_shapes)
    small_blocks = jnp.stack([
        _pack([_to_slots(g_full[n], CUT_AXIS[n])[i] for n in SHARDED_SMALL], small_rows) for i in range(N_DEV)])
    rep_pack = _pack([g_full[n] for n in REPLICATED] + [loss_part.reshape(1)], rep_rows)
    received = exchange([small_blocks, rep_pack], ["scatter", "gather"], name="exchange_small")
    outs = reduce_adam(received[-2], _pack([wts[n] for n in SHARDED_SMALL], small_rows),
                       _pack([mom[n] for n in SHARDED_SMALL], small_rows),
                       _pack([var[n] for n in SHARDED_SMALL], small_rows), name="adam_small")
    for d, o in zip((grads, delta, new_m, new_v), outs):
        for n, a in zip(SHARDED_SMALL, _unpack(o, small_shapes)):
            d[n] = a
    one = [jnp.zeros((1,), F32)]
    outs = reduce_adam(received[-1], _pack([wts[n] for n in REPLICATED] + one, rep_rows),
                       _pack([mom[n] for n in REPLICATED] + one, rep_rows),
                       _pack([var[n] for n in REPLICATED] + one, rep_rows), name="adam_replicated")
    for d, o in zip((grads, delta, new_m, new_v), outs):
        for n, a in zip(REPLICATED + ("loss",), _unpack(o, rep_shapes)):
            d[n] = a
    loss = grads["loss"][0]
    return (loss, grad_x[None], *[grads[n] for n in WEIGHTS], *[delta[n] for n in WEIGHTS],
            *[new_m[n] for n in WEIGHTS], *[new_v[n] for n in WEIGHTS])
```

```python
import functools

import jax
import jax.numpy as jnp
from jax import lax
from jax.experimental import pallas as pl
from jax.experimental.pallas import tpu as pltpu

F32 = jnp.float32
BF16 = jnp.bfloat16

D = 1024
DEPTH = 2
N_META = 16
HEADS = 4
DK = 512
DV = 1024
HK = 128
HV = 256
RANK = 16
TAU = 16.0
CHUNK = 64
POOL_WINDOWS = (2, 4, 8, 16)
GROUPS = 4
GDIM = 256
D_FF = 2816
F2 = 2 * D_FF
EPS = 1e-6
IN_WIDTH = 6160
LR, B1, B2, ADAM_EPS, WD, STEP = 0.001, 0.9, 0.999, 1e-8, 0.01, 10

N_DEV = 8
PAD = CHUNK - N_META
X0 = CHUNK
IN_R = 6272
C_Q, C_K, C_V, C_R, C_U, C_GA, C_GB, C_GLR = 0, 512, 1024, 2048, 3072, 4096, 5120, 6144
VMEM_LIMIT = 56 * 1024 * 1024
LANES = 128


def _params(sem=None):
    return pltpu.CompilerParams(dimension_semantics=sem, vmem_limit_bytes=VMEM_LIMIT)


def _pick(n, prefs):
    for t in prefs:
        if n % t == 0:
            return t
    raise ValueError(f"no tile for {n} in {prefs}")


def _row_tile(lp):
    return _pick(lp, (688, 192, 128, 64))


def _ew_tile(lp):
    return _pick(lp, (192, 128, 64))


def _sigmoid(x):
    return 1.0 / (1.0 + jnp.exp(-x))


def _dot(a, b, dims):
    return lax.dot_general(a, b, (dims, ((), ())), preferred_element_type=F32)


def _nn(a, b):
    return _dot(a, b, ((1,), (0,)))


def _nt(a, b):
    return _dot(a, b, ((1,), (1,)))


def _tn(a, b):
    return _dot(a, b, ((0,), (0,)))


def mm_nn(a, b, *, out_dtype=BF16, tn=None, res=None, name):
    m, k = a.shape
    n = b.shape[1]
    tm = _row_tile(m)
    tn = tn or n
    has_res = res is not None

    def body(*refs):
        if has_res:
            a_ref, b_ref, r_ref, o_ref = refs
        else:
            a_ref, b_ref, o_ref = refs
        acc = _nn(a_ref[...], b_ref[...])
        if has_res:
            row = pl.program_id(1) * tm + lax.broadcasted_iota(jnp.int32, (tm, 1), 0)
            acc = jnp.where(row >= PAD, acc + r_ref[...], 0.0)
        o_ref[...] = acc.astype(o_ref.dtype)

    in_specs = [pl.BlockSpec((tm, k), lambda j, i: (i, 0)),
                pl.BlockSpec((k, tn), lambda j, i: (0, j))]
    args = [a, b]
    if has_res:
        in_specs.append(pl.BlockSpec((tm, tn), lambda j, i: (i, j)))
        args.append(res)
    return pl.pallas_call(
        body, name=name, grid=(n // tn, m // tm), in_specs=in_specs,
        out_specs=pl.BlockSpec((tm, tn), lambda j, i: (i, j)),
        out_shape=jax.ShapeDtypeStruct((m, n), out_dtype),
        compiler_params=_params(("parallel", "parallel")))(*args)


def mm_nt(a, b, *, out_dtype=BF16, tn=None, tk=None, after=None, name):
    m, k = a.shape
    n = b.shape[0]
    tm = _row_tile(m)
    tn = tn or n
    tk = tk or k
    nk = k // tk
    extra = [] if after is None else [after]

    def body(a_ref, b_ref, *rest):
        o_ref, acc_ref = rest[-2:]
        kk = pl.program_id(2)
        part = _nt(a_ref[...], b_ref[...])

        @pl.when(kk == 0)
        def _():
            acc_ref[...] = part

        @pl.when(kk > 0)
        def _():
            acc_ref[...] += part

        @pl.when(kk == nk - 1)
        def _():
            o_ref[...] = acc_ref[...].astype(o_ref.dtype)

    return pl.pallas_call(
        body, name=name, grid=(n // tn, m // tm, nk),
        in_specs=[pl.BlockSpec((tm, tk), lambda j, i, kk: (i, kk)),
                  pl.BlockSpec((tn, tk), lambda j, i, kk: (j, kk))]
                 + [pl.BlockSpec(memory_space=pl.ANY)] * len(extra),
        out_specs=pl.BlockSpec((tm, tn), lambda j, i, kk: (i, j)),
        out_shape=jax.ShapeDtypeStruct((m, n), out_dtype),
        scratch_shapes=[pltpu.VMEM((tm, tn), F32)],
        compiler_params=_params(("parallel", "parallel", "arbitrary")))(a, b, *extra)


def mm_tn(a, b, *, tk1=None, tn=None, out_dtype=F32, name):
    m, k1 = a.shape
    n = b.shape[1]
    tm = _row_tile(m)
    tk1 = tk1 or k1
    tn = tn or n
    nm = m // tm

    def body(a_ref, b_ref, o_ref, acc_ref):
        i = pl.program_id(2)
        part = _tn(a_ref[...], b_ref[...])

        @pl.when(i == 0)
        def _():
            acc_ref[...] = part

        @pl.when(i > 0)
        def _():
            acc_ref[...] += part

        @pl.when(i == nm - 1)
        def _():
            o_ref[...] = acc_ref[...].astype(o_ref.dtype)

    return pl.pallas_call(
        body, name=name, grid=(k1 // tk1, n // tn, nm),
        in_specs=[pl.BlockSpec((tm, tk1), lambda p, j, i: (i, p)),
                  pl.BlockSpec((tm, tn), lambda p, j, i: (i, j))],
        out_specs=pl.BlockSpec((tk1, tn), lambda p, j, i: (p, j)),
        out_shape=jax.ShapeDtypeStruct((k1, n), out_dtype),
        scratch_shapes=[pltpu.VMEM((tk1, tn), F32)],
        compiler_params=_params(("parallel", "parallel", "arbitrary")))(a, b)


def pool_mm_fwd(pooled, wp, scale, *, name):
    m = pooled.shape[0]
    tm = _row_tile(m)

    def body(a_ref, w_ref, s_ref, y0_ref, y1_ref):
        acc = _nn(a_ref[...], w_ref[...])
        y0_ref[...] = acc.astype(BF16)
        y1_ref[...] = (acc * s_ref[...]).astype(BF16)

    blk = pl.BlockSpec((tm, GDIM), lambda g, i: (i, g))
    return pl.pallas_call(
        body, name=name, grid=(GROUPS, m // tm),
        in_specs=[blk, pl.BlockSpec((None, GDIM, GDIM), lambda g, i: (g, 0, 0)),
                  pl.BlockSpec((1, GDIM), lambda g, i: (0, g))],
        out_specs=[blk, blk],
        out_shape=[jax.ShapeDtypeStruct((m, D), BF16)] * 2,
        compiler_params=_params(("parallel", "parallel")))(pooled, wp, scale)


def pool_mm_bwd_x(dy0, wp, *, name):
    m = dy0.shape[0]
    tm = _row_tile(m)

    def body(a_ref, w_ref, o_ref):
        o_ref[...] = _nt(a_ref[...], w_ref[...]).astype(BF16)

    blk = pl.BlockSpec((tm, GDIM), lambda g, i: (i, g))
    return pl.pallas_call(
        body, name=name, grid=(GROUPS, m // tm),
        in_specs=[blk, pl.BlockSpec((None, GDIM, GDIM), lambda g, i: (g, 0, 0))],
        out_specs=blk, out_shape=jax.ShapeDtypeStruct((m, D), BF16),
        compiler_params=_params(("parallel", "parallel")))(dy0, wp)


def pool_mm_bwd_w(pooled, dy0, *, name):
    m = pooled.shape[0]
    tm = _row_tile(m)

    def body(a_ref, b_ref, o_ref):
        part = _tn(a_ref[...], b_ref[...])

        @pl.when(pl.program_id(1) == 0)
        def _():
            o_ref[...] = part

        @pl.when(pl.program_id(1) > 0)
        def _():
            o_ref[...] += part

    blk = pl.BlockSpec((tm, GDIM), lambda g, i: (i, g))
    return pl.pallas_call(
        body, name=name, grid=(GROUPS, m // tm), in_specs=[blk, blk],
        out_specs=pl.BlockSpec((None, GDIM, GDIM), lambda g, i: (g, 0, 0)),
        out_shape=jax.ShapeDtypeStruct((GROUPS, GDIM, GDIM), F32),
        compiler_params=_params(("parallel", "arbitrary")))(pooled, dy0)


def rmsnorm_fwd(x, g, *, name):
    m = x.shape[0]
    tm = _ew_tile(m)

    def body(x_ref, g_ref, o_ref):
        xv = x_ref[...]
        r = lax.rsqrt(jnp.mean(xv * xv, axis=-1, keepdims=True) + EPS)
        o_ref[...] = (xv * r * g_ref[...]).astype(BF16)

    return pl.pallas_call(
        body, name=name, grid=(m // tm,),
        in_specs=[pl.BlockSpec((tm, D), lambda i: (i, 0)), pl.BlockSpec((1, D), lambda i: (0, 0))],
        out_specs=pl.BlockSpec((tm, D), lambda i: (i, 0)),
        out_shape=jax.ShapeDtypeStruct((m, D), BF16),
        compiler_params=_params(("parallel",)))(x, g)


def rmsnorm_bwd(dy, x, g, dres, *, name):
    m = x.shape[0]
    tm = _ew_tile(m)

    def body(dy_ref, x_ref, g_ref, r_ref, dx_ref, dg_ref):
        i = pl.program_id(0)
        xv = x_ref[...]
        dyv = dy_ref[...].astype(F32)
        r = lax.rsqrt(jnp.mean(xv * xv, axis=-1, keepdims=True) + EPS)
        xh = xv * r
        dxh = dyv * g_ref[...]
        dx = r * (dxh - xh * jnp.mean(dxh * xh, axis=-1, keepdims=True))
        row = i * tm + lax.broadcasted_iota(jnp.int32, (tm, 1), 0)
        dx_ref[...] = jnp.where(row >= PAD, dx + r_ref[...], 0.0)

        @pl.when(i == 0)
        def _():
            dg_ref[...] = jnp.zeros_like(dg_ref)

        dg_ref[...] += jnp.sum(dyv * xh, axis=0, keepdims=True)

    blk = pl.BlockSpec((tm, D), lambda i: (i, 0))
    vec = pl.BlockSpec((1, D), lambda i: (0, 0))
    return pl.pallas_call(
        body, name=name, grid=(m // tm,), in_specs=[blk, blk, vec, blk],
        out_specs=[blk, vec],
        out_shape=[jax.ShapeDtypeStruct((m, D), F32), jax.ShapeDtypeStruct((1, D), F32)],
        compiler_params=_params(("arbitrary",)))(dy, x, g, dres)


def loss_head(h, gf, target, *, name):
    m = h.shape[0]
    t = X0
    inv_d = 1.0 / D

    def body(h_ref, g_ref, t_ref, dh_ref, dg_ref, ls_ref):
        i = pl.program_id(0)

        @pl.when(i == 0)
        def _():
            dg_ref[...] = jnp.zeros_like(dg_ref)
            ls_ref[...] = jnp.zeros_like(ls_ref)
            dh_ref[...] = jnp.zeros_like(dh_ref)

        @pl.when(i > 0)
        def _():
            xv = h_ref[...]
            r = lax.rsqrt(jnp.mean(xv * xv, axis=-1, keepdims=True) + EPS)
            xh = xv * r
            err = xh * g_ref[...] - t_ref[...]
            ls_ref[...] += jnp.sum(err * err, axis=0, keepdims=True)
            dy = err * inv_d
            dg_ref[...] += jnp.sum(dy * xh, axis=0, keepdims=True)
            dxh = dy * g_ref[...]
            dh_ref[...] = r * (dxh - xh * jnp.mean(dxh * xh, axis=-1, keepdims=True))

    blk = pl.BlockSpec((t, D), lambda i: (i, 0))
    vec = pl.BlockSpec((1, D), lambda i: (0, 0))
    return pl.pallas_call(
        body, name=name, grid=(m // t,),
        in_specs=[blk, vec, pl.BlockSpec((t, D), lambda i: (jnp.maximum(i - 1, 0), 0))],
        out_specs=[blk, vec, vec],
        out_shape=[jax.ShapeDtypeStruct((m, D), F32), jax.ShapeDtypeStruct((1, D), F32),
                   jax.ShapeDtypeStruct((1, D), F32)],
        compiler_params=_params(("arbitrary",)))(h, gf, target)


def _split3(x):
    x1 = x.astype(BF16)
    r1 = x - x1.astype(F32)
    x2 = r1.astype(BF16)
    x3 = (r1 - x2.astype(F32)).astype(BF16)
    return x1, x2, x3


def _tri_mm(tri, x):
    x1, x2, x3 = _split3(x)
    return _nn(tri, x1) + _nn(tri, x2) + _nn(tri, x3)


def _log_decay(glr, wgk, bgk, row0, rows):
    z = _nn(glr, wgk) + bgk
    la = (jnp.minimum(z, 0.0) - jnp.log(1.0 + jnp.exp(-jnp.abs(z)))) * (1.0 / TAU)
    row = row0 + lax.broadcasted_iota(jnp.int32, (rows, 1), 0)
    return z, jnp.where(row >= PAD, la, 0.0)


def _chunk_group(n_chunks):
    return _pick(n_chunks, (3, 2, 1))


def gla_fwd(p, wgk, bgk, *, name):
    m = p.shape[0]
    n_chunks = m // CHUNK
    cg = _chunk_group(n_chunks)
    t = cg * CHUNK
    scale = HK ** -0.5

    def body(q_ref, k_ref, v_ref, glr_ref, wgk_ref, bgk_ref, o_ref, st_ref, state):
        i = pl.program_id(0)

        @pl.when(i == 0)
        def _():
            state[...] = jnp.zeros_like(state)

        _, la = _log_decay(glr_ref[...], wgk_ref[...], bgk_ref[...], i * t, t)
        ri = lax.broadcasted_iota(jnp.int32, (CHUNK, CHUNK), 0)
        ci = lax.broadcasted_iota(jnp.int32, (CHUNK, CHUNK), 1)
        causal = ri >= ci
        tri = causal.astype(BF16)
        for c in range(cg):
            rows = pl.ds(c * CHUNK, CHUNK)
            b = _tri_mm(tri, la[c * CHUNK:(c + 1) * CHUNK])
            bl = b[CHUNK - 1:CHUNK, :]
            q = q_ref[rows, :].astype(F32) * scale
            k = k_ref[rows, :].astype(F32)
            qd = (q * jnp.exp(b)).astype(BF16)
            ki = (k * jnp.exp(-b)).astype(BF16)
            ke = (k * jnp.exp(bl - b)).astype(BF16)
            dec = jnp.exp(bl)
            for h in range(HEADS):
                ks = slice(h * HK, (h + 1) * HK)
                vs = pl.ds(h * HV, HV)
                vh = v_ref[rows, vs]
                s_t = state[h]
                st_ref[c, h] = s_t
                att = jnp.where(causal, _nt(qd[:, ks], ki[:, ks]), 0.0).astype(BF16)
                o_ref[rows, vs] = _nn(att, vh) + _nt(qd[:, ks], s_t.astype(BF16))
                state[h] = s_t * dec[:, ks] + _tn(vh, ke[:, ks])

    return pl.pallas_call(
        body, name=name, grid=(n_chunks // cg,),
        in_specs=[pl.BlockSpec((t, DK), lambda i: (i, C_Q // DK)),
                  pl.BlockSpec((t, DK), lambda i: (i, C_K // DK)),
                  pl.BlockSpec((t, DV), lambda i: (i, C_V // DV)),
                  pl.BlockSpec((t, LANES), lambda i: (i, C_GLR // LANES)),
                  pl.BlockSpec((LANES, DK), lambda i: (0, 0)),
                  pl.BlockSpec((1, DK), lambda i: (0, 0))],
        out_specs=[pl.BlockSpec((t, DV), lambda i: (i, 0)),
                   pl.BlockSpec((cg, HEADS, HV, HK), lambda i: (i, 0, 0, 0))],
        out_shape=[jax.ShapeDtypeStruct((m, DV), F32),
                   jax.ShapeDtypeStruct((n_chunks, HEADS, HV, HK), F32)],
        scratch_shapes=[pltpu.VMEM((HEADS, HV, HK), F32)],
        compiler_params=_params(("arbitrary",)))(p, p, p, p, wgk, bgk)


def gla_bwd(p, wgk, bgk, st, do, *, name):
    m = p.shape[0]
    n_chunks = m // CHUNK
    cg = _chunk_group(n_chunks)
    t = cg * CHUNK
    ns = n_chunks // cg
    scale = HK ** -0.5

    def body(q_ref, k_ref, v_ref, glr_ref, wgk_ref, bgk_ref, st_ref, do_ref,
             dqkv_ref, dglr_ref, dwgk_ref, dbgk_ref, dstate, dz_buf):
        i = pl.program_id(0)
        blk = ns - 1 - i

        @pl.when(i == 0)
        def _():
            dstate[...] = jnp.zeros_like(dstate)
            dwgk_ref[...] = jnp.zeros_like(dwgk_ref)
            dbgk_ref[...] = jnp.zeros_like(dbgk_ref)

        z, la = _log_decay(glr_ref[...], wgk_ref[...], bgk_ref[...], blk * t, t)
        ri = lax.broadcasted_iota(jnp.int32, (CHUNK, CHUNK), 0)
        ci = lax.broadcasted_iota(jnp.int32, (CHUNK, CHUNK), 1)
        causal = ri >= ci
        tri = causal.astype(BF16)
        tri_u = (ri <= ci).astype(BF16)
        for c in reversed(range(cg)):
            rows = pl.ds(c * CHUNK, CHUNK)
            b = _tri_mm(tri, la[c * CHUNK:(c + 1) * CHUNK])
            bl = b[CHUNK - 1:CHUNK, :]
            eb = jnp.exp(b)
            enb = jnp.exp(-b)
            ebl = jnp.exp(bl - b)
            dec = jnp.exp(bl)
            q = q_ref[rows, :].astype(F32) * scale
            k = k_ref[rows, :].astype(F32)
            qd32 = q * eb
            ki32 = k * enb
            ke32 = k * ebl
            qd = qd32.astype(BF16)
            ki = ki32.astype(BF16)
            ke = ke32.astype(BF16)
            dqd_parts, dki_parts, dke_parts, ddec_parts = [], [], [], []
            for h in range(HEADS):
                ks = slice(h * HK, (h + 1) * HK)
                vs = pl.ds(h * HV, HV)
                vh = v_ref[rows, vs]
                doh = do_ref[rows, vs].astype(BF16)
                s_t = st_ref[c, h]
                ds_t = dstate[h]
                ds_b = ds_t.astype(BF16)
                att = jnp.where(causal, _nt(qd[:, ks], ki[:, ks]), 0.0).astype(BF16)
                datt = jnp.where(causal, _nt(doh, vh), 0.0).astype(BF16)
                dvh = _tn(att, doh) + _nt(ke[:, ks], ds_b)
                dqkv_ref[rows, pl.ds(2 * DK + h * HV, HV)] = dvh.astype(BF16)
                dqd_parts.append(_nn(datt, ki[:, ks]) + _nn(doh, s_t.astype(BF16)))
                dki_parts.append(_tn(datt, qd[:, ks]))
                dke_parts.append(_nn(vh, ds_b))
                ddec_parts.append(jnp.sum(s_t * ds_t, axis=0, keepdims=True))
                dstate[h] = _tn(doh, qd[:, ks]) + ds_t * dec[:, ks]
            dqd = jnp.concatenate(dqd_parts, axis=1)
            dki = jnp.concatenate(dki_parts, axis=1)
            dke = jnp.concatenate(dke_parts, axis=1)
            ddec = jnp.concatenate(ddec_parts, axis=1)
            dqkv_ref[rows, pl.ds(0, DK)] = (dqd * eb * scale).astype(BF16)
            dqkv_ref[rows, pl.ds(DK, DK)] = (dki * enb + dke * ebl).astype(BF16)
            dke_ke = dke * ke32
            db = dqd * qd32 - dki * ki32 - dke_ke
            dbl = jnp.sum(dke_ke, axis=0, keepdims=True) + ddec * dec
            dg = _tri_mm(tri_u, db) + dbl
            row = blk * t + c * CHUNK + lax.broadcasted_iota(jnp.int32, (CHUNK, 1), 0)
            zc = z[c * CHUNK:(c + 1) * CHUNK]
            dz = jnp.where(row >= PAD, dg * (1.0 / TAU) * _sigmoid(-zc), 0.0)
            dz_buf[rows, :] = dz
        dz_all = dz_buf[...]
        dz_b = dz_all.astype(BF16)
        dbgk_ref[...] += jnp.sum(dz_all, axis=0, keepdims=True)
        dglr_ref[...] = _nt(dz_b, wgk_ref[...]).astype(BF16)
        dwgk_ref[...] += _tn(glr_ref[...], dz_b)

    rev = lambda i: ns - 1 - i
    return pl.pallas_call(
        body, name=name, grid=(ns,),
        in_specs=[pl.BlockSpec((t, DK), lambda i: (rev(i), C_Q // DK)),
                  pl.BlockSpec((t, DK), lambda i: (rev(i), C_K // DK)),
                  pl.BlockSpec((t, DV), lambda i: (rev(i), C_V // DV)),
                  pl.BlockSpec((t, LANES), lambda i: (rev(i), C_GLR // LANES)),
                  pl.BlockSpec((LANES, DK), lambda i: (0, 0)),
                  pl.BlockSpec((1, DK), lambda i: (0, 0)),
                  pl.BlockSpec((cg, HEADS, HV, HK), lambda i: (rev(i), 0, 0, 0)),
                  pl.BlockSpec((t, DV), lambda i: (rev(i), 0))],
        out_specs=[pl.BlockSpec((t, 2 * DK + DV), lambda i: (rev(i), 0)),
                   pl.BlockSpec((t, LANES), lambda i: (rev(i), 0)),
                   pl.BlockSpec((LANES, DK), lambda i: (0, 0)),
                   pl.BlockSpec((1, DK), lambda i: (0, 0))],
        out_shape=[jax.ShapeDtypeStruct((m, 2 * DK + DV), BF16),
                   jax.ShapeDtypeStruct((m, LANES), BF16),
                   jax.ShapeDtypeStruct((LANES, DK), F32),
                   jax.ShapeDtypeStruct((1, DK), F32)],
        scratch_shapes=[pltpu.VMEM((HEADS, HV, HK), F32), pltpu.VMEM((t, DK), F32)],
        compiler_params=_params(("arbitrary",)))(p, p, p, p, wgk, bgk, st, do)


HALO = 16


def _shift_down(xx, s):
    return pltpu.roll(xx, s, 0)


def _shift_up(xx, s):
    return pltpu.roll(xx, xx.shape[0] - s, 0)


def mix_pre(o, p, gn, *, name):
    m = o.shape[0]
    tm = _ew_tile(m)

    def body(o_ref, r_ref, u_ref, gn_ref, ya_ref, pooled_ref, halo):
        i = pl.program_id(0)

        @pl.when(i == 0)
        def _():
            halo[...] = jnp.zeros_like(halo)

        rv = r_ref[...].astype(F32)
        silu_r = rv * _sigmoid(rv)
        for h in range(HEADS):
            cs = pl.ds(h * HV, HV)
            ov = o_ref[:, cs]
            rs = lax.rsqrt(jnp.mean(ov * ov, axis=-1, keepdims=True) + EPS)
            ya_ref[:, cs] = (ov * rs * gn_ref[...] * silu_r[:, h * HV:(h + 1) * HV]).astype(BF16)

        row = i * tm + lax.broadcasted_iota(jnp.int32, (tm, 1), 0)
        pos1 = jnp.maximum(row - PAD + 1, 1).astype(F32)
        for g, w in enumerate(POOL_WINDOWS):
            cs = pl.ds(g * GDIM, GDIM)
            uv = u_ref[:, cs].astype(F32)
            xx = jnp.concatenate([halo[:, cs], uv], axis=0)
            s = xx
            span = 1
            while span < w:
                s = s + _shift_down(s, span)
                span *= 2
            inv = 1.0 / jnp.minimum(pos1, float(w))
            pooled_ref[:, cs] = (s[HALO:] * inv - uv).astype(BF16)
            halo[:, cs] = uv[tm - HALO:]

    blk = pl.BlockSpec((tm, D), lambda i: (i, 0))
    return pl.pallas_call(
        body, name=name, grid=(m // tm,),
        in_specs=[blk, pl.BlockSpec((tm, D), lambda i: (i, C_R // D)),
                  pl.BlockSpec((tm, D), lambda i: (i, C_U // D)),
                  pl.BlockSpec((1, HV), lambda i: (0, 0))],
        out_specs=[blk, blk],
        out_shape=[jax.ShapeDtypeStruct((m, D), BF16)] * 2,
        scratch_shapes=[pltpu.VMEM((HALO, D), F32)],
        compiler_params=_params(("arbitrary",)))(o, p, p, gn)


def mix_pre_bwd(dya, dpooled, o, p, gn, *, name):
    m = o.shape[0]
    tm = _ew_tile(m)
    nt = m // tm

    def body(dya_ref, dpl_ref, o_ref, r_ref, gn_ref, do_ref, dr_ref, du_ref, dgn_ref, halo):
        i = pl.program_id(0)
        blk_i = nt - 1 - i

        @pl.when(i == 0)
        def _():
            halo[...] = jnp.zeros_like(halo)
            dgn_ref[...] = jnp.zeros_like(dgn_ref)

        rv = r_ref[...].astype(F32)
        sg = _sigmoid(rv)
        silu_r = rv * sg
        dsilu = sg * (1.0 + rv * (1.0 - sg))
        dgn = jnp.zeros((1, HV), F32)
        for h in range(HEADS):
            cs = pl.ds(h * HV, HV)
            hs = slice(h * HV, (h + 1) * HV)
            ov = o_ref[:, cs]
            dy = dya_ref[:, cs].astype(F32)
            rs = lax.rsqrt(jnp.mean(ov * ov, axis=-1, keepdims=True) + EPS)
            xh = ov * rs
            on = xh * gn_ref[...]
            don = dy * silu_r[:, hs]
            dr_ref[:, cs] = (dy * on * dsilu[:, hs]).astype(BF16)
            dxh = don * gn_ref[...]
            do_ref[:, cs] = rs * (dxh - xh * jnp.mean(dxh * xh, axis=-1, keepdims=True))
            dgn = dgn + jnp.sum(don * xh, axis=0, keepdims=True)
        dgn_ref[...] += dgn

        row = blk_i * tm + lax.broadcasted_iota(jnp.int32, (tm, 1), 0)
        pos1 = jnp.maximum(row - PAD + 1, 1).astype(F32)
        for g, w in enumerate(POOL_WINDOWS):
            cs = pl.ds(g * GDIM, GDIM)
            dpv = dpl_ref[:, cs].astype(F32)
            e = dpv * (1.0 / jnp.minimum(pos1, float(w)))
            xx = jnp.concatenate([e, halo[:, cs]], axis=0)
            s = xx
            span = 1
            while span < w:
                s = s + _shift_up(s, span)
                span *= 2
            du_ref[:, cs] = (s[:tm] - dpv).astype(BF16)
            halo[:, cs] = e[:HALO]

    rev = lambda i: nt - 1 - i
    blk = pl.BlockSpec((tm, D), lambda i: (rev(i), 0))
    return pl.pallas_call(
        body, name=name, grid=(nt,),
        in_specs=[blk, blk, blk, pl.BlockSpec((tm, D), lambda i: (rev(i), C_R // D)),
                  pl.BlockSpec((1, HV), lambda i: (0, 0))],
        out_specs=[blk, blk, blk, pl.BlockSpec((1, HV), lambda i: (0, 0))],
        out_shape=[jax.ShapeDtypeStruct((m, D), F32), jax.ShapeDtypeStruct((m, D), BF16),
                   jax.ShapeDtypeStruct((m, D), BF16), jax.ShapeDtypeStruct((1, HV), F32)],
        scratch_shapes=[pltpu.VMEM((HALO, D), F32)],
        compiler_params=_params(("arbitrary",)))(dya, dpooled, o, p, gn)


def merge_fwd(p, ya, yb, bg, *, name):
    m = ya.shape[0]
    tm = _ew_tile(m)

    def body(ga_ref, gb_ref, ya_ref, yb_ref, ba_ref, bb_ref, o_ref):
        gate_a = _sigmoid(ga_ref[...].astype(F32) + ba_ref[...])
        gate_b = _sigmoid(gb_ref[...].astype(F32) + bb_ref[...])
        o_ref[...] = (gate_a * ya_ref[...].astype(F32) + gate_b * yb_ref[...].astype(F32)).astype(BF16)

    blk = pl.BlockSpec((tm, D), lambda i: (i, 0))
    return pl.pallas_call(
        body, name=name, grid=(m // tm,),
        in_specs=[pl.BlockSpec((tm, D), lambda i: (i, C_GA // D)),
                  pl.BlockSpec((tm, D), lambda i: (i, C_GB // D)), blk, blk,
                  pl.BlockSpec((1, D), lambda i: (0, 0)), pl.BlockSpec((1, D), lambda i: (0, 1))],
        out_specs=blk, out_shape=jax.ShapeDtypeStruct((m, D), BF16),
        compiler_params=_params(("parallel",)))(p, p, ya, yb, bg, bg)


def merge_bwd(dmrg, p, ya, yb, bg, *, name):
    m = ya.shape[0]
    tm = _ew_tile(m)

    def body(dm_ref, ga_ref, gb_ref, ya_ref, yb_ref, ba_ref, bb_ref,
             dya_ref, dyb_ref, dga_ref, dgb_ref, dbg_ref):
        @pl.when(pl.program_id(0) == 0)
        def _():
            dbg_ref[...] = jnp.zeros_like(dbg_ref)

        dm = dm_ref[...].astype(F32)
        gate_a = _sigmoid(ga_ref[...].astype(F32) + ba_ref[...])
        gate_b = _sigmoid(gb_ref[...].astype(F32) + bb_ref[...])
        dya_ref[...] = (dm * gate_a).astype(BF16)
        dyb_ref[...] = (dm * gate_b).astype(BF16)
        dga = dm * ya_ref[...].astype(F32) * gate_a * (1.0 - gate_a)
        dgb = dm * yb_ref[...].astype(F32) * gate_b * (1.0 - gate_b)
        dga_ref[...] = dga.astype(BF16)
        dgb_ref[...] = dgb.astype(BF16)
        dbg_ref[:, pl.ds(0, D)] += jnp.sum(dga, axis=0, keepdims=True)
        dbg_ref[:, pl.ds(D, D)] += jnp.sum(dgb, axis=0, keepdims=True)

    blk = pl.BlockSpec((tm, D), lambda i: (i, 0))
    return pl.pallas_call(
        body, name=name, grid=(m // tm,),
        in_specs=[blk, pl.BlockSpec((tm, D), lambda i: (i, C_GA // D)),
                  pl.BlockSpec((tm, D), lambda i: (i, C_GB // D)), blk, blk,
                  pl.BlockSpec((1, D), lambda i: (0, 0)), pl.BlockSpec((1, D), lambda i: (0, 1))],
        out_specs=[blk, blk, blk, blk, pl.BlockSpec((1, 2 * D), lambda i: (0, 0))],
        out_shape=[jax.ShapeDtypeStruct((m, D), BF16)] * 4 + [jax.ShapeDtypeStruct((1, 2 * D), F32)],
        compiler_params=_params(("arbitrary",)))(dmrg, p, p, ya, yb, bg, bg)


def scale_bwd(dy1, y0, scale, *, name):
    m = y0.shape[0]
    tm = _ew_tile(m)

    def body(dy_ref, y0_ref, s_ref, o_ref, ds_ref):
        @pl.when(pl.program_id(0) == 0)
        def _():
            ds_ref[...] = jnp.zeros_like(ds_ref)

        dy = dy_ref[...].astype(F32)
        o_ref[...] = (dy * s_ref[...]).astype(BF16)
        ds_ref[...] += jnp.sum(dy * y0_ref[...].astype(F32), axis=0, keepdims=True)

    blk = pl.BlockSpec((tm, D), lambda i: (i, 0))
    vec = pl.BlockSpec((1, D), lambda i: (0, 0))
    return pl.pallas_call(
        body, name=name, grid=(m // tm,), in_specs=[blk, blk, vec], out_specs=[blk, vec],
        out_shape=[jax.ShapeDtypeStruct((m, D), BF16), jax.ShapeDtypeStruct((1, D), F32)],
        compiler_params=_params(("arbitrary",)))(dy1, y0, scale)


CONV_BLK = 1408
N_CONV_BLK = D_FF // CONV_BLK


def conv_act_fwd(up, cw, cb, *, name):
    m = up.shape[0]
    tm = _ew_tile(m)

    def conv(x_ref, halo, w_ref, b_ref):
        xv = x_ref[...].astype(F32)
        xx = jnp.concatenate([halo[...], xv], axis=0)
        y = (w_ref[2:3, :] * xx + w_ref[1:2, :] * _shift_down(xx, 1)
             + w_ref[0:1, :] * _shift_down(xx, 2))[HALO:] + b_ref[...]
        halo[...] = xv[tm - HALO:]
        return y

    def body(xa_ref, xb_ref, wa_ref, wb_ref, ba_ref, bb_ref, upc_a_ref, upc_b_ref, act_ref, halo_a, halo_b):
        @pl.when(pl.program_id(1) == 0)
        def _():
            halo_a[...] = jnp.zeros_like(halo_a)
            halo_b[...] = jnp.zeros_like(halo_b)

        a = conv(xa_ref, halo_a, wa_ref, ba_ref)
        bv = conv(xb_ref, halo_b, wb_ref, bb_ref)
        upc_a_ref[...] = a.astype(BF16)
        upc_b_ref[...] = bv.astype(BF16)
        act_ref[...] = (a * _sigmoid(a) * bv).astype(BF16)

    nb = N_CONV_BLK
    xa = pl.BlockSpec((tm, CONV_BLK), lambda j, i: (i, j))
    xb = pl.BlockSpec((tm, CONV_BLK), lambda j, i: (i, j + nb))
    return pl.pallas_call(
        body, name=name, grid=(nb, m // tm),
        in_specs=[xa, xb,
                  pl.BlockSpec((3, CONV_BLK), lambda j, i: (0, j)),
                  pl.BlockSpec((3, CONV_BLK), lambda j, i: (0, j + nb)),
                  pl.BlockSpec((1, CONV_BLK), lambda j, i: (0, j)),
                  pl.BlockSpec((1, CONV_BLK), lambda j, i: (0, j + nb))],
        out_specs=[xa, xa, xa],
        out_shape=[jax.ShapeDtypeStruct((m, D_FF), BF16)] * 3,
        scratch_shapes=[pltpu.VMEM((HALO, CONV_BLK), F32)] * 2,
        compiler_params=_params(("parallel", "arbitrary")))(up, up, cw, cw, cb, cb)


def conv_act_bwd(dact, upc_a, upc_b, up, cw, *, name):
    m = up.shape[0]
    tm = _ew_tile(m)
    nt = m // tm

    def conv_t(d, halo, x_ref, w_ref, dup_ref, dw_ref, db_ref):
        xx = jnp.concatenate([d, halo[...]], axis=0)
        d1 = _shift_up(xx, 1)[:tm]
        d2 = _shift_up(xx, 2)[:tm]
        dup_ref[...] = (w_ref[2:3, :] * d + w_ref[1:2, :] * d1 + w_ref[0:1, :] * d2).astype(BF16)
        xv = x_ref[...].astype(F32)
        dw_ref[2:3, :] += jnp.sum(xv * d, axis=0, keepdims=True)
        dw_ref[1:2, :] += jnp.sum(xv * d1, axis=0, keepdims=True)
        dw_ref[0:1, :] += jnp.sum(xv * d2, axis=0, keepdims=True)
        db_ref[...] += jnp.sum(d, axis=0, keepdims=True)
        halo[...] = d[:HALO]

    def body(da_ref, a_ref, b_ref, xa_ref, xb_ref, wa_ref, wb_ref,
             dupa_ref, dupb_ref, dwa_ref, dwb_ref, dba_ref, dbb_ref, halo_a, halo_b):
        @pl.when(pl.program_id(1) == 0)
        def _():
            for r in (halo_a, halo_b, dwa_ref, dwb_ref, dba_ref, dbb_ref):
                r[...] = jnp.zeros_like(r)

        dact_v = da_ref[...].astype(F32)
        a = a_ref[...].astype(F32)
        bv = b_ref[...].astype(F32)
        sg = _sigmoid(a)
        d_a = dact_v * bv * sg * (1.0 + a * (1.0 - sg))
        d_b = dact_v * a * sg
        conv_t(d_a, halo_a, xa_ref, wa_ref, dupa_ref, dwa_ref, dba_ref)
        conv_t(d_b, halo_b, xb_ref, wb_ref, dupb_ref, dwb_ref, dbb_ref)

    nb = N_CONV_BLK
    rev = lambda i: nt - 1 - i
    half = pl.BlockSpec((tm, CONV_BLK), lambda j, i: (rev(i), j))
    xa = half
    xb = pl.BlockSpec((tm, CONV_BLK), lambda j, i: (rev(i), j + nb))
    wa = pl.BlockSpec((3, CONV_BLK), lambda j, i: (0, j))
    wb = pl.BlockSpec((3, CONV_BLK), lambda j, i: (0, j + nb))
    va = pl.BlockSpec((1, CONV_BLK), lambda j, i: (0, j))
    outs = pl.pallas_call(
        body, name=name, grid=(nb, nt),
        in_specs=[half, half, half, xa, xb, wa, wb],
        out_specs=[half, half, wa, wa, va, va],
        out_shape=[jax.ShapeDtypeStruct((m, D_FF), BF16)] * 2
                  + [jax.ShapeDtypeStruct((3, D_FF), F32)] * 2
                  + [jax.ShapeDtypeStruct((1, D_FF), F32)] * 2,
        scratch_shapes=[pltpu.VMEM((HALO, CONV_BLK), F32)] * 2,
        compiler_params=_params(("parallel", "arbitrary")))(dact, upc_a, upc_b, up, up, cw, cw)
    return outs


def local_step(x, target, w):
    seq = x.shape[0]
    h = jnp.concatenate([jnp.zeros((PAD, D), F32), w["meta"], x], axis=0)
    saved = []
    for l in range(DEPTH):
        wl = {k: (v[l:l + 1] if k in ROW_PARAMS else v[l]) for k, v in w.items() if k not in ("meta", "final_norm_g")}
        s = {"h": h}
        fwd_in(s, wl, f"l{l}_")
        fwd_mixer(s, wl, f"l{l}_")
        fwd_ffn(s, wl, f"l{l}_")
        saved.append(s)
        h = s["h3"]

    dh, dgf, loss_rows = loss_head(h, w["final_norm_g"], target, name="loss_head")
    g = {"final_norm_g": dgf}
    per_layer = []
    for l in reversed(range(DEPTH)):
        wl = {k: (v[l:l + 1] if k in ROW_PARAMS else v[l]) for k, v in w.items() if k not in ("meta", "final_norm_g")}
        s = saved[l]
        gl = {}
        dh2 = bwd_ffn(dh, dh.astype(BF16), s, wl, gl, f"l{l}_")
        dp = bwd_mixer(dh2, dh2.astype(BF16), s, wl, gl, f"l{l}_")
        gl["w_in"] = bwd_in_w(dp, s, f"l{l}_")
        dh = bwd_in_x(dp, dh2, s, wl, gl, f"l{l}_")
        per_layer.append(gl)
    per_layer.reverse()
    for k in per_layer[0]:
        g[k] = jnp.stack([per_layer[l][k].astype(F32) for l in range(DEPTH)])
    g["meta"] = dh[PAD:X0]
    return loss_rows, dh[X0:X0 + seq], g


ROW_PARAMS = ("norm1_g", "b_gk", "gla_norm_g", "pool_scale", "b_gates", "norm2_g", "conv_b")


def fwd_in(s, w, ln):
    s["hn1"] = rmsnorm_fwd(s["h"], w["norm1_g"], name=ln + "norm1")
    s["p"] = mm_nn(s["hn1"], w["w_in"], tn=896, name=ln + "in_proj")


def fwd_mixer(s, w, ln):
    p = s["p"]
    s["o"], s["st"] = gla_fwd(p, w["w_gk"], w["b_gk"], name=ln + "gla_fwd")
    s["ya_in"], s["pooled"] = mix_pre(s["o"], p, w["gla_norm_g"], name=ln + "mix_pre")
    s["ya"] = mm_nn(s["ya_in"], w["w_a"], name=ln + "proj_a")
    s["yb0"], s["yb1"] = pool_mm_fwd(s["pooled"], w["w_pool"], w["pool_scale"], name=ln + "pool_mm")
    s["yb"] = mm_nn(s["yb1"], w["w_b"], name=ln + "proj_b")
    s["mrg"] = merge_fwd(p, s["ya"], s["yb"], w["b_gates"], name=ln + "merge")
    s["h2"] = mm_nn(s["mrg"], w["w_o"], out_dtype=F32, res=s["h"], name=ln + "proj_o")


def fwd_ffn(s, w, ln):
    s["hn2"] = rmsnorm_fwd(s["h2"], w["norm2_g"], name=ln + "norm2")
    s["up"] = mm_nn(s["hn2"], w["w_up"], tn=1408, name=ln + "up_proj")
    s["upc_a"], s["upc_b"], s["act"] = conv_act_fwd(s["up"], w["conv_w"], w["conv_b"], name=ln + "conv_act")
    s["h3"] = mm_nn(s["act"], w["w_down"], out_dtype=F32, res=s["h2"], name=ln + "down_proj")


def bwd_ffn(dh, dh_b, s, w, g, ln):
    dact = mm_nt(dh_b, w["w_down"], tn=1408, name=ln + "d_act")
    g["w_down"] = mm_tn(s["act"], dh_b, tk1=1408, out_dtype=BF16, name=ln + "dw_down")
    dup_a, dup_b, dcw_a, dcw_b, dcb_a, dcb_b = conv_act_bwd(
        dact, s["upc_a"], s["upc_b"], s["up"], w["conv_w"], name=ln + "conv_act_bwd")
    dup = jnp.concatenate([dup_a, dup_b], axis=1)
    dhn2 = mm_nt(dup, w["w_up"], out_dtype=F32, tk=1408, name=ln + "d_hn2")
    g["w_up"] = mm_tn(s["hn2"], dup, tn=1408, out_dtype=BF16, name=ln + "dw_up")
    dh2, g["norm2_g"] = rmsnorm_bwd(dhn2, s["h2"], w["norm2_g"], dh, name=ln + "norm2_bwd")
    g["conv_w"] = jnp.concatenate([dcw_a, dcw_b], axis=1)
    g["conv_b"] = jnp.concatenate([dcb_a, dcb_b], axis=1)
    return dh2


def bwd_mixer(dh2, dh2_b, s, w, g, ln):
    dmrg = mm_nt(dh2_b, w["w_o"], name=ln + "d_mrg")
    g["w_o"] = mm_tn(s["mrg"], dh2_b, out_dtype=BF16, name=ln + "dw_o")
    dya, dyb, dga, dgb, g["b_gates"] = merge_bwd(dmrg, s["p"], s["ya"], s["yb"], w["b_gates"], name=ln + "merge_bwd")
    dya_in = mm_nt(dya, w["w_a"], name=ln + "d_ya_in")
    g["w_a"] = mm_tn(s["ya_in"], dya, out_dtype=BF16, name=ln + "dw_a")
    dyb1 = mm_nt(dyb, w["w_b"], name=ln + "d_yb1")
    g["w_b"] = mm_tn(s["yb1"], dyb, out_dtype=BF16, name=ln + "dw_b")
    dyb0, g["pool_scale"] = scale_bwd(dyb1, s["yb0"], w["pool_scale"], name=ln + "scale_bwd")
    dpooled = pool_mm_bwd_x(dyb0, w["w_pool"], name=ln + "d_pooled")
    g["w_pool"] = pool_mm_bwd_w(s["pooled"], dyb0, name=ln + "dw_pool")
    do, dr, du, g["gla_norm_g"] = mix_pre_bwd(dya_in, dpooled, s["o"], s["p"], w["gla_norm_g"],
                                              name=ln + "mix_pre_bwd")
    dqkv, dglr, g["w_gk"], g["b_gk"] = gla_bwd(s["p"], w["w_gk"], w["b_gk"], s["st"], do, name=ln + "gla_bwd")
    return jnp.concatenate([dqkv, dr, du, dga, dgb, dglr], axis=1)


def bwd_in_w(dp, s, ln):
    return mm_tn(s["hn1"], dp, tn=896, out_dtype=BF16, name=ln + "dw_in")


def bwd_in_x(dp, dh2, s, w, g, ln, after=None):
    dhn1 = mm_nt(dp, w["w_in"], out_dtype=F32, tk=896, after=after, name=ln + "d_hn1")
    dh, g["norm1_g"] = rmsnorm_bwd(dhn1, s["h"], w["norm1_g"], dh2, name=ln + "norm1_bwd")
    return dh


def _my_place():
    return lax.axis_index("x"), lax.axis_index("y"), lax.axis_index("c")


def _peer(place, k):
    x, y, c = place
    return (1 - x if k & 4 else x, 1 - y if k & 2 else y, 1 - c if k & 1 else c)


def _index(place):
    x, y, c = place
    return 4 * x + 2 * y + c


def exchange(arrays, kinds, *, name):
    n = len(arrays)

    def body(*refs):
        ins, outs = refs[:n], refs[n:2 * n]
        send_sems, recv_sems, local_sems = refs[2 * n:]
        place = _my_place()
        me = _index(place)

        def src(a, dest):
            return ins[a] if kinds[a] == "gather" else ins[a].at[dest]

        def remote(a, k):
            peer = _peer(place, k)
            return pltpu.make_async_remote_copy(
                src_ref=src(a, _index(peer)), dst_ref=outs[a].at[me],
                send_sem=send_sems.at[a, k - 1], recv_sem=recv_sems.at[a, k - 1],
                device_id=peer, device_id_type=pl.DeviceIdType.MESH)

        def arrival(a, k):
            peer = _peer(place, k)
            return pltpu.make_async_remote_copy(
                src_ref=src(a, me), dst_ref=outs[a].at[_index(peer)],
                send_sem=send_sems.at[a, k - 1], recv_sem=recv_sems.at[a, k - 1],
                device_id=peer, device_id_type=pl.DeviceIdType.MESH)

        own = [pltpu.make_async_copy(src(a, me), outs[a].at[me], local_sems.at[a]) for a in range(n)]
        sends = [remote(a, k) for k in range(1, N_DEV) for a in range(n)]
        for cp in sends:
            cp.start()
        for cp in own:
            cp.start()
        for k in range(1, N_DEV):
            for a in range(n):
                arrival(a, k).wait_recv()
        for cp in sends:
            cp.wait_send()
        for cp in own:
            cp.wait()

    any_spec = pl.BlockSpec(memory_space=pl.ANY)
    out_shape = []
    for arr, kind in zip(arrays, kinds):
        shape = arr.shape if kind == "gather" else arr.shape[1:]
        out_shape.append(jax.ShapeDtypeStruct((N_DEV,) + tuple(shape), arr.dtype))
    return pl.pallas_call(
        body, name=name, in_specs=[any_spec] * n, out_specs=[any_spec] * n, out_shape=out_shape,
        scratch_shapes=[pltpu.SemaphoreType.DMA((n, N_DEV - 1)), pltpu.SemaphoreType.DMA((n, N_DEV - 1)),
                        pltpu.SemaphoreType.DMA((n,))],
    )(*arrays)


def _sem_slot(a, k):
    return a * (N_DEV - 1) + k - 1


_HBM = pl.BlockSpec(memory_space=pltpu.HBM)
_SEM = pl.BlockSpec(memory_space=pltpu.SEMAPHORE)
_DATAFLOW = pltpu.SideEffectType.DATAFLOW_SIDE_EFFECTING


def exchange_start(arrays, kinds, after, *, name):
    n = len(arrays)
    zones = []
    for arr, kind in zip(arrays, kinds):
        shape = arr.shape if kind == "gather" else arr.shape[1:]
        zones.append(lax.empty((N_DEV,) + tuple(shape), arr.dtype))

    def body(*refs):
        ins, lands = refs[:n], refs[n:2 * n]
        send_sems, recv_sems = refs[2 * n + 1], refs[2 * n + 2]
        token, local_sems = refs[4 * n + 3], refs[4 * n + 4]
        place = _my_place()
        me = _index(place)

        def src(a, dest):
            return ins[a] if kinds[a] == "gather" else ins[a].at[dest]

        own = [pltpu.make_async_copy(src(a, me), lands[a].at[me], local_sems.at[a]) for a in range(n)]
        for cp in own:
            cp.start()
        for a in range(n):
            for k in range(1, N_DEV):
                peer = _peer(place, k)
                pltpu.make_async_remote_copy(
                    src_ref=src(a, _index(peer)), dst_ref=lands[a].at[me],
                    send_sem=send_sems.at[_sem_slot(a, k)], recv_sem=recv_sems.at[_sem_slot(a, k)],
                    device_id=peer, device_id_type=pl.DeviceIdType.MESH).start()
        for cp in own:
            cp.wait()
        token[...] = jnp.zeros_like(token)

    sems = pltpu.SemaphoreType.DMA((n * (N_DEV - 1),))
    hbm = lambda a: pltpu.HBM(a.shape, a.dtype)
    outs = pl.pallas_call(
        body, name=name,
        out_shape=(sems, sems, *[hbm(a) for a in arrays], *[hbm(z) for z in zones],
                   jax.ShapeDtypeStruct((8, LANES), F32)),
        in_specs=[_HBM] * (2 * n) + [pl.BlockSpec(memory_space=pl.ANY)],
        out_specs=(_SEM, _SEM, *[_HBM] * (2 * n), pl.BlockSpec(memory_space=pltpu.VMEM)),
        input_output_aliases={i: 2 + i for i in range(2 * n)},
        scratch_shapes=[pltpu.SemaphoreType.DMA((n,))],
        compiler_params=pltpu.CompilerParams(has_side_effects=_DATAFLOW),
    )(*[pltpu.with_memory_space_constraint(a, pltpu.HBM) for a in arrays],
      *[pltpu.with_memory_space_constraint(z, pltpu.HBM) for z in zones], after)
    return dict(send=outs[0], recv=outs[1], srcs=outs[2:2 + n], zones=outs[2 + n:2 + 2 * n],
                token=outs[2 + 2 * n], kinds=kinds)


def exchange_wait(handle, after, *, name):
    kinds = handle["kinds"]
    n = len(kinds)

    def body(*refs):
        ins, lands = refs[:n], refs[n:2 * n]
        send_sems, recv_sems = refs[2 * n], refs[2 * n + 1]
        place = _my_place()
        me = _index(place)
        for a in range(n):
            for k in range(1, N_DEV):
                peer = _peer(place, k)
                src = ins[a] if kinds[a] == "gather" else ins[a].at[_index(peer)]
                copy = pltpu.make_async_remote_copy(
                    src_ref=src, dst_ref=lands[a].at[_index(peer)],
                    send_sem=send_sems.at[_sem_slot(a, k)], recv_sem=recv_sems.at[_sem_slot(a, k)],
                    device_id=peer, device_id_type=pl.DeviceIdType.MESH)
                copy.wait_send()
                copy.wait_recv()

    srcs, zones = handle["srcs"], handle["zones"]
    hbm = lambda a: pltpu.HBM(a.shape, a.dtype)
    outs = pl.pallas_call(
        body, name=name,
        out_shape=(*[hbm(a) for a in srcs], *[hbm(z) for z in zones]),
        in_specs=[_HBM] * (2 * n) + [_SEM, _SEM, pl.BlockSpec(memory_space=pl.ANY)],
        out_specs=[_HBM] * (2 * n),
        input_output_aliases={i: i for i in range(2 * n)},
        compiler_params=pltpu.CompilerParams(has_side_effects=_DATAFLOW),
    )(*srcs, *zones, handle["send"], handle["recv"], after)
    return outs[n:]


def reduce_adam_layer(parts, w, m, v, layer, prev, *, name):
    _, r, c = w.shape
    tr = _pick(r, (256, 352, 128))

    def body(*refs):
        p_ref, w_ref, m_ref, v_ref = refs[:4]
        g_ref, d_ref, m2_ref, v2_ref = refs[-4:]
        g = p_ref[0].astype(F32)
        for i in range(1, N_DEV):
            g = g + p_ref[i].astype(F32)
        m2 = B1 * m_ref[...] + (1.0 - B1) * g
        v2 = B2 * v_ref[...] + (1.0 - B2) * (g * g)
        m_hat = m2 / (1.0 - B1 ** STEP)
        v_hat = v2 / (1.0 - B2 ** STEP)
        g_ref[...] = g
        d_ref[...] = -LR * (m_hat / (jnp.sqrt(v_hat) + ADAM_EPS) + WD * w_ref[...])
        m2_ref[...] = m2
        v2_ref[...] = v2

    blk = pl.BlockSpec((None, tr, c), lambda i: (layer, i, 0))
    in_specs = [pl.BlockSpec((N_DEV, tr, c), lambda i: (0, i, 0)), blk, blk, blk]
    args = [parts, w, m, v]
    aliases = {}
    if prev is not None:
        in_specs += [pl.BlockSpec(memory_space=pl.ANY)] * 4
        args += list(prev)
        aliases = {4 + j: j for j in range(4)}
    return pl.pallas_call(
        body, name=name, grid=(r // tr,), in_specs=in_specs, out_specs=[blk] * 4,
        out_shape=[jax.ShapeDtypeStruct(w.shape, F32)] * 4, input_output_aliases=aliases,
        compiler_params=_params(("parallel",)))(*args)


def reduce_adam(parts, w, m, v, *, name):
    r, c = w.shape
    tr = _pick(r, (256, 352, 192, 128, 72, 64, 32, 16, 8))

    def body(p_ref, w_ref, m_ref, v_ref, g_ref, d_ref, m2_ref, v2_ref):
        g = p_ref[0].astype(F32)
        for i in range(1, N_DEV):
            g = g + p_ref[i].astype(F32)
        wv = w_ref[...]
        m2 = B1 * m_ref[...] + (1.0 - B1) * g
        v2 = B2 * v_ref[...] + (1.0 - B2) * (g * g)
        m_hat = m2 / (1.0 - B1 ** STEP)
        v_hat = v2 / (1.0 - B2 ** STEP)
        g_ref[...] = g
        d_ref[...] = -LR * (m_hat / (jnp.sqrt(v_hat) + ADAM_EPS) + WD * wv)
        m2_ref[...] = m2
        v2_ref[...] = v2

    blk = pl.BlockSpec((tr, c), lambda i: (i, 0))
    return pl.pallas_call(
        body, name=name, grid=(r // tr,),
        in_specs=[pl.BlockSpec((N_DEV, tr, c), lambda i: (0, i, 0)), blk, blk, blk],
        out_specs=[blk] * 4, out_shape=[jax.ShapeDtypeStruct((r, c), F32)] * 4,
        compiler_params=_params(("parallel",)))(parts, w, m, v)


BIG = ("w_in", "w_a", "w_pool_grp", "w_b", "w_o", "w_up", "w_down")
SHARDED_SMALL = ("meta_tokens", "w_gk", "conv_w")
REPLICATED = ("norm1_g", "b_gk", "gla_norm_g", "pool_scale", "b_gates", "norm2_g", "conv_b", "final_norm_g")
CUT_AXIS = {"w_in": 2, "w_a": 1, "w_pool_grp": 2, "w_b": 1, "w_o": 1, "w_up": 2, "w_down": 1,
            "meta_tokens": 1, "w_gk": 2, "conv_w": 2}
WEIGHTS = ("meta_tokens", "norm1_g", "w_in", "w_gk", "b_gk", "gla_norm_g", "w_a", "w_pool_grp", "pool_scale",
           "w_b", "b_gates", "w_o", "norm2_g", "w_up", "conv_w", "conv_b", "w_down", "final_norm_g")


def _as_2d(a):
    return a.reshape(-1, a.shape[-1])


def _from_slots(slots, axis):
    full = jnp.moveaxis(slots, 0, axis)
    shape = list(full.shape)
    shape[axis:axis + 2] = [shape[axis] * shape[axis + 1]]
    return full.reshape(shape)


def _to_slots(full, axis):
    shape = list(full.shape)
    shape[axis:axis + 1] = [N_DEV, shape[axis] // N_DEV]
    return jnp.moveaxis(full.reshape(shape), axis, 0)


def _pack(vectors, rows):
    flat = jnp.concatenate([v.reshape(-1).astype(F32) for v in vectors])
    return jnp.pad(flat, (0, rows * LANES - flat.shape[0])).reshape(rows, LANES)


def _unpack(packed, shapes):
    flat = packed.reshape(-1)
    out, off = [], 0
    for s in shapes:
        size = 1
        for d in s:
            size *= d
        out.append(flat[off:off + size].reshape(s))
        off += size
    return out


def _rows_for(shapes, mult=8):
    total = 0
    for s in shapes:
        size = 1
        for d in s:
            size *= d
        total += size
    rows = -(-total // LANES)
    return -(-rows // mult) * mult


def _permute_in(w_in):
    pad = jnp.zeros(w_in.shape[:-1] + (IN_R - IN_WIDTH,), w_in.dtype)
    return jnp.concatenate([w_in[..., :2048], w_in[..., 2064:], w_in[..., 2048:2064], pad], axis=-1)


def _unpermute_in(w_r):
    return jnp.concatenate([w_r[..., :2048], w_r[..., C_GLR:C_GLR + RANK], w_r[..., 2048:C_GLR]], axis=-1)


def kernel(x, meta_tokens, norm1_g, w_in, w_gk, b_gk, gla_norm_g, w_a, w_pool_grp, pool_scale, w_b, b_gates, w_o, norm2_g, w_up, conv_w, conv_b, w_down, final_norm_g, loss_target, m_meta_tokens, m_norm1_g, m_w_in, m_w_gk, m_b_gk, m_gla_norm_g, m_w_a, m_w_pool_grp, m_pool_scale, m_w_b, m_b_gates, m_w_o, m_norm2_g, m_w_up, m_conv_w, m_conv_b, m_w_down, m_final_norm_g, v_meta_tokens, v_norm1_g, v_w_in, v_w_gk, v_b_gk, v_gla_norm_g, v_w_a, v_w_pool_grp, v_pool_scale, v_w_b, v_b_gates, v_w_o, v_norm2_g, v_w_up, v_conv_w, v_conv_b, v_w_down, v_final_norm_g):
    wts = dict(meta_tokens=meta_tokens, norm1_g=norm1_g, w_in=w_in, w_gk=w_gk, b_gk=b_gk, gla_norm_g=gla_norm_g,
               w_a=w_a, w_pool_grp=w_pool_grp, pool_scale=pool_scale, w_b=w_b, b_gates=b_gates, w_o=w_o,
               norm2_g=norm2_g, w_up=w_up, conv_w=conv_w, conv_b=conv_b, w_down=w_down, final_norm_g=final_norm_g)
    mom = dict(meta_tokens=m_meta_tokens, norm1_g=m_norm1_g, w_in=m_w_in, w_gk=m_w_gk, b_gk=m_b_gk,
               gla_norm_g=m_gla_norm_g, w_a=m_w_a, w_pool_grp=m_w_pool_grp, pool_scale=m_pool_scale, w_b=m_w_b,
               b_gates=m_b_gates, w_o=m_w_o, norm2_g=m_norm2_g, w_up=m_w_up, conv_w=m_conv_w, conv_b=m_conv_b,
               w_down=m_w_down, final_norm_g=m_final_norm_g)
    var = dict(meta_tokens=v_meta_tokens, norm1_g=v_norm1_g, w_in=v_w_in, w_gk=v_w_gk, b_gk=v_b_gk,
               gla_norm_g=v_gla_norm_g, w_a=v_w_a, w_pool_grp=v_w_pool_grp, pool_scale=v_pool_scale, w_b=v_w_b,
               b_gates=v_b_gates, w_o=v_w_o, norm2_g=v_norm2_g, w_up=v_w_up, conv_w=v_conv_w, conv_b=v_conv_b,
               w_down=v_w_down, final_norm_g=v_final_norm_g)

    small_shapes = [wts[n].shape for n in SHARDED_SMALL]
    small_rows = _rows_for(small_shapes)

    def shard3(a):
        return a.reshape(DEPTH, -1, a.shape[-1])

    def layer_shards(l, names):
        return [shard3(wts[n])[l].astype(BF16) for n in names]

    def tie(row, handle):
        return row + handle["token"][0:1, 0:1]

    def full_weight(n, zone):
        if n == "w_in":
            return _permute_in(jnp.moveaxis(zone, 0, 1).reshape(D, IN_WIDTH))
        if n == "w_up":
            return jnp.moveaxis(zone, 0, 1).reshape(D, F2)
        if n == "w_pool_grp":
            return jnp.moveaxis(zone.reshape(N_DEV, GROUPS, GDIM // N_DEV, GDIM), 0, 1).reshape(GROUPS, GDIM, GDIM)
        return zone.reshape(-1, zone.shape[-1])

    groups = [("w_in",), ("w_a", "w_pool_grp", "w_b", "w_o"), ("w_up", "w_down")]
    rest = groups[0] + groups[1]
    key = {"w_pool_grp": "w_pool"}
    rows = dict(norm1_g=norm1_g, b_gk=b_gk, gla_norm_g=gla_norm_g, pool_scale=pool_scale, b_gates=b_gates,
                norm2_g=norm2_g, conv_b=conv_b)

    def gather(l, names, after, name, head=()):
        return exchange_start(list(head) + layer_shards(l, names), ["gather"] * (len(head) + len(names)), after,
                              name=name + "_start")

    def landed(handle, after, name, names, w_layer):
        zones = exchange_wait(handle, after, name=name + "_wait")
        for n, z in zip(names, zones[len(zones) - len(names):]):
            w_layer[key.get(n, n)] = full_weight(n, z)
        return zones

    wl = [{n: v[l:l + 1] for n, v in rows.items()} for l in range(DEPTH)]
    g_in0 = gather(0, groups[0], x, "gather_in0", head=[_pack([wts[n] for n in SHARDED_SMALL], small_rows)])
    zones = landed(g_in0, g_in0["token"], "gather_in0", groups[0], wl[0])
    small_slots = [jnp.stack(parts) for parts in zip(*[_unpack(zones[0][i], small_shapes) for i in range(N_DEV)])]
    small_full = {n: _from_slots(slots, CUT_AXIS[n]) for n, slots in zip(SHARDED_SMALL, small_slots)}
    w_gk_pad = jnp.pad(small_full["w_gk"], ((0, 0), (0, LANES - RANK), (0, 0))).astype(BF16)
    for l in range(DEPTH):
        wl[l]["w_gk"] = w_gk_pad[l]
        wl[l]["conv_w"] = small_full["conv_w"][l]
    g_mix0 = gather(0, groups[1], zones[1], "gather_mix0")
    g_ffn0 = gather(0, groups[2], g_mix0["token"], "gather_ffn0")
    wl[0]["norm1_g"] = tie(wl[0]["norm1_g"], g_ffn0)

    h = jnp.concatenate([jnp.zeros((PAD, D), F32), small_full["meta_tokens"], x[0]], axis=0)
    s0 = {"h": h}
    fwd_in(s0, wl[0], "l0_")
    zones = landed(g_mix0, s0["p"], "gather_mix0", groups[1], wl[0])
    g_in1 = gather(1, groups[0], zones[0], "gather_in1")
    wl[0]["b_gk"] = tie(wl[0]["b_gk"], g_in1)
    fwd_mixer(s0, wl[0], "l0_")
    zones = landed(g_ffn0, s0["h2"], "gather_ffn0", groups[2], wl[0])
    g_mix1 = gather(1, groups[1], zones[0], "gather_mix1")
    g_ffn1 = gather(1, groups[2], g_mix1["token"], "gather_ffn1")
    wl[0]["norm2_g"] = tie(wl[0]["norm2_g"], g_ffn1)
    fwd_ffn(s0, wl[0], "l0_")
    landed(g_in1, s0["h3"], "gather_in1", groups[0], wl[1])
    s1 = {"h": s0["h3"]}
    fwd_in(s1, wl[1], "l1_")
    landed(g_mix1, s1["p"], "gather_mix1", groups[1], wl[1])
    fwd_mixer(s1, wl[1], "l1_")
    landed(g_ffn1, s1["h2"], "gather_ffn1", groups[2], wl[1])
    fwd_ffn(s1, wl[1], "l1_")
    dh, dgf, loss_rows = loss_head(s1["h3"], final_norm_g[None], loss_target[0], name="loss_head")
    loss_part = 0.5 * jnp.sum(loss_rows) / D

    def blocks(n, gw):
        if n == "w_in":
            return jnp.moveaxis(_unpermute_in(gw).reshape(D, N_DEV, IN_WIDTH // N_DEV), 1, 0)
        if n == "w_up":
            return jnp.moveaxis(gw.reshape(D, N_DEV, F2 // N_DEV), 1, 0)
        if n == "w_pool_grp":
            gw = gw.astype(BF16).reshape(GROUPS, N_DEV, GDIM // N_DEV, GDIM)
            return jnp.moveaxis(gw, 1, 0).reshape(N_DEV, GROUPS * GDIM // N_DEV, GDIM)
        return gw.reshape(N_DEV, gw.shape[0] // N_DEV, gw.shape[1])

    def scatter(g, names, after, name):
        return exchange_start([blocks(n, g[key.get(n, n)]) for n in names], ["scatter"] * len(names), after,
                              name=name + "_start")

    g1, g0 = {}, {}
    dh2 = bwd_ffn(dh, dh.astype(BF16), s1, wl[1], g1, "l1_")
    s_ffn1 = scatter(g1, groups[2], dh2, "scatter_ffn1")
    dp = bwd_mixer(dh2, (dh2 + s_ffn1["token"][0, 0]).astype(BF16), s1, wl[1], g1, "l1_")
    g1["w_in"] = bwd_in_w(dp, s1, "l1_")
    s_rest1 = scatter(g1, rest, s_ffn1["token"], "scatter_rest1")
    dh = bwd_in_x(dp, dh2, s1, wl[1], g1, "l1_", after=s_rest1["token"])
    dh2 = bwd_ffn(dh, dh.astype(BF16), s0, wl[0], g0, "l0_")
    r_ffn1 = exchange_wait(s_ffn1, dh2, name="scatter_ffn1_wait")
    s_ffn0 = scatter(g0, groups[2], r_ffn1[0], "scatter_ffn0")
    dp = bwd_mixer(dh2, (dh2 + s_ffn0["token"][0, 0]).astype(BF16), s0, wl[0], g0, "l0_")
    r_rest1 = exchange_wait(s_rest1, dp, name="scatter_rest1_wait")
    g0["w_in"] = bwd_in_w(dp, s0, "l0_")
    s_rest0 = scatter(g0, rest, r_rest1[0], "scatter_rest0")
    dh = bwd_in_x(dp, dh2, s0, wl[0], g0, "l0_", after=s_rest0["token"])
    r_ffn0 = exchange_wait(s_ffn0, dh, name="scatter_ffn0_wait")
    r_rest0 = exchange_wait(s_rest0, r_ffn0[0], name="scatter_rest0_wait")
    grad_x = dh[X0:]
    recv = [dict(zip(groups[2] + rest, list(r_ffn0) + list(r_rest0))),
            dict(zip(groups[2] + rest, list(r_ffn1) + list(r_rest1)))]

    grads, delta, new_m, new_v = {}, {}, {}, {}
    for n in BIG:
        w3, m3, v3 = shard3(wts[n]), shard3(mom[n]), shard3(var[n])
        first = reduce_adam_layer(recv[1][n], w3, m3, v3, 1, None, name="adam_l1_" + n)
        outs = reduce_adam_layer(recv[0][n], w3, m3, v3, 0, first, name="adam_l0_" + n)
        grads[n], delta[n], new_m[n], new_v[n] = [o.reshape(wts[n].shape) for o in outs]

    g_full = {n: jnp.stack([g0[n], g1[n]])[:, 0] for n in rows}
    g_full["final_norm_g"] = dgf[0]
    g_full["meta_tokens"] = dh[PAD:X0]
    g_full["w_gk"] = jnp.stack([g0["w_gk"], g1["w_gk"]])[:, :RANK]
    g_full["conv_w"] = jnp.stack([g0["conv_w"], g1["conv_w"]])
    rep_shapes = [wts[n].shape for n in REPLICATED] + [(1,)]
    rep_rows = _rows_for(rep_shapes)
    small_blocks = jnp.stack([
        _pack([_to_slots(g_full[n], CUT_AXIS[n])[i] for n in SHARDED_SMALL], small_rows) for i in range(N_DEV)])
    rep_pack = _pack([g_full[n] for n in REPLICATED] + [loss_part.reshape(1)], rep_rows)
    received = exchange([small_blocks, rep_pack], ["scatter", "gather"], name="exchange_small")
    outs = reduce_adam(received[-2], _pack([wts[n] for n in SHARDED_SMALL], small_rows),
                       _pack([mom[n] for n in SHARDED_SMALL], small_rows),
                       _pack([var[n] for n in SHARDED_SMALL], small_rows), name="adam_small")
    for d, o in zip((grads, delta, new_m, new_v), outs):
        for n, a in zip(SHARDED_SMALL, _unpack(o, small_shapes)):
            d[n] = a
    one = [jnp.zeros((1,), F32)]
    outs = reduce_adam(received[-1], _pack([wts[n] for n in REPLICATED] + one, rep_rows),
                       _pack([mom[n] for n in REPLICATED] + one, rep_rows),
                       _pack([var[n] for n in REPLICATED] + one, rep_rows), name="adam_replicated")
    for d, o in zip((grads, delta, new_m, new_v), outs):
        for n, a in zip(REPLICATED + ("loss",), _unpack(o, rep_shapes)):
            d[n] = a
    loss = grads["loss"][0]
    return (loss, grad_x[None], *[grads[n] for n in WEIGHTS], *[delta[n] for n in WEIGHTS],
            *[new_m[n] for n in WEIGHTS], *[new_v[n] for n in WEIGHTS])
```

```python
import functools

import jax
import jax.numpy as jnp
from jax import lax
from jax.experimental import pallas as pl
from jax.experimental.pallas import tpu as pltpu

F32 = jnp.float32
BF16 = jnp.bfloat16

D = 1024
DEPTH = 2
N_META = 16
HEADS = 4
DK = 512
DV = 1024
HK = 128
HV = 256
RANK = 16
TAU = 16.0
CHUNK = 64
POOL_WINDOWS = (2, 4, 8, 16)
GROUPS = 4
GDIM = 256
D_FF = 2816
F2 = 2 * D_FF
EPS = 1e-6
IN_WIDTH = 6160
LR, B1, B2, ADAM_EPS, WD, STEP = 0.001, 0.9, 0.999, 1e-8, 0.01, 10

N_DEV = 8
PAD = CHUNK - N_META
X0 = CHUNK
IN_R = 6272
C_Q, C_K, C_V, C_R, C_U, C_GA, C_GB, C_GLR = 0, 512, 1024, 2048, 3072, 4096, 5120, 6144
VMEM_LIMIT = 56 * 1024 * 1024
LANES = 128


def _params(sem=None):
    return pltpu.CompilerParams(dimension_semantics=sem, vmem_limit_bytes=VMEM_LIMIT)


def _pick(n, prefs):
    for t in prefs:
        if n % t == 0:
            return t
    raise ValueError(f"no tile for {n} in {prefs}")


def _row_tile(lp):
    return _pick(lp, (688, 192, 128, 64))


def _ew_tile(lp):
    return _pick(lp, (192, 128, 64))


def _sigmoid(x):
    return 1.0 / (1.0 + jnp.exp(-x))


def _dot(a, b, dims):
    return lax.dot_general(a, b, (dims, ((), ())), preferred_element_type=F32)


def _nn(a, b):
    return _dot(a, b, ((1,), (0,)))


def _nt(a, b):
    return _dot(a, b, ((1,), (1,)))


def _tn(a, b):
    return _dot(a, b, ((0,), (0,)))


def mm_nn(a, b, *, out_dtype=BF16, tn=None, res=None, name):
    m, k = a.shape
    n = b.shape[1]
    tm = _row_tile(m)
    tn = tn or n
    has_res = res is not None

    def body(*refs):
        if has_res:
            a_ref, b_ref, r_ref, o_ref = refs
        else:
            a_ref, b_ref, o_ref = refs
        acc = _nn(a_ref[...], b_ref[...])
        if has_res:
            row = pl.program_id(1) * tm + lax.broadcasted_iota(jnp.int32, (tm, 1), 0)
            acc = jnp.where(row >= PAD, acc + r_ref[...], 0.0)
        o_ref[...] = acc.astype(o_ref.dtype)

    in_specs = [pl.BlockSpec((tm, k), lambda j, i: (i, 0)),
                pl.BlockSpec((k, tn), lambda j, i: (0, j))]
    args = [a, b]
    if has_res:
        in_specs.append(pl.BlockSpec((tm, tn), lambda j, i: (i, j)))
        args.append(res)
    return pl.pallas_call(
        body, name=name, grid=(n // tn, m // tm), in_specs=in_specs,
        out_specs=pl.BlockSpec((tm, tn), lambda j, i: (i, j)),
        out_shape=jax.ShapeDtypeStruct((m, n), out_dtype),
        compiler_params=_params(("parallel", "parallel")))(*args)


def mm_nt(a, b, *, out_dtype=BF16, tn=None, tk=None, after=None, halves=False, name):
    m, k = (a.shape[1], 2 * a.shape[2]) if halves else a.shape
    n = b.shape[0]
    tm = _row_tile(m)
    tn = tn or n
    tk = tk or k
    nk = k // tk
    extra = [] if after is None else [after]
    if halves:
        per = nk // 2
        a_spec = pl.BlockSpec((None, tm, tk), lambda j, i, kk: (kk // per, i, kk % per))
    else:
        a_spec = pl.BlockSpec((tm, tk), lambda j, i, kk: (i, kk))

    def body(a_ref, b_ref, *rest):
        o_ref, acc_ref = rest[-2:]
        kk = pl.program_id(2)
        part = _nt(a_ref[...], b_ref[...])

        @pl.when(kk == 0)
        def _():
            acc_ref[...] = part

        @pl.when(kk > 0)
        def _():
            acc_ref[...] += part

        @pl.when(kk == nk - 1)
        def _():
            o_ref[...] = acc_ref[...].astype(o_ref.dtype)

    return pl.pallas_call(
        body, name=name, grid=(n // tn, m // tm, nk),
        in_specs=[a_spec, pl.BlockSpec((tn, tk), lambda j, i, kk: (j, kk))]
                 + [pl.BlockSpec(memory_space=pl.ANY)] * len(extra),
        out_specs=pl.BlockSpec((tm, tn), lambda j, i, kk: (i, j)),
        out_shape=jax.ShapeDtypeStruct((m, n), out_dtype),
        scratch_shapes=[pltpu.VMEM((tm, tn), F32)],
        compiler_params=_params(("parallel", "parallel", "arbitrary")))(a, b, *extra)


def mm_tn(a, b, *, tk1=None, tn=None, out_dtype=F32, after=None, halves=False, name):
    m, k1 = a.shape
    n = 2 * b.shape[2] if halves else b.shape[1]
    tm = _row_tile(m)
    tk1 = tk1 or k1
    tn = tn or n
    nm = m // tm
    extra = [] if after is None else [after]
    if halves:
        per = n // tn // 2
        b_spec = pl.BlockSpec((None, tm, tn), lambda p, j, i: (j // per, i, j % per))
    else:
        b_spec = pl.BlockSpec((tm, tn), lambda p, j, i: (i, j))

    def body(a_ref, b_ref, *rest):
        o_ref, acc_ref = rest[-2:]
        i = pl.program_id(2)
        part = _tn(a_ref[...], b_ref[...])

        @pl.when(i == 0)
        def _():
            acc_ref[...] = part

        @pl.when(i > 0)
        def _():
            acc_ref[...] += part

        @pl.when(i == nm - 1)
        def _():
            o_ref[...] = acc_ref[...].astype(o_ref.dtype)

    return pl.pallas_call(
        body, name=name, grid=(k1 // tk1, n // tn, nm),
        in_specs=[pl.BlockSpec((tm, tk1), lambda p, j, i: (i, p)), b_spec]
                 + [pl.BlockSpec(memory_space=pl.ANY)] * len(extra),
        out_specs=pl.BlockSpec((tk1, tn), lambda p, j, i: (p, j)),
        out_shape=jax.ShapeDtypeStruct((k1, n), out_dtype),
        scratch_shapes=[pltpu.VMEM((tk1, tn), F32)],
        compiler_params=_params(("parallel", "parallel", "arbitrary")))(a, b, *extra)


def pool_mm_fwd(pooled, wp, scale, *, name):
    m = pooled.shape[0]
    tm = _row_tile(m)

    def body(a_ref, w_ref, s_ref, y0_ref, y1_ref):
        acc = _nn(a_ref[...], w_ref[...])
        y0_ref[...] = acc.astype(BF16)
        y1_ref[...] = (acc * s_ref[...]).astype(BF16)

    blk = pl.BlockSpec((tm, GDIM), lambda g, i: (i, g))
    return pl.pallas_call(
        body, name=name, grid=(GROUPS, m // tm),
        in_specs=[blk, pl.BlockSpec((None, GDIM, GDIM), lambda g, i: (g, 0, 0)),
                  pl.BlockSpec((1, GDIM), lambda g, i: (0, g))],
        out_specs=[blk, blk],
        out_shape=[jax.ShapeDtypeStruct((m, D), BF16)] * 2,
        compiler_params=_params(("parallel", "parallel")))(pooled, wp, scale)


def pool_mm_bwd_x(dy0, wp, *, name):
    m = dy0.shape[0]
    tm = _row_tile(m)

    def body(a_ref, w_ref, o_ref):
        o_ref[...] = _nt(a_ref[...], w_ref[...]).astype(BF16)

    blk = pl.BlockSpec((tm, GDIM), lambda g, i: (i, g))
    return pl.pallas_call(
        body, name=name, grid=(GROUPS, m // tm),
        in_specs=[blk, pl.BlockSpec((None, GDIM, GDIM), lambda g, i: (g, 0, 0))],
        out_specs=blk, out_shape=jax.ShapeDtypeStruct((m, D), BF16),
        compiler_params=_params(("parallel", "parallel")))(dy0, wp)


def pool_mm_bwd_w(pooled, dy0, *, name):
    m = pooled.shape[0]
    tm = _row_tile(m)

    def body(a_ref, b_ref, o_ref):
        part = _tn(a_ref[...], b_ref[...])

        @pl.when(pl.program_id(1) == 0)
        def _():
            o_ref[...] = part

        @pl.when(pl.program_id(1) > 0)
        def _():
            o_ref[...] += part

    blk = pl.BlockSpec((tm, GDIM), lambda g, i: (i, g))
    return pl.pallas_call(
        body, name=name, grid=(GROUPS, m // tm), in_specs=[blk, blk],
        out_specs=pl.BlockSpec((None, GDIM, GDIM), lambda g, i: (g, 0, 0)),
        out_shape=jax.ShapeDtypeStruct((GROUPS, GDIM, GDIM), F32),
        compiler_params=_params(("parallel", "arbitrary")))(pooled, dy0)


def rmsnorm_fwd(x, g, *, name):
    m = x.shape[0]
    tm = _ew_tile(m)

    def body(x_ref, g_ref, o_ref):
        xv = x_ref[...]
        r = lax.rsqrt(jnp.mean(xv * xv, axis=-1, keepdims=True) + EPS)
        o_ref[...] = (xv * r * g_ref[...]).astype(BF16)

    return pl.pallas_call(
        body, name=name, grid=(m // tm,),
        in_specs=[pl.BlockSpec((tm, D), lambda i: (i, 0)), pl.BlockSpec((1, D), lambda i: (0, 0))],
        out_specs=pl.BlockSpec((tm, D), lambda i: (i, 0)),
        out_shape=jax.ShapeDtypeStruct((m, D), BF16),
        compiler_params=_params(("parallel",)))(x, g)


def rmsnorm_bwd(dy, x, g, dres, *, name):
    m = x.shape[0]
    tm = _ew_tile(m)

    def body(dy_ref, x_ref, g_ref, r_ref, dx_ref, dxb_ref, dg_ref):
        i = pl.program_id(0)
        xv = x_ref[...]
        dyv = dy_ref[...].astype(F32)
        r = lax.rsqrt(jnp.mean(xv * xv, axis=-1, keepdims=True) + EPS)
        xh = xv * r
        dxh = dyv * g_ref[...]
        dx = r * (dxh - xh * jnp.mean(dxh * xh, axis=-1, keepdims=True))
        row = i * tm + lax.broadcasted_iota(jnp.int32, (tm, 1), 0)
        dx = jnp.where(row >= PAD, dx + r_ref[...], 0.0)
        dx_ref[...] = dx
        dxb_ref[...] = dx.astype(BF16)

        @pl.when(i == 0)
        def _():
            dg_ref[...] = jnp.zeros_like(dg_ref)

        dg_ref[...] += jnp.sum(dyv * xh, axis=0, keepdims=True)

    blk = pl.BlockSpec((tm, D), lambda i: (i, 0))
    vec = pl.BlockSpec((1, D), lambda i: (0, 0))
    return pl.pallas_call(
        body, name=name, grid=(m // tm,), in_specs=[blk, blk, vec, blk],
        out_specs=[blk, blk, vec],
        out_shape=[jax.ShapeDtypeStruct((m, D), F32), jax.ShapeDtypeStruct((m, D), BF16),
                   jax.ShapeDtypeStruct((1, D), F32)],
        compiler_params=_params(("arbitrary",)))(dy, x, g, dres)


def loss_head(h, gf, target, *, name):
    m = h.shape[0]
    t = X0
    inv_d = 1.0 / D

    def body(h_ref, g_ref, t_ref, dh_ref, dhb_ref, dg_ref, ls_ref):
        i = pl.program_id(0)

        @pl.when(i == 0)
        def _():
            dg_ref[...] = jnp.zeros_like(dg_ref)
            ls_ref[...] = jnp.zeros_like(ls_ref)
            dh_ref[...] = jnp.zeros_like(dh_ref)
            dhb_ref[...] = jnp.zeros_like(dhb_ref)

        @pl.when(i > 0)
        def _():
            xv = h_ref[...]
            r = lax.rsqrt(jnp.mean(xv * xv, axis=-1, keepdims=True) + EPS)
            xh = xv * r
            err = xh * g_ref[...] - t_ref[...]
            ls_ref[...] += jnp.sum(err * err, axis=0, keepdims=True)
            dy = err * inv_d
            dg_ref[...] += jnp.sum(dy * xh, axis=0, keepdims=True)
            dxh = dy * g_ref[...]
            dh = r * (dxh - xh * jnp.mean(dxh * xh, axis=-1, keepdims=True))
            dh_ref[...] = dh
            dhb_ref[...] = dh.astype(BF16)

    blk = pl.BlockSpec((t, D), lambda i: (i, 0))
    vec = pl.BlockSpec((1, D), lambda i: (0, 0))
    return pl.pallas_call(
        body, name=name, grid=(m // t,),
        in_specs=[blk, vec, pl.BlockSpec((t, D), lambda i: (jnp.maximum(i - 1, 0), 0))],
        out_specs=[blk, blk, vec, vec],
        out_shape=[jax.ShapeDtypeStruct((m, D), F32), jax.ShapeDtypeStruct((m, D), BF16),
                   jax.ShapeDtypeStruct((1, D), F32), jax.ShapeDtypeStruct((1, D), F32)],
        compiler_params=_params(("arbitrary",)))(h, gf, target)


def _split3(x):
    x1 = x.astype(BF16)
    r1 = x - x1.astype(F32)
    x2 = r1.astype(BF16)
    x3 = (r1 - x2.astype(F32)).astype(BF16)
    return x1, x2, x3


def _tri_mm(tri, x):
    x1, x2, x3 = _split3(x)
    return _nn(tri, x1) + _nn(tri, x2) + _nn(tri, x3)


def _log_decay(glr, wgk, bgk, row0, rows):
    z = _nn(glr, wgk) + bgk
    la = (jnp.minimum(z, 0.0) - jnp.log(1.0 + jnp.exp(-jnp.abs(z)))) * (1.0 / TAU)
    row = row0 + lax.broadcasted_iota(jnp.int32, (rows, 1), 0)
    return z, jnp.where(row >= PAD, la, 0.0)


def _chunk_group(n_chunks):
    return _pick(n_chunks, (3, 2, 1))


def gla_fwd(p, wgk, bgk, *, name):
    m = p.shape[0]
    n_chunks = m // CHUNK
    cg = _chunk_group(n_chunks)
    t = cg * CHUNK
    scale = HK ** -0.5

    def body(q_ref, k_ref, v_ref, glr_ref, wgk_ref, bgk_ref, o_ref, st_ref, state):
        i = pl.program_id(0)

        @pl.when(i == 0)
        def _():
            state[...] = jnp.zeros_like(state)

        _, la = _log_decay(glr_ref[...], wgk_ref[...], bgk_ref[...], i * t, t)
        ri = lax.broadcasted_iota(jnp.int32, (CHUNK, CHUNK), 0)
        ci = lax.broadcasted_iota(jnp.int32, (CHUNK, CHUNK), 1)
        causal = ri >= ci
        tri = causal.astype(BF16)
        for c in range(cg):
            rows = pl.ds(c * CHUNK, CHUNK)
            b = _tri_mm(tri, la[c * CHUNK:(c + 1) * CHUNK])
            bl = b[CHUNK - 1:CHUNK, :]
            q = q_ref[rows, :].astype(F32) * scale
            k = k_ref[rows, :].astype(F32)
            qd = (q * jnp.exp(b)).astype(BF16)
            ki = (k * jnp.exp(-b)).astype(BF16)
            ke = (k * jnp.exp(bl - b)).astype(BF16)
            dec = jnp.exp(bl)
            for h in range(HEADS):
                ks = slice(h * HK, (h + 1) * HK)
                vs = pl.ds(h * HV, HV)
                vh = v_ref[rows, vs]
                s_t = state[h]
                st_ref[c, h] = s_t
                att = jnp.where(causal, _nt(qd[:, ks], ki[:, ks]), 0.0).astype(BF16)
                o_ref[rows, vs] = _nn(att, vh) + _nt(qd[:, ks], s_t.astype(BF16))
                state[h] = s_t * dec[:, ks] + _tn(vh, ke[:, ks])

    return pl.pallas_call(
        body, name=name, grid=(n_chunks // cg,),
        in_specs=[pl.BlockSpec((t, DK), lambda i: (i, C_Q // DK)),
                  pl.BlockSpec((t, DK), lambda i: (i, C_K // DK)),
                  pl.BlockSpec((t, DV), lambda i: (i, C_V // DV)),
                  pl.BlockSpec((t, LANES), lambda i: (i, C_GLR // LANES)),
                  pl.BlockSpec((LANES, DK), lambda i: (0, 0)),
                  pl.BlockSpec((1, DK), lambda i: (0, 0))],
        out_specs=[pl.BlockSpec((t, DV), lambda i: (i, 0)),
                   pl.BlockSpec((cg, HEADS, HV, HK), lambda i: (i, 0, 0, 0))],
        out_shape=[jax.ShapeDtypeStruct((m, DV), F32),
                   jax.ShapeDtypeStruct((n_chunks, HEADS, HV, HK), F32)],
        scratch_shapes=[pltpu.VMEM((HEADS, HV, HK), F32)],
        compiler_params=_params(("arbitrary",)))(p, p, p, p, wgk, bgk)


def gla_bwd(p, wgk, bgk, st, do, *, name):
    m = p.shape[0]
    n_chunks = m // CHUNK
    cg = _chunk_group(n_chunks)
    t = cg * CHUNK
    ns = n_chunks // cg
    scale = HK ** -0.5

    def body(q_ref, k_ref, v_ref, glr_ref, wgk_ref, bgk_ref, st_ref, do_ref,
             dqkv_ref, dglr_ref, dwgk_ref, dbgk_ref, dstate, dz_buf):
        i = pl.program_id(0)
        blk = ns - 1 - i

        @pl.when(i == 0)
        def _():
            dstate[...] = jnp.zeros_like(dstate)
            dwgk_ref[...] = jnp.zeros_like(dwgk_ref)
            dbgk_ref[...] = jnp.zeros_like(dbgk_ref)

        z, la = _log_decay(glr_ref[...], wgk_ref[...], bgk_ref[...], blk * t, t)
        ri = lax.broadcasted_iota(jnp.int32, (CHUNK, CHUNK), 0)
        ci = lax.broadcasted_iota(jnp.int32, (CHUNK, CHUNK), 1)
        causal = ri >= ci
        tri = causal.astype(BF16)
        tri_u = (ri <= ci).astype(BF16)
        for c in reversed(range(cg)):
            rows = pl.ds(c * CHUNK, CHUNK)
            b = _tri_mm(tri, la[c * CHUNK:(c + 1) * CHUNK])
            bl = b[CHUNK - 1:CHUNK, :]
            eb = jnp.exp(b)
            enb = jnp.exp(-b)
            ebl = jnp.exp(bl - b)
            dec = jnp.exp(bl)
            q = q_ref[rows, :].astype(F32) * scale
            k = k_ref[rows, :].astype(F32)
            qd32 = q * eb
            ki32 = k * enb
            ke32 = k * ebl
            qd = qd32.astype(BF16)
            ki = ki32.astype(BF16)
            ke = ke32.astype(BF16)
            dqd_parts, dki_parts, dke_parts, ddec_parts = [], [], [], []
            for h in range(HEADS):
                ks = slice(h * HK, (h + 1) * HK)
                vs = pl.ds(h * HV, HV)
                vh = v_ref[rows, vs]
                doh = do_ref[rows, vs].astype(BF16)
                s_t = st_ref[c, h]
                ds_t = dstate[h]
                ds_b = ds_t.astype(BF16)
                att = jnp.where(causal, _nt(qd[:, ks], ki[:, ks]), 0.0).astype(BF16)
                datt = jnp.where(causal, _nt(doh, vh), 0.0).astype(BF16)
                dvh = _tn(att, doh) + _nt(ke[:, ks], ds_b)
                dqkv_ref[rows, pl.ds(2 * DK + h * HV, HV)] = dvh.astype(BF16)
                dqd_parts.append(_nn(datt, ki[:, ks]) + _nn(doh, s_t.astype(BF16)))
                dki_parts.append(_tn(datt, qd[:, ks]))
                dke_parts.append(_nn(vh, ds_b))
                ddec_parts.append(jnp.sum(s_t * ds_t, axis=0, keepdims=True))
                dstate[h] = _tn(doh, qd[:, ks]) + ds_t * dec[:, ks]
            dqd = jnp.concatenate(dqd_parts, axis=1)
            dki = jnp.concatenate(dki_parts, axis=1)
            dke = jnp.concatenate(dke_parts, axis=1)
            ddec = jnp.concatenate(ddec_parts, axis=1)
            dqkv_ref[rows, pl.ds(0, DK)] = (dqd * eb * scale).astype(BF16)
            dqkv_ref[rows, pl.ds(DK, DK)] = (dki * enb + dke * ebl).astype(BF16)
            dke_ke = dke * ke32
            db = dqd * qd32 - dki * ki32 - dke_ke
            dbl = jnp.sum(dke_ke, axis=0, keepdims=True) + ddec * dec
            dg = _tri_mm(tri_u, db) + dbl
            row = blk * t + c * CHUNK + lax.broadcasted_iota(jnp.int32, (CHUNK, 1), 0)
            zc = z[c * CHUNK:(c + 1) * CHUNK]
            dz = jnp.where(row >= PAD, dg * (1.0 / TAU) * _sigmoid(-zc), 0.0)
            dz_buf[rows, :] = dz
        dz_all = dz_buf[...]
        dz_b = dz_all.astype(BF16)
        dbgk_ref[...] += jnp.sum(dz_all, axis=0, keepdims=True)
        dglr_ref[...] = _nt(dz_b, wgk_ref[...]).astype(BF16)
        dwgk_ref[...] += _tn(glr_ref[...], dz_b)

    rev = lambda i: ns - 1 - i
    return pl.pallas_call(
        body, name=name, grid=(ns,),
        in_specs=[pl.BlockSpec((t, DK), lambda i: (rev(i), C_Q // DK)),
                  pl.BlockSpec((t, DK), lambda i: (rev(i), C_K // DK)),
                  pl.BlockSpec((t, DV), lambda i: (rev(i), C_V // DV)),
                  pl.BlockSpec((t, LANES), lambda i: (rev(i), C_GLR // LANES)),
                  pl.BlockSpec((LANES, DK), lambda i: (0, 0)),
                  pl.BlockSpec((1, DK), lambda i: (0, 0)),
                  pl.BlockSpec((cg, HEADS, HV, HK), lambda i: (rev(i), 0, 0, 0)),
                  pl.BlockSpec((t, DV), lambda i: (rev(i), 0))],
        out_specs=[pl.BlockSpec((t, 2 * DK + DV), lambda i: (rev(i), 0)),
                   pl.BlockSpec((t, LANES), lambda i: (rev(i), 0)),
                   pl.BlockSpec((LANES, DK), lambda i: (0, 0)),
                   pl.BlockSpec((1, DK), lambda i: (0, 0))],
        out_shape=[jax.ShapeDtypeStruct((m, 2 * DK + DV), BF16),
                   jax.ShapeDtypeStruct((m, LANES), BF16),
                   jax.ShapeDtypeStruct((LANES, DK), F32),
                   jax.ShapeDtypeStruct((1, DK), F32)],
        scratch_shapes=[pltpu.VMEM((HEADS, HV, HK), F32), pltpu.VMEM((t, DK), F32)],
        compiler_params=_params(("arbitrary",)))(p, p, p, p, wgk, bgk, st, do)


HALO = 16


def _shift_down(xx, s):
    return pltpu.roll(xx, s, 0)


def _shift_up(xx, s):
    return pltpu.roll(xx, xx.shape[0] - s, 0)


def mix_pre(o, p, gn, *, name):
    m = o.shape[0]
    tm = _ew_tile(m)

    def body(o_ref, r_ref, u_ref, gn_ref, ya_ref, pooled_ref, halo):
        i = pl.program_id(0)

        @pl.when(i == 0)
        def _():
            halo[...] = jnp.zeros_like(halo)

        rv = r_ref[...].astype(F32)
        silu_r = rv * _sigmoid(rv)
        for h in range(HEADS):
            cs = pl.ds(h * HV, HV)
            ov = o_ref[:, cs]
            rs = lax.rsqrt(jnp.mean(ov * ov, axis=-1, keepdims=True) + EPS)
            ya_ref[:, cs] = (ov * rs * gn_ref[...] * silu_r[:, h * HV:(h + 1) * HV]).astype(BF16)

        row = i * tm + lax.broadcasted_iota(jnp.int32, (tm, 1), 0)
        pos1 = jnp.maximum(row - PAD + 1, 1).astype(F32)
        for g, w in enumerate(POOL_WINDOWS):
            cs = pl.ds(g * GDIM, GDIM)
            uv = u_ref[:, cs].astype(F32)
            xx = jnp.concatenate([halo[:, cs], uv], axis=0)
            s = xx
            span = 1
            while span < w:
                s = s + _shift_down(s, span)
                span *= 2
            inv = 1.0 / jnp.minimum(pos1, float(w))
            pooled_ref[:, cs] = (s[HALO:] * inv - uv).astype(BF16)
            halo[:, cs] = uv[tm - HALO:]

    blk = pl.BlockSpec((tm, D), lambda i: (i, 0))
    return pl.pallas_call(
        body, name=name, grid=(m // tm,),
        in_specs=[blk, pl.BlockSpec((tm, D), lambda i: (i, C_R // D)),
                  pl.BlockSpec((tm, D), lambda i: (i, C_U // D)),
                  pl.BlockSpec((1, HV), lambda i: (0, 0))],
        out_specs=[blk, blk],
        out_shape=[jax.ShapeDtypeStruct((m, D), BF16)] * 2,
        scratch_shapes=[pltpu.VMEM((HALO, D), F32)],
        compiler_params=_params(("arbitrary",)))(o, p, p, gn)


def mix_pre_bwd(dya, dpooled, o, p, gn, *, name):
    m = o.shape[0]
    tm = _ew_tile(m)
    nt = m // tm

    def body(dya_ref, dpl_ref, o_ref, r_ref, gn_ref, do_ref, dr_ref, du_ref, dgn_ref, halo):
        i = pl.program_id(0)
        blk_i = nt - 1 - i

        @pl.when(i == 0)
        def _():
            halo[...] = jnp.zeros_like(halo)
            dgn_ref[...] = jnp.zeros_like(dgn_ref)

        rv = r_ref[...].astype(F32)
        sg = _sigmoid(rv)
        silu_r = rv * sg
        dsilu = sg * (1.0 + rv * (1.0 - sg))
        dgn = jnp.zeros((1, HV), F32)
        for h in range(HEADS):
            cs = pl.ds(h * HV, HV)
            hs = slice(h * HV, (h + 1) * HV)
            ov = o_ref[:, cs]
            dy = dya_ref[:, cs].astype(F32)
            rs = lax.rsqrt(jnp.mean(ov * ov, axis=-1, keepdims=True) + EPS)
            xh = ov * rs
            on = xh * gn_ref[...]
            don = dy * silu_r[:, hs]
            dr_ref[:, cs] = (dy * on * dsilu[:, hs]).astype(BF16)
            dxh = don * gn_ref[...]
            do_ref[:, cs] = rs * (dxh - xh * jnp.mean(dxh * xh, axis=-1, keepdims=True))
            dgn = dgn + jnp.sum(don * xh, axis=0, keepdims=True)
        dgn_ref[...] += dgn

        row = blk_i * tm + lax.broadcasted_iota(jnp.int32, (tm, 1), 0)
        pos1 = jnp.maximum(row - PAD + 1, 1).astype(F32)
        for g, w in enumerate(POOL_WINDOWS):
            cs = pl.ds(g * GDIM, GDIM)
            dpv = dpl_ref[:, cs].astype(F32)
            e = dpv * (1.0 / jnp.minimum(pos1, float(w)))
            xx = jnp.concatenate([e, halo[:, cs]], axis=0)
            s = xx
            span = 1
            while span < w:
                s = s + _shift_up(s, span)
                span *= 2
            du_ref[:, cs] = (s[:tm] - dpv).astype(BF16)
            halo[:, cs] = e[:HALO]

    rev = lambda i: nt - 1 - i
    blk = pl.BlockSpec((tm, D), lambda i: (rev(i), 0))
    return pl.pallas_call(
        body, name=name, grid=(nt,),
        in_specs=[blk, blk, blk, pl.BlockSpec((tm, D), lambda i: (rev(i), C_R // D)),
                  pl.BlockSpec((1, HV), lambda i: (0, 0))],
        out_specs=[blk, blk, blk, pl.BlockSpec((1, HV), lambda i: (0, 0))],
        out_shape=[jax.ShapeDtypeStruct((m, D), F32), jax.ShapeDtypeStruct((m, D), BF16),
                   jax.ShapeDtypeStruct((m, D), BF16), jax.ShapeDtypeStruct((1, HV), F32)],
        scratch_shapes=[pltpu.VMEM((HALO, D), F32)],
        compiler_params=_params(("arbitrary",)))(dya, dpooled, o, p, gn)


def merge_fwd(p, ya, yb, bg, *, name):
    m = ya.shape[0]
    tm = _ew_tile(m)

    def body(ga_ref, gb_ref, ya_ref, yb_ref, ba_ref, bb_ref, o_ref):
        gate_a = _sigmoid(ga_ref[...].astype(F32) + ba_ref[...])
        gate_b = _sigmoid(gb_ref[...].astype(F32) + bb_ref[...])
        o_ref[...] = (gate_a * ya_ref[...].astype(F32) + gate_b * yb_ref[...].astype(F32)).astype(BF16)

    blk = pl.BlockSpec((tm, D), lambda i: (i, 0))
    return pl.pallas_call(
        body, name=name, grid=(m // tm,),
        in_specs=[pl.BlockSpec((tm, D), lambda i: (i, C_GA // D)),
                  pl.BlockSpec((tm, D), lambda i: (i, C_GB // D)), blk, blk,
                  pl.BlockSpec((1, D), lambda i: (0, 0)), pl.BlockSpec((1, D), lambda i: (0, 1))],
        out_specs=blk, out_shape=jax.ShapeDtypeStruct((m, D), BF16),
        compiler_params=_params(("parallel",)))(p, p, ya, yb, bg, bg)


def merge_bwd(dmrg, p, ya, yb, bg, *, name):
    m = ya.shape[0]
    tm = _ew_tile(m)

    def body(dm_ref, ga_ref, gb_ref, ya_ref, yb_ref, ba_ref, bb_ref,
             dya_ref, dyb_ref, dga_ref, dgb_ref, dbg_ref):
        @pl.when(pl.program_id(0) == 0)
        def _():
            dbg_ref[...] = jnp.zeros_like(dbg_ref)

        dm = dm_ref[...].astype(F32)
        gate_a = _sigmoid(ga_ref[...].astype(F32) + ba_ref[...])
        gate_b = _sigmoid(gb_ref[...].astype(F32) + bb_ref[...])
        dya_ref[...] = (dm * gate_a).astype(BF16)
        dyb_ref[...] = (dm * gate_b).astype(BF16)
        dga = dm * ya_ref[...].astype(F32) * gate_a * (1.0 - gate_a)
        dgb = dm * yb_ref[...].astype(F32) * gate_b * (1.0 - gate_b)
        dga_ref[...] = dga.astype(BF16)
        dgb_ref[...] = dgb.astype(BF16)
        dbg_ref[:, pl.ds(0, D)] += jnp.sum(dga, axis=0, keepdims=True)
        dbg_ref[:, pl.ds(D, D)] += jnp.sum(dgb, axis=0, keepdims=True)

    blk = pl.BlockSpec((tm, D), lambda i: (i, 0))
    return pl.pallas_call(
        body, name=name, grid=(m // tm,),
        in_specs=[blk, pl.BlockSpec((tm, D), lambda i: (i, C_GA // D)),
                  pl.BlockSpec((tm, D), lambda i: (i, C_GB // D)), blk, blk,
                  pl.BlockSpec((1, D), lambda i: (0, 0)), pl.BlockSpec((1, D), lambda i: (0, 1))],
        out_specs=[blk, blk, blk, blk, pl.BlockSpec((1, 2 * D), lambda i: (0, 0))],
        out_shape=[jax.ShapeDtypeStruct((m, D), BF16)] * 4 + [jax.ShapeDtypeStruct((1, 2 * D), F32)],
        compiler_params=_params(("arbitrary",)))(dmrg, p, p, ya, yb, bg, bg)


def scale_bwd(dy1, y0, scale, *, name):
    m = y0.shape[0]
    tm = _ew_tile(m)

    def body(dy_ref, y0_ref, s_ref, o_ref, ds_ref):
        @pl.when(pl.program_id(0) == 0)
        def _():
            ds_ref[...] = jnp.zeros_like(ds_ref)

        dy = dy_ref[...].astype(F32)
        o_ref[...] = (dy * s_ref[...]).astype(BF16)
        ds_ref[...] += jnp.sum(dy * y0_ref[...].astype(F32), axis=0, keepdims=True)

    blk = pl.BlockSpec((tm, D), lambda i: (i, 0))
    vec = pl.BlockSpec((1, D), lambda i: (0, 0))
    return pl.pallas_call(
        body, name=name, grid=(m // tm,), in_specs=[blk, blk, vec], out_specs=[blk, vec],
        out_shape=[jax.ShapeDtypeStruct((m, D), BF16), jax.ShapeDtypeStruct((1, D), F32)],
        compiler_params=_params(("arbitrary",)))(dy1, y0, scale)


CONV_BLK = 1408
N_CONV_BLK = D_FF // CONV_BLK


def conv_act_fwd(up, cw, cb, *, name):
    m = up.shape[0]
    tm = _ew_tile(m)

    def conv(x_ref, halo, w_ref, b_ref):
        xv = x_ref[...].astype(F32)
        xx = jnp.concatenate([halo[...], xv], axis=0)
        y = (w_ref[2:3, :] * xx + w_ref[1:2, :] * _shift_down(xx, 1)
             + w_ref[0:1, :] * _shift_down(xx, 2))[HALO:] + b_ref[...]
        halo[...] = xv[tm - HALO:]
        return y

    def body(xa_ref, xb_ref, wa_ref, wb_ref, ba_ref, bb_ref, upc_a_ref, upc_b_ref, act_ref, halo_a, halo_b):
        @pl.when(pl.program_id(1) == 0)
        def _():
            halo_a[...] = jnp.zeros_like(halo_a)
            halo_b[...] = jnp.zeros_like(halo_b)

        a = conv(xa_ref, halo_a, wa_ref, ba_ref)
        bv = conv(xb_ref, halo_b, wb_ref, bb_ref)
        upc_a_ref[...] = a.astype(BF16)
        upc_b_ref[...] = bv.astype(BF16)
        act_ref[...] = (a * _sigmoid(a) * bv).astype(BF16)

    nb = N_CONV_BLK
    xa = pl.BlockSpec((tm, CONV_BLK), lambda j, i: (i, j))
    xb = pl.BlockSpec((tm, CONV_BLK), lambda j, i: (i, j + nb))
    return pl.pallas_call(
        body, name=name, grid=(nb, m // tm),
        in_specs=[xa, xb,
                  pl.BlockSpec((3, CONV_BLK), lambda j, i: (0, j)),
                  pl.BlockSpec((3, CONV_BLK), lambda j, i: (0, j + nb)),
                  pl.BlockSpec((1, CONV_BLK), lambda j, i: (0, j)),
                  pl.BlockSpec((1, CONV_BLK), lambda j, i: (0, j + nb))],
        out_specs=[xa, xa, xa],
        out_shape=[jax.ShapeDtypeStruct((m, D_FF), BF16)] * 3,
        scratch_shapes=[pltpu.VMEM((HALO, CONV_BLK), F32)] * 2,
        compiler_params=_params(("parallel", "arbitrary")))(up, up, cw, cw, cb, cb)


def conv_act_bwd(dact, upc_a, upc_b, up, cw, *, name):
    m = up.shape[0]
    tm = _ew_tile(m)
    nt = m // tm

    def conv_t(d, halo, x_ref, w_ref, dup_ref, half, dw_ref, db_ref):
        xx = jnp.concatenate([d, halo[...]], axis=0)
        d1 = _shift_up(xx, 1)[:tm]
        d2 = _shift_up(xx, 2)[:tm]
        dup_ref[half] = (w_ref[2:3, :] * d + w_ref[1:2, :] * d1 + w_ref[0:1, :] * d2).astype(BF16)
        xv = x_ref[...].astype(F32)
        dw_ref[2:3, :] += jnp.sum(xv * d, axis=0, keepdims=True)
        dw_ref[1:2, :] += jnp.sum(xv * d1, axis=0, keepdims=True)
        dw_ref[0:1, :] += jnp.sum(xv * d2, axis=0, keepdims=True)
        db_ref[...] += jnp.sum(d, axis=0, keepdims=True)
        halo[...] = d[:HALO]

    def body(da_ref, a_ref, b_ref, xa_ref, xb_ref, wa_ref, wb_ref,
             dup_ref, dwa_ref, dwb_ref, dba_ref, dbb_ref, halo_a, halo_b):
        @pl.when(pl.program_id(1) == 0)
        def _():
            for r in (halo_a, halo_b, dwa_ref, dwb_ref, dba_ref, dbb_ref):
                r[...] = jnp.zeros_like(r)

        dact_v = da_ref[...].astype(F32)
        a = a_ref[...].astype(F32)
        bv = b_ref[...].astype(F32)
        sg = _sigmoid(a)
        d_a = dact_v * bv * sg * (1.0 + a * (1.0 - sg))
        d_b = dact_v * a * sg
        conv_t(d_a, halo_a, xa_ref, wa_ref, dup_ref, 0, dwa_ref, dba_ref)
        conv_t(d_b, halo_b, xb_ref, wb_ref, dup_ref, 1, dwb_ref, dbb_ref)

    nb = N_CONV_BLK
    rev = lambda i: nt - 1 - i
    half = pl.BlockSpec((tm, CONV_BLK), lambda j, i: (rev(i), j))
    xa = half
    xb = pl.BlockSpec((tm, CONV_BLK), lambda j, i: (rev(i), j + nb))
    wa = pl.BlockSpec((3, CONV_BLK), lambda j, i: (0, j))
    wb = pl.BlockSpec((3, CONV_BLK), lambda j, i: (0, j + nb))
    va = pl.BlockSpec((1, CONV_BLK), lambda j, i: (0, j))
    outs = pl.pallas_call(
        body, name=name, grid=(nb, nt),
        in_specs=[half, half, half, xa, xb, wa, wb],
        out_specs=[pl.BlockSpec((2, tm, CONV_BLK), lambda j, i: (0, rev(i), j)), wa, wa, va, va],
        out_shape=[jax.ShapeDtypeStruct((2, m, D_FF), BF16)]
                  + [jax.ShapeDtypeStruct((3, D_FF), F32)] * 2
                  + [jax.ShapeDtypeStruct((1, D_FF), F32)] * 2,
        scratch_shapes=[pltpu.VMEM((HALO, CONV_BLK), F32)] * 2,
        compiler_params=_params(("parallel", "arbitrary")))(dact, upc_a, upc_b, up, up, cw, cw)
    return outs


def local_step(x, target, w):
    seq = x.shape[0]
    h = jnp.concatenate([jnp.zeros((PAD, D), F32), w["meta"], x], axis=0)
    saved = []
    for l in range(DEPTH):
        wl = {k: (v[l:l + 1] if k in ROW_PARAMS else v[l]) for k, v in w.items() if k not in ("meta", "final_norm_g")}
        s = {"h": h}
        fwd_in(s, wl, f"l{l}_")
        fwd_mixer(s, wl, f"l{l}_")
        fwd_ffn(s, wl, f"l{l}_")
        saved.append(s)
        h = s["h3"]

    dh, dh_b, dgf, loss_rows = loss_head(h, w["final_norm_g"], target, name="loss_head")
    g = {"final_norm_g": dgf}
    per_layer = []
    for l in reversed(range(DEPTH)):
        wl = {k: (v[l:l + 1] if k in ROW_PARAMS else v[l]) for k, v in w.items() if k not in ("meta", "final_norm_g")}
        s = saved[l]
        gl = {}
        dh2, dh2_b = bwd_ffn(dh, dh_b, s, wl, gl, f"l{l}_")
        dp = bwd_mixer(dh2_b, s, wl, gl, f"l{l}_")
        gl["w_in"] = bwd_in_w(dp, s, f"l{l}_")
        dh, dh_b = bwd_in_x(dp, dh2, s, wl, gl, f"l{l}_")
        per_layer.append(gl)
    per_layer.reverse()
    for k in per_layer[0]:
        g[k] = jnp.stack([per_layer[l][k].astype(F32) for l in range(DEPTH)])
    g["meta"] = dh[PAD:X0]
    return loss_rows, dh[X0:X0 + seq], g


ROW_PARAMS = ("norm1_g", "b_gk", "gla_norm_g", "pool_scale", "b_gates", "norm2_g", "conv_b")


def fwd_in(s, w, ln):
    s["hn1"] = rmsnorm_fwd(s["h"], w["norm1_g"], name=ln + "norm1")
    s["p"] = mm_nn(s["hn1"], w["w_in"], tn=896, name=ln + "in_proj")


def fwd_mixer(s, w, ln):
    p = s["p"]
    s["o"], s["st"] = gla_fwd(p, w["w_gk"], w["b_gk"], name=ln + "gla_fwd")
    s["ya_in"], s["pooled"] = mix_pre(s["o"], p, w["gla_norm_g"], name=ln + "mix_pre")
    s["ya"] = mm_nn(s["ya_in"], w["w_a"], name=ln + "proj_a")
    s["yb0"], s["yb1"] = pool_mm_fwd(s["pooled"], w["w_pool"], w["pool_scale"], name=ln + "pool_mm")
    s["yb"] = mm_nn(s["yb1"], w["w_b"], name=ln + "proj_b")
    s["mrg"] = merge_fwd(p, s["ya"], s["yb"], w["b_gates"], name=ln + "merge")
    s["h2"] = mm_nn(s["mrg"], w["w_o"], out_dtype=F32, res=s["h"], name=ln + "proj_o")


def fwd_ffn(s, w, ln):
    s["hn2"] = rmsnorm_fwd(s["h2"], w["norm2_g"], name=ln + "norm2")
    s["up"] = mm_nn(s["hn2"], w["w_up"], tn=1408, name=ln + "up_proj")
    s["upc_a"], s["upc_b"], s["act"] = conv_act_fwd(s["up"], w["conv_w"], w["conv_b"], name=ln + "conv_act")
    s["h3"] = mm_nn(s["act"], w["w_down"], out_dtype=F32, res=s["h2"], name=ln + "down_proj")


def bwd_ffn(dh, dh_b, s, w, g, ln, after=None):
    dact = mm_nt(dh_b, w["w_down"], tn=1408, after=after, name=ln + "d_act")
    g["w_down"] = mm_tn(s["act"], dh_b, tk1=1408, out_dtype=BF16, after=after, name=ln + "dw_down")
    dup, dcw_a, dcw_b, dcb_a, dcb_b = conv_act_bwd(
        dact, s["upc_a"], s["upc_b"], s["up"], w["conv_w"], name=ln + "conv_act_bwd")
    dhn2 = mm_nt(dup, w["w_up"], out_dtype=F32, tk=1408, halves=True, name=ln + "d_hn2")
    g["w_up"] = mm_tn(s["hn2"], dup, tn=1408, out_dtype=BF16, halves=True, name=ln + "dw_up")
    dh2, dh2_b, g["norm2_g"] = rmsnorm_bwd(dhn2, s["h2"], w["norm2_g"], dh, name=ln + "norm2_bwd")
    g["conv_w"] = jnp.concatenate([dcw_a, dcw_b], axis=1)
    g["conv_b"] = jnp.concatenate([dcb_a, dcb_b], axis=1)
    return dh2, dh2_b


def bwd_mixer(dh2_b, s, w, g, ln, after=None):
    dmrg = mm_nt(dh2_b, w["w_o"], after=after, name=ln + "d_mrg")
    g["w_o"] = mm_tn(s["mrg"], dh2_b, out_dtype=BF16, after=after, name=ln + "dw_o")
    dya, dyb, dga, dgb, g["b_gates"] = merge_bwd(dmrg, s["p"], s["ya"], s["yb"], w["b_gates"], name=ln + "merge_bwd")
    dya_in = mm_nt(dya, w["w_a"], name=ln + "d_ya_in")
    g["w_a"] = mm_tn(s["ya_in"], dya, out_dtype=BF16, name=ln + "dw_a")
    dyb1 = mm_nt(dyb, w["w_b"], name=ln + "d_yb1")
    g["w_b"] = mm_tn(s["yb1"], dyb, out_dtype=BF16, name=ln + "dw_b")
    dyb0, g["pool_scale"] = scale_bwd(dyb1, s["yb0"], w["pool_scale"], name=ln + "scale_bwd")
    dpooled = pool_mm_bwd_x(dyb0, w["w_pool"], name=ln + "d_pooled")
    g["w_pool"] = pool_mm_bwd_w(s["pooled"], dyb0, name=ln + "dw_pool")
    do, dr, du, g["gla_norm_g"] = mix_pre_bwd(dya_in, dpooled, s["o"], s["p"], w["gla_norm_g"],
                                              name=ln + "mix_pre_bwd")
    dqkv, dglr, g["w_gk"], g["b_gk"] = gla_bwd(s["p"], w["w_gk"], w["b_gk"], s["st"], do, name=ln + "gla_bwd")
    return jnp.concatenate([dqkv, dr, du, dga, dgb, dglr], axis=1)


def bwd_in_w(dp, s, ln):
    return mm_tn(s["hn1"], dp, tn=896, out_dtype=BF16, name=ln + "dw_in")


def bwd_in_x(dp, dh2, s, w, g, ln, after=None):
    dhn1 = mm_nt(dp, w["w_in"], out_dtype=F32, tk=896, after=after, name=ln + "d_hn1")
    dh, dh_b, g["norm1_g"] = rmsnorm_bwd(dhn1, s["h"], w["norm1_g"], dh2, name=ln + "norm1_bwd")
    return dh, dh_b


def _my_place():
    return lax.axis_index("x"), lax.axis_index("y"), lax.axis_index("c")


def _peer(place, k):
    x, y, c = place
    return (1 - x if k & 4 else x, 1 - y if k & 2 else y, 1 - c if k & 1 else c)


def _index(place):
    x, y, c = place
    return 4 * x + 2 * y + c


def exchange(arrays, kinds, *, name):
    n = len(arrays)

    def body(*refs):
        ins, outs = refs[:n], refs[n:2 * n]
        send_sems, recv_sems, local_sems = refs[2 * n:]
        place = _my_place()
        me = _index(place)

        def src(a, dest):
            return ins[a] if kinds[a] == "gather" else ins[a].at[dest]

        def remote(a, k):
            peer = _peer(place, k)
            return pltpu.make_async_remote_copy(
                src_ref=src(a, _index(peer)), dst_ref=outs[a].at[me],
                send_sem=send_sems.at[a, k - 1], recv_sem=recv_sems.at[a, k - 1],
                device_id=peer, device_id_type=pl.DeviceIdType.MESH)

        def arrival(a, k):
            peer = _peer(place, k)
            return pltpu.make_async_remote_copy(
                src_ref=src(a, me), dst_ref=outs[a].at[_index(peer)],
                send_sem=send_sems.at[a, k - 1], recv_sem=recv_sems.at[a, k - 1],
                device_id=peer, device_id_type=pl.DeviceIdType.MESH)

        own = [pltpu.make_async_copy(src(a, me), outs[a].at[me], local_sems.at[a]) for a in range(n)]
        sends = [remote(a, k) for k in range(1, N_DEV) for a in range(n)]
        for cp in sends:
            cp.start()
        for cp in own:
            cp.start()
        for k in range(1, N_DEV):
            for a in range(n):
                arrival(a, k).wait_recv()
        for cp in sends:
            cp.wait_send()
        for cp in own:
            cp.wait()

    any_spec = pl.BlockSpec(memory_space=pl.ANY)
    out_shape = []
    for arr, kind in zip(arrays, kinds):
        shape = arr.shape if kind == "gather" else arr.shape[1:]
        out_shape.append(jax.ShapeDtypeStruct((N_DEV,) + tuple(shape), arr.dtype))
    return pl.pallas_call(
        body, name=name, in_specs=[any_spec] * n, out_specs=[any_spec] * n, out_shape=out_shape,
        scratch_shapes=[pltpu.SemaphoreType.DMA((n, N_DEV - 1)), pltpu.SemaphoreType.DMA((n, N_DEV - 1)),
                        pltpu.SemaphoreType.DMA((n,))],
    )(*arrays)


def _sem_slot(a, k):
    return a * (N_DEV - 1) + k - 1


_HBM = pl.BlockSpec(memory_space=pltpu.HBM)
_SEM = pl.BlockSpec(memory_space=pltpu.SEMAPHORE)
_DATAFLOW = pltpu.SideEffectType.DATAFLOW_SIDE_EFFECTING


def exchange_start(arrays, kinds, after, *, name):
    n = len(arrays)
    zones = []
    for arr, kind in zip(arrays, kinds):
        shape = arr.shape if kind == "gather" else arr.shape[1:]
        zones.append(jax.ShapeDtypeStruct((N_DEV,) + tuple(shape), arr.dtype))

    def place_body(*refs):
        ins, lands, local_sems = refs[:n], refs[n:2 * n], refs[2 * n]
        me = _index(_my_place())
        own = [pltpu.make_async_copy(ins[a] if kinds[a] == "gather" else ins[a].at[me], lands[a].at[me],
                                     local_sems.at[a]) for a in range(n)]
        for cp in own:
            cp.start()
        for cp in own:
            cp.wait()

    any_spec = pl.BlockSpec(memory_space=pl.ANY)
    zones = pl.pallas_call(
        place_body, name=name.replace("_start", "_own"), in_specs=[any_spec] * n, out_specs=[any_spec] * n,
        out_shape=zones, scratch_shapes=[pltpu.SemaphoreType.DMA((n,))])(*arrays)

    def body(*refs):
        ins, lands = refs[:n], refs[n:2 * n]
        send_sems, recv_sems = refs[2 * n + 1], refs[2 * n + 2]
        token = refs[4 * n + 3]
        place = _my_place()
        me = _index(place)
        for a in range(n):
            for k in range(1, N_DEV):
                peer = _peer(place, k)
                pltpu.make_async_remote_copy(
                    src_ref=ins[a] if kinds[a] == "gather" else ins[a].at[_index(peer)], dst_ref=lands[a].at[me],
                    send_sem=send_sems.at[_sem_slot(a, k)], recv_sem=recv_sems.at[_sem_slot(a, k)],
                    device_id=peer, device_id_type=pl.DeviceIdType.MESH).start()
        token[...] = jnp.zeros_like(token)

    sems = pltpu.SemaphoreType.DMA((n * (N_DEV - 1),))
    hbm = lambda a: pltpu.HBM(a.shape, a.dtype)
    outs = pl.pallas_call(
        body, name=name,
        out_shape=(sems, sems, *[hbm(a) for a in arrays], *[hbm(z) for z in zones],
                   jax.ShapeDtypeStruct((8, LANES), F32)),
        in_specs=[_HBM] * (2 * n) + [pl.BlockSpec(memory_space=pl.ANY)],
        out_specs=(_SEM, _SEM, *[_HBM] * (2 * n), pl.BlockSpec(memory_space=pltpu.VMEM)),
        input_output_aliases={i: 2 + i for i in range(2 * n)},
        compiler_params=pltpu.CompilerParams(has_side_effects=_DATAFLOW),
    )(*[pltpu.with_memory_space_constraint(a, pltpu.HBM) for a in arrays],
      *[pltpu.with_memory_space_constraint(z, pltpu.HBM) for z in zones], after)
    return dict(send=outs[0], recv=outs[1], srcs=outs[2:2 + n], zones=outs[2 + n:2 + 2 * n],
                token=outs[2 + 2 * n], kinds=kinds)


def exchange_wait(handle, after, *, name):
    kinds = handle["kinds"]
    n = len(kinds)

    def body(*refs):
        ins, lands = refs[:n], refs[n:2 * n]
        send_sems, recv_sems = refs[2 * n], refs[2 * n + 1]
        place = _my_place()
        me = _index(place)
        for a in range(n):
            for k in range(1, N_DEV):
                peer = _peer(place, k)
                src = ins[a] if kinds[a] == "gather" else ins[a].at[_index(peer)]
                copy = pltpu.make_async_remote_copy(
                    src_ref=src, dst_ref=lands[a].at[_index(peer)],
                    send_sem=send_sems.at[_sem_slot(a, k)], recv_sem=recv_sems.at[_sem_slot(a, k)],
                    device_id=peer, device_id_type=pl.DeviceIdType.MESH)
                copy.wait_send()
                copy.wait_recv()

    srcs, zones = handle["srcs"], handle["zones"]
    hbm = lambda a: pltpu.HBM(a.shape, a.dtype)
    outs = pl.pallas_call(
        body, name=name,
        out_shape=(*[hbm(a) for a in srcs], *[hbm(z) for z in zones]),
        in_specs=[_HBM] * (2 * n) + [_SEM, _SEM, pl.BlockSpec(memory_space=pl.ANY)],
        out_specs=[_HBM] * (2 * n),
        input_output_aliases={i: i for i in range(2 * n)},
        compiler_params=pltpu.CompilerParams(has_side_effects=_DATAFLOW),
    )(*srcs, *zones, handle["send"], handle["recv"], after)
    return outs[n:]


def reduce_adam_layer(parts, w, m, v, layer, prev, *, name):
    _, r, c = w.shape
    tr = _pick(r, (256, 352, 128))

    def body(*refs):
        p_ref, w_ref, m_ref, v_ref = refs[:4]
        g_ref, d_ref, m2_ref, v2_ref = refs[-4:]
        g = p_ref[0].astype(F32)
        for i in range(1, N_DEV):
            g = g + p_ref[i].astype(F32)
        m2 = B1 * m_ref[...] + (1.0 - B1) * g
        v2 = B2 * v_ref[...] + (1.0 - B2) * (g * g)
        m_hat = m2 / (1.0 - B1 ** STEP)
        v_hat = v2 / (1.0 - B2 ** STEP)
        g_ref[...] = g
        d_ref[...] = -LR * (m_hat / (jnp.sqrt(v_hat) + ADAM_EPS) + WD * w_ref[...])
        m2_ref[...] = m2
        v2_ref[...] = v2

    blk = pl.BlockSpec((None, tr, c), lambda i: (layer, i, 0))
    in_specs = [pl.BlockSpec((N_DEV, tr, c), lambda i: (0, i, 0)), blk, blk, blk]
    args = [parts, w, m, v]
    aliases = {}
    if prev is not None:
        in_specs += [pl.BlockSpec(memory_space=pl.ANY)] * 4
        args += list(prev)
        aliases = {4 + j: j for j in range(4)}
    return pl.pallas_call(
        body, name=name, grid=(r // tr,), in_specs=in_specs, out_specs=[blk] * 4,
        out_shape=[jax.ShapeDtypeStruct(w.shape, F32)] * 4, input_output_aliases=aliases,
        compiler_params=_params(("parallel",)))(*args)


def reduce_adam(parts, w, m, v, *, name):
    r, c = w.shape
    tr = _pick(r, (256, 352, 192, 128, 72, 64, 32, 16, 8))

    def body(p_ref, w_ref, m_ref, v_ref, g_ref, d_ref, m2_ref, v2_ref):
        g = p_ref[0].astype(F32)
        for i in range(1, N_DEV):
            g = g + p_ref[i].astype(F32)
        wv = w_ref[...]
        m2 = B1 * m_ref[...] + (1.0 - B1) * g
        v2 = B2 * v_ref[...] + (1.0 - B2) * (g * g)
        m_hat = m2 / (1.0 - B1 ** STEP)
        v_hat = v2 / (1.0 - B2 ** STEP)
        g_ref[...] = g
        d_ref[...] = -LR * (m_hat / (jnp.sqrt(v_hat) + ADAM_EPS) + WD * wv)
        m2_ref[...] = m2
        v2_ref[...] = v2

    blk = pl.BlockSpec((tr, c), lambda i: (i, 0))
    return pl.pallas_call(
        body, name=name, grid=(r // tr,),
        in_specs=[pl.BlockSpec((N_DEV, tr, c), lambda i: (0, i, 0)), blk, blk, blk],
        out_specs=[blk] * 4, out_shape=[jax.ShapeDtypeStruct((r, c), F32)] * 4,
        compiler_params=_params(("parallel",)))(parts, w, m, v)


BIG = ("w_in", "w_a", "w_pool_grp", "w_b", "w_o", "w_up", "w_down")
SHARDED_SMALL = ("meta_tokens", "w_gk", "conv_w")
REPLICATED = ("norm1_g", "b_gk", "gla_norm_g", "pool_scale", "b_gates", "norm2_g", "conv_b", "final_norm_g")
CUT_AXIS = {"w_in": 2, "w_a": 1, "w_pool_grp": 2, "w_b": 1, "w_o": 1, "w_up": 2, "w_down": 1,
            "meta_tokens": 1, "w_gk": 2, "conv_w": 2}
WEIGHTS = ("meta_tokens", "norm1_g", "w_in", "w_gk", "b_gk", "gla_norm_g", "w_a", "w_pool_grp", "pool_scale",
           "w_b", "b_gates", "w_o", "norm2_g", "w_up", "conv_w", "conv_b", "w_down", "final_norm_g")


def _as_2d(a):
    return a.reshape(-1, a.shape[-1])


def _from_slots(slots, axis):
    full = jnp.moveaxis(slots, 0, axis)
    shape = list(full.shape)
    shape[axis:axis + 2] = [shape[axis] * shape[axis + 1]]
    return full.reshape(shape)


def _to_slots(full, axis):
    shape = list(full.shape)
    shape[axis:axis + 1] = [N_DEV, shape[axis] // N_DEV]
    return jnp.moveaxis(full.reshape(shape), axis, 0)


def _pack(vectors, rows):
    flat = jnp.concatenate([v.reshape(-1).astype(F32) for v in vectors])
    return jnp.pad(flat, (0, rows * LANES - flat.shape[0])).reshape(rows, LANES)


def _unpack(packed, shapes):
    flat = packed.reshape(-1)
    out, off = [], 0
    for s in shapes:
        size = 1
        for d in s:
            size *= d
        out.append(flat[off:off + size].reshape(s))
        off += size
    return out


def _rows_for(shapes, mult=8):
    total = 0
    for s in shapes:
        size = 1
        for d in s:
            size *= d
        total += size
    rows = -(-total // LANES)
    return -(-rows // mult) * mult


def _permute_in(w_in):
    pad = jnp.zeros(w_in.shape[:-1] + (IN_R - IN_WIDTH,), w_in.dtype)
    return jnp.concatenate([w_in[..., :2048], w_in[..., 2064:], w_in[..., 2048:2064], pad], axis=-1)


def _unpermute_in(w_r):
    return jnp.concatenate([w_r[..., :2048], w_r[..., C_GLR:C_GLR + RANK], w_r[..., 2048:C_GLR]], axis=-1)


def kernel(x, meta_tokens, norm1_g, w_in, w_gk, b_gk, gla_norm_g, w_a, w_pool_grp, pool_scale, w_b, b_gates, w_o, norm2_g, w_up, conv_w, conv_b, w_down, final_norm_g, loss_target, m_meta_tokens, m_norm1_g, m_w_in, m_w_gk, m_b_gk, m_gla_norm_g, m_w_a, m_w_pool_grp, m_pool_scale, m_w_b, m_b_gates, m_w_o, m_norm2_g, m_w_up, m_conv_w, m_conv_b, m_w_down, m_final_norm_g, v_meta_tokens, v_norm1_g, v_w_in, v_w_gk, v_b_gk, v_gla_norm_g, v_w_a, v_w_pool_grp, v_pool_scale, v_w_b, v_b_gates, v_w_o, v_norm2_g, v_w_up, v_conv_w, v_conv_b, v_w_down, v_final_norm_g):
    wts = dict(meta_tokens=meta_tokens, norm1_g=norm1_g, w_in=w_in, w_gk=w_gk, b_gk=b_gk, gla_norm_g=gla_norm_g,
               w_a=w_a, w_pool_grp=w_pool_grp, pool_scale=pool_scale, w_b=w_b, b_gates=b_gates, w_o=w_o,
               norm2_g=norm2_g, w_up=w_up, conv_w=conv_w, conv_b=conv_b, w_down=w_down, final_norm_g=final_norm_g)
    mom = dict(meta_tokens=m_meta_tokens, norm1_g=m_norm1_g, w_in=m_w_in, w_gk=m_w_gk, b_gk=m_b_gk,
               gla_norm_g=m_gla_norm_g, w_a=m_w_a, w_pool_grp=m_w_pool_grp, pool_scale=m_pool_scale, w_b=m_w_b,
               b_gates=m_b_gates, w_o=m_w_o, norm2_g=m_norm2_g, w_up=m_w_up, conv_w=m_conv_w, conv_b=m_conv_b,
               w_down=m_w_down, final_norm_g=m_final_norm_g)
    var = dict(meta_tokens=v_meta_tokens, norm1_g=v_norm1_g, w_in=v_w_in, w_gk=v_w_gk, b_gk=v_b_gk,
               gla_norm_g=v_gla_norm_g, w_a=v_w_a, w_pool_grp=v_w_pool_grp, pool_scale=v_pool_scale, w_b=v_w_b,
               b_gates=v_b_gates, w_o=v_w_o, norm2_g=v_norm2_g, w_up=v_w_up, conv_w=v_conv_w, conv_b=v_conv_b,
               w_down=v_w_down, final_norm_g=v_final_norm_g)

    small_shapes = [wts[n].shape for n in SHARDED_SMALL]
    small_rows = _rows_for(small_shapes)

    def shard3(a):
        return a.reshape(DEPTH, -1, a.shape[-1])

    def layer_shards(l, names):
        return [shard3(wts[n])[l].astype(BF16) for n in names]

    def tie(row, handle):
        return row + handle["token"][0:1, 0:1]

    def full_weight(n, zone):
        if n == "w_in":
            return _permute_in(jnp.moveaxis(zone, 0, 1).reshape(D, IN_WIDTH))
        if n == "w_up":
            return jnp.moveaxis(zone, 0, 1).reshape(D, F2)
        if n == "w_pool_grp":
            return jnp.moveaxis(zone.reshape(N_DEV, GROUPS, GDIM // N_DEV, GDIM), 0, 1).reshape(GROUPS, GDIM, GDIM)
        return zone.reshape(-1, zone.shape[-1])

    groups = [("w_in",), ("w_a", "w_pool_grp", "w_b", "w_o"), ("w_up", "w_down")]
    rest = groups[0] + groups[1]
    key = {"w_pool_grp": "w_pool"}
    rows = dict(norm1_g=norm1_g, b_gk=b_gk, gla_norm_g=gla_norm_g, pool_scale=pool_scale, b_gates=b_gates,
                norm2_g=norm2_g, conv_b=conv_b)

    def gather(l, names, after, name, head=()):
        return exchange_start(list(head) + layer_shards(l, names), ["gather"] * (len(head) + len(names)), after,
                              name=name + "_start")

    def landed(handle, after, name, names, w_layer):
        zones = exchange_wait(handle, after, name=name + "_wait")
        for n, z in zip(names, zones[len(zones) - len(names):]):
            w_layer[key.get(n, n)] = full_weight(n, z)
        return zones

    wl = [{n: v[l:l + 1] for n, v in rows.items()} for l in range(DEPTH)]
    g_in0 = gather(0, groups[0], x, "gather_in0", head=[_pack([wts[n] for n in SHARDED_SMALL], small_rows)])
    zones = landed(g_in0, g_in0["token"], "gather_in0", groups[0], wl[0])
    small_slots = [jnp.stack(parts) for parts in zip(*[_unpack(zones[0][i], small_shapes) for i in range(N_DEV)])]
    small_full = {n: _from_slots(slots, CUT_AXIS[n]) for n, slots in zip(SHARDED_SMALL, small_slots)}
    w_gk_pad = jnp.pad(small_full["w_gk"], ((0, 0), (0, LANES - RANK), (0, 0))).astype(BF16)
    for l in range(DEPTH):
        wl[l]["w_gk"] = w_gk_pad[l]
        wl[l]["conv_w"] = small_full["conv_w"][l]
    g_mix0 = gather(0, groups[1], zones[1], "gather_mix0")
    g_ffn0 = gather(0, groups[2], g_mix0["token"], "gather_ffn0")
    wl[0]["norm1_g"] = tie(wl[0]["norm1_g"], g_ffn0)

    h = jnp.concatenate([jnp.zeros((PAD, D), F32), small_full["meta_tokens"], x[0]], axis=0)
    s0 = {"h": h}
    fwd_in(s0, wl[0], "l0_")
    zones = landed(g_mix0, s0["p"], "gather_mix0", groups[1], wl[0])
    g_in1 = gather(1, groups[0], zones[0], "gather_in1")
    wl[0]["b_gk"] = tie(wl[0]["b_gk"], g_in1)
    fwd_mixer(s0, wl[0], "l0_")
    zones = landed(g_ffn0, s0["h2"], "gather_ffn0", groups[2], wl[0])
    g_mix1 = gather(1, groups[1], zones[0], "gather_mix1")
    g_ffn1 = gather(1, groups[2], g_mix1["token"], "gather_ffn1")
    wl[0]["norm2_g"] = tie(wl[0]["norm2_g"], g_ffn1)
    fwd_ffn(s0, wl[0], "l0_")
    landed(g_in1, s0["h3"], "gather_in1", groups[0], wl[1])
    s1 = {"h": s0["h3"]}
    fwd_in(s1, wl[1], "l1_")
    landed(g_mix1, s1["p"], "gather_mix1", groups[1], wl[1])
    fwd_mixer(s1, wl[1], "l1_")
    landed(g_ffn1, s1["h2"], "gather_ffn1", groups[2], wl[1])
    fwd_ffn(s1, wl[1], "l1_")
    dh, dh_b, dgf, loss_rows = loss_head(s1["h3"], final_norm_g[None], loss_target[0], name="loss_head")
    loss_part = 0.5 * jnp.sum(loss_rows) / D

    def blocks(n, gw):
        if n == "w_in":
            return jnp.moveaxis(_unpermute_in(gw).reshape(D, N_DEV, IN_WIDTH // N_DEV), 1, 0)
        if n == "w_up":
            return jnp.moveaxis(gw.reshape(D, N_DEV, F2 // N_DEV), 1, 0)
        if n == "w_pool_grp":
            gw = gw.astype(BF16).reshape(GROUPS, N_DEV, GDIM // N_DEV, GDIM)
            return jnp.moveaxis(gw, 1, 0).reshape(N_DEV, GROUPS * GDIM // N_DEV, GDIM)
        return gw.reshape(N_DEV, gw.shape[0] // N_DEV, gw.shape[1])

    def scatter(g, names, after, name):
        return exchange_start([blocks(n, g[key.get(n, n)]) for n in names], ["scatter"] * len(names), after,
                              name=name + "_start")

    g1, g0 = {}, {}
    dh2, dh2_b = bwd_ffn(dh, dh_b, s1, wl[1], g1, "l1_")
    s_ffn1 = scatter(g1, groups[2], dh2, "scatter_ffn1")
    dp = bwd_mixer(dh2_b, s1, wl[1], g1, "l1_", after=s_ffn1["token"])
    g1["w_in"] = bwd_in_w(dp, s1, "l1_")
    s_rest1 = scatter(g1, rest, s_ffn1["token"], "scatter_rest1")
    dh, dh_b = bwd_in_x(dp, dh2, s1, wl[1], g1, "l1_", after=s_rest1["token"])
    dh2, dh2_b = bwd_ffn(dh, dh_b, s0, wl[0], g0, "l0_")
    r_ffn1 = exchange_wait(s_ffn1, dh2, name="scatter_ffn1_wait")
    s_ffn0 = scatter(g0, groups[2], r_ffn1[0], "scatter_ffn0")
    dp = bwd_mixer(dh2_b, s0, wl[0], g0, "l0_", after=s_ffn0["token"])
    r_rest1 = exchange_wait(s_rest1, dp, name="scatter_rest1_wait")
    g0["w_in"] = bwd_in_w(dp, s0, "l0_")
    s_rest0 = scatter(g0, rest, r_rest1[0], "scatter_rest0")
    dh, _ = bwd_in_x(dp, dh2, s0, wl[0], g0, "l0_", after=s_rest0["token"])
    r_ffn0 = exchange_wait(s_ffn0, dh, name="scatter_ffn0_wait")
    r_rest0 = exchange_wait(s_rest0, r_ffn0[0], name="scatter_rest0_wait")
    grad_x = dh[X0:]
    recv = [dict(zip(groups[2] + rest, list(r_ffn0) + list(r_rest0))),
            dict(zip(groups[2] + rest, list(r_ffn1) + list(r_rest1)))]

    grads, delta, new_m, new_v = {}, {}, {}, {}
    for n in BIG:
        w3, m3, v3 = shard3(wts[n]), shard3(mom[n]), shard3(var[n])
        first = reduce_adam_layer(recv[1][n], w3, m3, v3, 1, None, name="adam_l1_" + n)
        outs = reduce_adam_layer(recv[0][n], w3, m3, v3, 0, first, name="adam_l0_" + n)
        grads[n], delta[n], new_m[n], new_v[n] = [o.reshape(wts[n].shape) for o in outs]

    g_full = {n: jnp.stack([g0[n], g1[n]])[:, 0] for n in rows}
    g_full["final_norm_g"] = dgf[0]
    g_full["meta_tokens"] = dh[PAD:X0]
    g_full["w_gk"] = jnp.stack([g0["w_gk"], g1["w_gk"]])[:, :RANK]
    g_full["conv_w"] = jnp.stack([g0["conv_w"], g1["conv_w"]])
    rep_shapes = [wts[n].shape for n in REPLICATED] + [(1,)]
    rep_rows = _rows_for(rep_shapes)
    small_blocks = jnp.stack([
        _pack([_to_slots(g_full[n], CUT_AXIS[n])[i] for n in SHARDED_SMALL], small_rows) for i in range(N_DEV)])
    rep_pack = _pack([g_full[n] for n in REPLICATED] + [loss_part.reshape(1)], rep_rows)
    received = exchange([small_blocks, rep_pack], ["scatter", "gather"], name="exchange_small")
    outs = reduce_adam(received[-2], _pack([wts[n] for n in SHARDED_SMALL], small_rows),
                       _pack([mom[n] for n in SHARDED_SMALL], small_rows),
                       _pack([var[n] for n in SHARDED_SMALL], small_rows), name="adam_small")
    for d, o in zip((grads, delta, new_m, new_v), outs):
        for n, a in zip(SHARDED_SMALL, _unpack(o, small_shapes)):
            d[n] = a
    one = [jnp.zeros((1,), F32)]
    outs = reduce_adam(received[-1], _pack([wts[n] for n in REPLICATED] + one, rep_rows),
                       _pack([mom[n] for n in REPLICATED] + one, rep_rows),
                       _pack([var[n] for n in REPLICATED] + one, rep_rows), name="adam_replicated")
    for d, o in zip((grads, delta, new_m, new_v), outs):
        for n, a in zip(REPLICATED + ("loss",), _unpack(o, rep_shapes)):
            d[n] = a
    loss = grads["loss"][0]
    return (loss, grad_x[None], *[grads[n] for n in WEIGHTS], *[delta[n] for n in WEIGHTS],
            *[new_m[n] for n in WEIGHTS], *[new_v[n] for n in WEIGHTS])
```

```python
import functools

import jax
import jax.numpy as jnp
from jax import lax
from jax.experimental import pallas as pl
from jax.experimental.pallas import tpu as pltpu

F32 = jnp.float32
BF16 = jnp.bfloat16

D = 1024
DEPTH = 2
N_META = 16
HEADS = 4
DK = 512
DV = 1024
HK = 128
HV = 256
RANK = 16
TAU = 16.0
CHUNK = 64
POOL_WINDOWS = (2, 4, 8, 16)
GROUPS = 4
GDIM = 256
D_FF = 2816
F2 = 2 * D_FF
EPS = 1e-6
IN_WIDTH = 6160
LR, B1, B2, ADAM_EPS, WD, STEP = 0.001, 0.9, 0.999, 1e-8, 0.01, 10

N_DEV = 8
PAD = CHUNK - N_META
X0 = CHUNK
IN_R = 6272
C_Q, C_K, C_V, C_R, C_U, C_GA, C_GB, C_GLR = 0, 512, 1024, 2048, 3072, 4096, 5120, 6144
VMEM_LIMIT = 56 * 1024 * 1024
LANES = 128


def _params(sem=None):
    return pltpu.CompilerParams(dimension_semantics=sem, vmem_limit_bytes=VMEM_LIMIT)


def _pick(n, prefs):
    for t in prefs:
        if n % t == 0:
            return t
    raise ValueError(f"no tile for {n} in {prefs}")


def _row_tile(lp):
    return _pick(lp, (688, 192, 128, 64))


def _ew_tile(lp):
    return _pick(lp, (192, 128, 64))


def _sigmoid(x):
    return 1.0 / (1.0 + jnp.exp(-x))


def _dot(a, b, dims):
    return lax.dot_general(a, b, (dims, ((), ())), preferred_element_type=F32)


def _nn(a, b):
    return _dot(a, b, ((1,), (0,)))


def _nt(a, b):
    return _dot(a, b, ((1,), (1,)))


def _tn(a, b):
    return _dot(a, b, ((0,), (0,)))


def mm_nn(a, b, *, out_dtype=BF16, tn=None, res=None, name):
    m, k = a.shape
    n = b.shape[1]
    tm = _row_tile(m)
    tn = tn or n
    has_res = res is not None

    def body(*refs):
        if has_res:
            a_ref, b_ref, r_ref, o_ref = refs
        else:
            a_ref, b_ref, o_ref = refs
        acc = _nn(a_ref[...], b_ref[...])
        if has_res:
            row = pl.program_id(1) * tm + lax.broadcasted_iota(jnp.int32, (tm, 1), 0)
            acc = jnp.where(row >= PAD, acc + r_ref[...], 0.0)
        o_ref[...] = acc.astype(o_ref.dtype)

    in_specs = [pl.BlockSpec((tm, k), lambda j, i: (i, 0)),
                pl.BlockSpec((k, tn), lambda j, i: (0, j))]
    args = [a, b]
    if has_res:
        in_specs.append(pl.BlockSpec((tm, tn), lambda j, i: (i, j)))
        args.append(res)
    return pl.pallas_call(
        body, name=name, grid=(n // tn, m // tm), in_specs=in_specs,
        out_specs=pl.BlockSpec((tm, tn), lambda j, i: (i, j)),
        out_shape=jax.ShapeDtypeStruct((m, n), out_dtype),
        compiler_params=_params(("parallel", "parallel")))(*args)


def mm_nt(a, b, *, out_dtype=BF16, tn=None, tk=None, after=None, halves=False, name):
    m, k = (a.shape[1], 2 * a.shape[2]) if halves else a.shape
    n = b.shape[0]
    tm = _row_tile(m)
    tn = tn or n
    tk = tk or k
    nk = k // tk
    extra = [] if after is None else [after]
    if halves:
        per = nk // 2
        a_spec = pl.BlockSpec((None, tm, tk), lambda j, i, kk: (kk // per, i, kk % per))
    else:
        a_spec = pl.BlockSpec((tm, tk), lambda j, i, kk: (i, kk))

    def body(a_ref, b_ref, *rest):
        o_ref, acc_ref = rest[-2:]
        kk = pl.program_id(2)
        part = _nt(a_ref[...], b_ref[...])

        @pl.when(kk == 0)
        def _():
            acc_ref[...] = part

        @pl.when(kk > 0)
        def _():
            acc_ref[...] += part

        @pl.when(kk == nk - 1)
        def _():
            o_ref[...] = acc_ref[...].astype(o_ref.dtype)

    return pl.pallas_call(
        body, name=name, grid=(n // tn, m // tm, nk),
        in_specs=[a_spec, pl.BlockSpec((tn, tk), lambda j, i, kk: (j, kk))]
                 + [pl.BlockSpec(memory_space=pl.ANY)] * len(extra),
        out_specs=pl.BlockSpec((tm, tn), lambda j, i, kk: (i, j)),
        out_shape=jax.ShapeDtypeStruct((m, n), out_dtype),
        scratch_shapes=[pltpu.VMEM((tm, tn), F32)],
        compiler_params=_params(("parallel", "parallel", "arbitrary")))(a, b, *extra)


def mm_tn(a, b, *, tk1=None, tn=None, out_dtype=F32, after=None, halves=False, name):
    m, k1 = a.shape
    n = 2 * b.shape[2] if halves else b.shape[1]
    tm = _row_tile(m)
    tk1 = tk1 or k1
    tn = tn or n
    nm = m // tm
    extra = [] if after is None else [after]
    if halves:
        per = n // tn // 2
        b_spec = pl.BlockSpec((None, tm, tn), lambda p, j, i: (j // per, i, j % per))
    else:
        b_spec = pl.BlockSpec((tm, tn), lambda p, j, i: (i, j))

    def body(a_ref, b_ref, *rest):
        o_ref, acc_ref = rest[-2:]
        i = pl.program_id(2)
        part = _tn(a_ref[...], b_ref[...])

        @pl.when(i == 0)
        def _():
            acc_ref[...] = part

        @pl.when(i > 0)
        def _():
            acc_ref[...] += part

        @pl.when(i == nm - 1)
        def _():
            o_ref[...] = acc_ref[...].astype(o_ref.dtype)

    return pl.pallas_call(
        body, name=name, grid=(k1 // tk1, n // tn, nm),
        in_specs=[pl.BlockSpec((tm, tk1), lambda p, j, i: (i, p)), b_spec]
                 + [pl.BlockSpec(memory_space=pl.ANY)] * len(extra),
        out_specs=pl.BlockSpec((tk1, tn), lambda p, j, i: (p, j)),
        out_shape=jax.ShapeDtypeStruct((k1, n), out_dtype),
        scratch_shapes=[pltpu.VMEM((tk1, tn), F32)],
        compiler_params=_params(("parallel", "parallel", "arbitrary")))(a, b, *extra)


def pool_mm_fwd(pooled, wp, scale, *, name):
    m = pooled.shape[0]
    tm = _row_tile(m)

    def body(a_ref, w_ref, s_ref, y0_ref, y1_ref):
        acc = _nn(a_ref[...], w_ref[...])
        y0_ref[...] = acc.astype(BF16)
        y1_ref[...] = (acc * s_ref[...]).astype(BF16)

    blk = pl.BlockSpec((tm, GDIM), lambda g, i: (i, g))
    return pl.pallas_call(
        body, name=name, grid=(GROUPS, m // tm),
        in_specs=[blk, pl.BlockSpec((None, GDIM, GDIM), lambda g, i: (g, 0, 0)),
                  pl.BlockSpec((1, GDIM), lambda g, i: (0, g))],
        out_specs=[blk, blk],
        out_shape=[jax.ShapeDtypeStruct((m, D), BF16)] * 2,
        compiler_params=_params(("parallel", "parallel")))(pooled, wp, scale)


def pool_mm_bwd_x(dy0, wp, *, name):
    m = dy0.shape[0]
    tm = _row_tile(m)

    def body(a_ref, w_ref, o_ref):
        o_ref[...] = _nt(a_ref[...], w_ref[...]).astype(BF16)

    blk = pl.BlockSpec((tm, GDIM), lambda g, i: (i, g))
    return pl.pallas_call(
        body, name=name, grid=(GROUPS, m // tm),
        in_specs=[blk, pl.BlockSpec((None, GDIM, GDIM), lambda g, i: (g, 0, 0))],
        out_specs=blk, out_shape=jax.ShapeDtypeStruct((m, D), BF16),
        compiler_params=_params(("parallel", "parallel")))(dy0, wp)


def pool_mm_bwd_w(pooled, dy0, *, name):
    m = pooled.shape[0]
    tm = _row_tile(m)

    def body(a_ref, b_ref, o_ref):
        part = _tn(a_ref[...], b_ref[...])

        @pl.when(pl.program_id(1) == 0)
        def _():
            o_ref[...] = part

        @pl.when(pl.program_id(1) > 0)
        def _():
            o_ref[...] += part

    blk = pl.BlockSpec((tm, GDIM), lambda g, i: (i, g))
    return pl.pallas_call(
        body, name=name, grid=(GROUPS, m // tm), in_specs=[blk, blk],
        out_specs=pl.BlockSpec((None, GDIM, GDIM), lambda g, i: (g, 0, 0)),
        out_shape=jax.ShapeDtypeStruct((GROUPS, GDIM, GDIM), F32),
        compiler_params=_params(("parallel", "arbitrary")))(pooled, dy0)


def rmsnorm_fwd(x, g, *, name):
    m = x.shape[0]
    tm = _ew_tile(m)

    def body(x_ref, g_ref, o_ref):
        xv = x_ref[...]
        r = lax.rsqrt(jnp.mean(xv * xv, axis=-1, keepdims=True) + EPS)
        o_ref[...] = (xv * r * g_ref[...]).astype(BF16)

    return pl.pallas_call(
        body, name=name, grid=(m // tm,),
        in_specs=[pl.BlockSpec((tm, D), lambda i: (i, 0)), pl.BlockSpec((1, D), lambda i: (0, 0))],
        out_specs=pl.BlockSpec((tm, D), lambda i: (i, 0)),
        out_shape=jax.ShapeDtypeStruct((m, D), BF16),
        compiler_params=_params(("parallel",)))(x, g)


def rmsnorm_bwd(dy, x, g, dres, *, name):
    m = x.shape[0]
    tm = _ew_tile(m)

    def body(dy_ref, x_ref, g_ref, r_ref, dx_ref, dxb_ref, dg_ref):
        i = pl.program_id(0)
        xv = x_ref[...]
        dyv = dy_ref[...].astype(F32)
        r = lax.rsqrt(jnp.mean(xv * xv, axis=-1, keepdims=True) + EPS)
        xh = xv * r
        dxh = dyv * g_ref[...]
        dx = r * (dxh - xh * jnp.mean(dxh * xh, axis=-1, keepdims=True))
        row = i * tm + lax.broadcasted_iota(jnp.int32, (tm, 1), 0)
        dx = jnp.where(row >= PAD, dx + r_ref[...], 0.0)
        dx_ref[...] = dx
        dxb_ref[...] = dx.astype(BF16)

        @pl.when(i == 0)
        def _():
            dg_ref[...] = jnp.zeros_like(dg_ref)

        dg_ref[...] += jnp.sum(dyv * xh, axis=0, keepdims=True)

    blk = pl.BlockSpec((tm, D), lambda i: (i, 0))
    vec = pl.BlockSpec((1, D), lambda i: (0, 0))
    return pl.pallas_call(
        body, name=name, grid=(m // tm,), in_specs=[blk, blk, vec, blk],
        out_specs=[blk, blk, vec],
        out_shape=[jax.ShapeDtypeStruct((m, D), F32), jax.ShapeDtypeStruct((m, D), BF16),
                   jax.ShapeDtypeStruct((1, D), F32)],
        compiler_params=_params(("arbitrary",)))(dy, x, g, dres)


def loss_head(h, gf, target, *, name):
    m = h.shape[0]
    t = X0
    inv_d = 1.0 / D

    def body(h_ref, g_ref, t_ref, dh_ref, dhb_ref, dg_ref, ls_ref):
        i = pl.program_id(0)

        @pl.when(i == 0)
        def _():
            dg_ref[...] = jnp.zeros_like(dg_ref)
            ls_ref[...] = jnp.zeros_like(ls_ref)
            dh_ref[...] = jnp.zeros_like(dh_ref)
            dhb_ref[...] = jnp.zeros_like(dhb_ref)

        @pl.when(i > 0)
        def _():
            xv = h_ref[...]
            r = lax.rsqrt(jnp.mean(xv * xv, axis=-1, keepdims=True) + EPS)
            xh = xv * r
            err = xh * g_ref[...] - t_ref[...]
            ls_ref[...] += jnp.sum(err * err, axis=0, keepdims=True)
            dy = err * inv_d
            dg_ref[...] += jnp.sum(dy * xh, axis=0, keepdims=True)
            dxh = dy * g_ref[...]
            dh = r * (dxh - xh * jnp.mean(dxh * xh, axis=-1, keepdims=True))
            dh_ref[...] = dh
            dhb_ref[...] = dh.astype(BF16)

    blk = pl.BlockSpec((t, D), lambda i: (i, 0))
    vec = pl.BlockSpec((1, D), lambda i: (0, 0))
    return pl.pallas_call(
        body, name=name, grid=(m // t,),
        in_specs=[blk, vec, pl.BlockSpec((t, D), lambda i: (jnp.maximum(i - 1, 0), 0))],
        out_specs=[blk, blk, vec, vec],
        out_shape=[jax.ShapeDtypeStruct((m, D), F32), jax.ShapeDtypeStruct((m, D), BF16),
                   jax.ShapeDtypeStruct((1, D), F32), jax.ShapeDtypeStruct((1, D), F32)],
        compiler_params=_params(("arbitrary",)))(h, gf, target)


def _split3(x):
    x1 = x.astype(BF16)
    r1 = x - x1.astype(F32)
    x2 = r1.astype(BF16)
    x3 = (r1 - x2.astype(F32)).astype(BF16)
    return x1, x2, x3


def _tri_mm(tri, x):
    x1, x2, x3 = _split3(x)
    return _nn(tri, x1) + _nn(tri, x2) + _nn(tri, x3)


def _log_decay(glr, wgk, bgk, row0, rows):
    z = _nn(glr, wgk) + bgk
    la = (jnp.minimum(z, 0.0) - jnp.log(1.0 + jnp.exp(-jnp.abs(z)))) * (1.0 / TAU)
    row = row0 + lax.broadcasted_iota(jnp.int32, (rows, 1), 0)
    return z, jnp.where(row >= PAD, la, 0.0)


def _chunk_group(n_chunks):
    return _pick(n_chunks, (3, 2, 1))


def gla_fwd(p, wgk, bgk, *, name):
    m = p.shape[0]
    n_chunks = m // CHUNK
    cg = _chunk_group(n_chunks)
    t = cg * CHUNK
    scale = HK ** -0.5

    def body(q_ref, k_ref, v_ref, glr_ref, wgk_ref, bgk_ref, o_ref, st_ref, state):
        i = pl.program_id(0)

        @pl.when(i == 0)
        def _():
            state[...] = jnp.zeros_like(state)

        _, la = _log_decay(glr_ref[...], wgk_ref[...], bgk_ref[...], i * t, t)
        ri = lax.broadcasted_iota(jnp.int32, (CHUNK, CHUNK), 0)
        ci = lax.broadcasted_iota(jnp.int32, (CHUNK, CHUNK), 1)
        causal = ri >= ci
        tri = causal.astype(BF16)
        for c in range(cg):
            rows = pl.ds(c * CHUNK, CHUNK)
            b = _tri_mm(tri, la[c * CHUNK:(c + 1) * CHUNK])
            bl = b[CHUNK - 1:CHUNK, :]
            q = q_ref[rows, :].astype(F32) * scale
            k = k_ref[rows, :].astype(F32)
            qd = (q * jnp.exp(b)).astype(BF16)
            ki = (k * jnp.exp(-b)).astype(BF16)
            ke = (k * jnp.exp(bl - b)).astype(BF16)
            dec = jnp.exp(bl)
            for h in range(HEADS):
                ks = slice(h * HK, (h + 1) * HK)
                vs = pl.ds(h * HV, HV)
                vh = v_ref[rows, vs]
                s_t = state[h]
                st_ref[c, h] = s_t
                att = jnp.where(causal, _nt(qd[:, ks], ki[:, ks]), 0.0).astype(BF16)
                o_ref[rows, vs] = _nn(att, vh) + _nt(qd[:, ks], s_t.astype(BF16))
                state[h] = s_t * dec[:, ks] + _tn(vh, ke[:, ks])

    return pl.pallas_call(
        body, name=name, grid=(n_chunks // cg,),
        in_specs=[pl.BlockSpec((t, DK), lambda i: (i, C_Q // DK)),
                  pl.BlockSpec((t, DK), lambda i: (i, C_K // DK)),
                  pl.BlockSpec((t, DV), lambda i: (i, C_V // DV)),
                  pl.BlockSpec((t, LANES), lambda i: (i, C_GLR // LANES)),
                  pl.BlockSpec((LANES, DK), lambda i: (0, 0)),
                  pl.BlockSpec((1, DK), lambda i: (0, 0))],
        out_specs=[pl.BlockSpec((t, DV), lambda i: (i, 0)),
                   pl.BlockSpec((cg, HEADS, HV, HK), lambda i: (i, 0, 0, 0))],
        out_shape=[jax.ShapeDtypeStruct((m, DV), F32),
                   jax.ShapeDtypeStruct((n_chunks, HEADS, HV, HK), F32)],
        scratch_shapes=[pltpu.VMEM((HEADS, HV, HK), F32)],
        compiler_params=_params(("arbitrary",)))(p, p, p, p, wgk, bgk)


def gla_bwd(p, wgk, bgk, st, do, *, name):
    m = p.shape[0]
    n_chunks = m // CHUNK
    cg = _chunk_group(n_chunks)
    t = cg * CHUNK
    ns = n_chunks // cg
    scale = HK ** -0.5

    def body(q_ref, k_ref, v_ref, glr_ref, wgk_ref, bgk_ref, st_ref, do_ref,
             dqkv_ref, dglr_ref, dwgk_ref, dbgk_ref, dstate, dz_buf):
        i = pl.program_id(0)
        blk = ns - 1 - i

        @pl.when(i == 0)
        def _():
            dstate[...] = jnp.zeros_like(dstate)
            dwgk_ref[...] = jnp.zeros_like(dwgk_ref)
            dbgk_ref[...] = jnp.zeros_like(dbgk_ref)

        z, la = _log_decay(glr_ref[...], wgk_ref[...], bgk_ref[...], blk * t, t)
        ri = lax.broadcasted_iota(jnp.int32, (CHUNK, CHUNK), 0)
        ci = lax.broadcasted_iota(jnp.int32, (CHUNK, CHUNK), 1)
        causal = ri >= ci
        tri = causal.astype(BF16)
        tri_u = (ri <= ci).astype(BF16)
        for c in reversed(range(cg)):
            rows = pl.ds(c * CHUNK, CHUNK)
            b = _tri_mm(tri, la[c * CHUNK:(c + 1) * CHUNK])
            bl = b[CHUNK - 1:CHUNK, :]
            eb = jnp.exp(b)
            enb = jnp.exp(-b)
            ebl = jnp.exp(bl - b)
            dec = jnp.exp(bl)
            q = q_ref[rows, :].astype(F32) * scale
            k = k_ref[rows, :].astype(F32)
            qd32 = q * eb
            ki32 = k * enb
            ke32 = k * ebl
            qd = qd32.astype(BF16)
            ki = ki32.astype(BF16)
            ke = ke32.astype(BF16)
            dqd_parts, dki_parts, dke_parts, ddec_parts = [], [], [], []
            for h in range(HEADS):
                ks = slice(h * HK, (h + 1) * HK)
                vs = pl.ds(h * HV, HV)
                vh = v_ref[rows, vs]
                doh = do_ref[rows, vs].astype(BF16)
                s_t = st_ref[c, h]
                ds_t = dstate[h]
                ds_b = ds_t.astype(BF16)
                att = jnp.where(causal, _nt(qd[:, ks], ki[:, ks]), 0.0).astype(BF16)
                datt = jnp.where(causal, _nt(doh, vh), 0.0).astype(BF16)
                dvh = _tn(att, doh) + _nt(ke[:, ks], ds_b)
                dqkv_ref[rows, pl.ds(2 * DK + h * HV, HV)] = dvh.astype(BF16)
                dqd_parts.append(_nn(datt, ki[:, ks]) + _nn(doh, s_t.astype(BF16)))
                dki_parts.append(_tn(datt, qd[:, ks]))
                dke_parts.append(_nn(vh, ds_b))
                ddec_parts.append(jnp.sum(s_t * ds_t, axis=0, keepdims=True))
                dstate[h] = _tn(doh, qd[:, ks]) + ds_t * dec[:, ks]
            dqd = jnp.concatenate(dqd_parts, axis=1)
            dki = jnp.concatenate(dki_parts, axis=1)
            dke = jnp.concatenate(dke_parts, axis=1)
            ddec = jnp.concatenate(ddec_parts, axis=1)
            dqkv_ref[rows, pl.ds(0, DK)] = (dqd * eb * scale).astype(BF16)
            dqkv_ref[rows, pl.ds(DK, DK)] = (dki * enb + dke * ebl).astype(BF16)
            dke_ke = dke * ke32
            db = dqd * qd32 - dki * ki32 - dke_ke
            dbl = jnp.sum(dke_ke, axis=0, keepdims=True) + ddec * dec
            dg = _tri_mm(tri_u, db) + dbl
            row = blk * t + c * CHUNK + lax.broadcasted_iota(jnp.int32, (CHUNK, 1), 0)
            zc = z[c * CHUNK:(c + 1) * CHUNK]
            dz = jnp.where(row >= PAD, dg * (1.0 / TAU) * _sigmoid(-zc), 0.0)
            dz_buf[rows, :] = dz
        dz_all = dz_buf[...]
        dz_b = dz_all.astype(BF16)
        dbgk_ref[...] += jnp.sum(dz_all, axis=0, keepdims=True)
        dglr_ref[...] = _nt(dz_b, wgk_ref[...]).astype(BF16)
        dwgk_ref[...] += _tn(glr_ref[...], dz_b)

    rev = lambda i: ns - 1 - i
    return pl.pallas_call(
        body, name=name, grid=(ns,),
        in_specs=[pl.BlockSpec((t, DK), lambda i: (rev(i), C_Q // DK)),
                  pl.BlockSpec((t, DK), lambda i: (rev(i), C_K // DK)),
                  pl.BlockSpec((t, DV), lambda i: (rev(i), C_V // DV)),
                  pl.BlockSpec((t, LANES), lambda i: (rev(i), C_GLR // LANES)),
                  pl.BlockSpec((LANES, DK), lambda i: (0, 0)),
                  pl.BlockSpec((1, DK), lambda i: (0, 0)),
                  pl.BlockSpec((cg, HEADS, HV, HK), lambda i: (rev(i), 0, 0, 0)),
                  pl.BlockSpec((t, DV), lambda i: (rev(i), 0))],
        out_specs=[pl.BlockSpec((t, 2 * DK + DV), lambda i: (rev(i), 0)),
                   pl.BlockSpec((t, LANES), lambda i: (rev(i), 0)),
                   pl.BlockSpec((LANES, DK), lambda i: (0, 0)),
                   pl.BlockSpec((1, DK), lambda i: (0, 0))],
        out_shape=[jax.ShapeDtypeStruct((m, 2 * DK + DV), BF16),
                   jax.ShapeDtypeStruct((m, LANES), BF16),
                   jax.ShapeDtypeStruct((LANES, DK), F32),
                   jax.ShapeDtypeStruct((1, DK), F32)],
        scratch_shapes=[pltpu.VMEM((HEADS, HV, HK), F32), pltpu.VMEM((t, DK), F32)],
        compiler_params=_params(("arbitrary",)))(p, p, p, p, wgk, bgk, st, do)


HALO = 16


def _shift_down(xx, s):
    return pltpu.roll(xx, s, 0)


def _shift_up(xx, s):
    return pltpu.roll(xx, xx.shape[0] - s, 0)


def mix_pre(o, p, gn, *, name):
    m = o.shape[0]
    tm = _ew_tile(m)

    def body(o_ref, r_ref, u_ref, gn_ref, ya_ref, pooled_ref, halo):
        i = pl.program_id(0)

        @pl.when(i == 0)
        def _():
            halo[...] = jnp.zeros_like(halo)

        rv = r_ref[...].astype(F32)
        silu_r = rv * _sigmoid(rv)
        for h in range(HEADS):
            cs = pl.ds(h * HV, HV)
            ov = o_ref[:, cs]
            rs = lax.rsqrt(jnp.mean(ov * ov, axis=-1, keepdims=True) + EPS)
            ya_ref[:, cs] = (ov * rs * gn_ref[...] * silu_r[:, h * HV:(h + 1) * HV]).astype(BF16)

        row = i * tm + lax.broadcasted_iota(jnp.int32, (tm, 1), 0)
        pos1 = jnp.maximum(row - PAD + 1, 1).astype(F32)
        for g, w in enumerate(POOL_WINDOWS):
            cs = pl.ds(g * GDIM, GDIM)
            uv = u_ref[:, cs].astype(F32)
            xx = jnp.concatenate([halo[:, cs], uv], axis=0)
            s = xx
            span = 1
            while span < w:
                s = s + _shift_down(s, span)
                span *= 2
            inv = 1.0 / jnp.minimum(pos1, float(w))
            pooled_ref[:, cs] = (s[HALO:] * inv - uv).astype(BF16)
            halo[:, cs] = uv[tm - HALO:]

    blk = pl.BlockSpec((tm, D), lambda i: (i, 0))
    return pl.pallas_call(
        body, name=name, grid=(m // tm,),
        in_specs=[blk, pl.BlockSpec((tm, D), lambda i: (i, C_R // D)),
                  pl.BlockSpec((tm, D), lambda i: (i, C_U // D)),
                  pl.BlockSpec((1, HV), lambda i: (0, 0))],
        out_specs=[blk, blk],
        out_shape=[jax.ShapeDtypeStruct((m, D), BF16)] * 2,
        scratch_shapes=[pltpu.VMEM((HALO, D), F32)],
        compiler_params=_params(("arbitrary",)))(o, p, p, gn)


def mix_pre_bwd(dya, dpooled, o, p, gn, *, name):
    m = o.shape[0]
    tm = _ew_tile(m)
    nt = m // tm

    def body(dya_ref, dpl_ref, o_ref, r_ref, gn_ref, do_ref, dr_ref, du_ref, dgn_ref, halo):
        i = pl.program_id(0)
        blk_i = nt - 1 - i

        @pl.when(i == 0)
        def _():
            halo[...] = jnp.zeros_like(halo)
            dgn_ref[...] = jnp.zeros_like(dgn_ref)

        rv = r_ref[...].astype(F32)
        sg = _sigmoid(rv)
        silu_r = rv * sg
        dsilu = sg * (1.0 + rv * (1.0 - sg))
        dgn = jnp.zeros((1, HV), F32)
        for h in range(HEADS):
            cs = pl.ds(h * HV, HV)
            hs = slice(h * HV, (h + 1) * HV)
            ov = o_ref[:, cs]
            dy = dya_ref[:, cs].astype(F32)
            rs = lax.rsqrt(jnp.mean(ov * ov, axis=-1, keepdims=True) + EPS)
            xh = ov * rs
            on = xh * gn_ref[...]
            don = dy * silu_r[:, hs]
            dr_ref[:, cs] = (dy * on * dsilu[:, hs]).astype(BF16)
            dxh = don * gn_ref[...]
            do_ref[:, cs] = rs * (dxh - xh * jnp.mean(dxh * xh, axis=-1, keepdims=True))
            dgn = dgn + jnp.sum(don * xh, axis=0, keepdims=True)
        dgn_ref[...] += dgn

        row = blk_i * tm + lax.broadcasted_iota(jnp.int32, (tm, 1), 0)
        pos1 = jnp.maximum(row - PAD + 1, 1).astype(F32)
        for g, w in enumerate(POOL_WINDOWS):
            cs = pl.ds(g * GDIM, GDIM)
            dpv = dpl_ref[:, cs].astype(F32)
            e = dpv * (1.0 / jnp.minimum(pos1, float(w)))
            xx = jnp.concatenate([e, halo[:, cs]], axis=0)
            s = xx
            span = 1
            while span < w:
                s = s + _shift_up(s, span)
                span *= 2
            du_ref[:, cs] = (s[:tm] - dpv).astype(BF16)
            halo[:, cs] = e[:HALO]

    rev = lambda i: nt - 1 - i
    blk = pl.BlockSpec((tm, D), lambda i: (rev(i), 0))
    return pl.pallas_call(
        body, name=name, grid=(nt,),
        in_specs=[blk, blk, blk, pl.BlockSpec((tm, D), lambda i: (rev(i), C_R // D)),
                  pl.BlockSpec((1, HV), lambda i: (0, 0))],
        out_specs=[blk, blk, blk, pl.BlockSpec((1, HV), lambda i: (0, 0))],
        out_shape=[jax.ShapeDtypeStruct((m, D), F32), jax.ShapeDtypeStruct((m, D), BF16),
                   jax.ShapeDtypeStruct((m, D), BF16), jax.ShapeDtypeStruct((1, HV), F32)],
        scratch_shapes=[pltpu.VMEM((HALO, D), F32)],
        compiler_params=_params(("arbitrary",)))(dya, dpooled, o, p, gn)


def merge_fwd(p, ya, yb, bg, *, name):
    m = ya.shape[0]
    tm = _ew_tile(m)

    def body(ga_ref, gb_ref, ya_ref, yb_ref, ba_ref, bb_ref, o_ref):
        gate_a = _sigmoid(ga_ref[...].astype(F32) + ba_ref[...])
        gate_b = _sigmoid(gb_ref[...].astype(F32) + bb_ref[...])
        o_ref[...] = (gate_a * ya_ref[...].astype(F32) + gate_b * yb_ref[...].astype(F32)).astype(BF16)

    blk = pl.BlockSpec((tm, D), lambda i: (i, 0))
    return pl.pallas_call(
        body, name=name, grid=(m // tm,),
        in_specs=[pl.BlockSpec((tm, D), lambda i: (i, C_GA // D)),
                  pl.BlockSpec((tm, D), lambda i: (i, C_GB // D)), blk, blk,
                  pl.BlockSpec((1, D), lambda i: (0, 0)), pl.BlockSpec((1, D), lambda i: (0, 1))],
        out_specs=blk, out_shape=jax.ShapeDtypeStruct((m, D), BF16),
        compiler_params=_params(("parallel",)))(p, p, ya, yb, bg, bg)


def merge_bwd(dmrg, p, ya, yb, bg, *, name):
    m = ya.shape[0]
    tm = _ew_tile(m)

    def body(dm_ref, ga_ref, gb_ref, ya_ref, yb_ref, ba_ref, bb_ref,
             dya_ref, dyb_ref, dga_ref, dgb_ref, dbg_ref):
        @pl.when(pl.program_id(0) == 0)
        def _():
            dbg_ref[...] = jnp.zeros_like(dbg_ref)

        dm = dm_ref[...].astype(F32)
        gate_a = _sigmoid(ga_ref[...].astype(F32) + ba_ref[...])
        gate_b = _sigmoid(gb_ref[...].astype(F32) + bb_ref[...])
        dya_ref[...] = (dm * gate_a).astype(BF16)
        dyb_ref[...] = (dm * gate_b).astype(BF16)
        dga = dm * ya_ref[...].astype(F32) * gate_a * (1.0 - gate_a)
        dgb = dm * yb_ref[...].astype(F32) * gate_b * (1.0 - gate_b)
        dga_ref[...] = dga.astype(BF16)
        dgb_ref[...] = dgb.astype(BF16)
        dbg_ref[:, pl.ds(0, D)] += jnp.sum(dga, axis=0, keepdims=True)
        dbg_ref[:, pl.ds(D, D)] += jnp.sum(dgb, axis=0, keepdims=True)

    blk = pl.BlockSpec((tm, D), lambda i: (i, 0))
    return pl.pallas_call(
        body, name=name, grid=(m // tm,),
        in_specs=[blk, pl.BlockSpec((tm, D), lambda i: (i, C_GA // D)),
                  pl.BlockSpec((tm, D), lambda i: (i, C_GB // D)), blk, blk,
                  pl.BlockSpec((1, D), lambda i: (0, 0)), pl.BlockSpec((1, D), lambda i: (0, 1))],
        out_specs=[blk, blk, blk, blk, pl.BlockSpec((1, 2 * D), lambda i: (0, 0))],
        out_shape=[jax.ShapeDtypeStruct((m, D), BF16)] * 4 + [jax.ShapeDtypeStruct((1, 2 * D), F32)],
        compiler_params=_params(("arbitrary",)))(dmrg, p, p, ya, yb, bg, bg)


def scale_bwd(dy1, y0, scale, *, name):
    m = y0.shape[0]
    tm = _ew_tile(m)

    def body(dy_ref, y0_ref, s_ref, o_ref, ds_ref):
        @pl.when(pl.program_id(0) == 0)
        def _():
            ds_ref[...] = jnp.zeros_like(ds_ref)

        dy = dy_ref[...].astype(F32)
        o_ref[...] = (dy * s_ref[...]).astype(BF16)
        ds_ref[...] += jnp.sum(dy * y0_ref[...].astype(F32), axis=0, keepdims=True)

    blk = pl.BlockSpec((tm, D), lambda i: (i, 0))
    vec = pl.BlockSpec((1, D), lambda i: (0, 0))
    return pl.pallas_call(
        body, name=name, grid=(m // tm,), in_specs=[blk, blk, vec], out_specs=[blk, vec],
        out_shape=[jax.ShapeDtypeStruct((m, D), BF16), jax.ShapeDtypeStruct((1, D), F32)],
        compiler_params=_params(("arbitrary",)))(dy1, y0, scale)


CONV_BLK = 1408
N_CONV_BLK = D_FF // CONV_BLK


def conv_act_fwd(up, cw, cb, *, name):
    m = up.shape[0]
    tm = _ew_tile(m)

    def conv(x_ref, halo, w_ref, b_ref):
        xv = x_ref[...].astype(F32)
        xx = jnp.concatenate([halo[...], xv], axis=0)
        y = (w_ref[2:3, :] * xx + w_ref[1:2, :] * _shift_down(xx, 1)
             + w_ref[0:1, :] * _shift_down(xx, 2))[HALO:] + b_ref[...]
        halo[...] = xv[tm - HALO:]
        return y

    def body(xa_ref, xb_ref, wa_ref, wb_ref, ba_ref, bb_ref, upc_a_ref, upc_b_ref, act_ref, halo_a, halo_b):
        @pl.when(pl.program_id(1) == 0)
        def _():
            halo_a[...] = jnp.zeros_like(halo_a)
            halo_b[...] = jnp.zeros_like(halo_b)

        a = conv(xa_ref, halo_a, wa_ref, ba_ref)
        bv = conv(xb_ref, halo_b, wb_ref, bb_ref)
        upc_a_ref[...] = a.astype(BF16)
        upc_b_ref[...] = bv.astype(BF16)
        act_ref[...] = (a * _sigmoid(a) * bv).astype(BF16)

    nb = N_CONV_BLK
    xa = pl.BlockSpec((tm, CONV_BLK), lambda j, i: (i, j))
    xb = pl.BlockSpec((tm, CONV_BLK), lambda j, i: (i, j + nb))
    return pl.pallas_call(
        body, name=name, grid=(nb, m // tm),
        in_specs=[xa, xb,
                  pl.BlockSpec((3, CONV_BLK), lambda j, i: (0, j)),
                  pl.BlockSpec((3, CONV_BLK), lambda j, i: (0, j + nb)),
                  pl.BlockSpec((1, CONV_BLK), lambda j, i: (0, j)),
                  pl.BlockSpec((1, CONV_BLK), lambda j, i: (0, j + nb))],
        out_specs=[xa, xa, xa],
        out_shape=[jax.ShapeDtypeStruct((m, D_FF), BF16)] * 3,
        scratch_shapes=[pltpu.VMEM((HALO, CONV_BLK), F32)] * 2,
        compiler_params=_params(("parallel", "arbitrary")))(up, up, cw, cw, cb, cb)


def conv_act_bwd(dact, upc_a, upc_b, up, cw, *, name):
    m = up.shape[0]
    tm = _ew_tile(m)
    nt = m // tm

    def conv_t(d, halo, x_ref, w_ref, dup_ref, half, dw_ref, db_ref):
        xx = jnp.concatenate([d, halo[...]], axis=0)
        d1 = _shift_up(xx, 1)[:tm]
        d2 = _shift_up(xx, 2)[:tm]
        dup_ref[half] = (w_ref[2:3, :] * d + w_ref[1:2, :] * d1 + w_ref[0:1, :] * d2).astype(BF16)
        xv = x_ref[...].astype(F32)
        dw_ref[2:3, :] += jnp.sum(xv * d, axis=0, keepdims=True)
        dw_ref[1:2, :] += jnp.sum(xv * d1, axis=0, keepdims=True)
        dw_ref[0:1, :] += jnp.sum(xv * d2, axis=0, keepdims=True)
        db_ref[...] += jnp.sum(d, axis=0, keepdims=True)
        halo[...] = d[:HALO]

    def body(da_ref, a_ref, b_ref, xa_ref, xb_ref, wa_ref, wb_ref,
             dup_ref, dwa_ref, dwb_ref, dba_ref, dbb_ref, halo_a, halo_b):
        @pl.when(pl.program_id(1) == 0)
        def _():
            for r in (halo_a, halo_b, dwa_ref, dwb_ref, dba_ref, dbb_ref):
                r[...] = jnp.zeros_like(r)

        dact_v = da_ref[...].astype(F32)
        a = a_ref[...].astype(F32)
        bv = b_ref[...].astype(F32)
        sg = _sigmoid(a)
        d_a = dact_v * bv * sg * (1.0 + a * (1.0 - sg))
        d_b = dact_v * a * sg
        conv_t(d_a, halo_a, xa_ref, wa_ref, dup_ref, 0, dwa_ref, dba_ref)
        conv_t(d_b, halo_b, xb_ref, wb_ref, dup_ref, 1, dwb_ref, dbb_ref)

    nb = N_CONV_BLK
    rev = lambda i: nt - 1 - i
    half = pl.BlockSpec((tm, CONV_BLK), lambda j, i: (rev(i), j))
    xa = half
    xb = pl.BlockSpec((tm, CONV_BLK), lambda j, i: (rev(i), j + nb))
    wa = pl.BlockSpec((3, CONV_BLK), lambda j, i: (0, j))
    wb = pl.BlockSpec((3, CONV_BLK), lambda j, i: (0, j + nb))
    va = pl.BlockSpec((1, CONV_BLK), lambda j, i: (0, j))
    outs = pl.pallas_call(
        body, name=name, grid=(nb, nt),
        in_specs=[half, half, half, xa, xb, wa, wb],
        out_specs=[pl.BlockSpec((2, tm, CONV_BLK), lambda j, i: (0, rev(i), j)), wa, wa, va, va],
        out_shape=[jax.ShapeDtypeStruct((2, m, D_FF), BF16)]
                  + [jax.ShapeDtypeStruct((3, D_FF), F32)] * 2
                  + [jax.ShapeDtypeStruct((1, D_FF), F32)] * 2,
        scratch_shapes=[pltpu.VMEM((HALO, CONV_BLK), F32)] * 2,
        compiler_params=_params(("parallel", "arbitrary")))(dact, upc_a, upc_b, up, up, cw, cw)
    return outs


def local_step(x, target, w):
    seq = x.shape[0]
    h = jnp.concatenate([jnp.zeros((PAD, D), F32), w["meta"], x], axis=0)
    saved = []
    for l in range(DEPTH):
        wl = {k: (v[l:l + 1] if k in ROW_PARAMS else v[l]) for k, v in w.items() if k not in ("meta", "final_norm_g")}
        s = {"h": h}
        fwd_in(s, wl, f"l{l}_")
        fwd_mixer(s, wl, f"l{l}_")
        fwd_ffn(s, wl, f"l{l}_")
        saved.append(s)
        h = s["h3"]

    dh, dh_b, dgf, loss_rows = loss_head(h, w["final_norm_g"], target, name="loss_head")
    g = {"final_norm_g": dgf}
    per_layer = []
    for l in reversed(range(DEPTH)):
        wl = {k: (v[l:l + 1] if k in ROW_PARAMS else v[l]) for k, v in w.items() if k not in ("meta", "final_norm_g")}
        s = saved[l]
        gl = {}
        dh2, dh2_b = bwd_ffn(dh, dh_b, s, wl, gl, f"l{l}_")
        dp = bwd_mixer(dh2_b, s, wl, gl, f"l{l}_")
        gl["w_in"] = bwd_in_w(dp, s, f"l{l}_")
        dh, dh_b = bwd_in_x(dp, dh2, s, wl, gl, f"l{l}_")
        per_layer.append(gl)
    per_layer.reverse()
    for k in per_layer[0]:
        g[k] = jnp.stack([per_layer[l][k].astype(F32) for l in range(DEPTH)])
    g["meta"] = dh[PAD:X0]
    return loss_rows, dh[X0:X0 + seq], g


ROW_PARAMS = ("norm1_g", "b_gk", "gla_norm_g", "pool_scale", "b_gates", "norm2_g", "conv_b")


def fwd_in(s, w, ln):
    s["hn1"] = rmsnorm_fwd(s["h"], w["norm1_g"], name=ln + "norm1")
    s["p"] = mm_nn(s["hn1"], w["w_in"], tn=896, name=ln + "in_proj")


def fwd_mixer(s, w, ln):
    p = s["p"]
    s["o"], s["st"] = gla_fwd(p, w["w_gk"], w["b_gk"], name=ln + "gla_fwd")
    s["ya_in"], s["pooled"] = mix_pre(s["o"], p, w["gla_norm_g"], name=ln + "mix_pre")
    s["ya"] = mm_nn(s["ya_in"], w["w_a"], name=ln + "proj_a")
    s["yb0"], s["yb1"] = pool_mm_fwd(s["pooled"], w["w_pool"], w["pool_scale"], name=ln + "pool_mm")
    s["yb"] = mm_nn(s["yb1"], w["w_b"], name=ln + "proj_b")
    s["mrg"] = merge_fwd(p, s["ya"], s["yb"], w["b_gates"], name=ln + "merge")
    s["h2"] = mm_nn(s["mrg"], w["w_o"], out_dtype=F32, res=s["h"], name=ln + "proj_o")


def fwd_ffn(s, w, ln):
    s["hn2"] = rmsnorm_fwd(s["h2"], w["norm2_g"], name=ln + "norm2")
    s["up"] = mm_nn(s["hn2"], w["w_up"], tn=1408, name=ln + "up_proj")
    s["upc_a"], s["upc_b"], s["act"] = conv_act_fwd(s["up"], w["conv_w"], w["conv_b"], name=ln + "conv_act")
    s["h3"] = mm_nn(s["act"], w["w_down"], out_dtype=F32, res=s["h2"], name=ln + "down_proj")


def bwd_ffn(dh, dh_b, s, w, g, ln, after=None):
    dact = mm_nt(dh_b, w["w_down"], tn=1408, after=after, name=ln + "d_act")
    g["w_down"] = mm_tn(s["act"], dh_b, tk1=1408, out_dtype=BF16, after=after, name=ln + "dw_down")
    dup, dcw_a, dcw_b, dcb_a, dcb_b = conv_act_bwd(
        dact, s["upc_a"], s["upc_b"], s["up"], w["conv_w"], name=ln + "conv_act_bwd")
    dhn2 = mm_nt(dup, w["w_up"], out_dtype=F32, tk=1408, halves=True, name=ln + "d_hn2")
    g["w_up"] = mm_tn(s["hn2"], dup, tn=1408, out_dtype=BF16, halves=True, name=ln + "dw_up")
    dh2, dh2_b, g["norm2_g"] = rmsnorm_bwd(dhn2, s["h2"], w["norm2_g"], dh, name=ln + "norm2_bwd")
    g["conv_w"] = jnp.concatenate([dcw_a, dcw_b], axis=1)
    g["conv_b"] = jnp.concatenate([dcb_a, dcb_b], axis=1)
    return dh2, dh2_b


def bwd_mixer(dh2_b, s, w, g, ln, after=None):
    dmrg = mm_nt(dh2_b, w["w_o"], after=after, name=ln + "d_mrg")
    g["w_o"] = mm_tn(s["mrg"], dh2_b, out_dtype=BF16, after=after, name=ln + "dw_o")
    dya, dyb, dga, dgb, g["b_gates"] = merge_bwd(dmrg, s["p"], s["ya"], s["yb"], w["b_gates"], name=ln + "merge_bwd")
    dya_in = mm_nt(dya, w["w_a"], name=ln + "d_ya_in")
    g["w_a"] = mm_tn(s["ya_in"], dya, out_dtype=BF16, name=ln + "dw_a")
    dyb1 = mm_nt(dyb, w["w_b"], name=ln + "d_yb1")
    g["w_b"] = mm_tn(s["yb1"], dyb, out_dtype=BF16, name=ln + "dw_b")
    dyb0, g["pool_scale"] = scale_bwd(dyb1, s["yb0"], w["pool_scale"], name=ln + "scale_bwd")
    dpooled = pool_mm_bwd_x(dyb0, w["w_pool"], name=ln + "d_pooled")
    g["w_pool"] = pool_mm_bwd_w(s["pooled"], dyb0, name=ln + "dw_pool")
    do, dr, du, g["gla_norm_g"] = mix_pre_bwd(dya_in, dpooled, s["o"], s["p"], w["gla_norm_g"],
                                              name=ln + "mix_pre_bwd")
    dqkv, dglr, g["w_gk"], g["b_gk"] = gla_bwd(s["p"], w["w_gk"], w["b_gk"], s["st"], do, name=ln + "gla_bwd")
    return jnp.concatenate([dqkv, dr, du, dga, dgb, dglr], axis=1)


def bwd_in_w(dp, s, ln):
    return mm_tn(s["hn1"], dp, tn=896, out_dtype=BF16, name=ln + "dw_in")


def bwd_in_x(dp, dh2, s, w, g, ln, after=None):
    dhn1 = mm_nt(dp, w["w_in"], out_dtype=F32, tk=896, after=after, name=ln + "d_hn1")
    dh, dh_b, g["norm1_g"] = rmsnorm_bwd(dhn1, s["h"], w["norm1_g"], dh2, name=ln + "norm1_bwd")
    return dh, dh_b


def _my_place():
    return lax.axis_index("x"), lax.axis_index("y"), lax.axis_index("c")


def _peer(place, k):
    x, y, c = place
    return (1 - x if k & 4 else x, 1 - y if k & 2 else y, 1 - c if k & 1 else c)


def _index(place):
    x, y, c = place
    return 4 * x + 2 * y + c


def exchange(arrays, kinds, *, name):
    n = len(arrays)

    def body(*refs):
        ins, outs = refs[:n], refs[n:2 * n]
        send_sems, recv_sems, local_sems = refs[2 * n:]
        place = _my_place()
        me = _index(place)

        def src(a, dest):
            return ins[a] if kinds[a] == "gather" else ins[a].at[dest]

        def remote(a, k):
            peer = _peer(place, k)
            return pltpu.make_async_remote_copy(
                src_ref=src(a, _index(peer)), dst_ref=outs[a].at[me],
                send_sem=send_sems.at[a, k - 1], recv_sem=recv_sems.at[a, k - 1],
                device_id=peer, device_id_type=pl.DeviceIdType.MESH)

        def arrival(a, k):
            peer = _peer(place, k)
            return pltpu.make_async_remote_copy(
                src_ref=src(a, me), dst_ref=outs[a].at[_index(peer)],
                send_sem=send_sems.at[a, k - 1], recv_sem=recv_sems.at[a, k - 1],
                device_id=peer, device_id_type=pl.DeviceIdType.MESH)

        own = [pltpu.make_async_copy(src(a, me), outs[a].at[me], local_sems.at[a]) for a in range(n)]
        sends = [remote(a, k) for k in range(1, N_DEV) for a in range(n)]
        for cp in sends:
            cp.start()
        for cp in own:
            cp.start()
        for k in range(1, N_DEV):
            for a in range(n):
                arrival(a, k).wait_recv()
        for cp in sends:
            cp.wait_send()
        for cp in own:
            cp.wait()

    any_spec = pl.BlockSpec(memory_space=pl.ANY)
    out_shape = []
    for arr, kind in zip(arrays, kinds):
        shape = arr.shape if kind == "gather" else arr.shape[1:]
        out_shape.append(jax.ShapeDtypeStruct((N_DEV,) + tuple(shape), arr.dtype))
    return pl.pallas_call(
        body, name=name, in_specs=[any_spec] * n, out_specs=[any_spec] * n, out_shape=out_shape,
        scratch_shapes=[pltpu.SemaphoreType.DMA((n, N_DEV - 1)), pltpu.SemaphoreType.DMA((n, N_DEV - 1)),
                        pltpu.SemaphoreType.DMA((n,))],
    )(*arrays)


def _sem_slot(a, k):
    return a * (N_DEV - 1) + k - 1


_HBM = pl.BlockSpec(memory_space=pltpu.HBM)
_SEM = pl.BlockSpec(memory_space=pltpu.SEMAPHORE)
_DATAFLOW = pltpu.SideEffectType.DATAFLOW_SIDE_EFFECTING


def exchange_start(arrays, kinds, after, *, name):
    n = len(arrays)
    zones = []
    for arr, kind in zip(arrays, kinds):
        shape = arr.shape if kind == "gather" else arr.shape[1:]
        zones.append(lax.empty((N_DEV,) + tuple(shape), arr.dtype))

    def body(*refs):
        ins, lands = refs[:n], refs[n:2 * n]
        send_sems, recv_sems = refs[2 * n + 1], refs[2 * n + 2]
        token = refs[4 * n + 3]
        place = _my_place()
        me = _index(place)
        for a in range(n):
            for k in range(1, N_DEV):
                peer = _peer(place, k)
                pltpu.make_async_remote_copy(
                    src_ref=ins[a] if kinds[a] == "gather" else ins[a].at[_index(peer)], dst_ref=lands[a].at[me],
                    send_sem=send_sems.at[_sem_slot(a, k)], recv_sem=recv_sems.at[_sem_slot(a, k)],
                    device_id=peer, device_id_type=pl.DeviceIdType.MESH).start()
        token[...] = jnp.zeros_like(token)

    sems = pltpu.SemaphoreType.DMA((n * (N_DEV - 1),))
    hbm = lambda a: pltpu.HBM(a.shape, a.dtype)
    outs = pl.pallas_call(
        body, name=name,
        out_shape=(sems, sems, *[hbm(a) for a in arrays], *[hbm(z) for z in zones],
                   jax.ShapeDtypeStruct((8, LANES), F32)),
        in_specs=[_HBM] * (2 * n) + [pl.BlockSpec(memory_space=pl.ANY)],
        out_specs=(_SEM, _SEM, *[_HBM] * (2 * n), pl.BlockSpec(memory_space=pltpu.VMEM)),
        input_output_aliases={i: 2 + i for i in range(2 * n)},
        compiler_params=pltpu.CompilerParams(has_side_effects=_DATAFLOW),
    )(*[pltpu.with_memory_space_constraint(a, pltpu.HBM) for a in arrays],
      *[pltpu.with_memory_space_constraint(z, pltpu.HBM) for z in zones], after)
    return dict(send=outs[0], recv=outs[1], srcs=outs[2:2 + n], zones=outs[2 + n:2 + 2 * n],
                token=outs[2 + 2 * n], kinds=kinds)


def exchange_wait(handle, after, *, name):
    kinds = handle["kinds"]
    n = len(kinds)

    def body(*refs):
        ins, lands = refs[:n], refs[n:2 * n]
        send_sems, recv_sems = refs[2 * n], refs[2 * n + 1]
        place = _my_place()
        me = _index(place)
        for a in range(n):
            for k in range(1, N_DEV):
                peer = _peer(place, k)
                src = ins[a] if kinds[a] == "gather" else ins[a].at[_index(peer)]
                copy = pltpu.make_async_remote_copy(
                    src_ref=src, dst_ref=lands[a].at[_index(peer)],
                    send_sem=send_sems.at[_sem_slot(a, k)], recv_sem=recv_sems.at[_sem_slot(a, k)],
                    device_id=peer, device_id_type=pl.DeviceIdType.MESH)
                copy.wait_send()
                copy.wait_recv()

    srcs, zones = handle["srcs"], handle["zones"]
    hbm = lambda a: pltpu.HBM(a.shape, a.dtype)
    outs = pl.pallas_call(
        body, name=name,
        out_shape=(*[hbm(a) for a in srcs], *[hbm(z) for z in zones]),
        in_specs=[_HBM] * (2 * n) + [_SEM, _SEM, pl.BlockSpec(memory_space=pl.ANY)],
        out_specs=[_HBM] * (2 * n),
        input_output_aliases={i: i for i in range(2 * n)},
        compiler_params=pltpu.CompilerParams(has_side_effects=_DATAFLOW),
    )(*srcs, *zones, handle["send"], handle["recv"], after)
    return _fill_own(outs[:n], outs[n:], kinds)


def _fill_own(srcs, zones, kinds):
    me = _index(_my_place())
    filled = []
    for src, zone, kind in zip(srcs, zones, kinds):
        mine = src if kind == "gather" else lax.dynamic_index_in_dim(src, me, 0, keepdims=False)
        filled.append(lax.dynamic_update_index_in_dim(zone, mine, me, 0))
    return filled


def reduce_adam_layer(parts, w, m, v, layer, prev, *, name):
    _, r, c = w.shape
    tr = _pick(r, (256, 352, 128))

    def body(*refs):
        p_ref, w_ref, m_ref, v_ref = refs[:4]
        g_ref, d_ref, m2_ref, v2_ref = refs[-4:]
        g = p_ref[0].astype(F32)
        for i in range(1, N_DEV):
            g = g + p_ref[i].astype(F32)
        m2 = B1 * m_ref[...] + (1.0 - B1) * g
        v2 = B2 * v_ref[...] + (1.0 - B2) * (g * g)
        m_hat = m2 / (1.0 - B1 ** STEP)
        v_hat = v2 / (1.0 - B2 ** STEP)
        g_ref[...] = g
        d_ref[...] = -LR * (m_hat / (jnp.sqrt(v_hat) + ADAM_EPS) + WD * w_ref[...])
        m2_ref[...] = m2
        v2_ref[...] = v2

    blk = pl.BlockSpec((None, tr, c), lambda i: (layer, i, 0))
    in_specs = [pl.BlockSpec((N_DEV, tr, c), lambda i: (0, i, 0)), blk, blk, blk]
    args = [parts, w, m, v]
    aliases = {}
    if prev is not None:
        in_specs += [pl.BlockSpec(memory_space=pl.ANY)] * 4
        args += list(prev)
        aliases = {4 + j: j for j in range(4)}
    return pl.pallas_call(
        body, name=name, grid=(r // tr,), in_specs=in_specs, out_specs=[blk] * 4,
        out_shape=[jax.ShapeDtypeStruct(w.shape, F32)] * 4, input_output_aliases=aliases,
        compiler_params=_params(("parallel",)))(*args)


def reduce_adam(parts, w, m, v, *, name):
    r, c = w.shape
    tr = _pick(r, (256, 352, 192, 128, 72, 64, 32, 16, 8))

    def body(p_ref, w_ref, m_ref, v_ref, g_ref, d_ref, m2_ref, v2_ref):
        g = p_ref[0].astype(F32)
        for i in range(1, N_DEV):
            g = g + p_ref[i].astype(F32)
        wv = w_ref[...]
        m2 = B1 * m_ref[...] + (1.0 - B1) * g
        v2 = B2 * v_ref[...] + (1.0 - B2) * (g * g)
        m_hat = m2 / (1.0 - B1 ** STEP)
        v_hat = v2 / (1.0 - B2 ** STEP)
        g_ref[...] = g
        d_ref[...] = -LR * (m_hat / (jnp.sqrt(v_hat) + ADAM_EPS) + WD * wv)
        m2_ref[...] = m2
        v2_ref[...] = v2

    blk = pl.BlockSpec((tr, c), lambda i: (i, 0))
    return pl.pallas_call(
        body, name=name, grid=(r // tr,),
        in_specs=[pl.BlockSpec((N_DEV, tr, c), lambda i: (0, i, 0)), blk, blk, blk],
        out_specs=[blk] * 4, out_shape=[jax.ShapeDtypeStruct((r, c), F32)] * 4,
        compiler_params=_params(("parallel",)))(parts, w, m, v)


BIG = ("w_in", "w_a", "w_pool_grp", "w_b", "w_o", "w_up", "w_down")
SHARDED_SMALL = ("meta_tokens", "w_gk", "conv_w")
REPLICATED = ("norm1_g", "b_gk", "gla_norm_g", "pool_scale", "b_gates", "norm2_g", "conv_b", "final_norm_g")
CUT_AXIS = {"w_in": 2, "w_a": 1, "w_pool_grp": 2, "w_b": 1, "w_o": 1, "w_up": 2, "w_down": 1,
            "meta_tokens": 1, "w_gk": 2, "conv_w": 2}
WEIGHTS = ("meta_tokens", "norm1_g", "w_in", "w_gk", "b_gk", "gla_norm_g", "w_a", "w_pool_grp", "pool_scale",
           "w_b", "b_gates", "w_o", "norm2_g", "w_up", "conv_w", "conv_b", "w_down", "final_norm_g")


def _as_2d(a):
    return a.reshape(-1, a.shape[-1])


def _from_slots(slots, axis):
    full = jnp.moveaxis(slots, 0, axis)
    shape = list(full.shape)
    shape[axis:axis + 2] = [shape[axis] * shape[axis + 1]]
    return full.reshape(shape)


def _to_slots(full, axis):
    shape = list(full.shape)
    shape[axis:axis + 1] = [N_DEV, shape[axis] // N_DEV]
    return jnp.moveaxis(full.reshape(shape), axis, 0)


def _pack(vectors, rows):
    flat = jnp.concatenate([v.reshape(-1).astype(F32) for v in vectors])
    return jnp.pad(flat, (0, rows * LANES - flat.shape[0])).reshape(rows, LANES)


def _unpack(packed, shapes):
    flat = packed.reshape(-1)
    out, off = [], 0
    for s in shapes:
        size = 1
        for d in s:
            size *= d
        out.append(flat[off:off + size].reshape(s))
        off += size
    return out


def _rows_for(shapes, mult=8):
    total = 0
    for s in shapes:
        size = 1
        for d in s:
            size *= d
        total += size
    rows = -(-total // LANES)
    return -(-rows // mult) * mult


def _permute_in(w_in):
    pad = jnp.zeros(w_in.shape[:-1] + (IN_R - IN_WIDTH,), w_in.dtype)
    return jnp.concatenate([w_in[..., :2048], w_in[..., 2064:], w_in[..., 2048:2064], pad], axis=-1)


def _unpermute_in(w_r):
    return jnp.concatenate([w_r[..., :2048], w_r[..., C_GLR:C_GLR + RANK], w_r[..., 2048:C_GLR]], axis=-1)


def kernel(x, meta_tokens, norm1_g, w_in, w_gk, b_gk, gla_norm_g, w_a, w_pool_grp, pool_scale, w_b, b_gates, w_o, norm2_g, w_up, conv_w, conv_b, w_down, final_norm_g, loss_target, m_meta_tokens, m_norm1_g, m_w_in, m_w_gk, m_b_gk, m_gla_norm_g, m_w_a, m_w_pool_grp, m_pool_scale, m_w_b, m_b_gates, m_w_o, m_norm2_g, m_w_up, m_conv_w, m_conv_b, m_w_down, m_final_norm_g, v_meta_tokens, v_norm1_g, v_w_in, v_w_gk, v_b_gk, v_gla_norm_g, v_w_a, v_w_pool_grp, v_pool_scale, v_w_b, v_b_gates, v_w_o, v_norm2_g, v_w_up, v_conv_w, v_conv_b, v_w_down, v_final_norm_g):
    wts = dict(meta_tokens=meta_tokens, norm1_g=norm1_g, w_in=w_in, w_gk=w_gk, b_gk=b_gk, gla_norm_g=gla_norm_g,
               w_a=w_a, w_pool_grp=w_pool_grp, pool_scale=pool_scale, w_b=w_b, b_gates=b_gates, w_o=w_o,
               norm2_g=norm2_g, w_up=w_up, conv_w=conv_w, conv_b=conv_b, w_down=w_down, final_norm_g=final_norm_g)
    mom = dict(meta_tokens=m_meta_tokens, norm1_g=m_norm1_g, w_in=m_w_in, w_gk=m_w_gk, b_gk=m_b_gk,
               gla_norm_g=m_gla_norm_g, w_a=m_w_a, w_pool_grp=m_w_pool_grp, pool_scale=m_pool_scale, w_b=m_w_b,
               b_gates=m_b_gates, w_o=m_w_o, norm2_g=m_norm2_g, w_up=m_w_up, conv_w=m_conv_w, conv_b=m_conv_b,
               w_down=m_w_down, final_norm_g=m_final_norm_g)
    var = dict(meta_tokens=v_meta_tokens, norm1_g=v_norm1_g, w_in=v_w_in, w_gk=v_w_gk, b_gk=v_b_gk,
               gla_norm_g=v_gla_norm_g, w_a=v_w_a, w_pool_grp=v_w_pool_grp, pool_scale=v_pool_scale, w_b=v_w_b,
               b_gates=v_b_gates, w_o=v_w_o, norm2_g=v_norm2_g, w_up=v_w_up, conv_w=v_conv_w, conv_b=v_conv_b,
               w_down=v_w_down, final_norm_g=v_final_norm_g)

    small_shapes = [wts[n].shape for n in SHARDED_SMALL]
    small_rows = _rows_for(small_shapes)

    def shard3(a):
        return a.reshape(DEPTH, -1, a.shape[-1])

    def layer_shards(l, names):
        return [shard3(wts[n])[l].astype(BF16) for n in names]

    def tie(row, handle):
        return row + handle["token"][0:1, 0:1]

    def full_weight(n, zone):
        if n == "w_in":
            return _permute_in(jnp.moveaxis(zone, 0, 1).reshape(D, IN_WIDTH))
        if n == "w_up":
            return jnp.moveaxis(zone, 0, 1).reshape(D, F2)
        if n == "w_pool_grp":
            return jnp.moveaxis(zone.reshape(N_DEV, GROUPS, GDIM // N_DEV, GDIM), 0, 1).reshape(GROUPS, GDIM, GDIM)
        return zone.reshape(-1, zone.shape[-1])

    groups = [("w_in",), ("w_a", "w_pool_grp", "w_b", "w_o"), ("w_up", "w_down")]
    rest = groups[0] + groups[1]
    key = {"w_pool_grp": "w_pool"}
    rows = dict(norm1_g=norm1_g, b_gk=b_gk, gla_norm_g=gla_norm_g, pool_scale=pool_scale, b_gates=b_gates,
                norm2_g=norm2_g, conv_b=conv_b)

    def gather(l, names, after, name, head=()):
        return exchange_start(list(head) + layer_shards(l, names), ["gather"] * (len(head) + len(names)), after,
                              name=name + "_start")

    def landed(handle, after, name, names, w_layer):
        zones = exchange_wait(handle, after, name=name + "_wait")
        for n, z in zip(names, zones[len(zones) - len(names):]):
            w_layer[key.get(n, n)] = full_weight(n, z)
        return zones

    wl = [{n: v[l:l + 1] for n, v in rows.items()} for l in range(DEPTH)]
    g_in0 = gather(0, groups[0], x, "gather_in0", head=[_pack([wts[n] for n in SHARDED_SMALL], small_rows)])
    zones = landed(g_in0, g_in0["token"], "gather_in0", groups[0], wl[0])
    small_slots = [jnp.stack(parts) for parts in zip(*[_unpack(zones[0][i], small_shapes) for i in range(N_DEV)])]
    small_full = {n: _from_slots(slots, CUT_AXIS[n]) for n, slots in zip(SHARDED_SMALL, small_slots)}
    w_gk_pad = jnp.pad(small_full["w_gk"], ((0, 0), (0, LANES - RANK), (0, 0))).astype(BF16)
    for l in range(DEPTH):
        wl[l]["w_gk"] = w_gk_pad[l]
        wl[l]["conv_w"] = small_full["conv_w"][l]
    g_mix0 = gather(0, groups[1], zones[1], "gather_mix0")
    g_ffn0 = gather(0, groups[2], g_mix0["token"], "gather_ffn0")
    wl[0]["norm1_g"] = tie(wl[0]["norm1_g"], g_ffn0)

    h = jnp.concatenate([jnp.zeros((PAD, D), F32), small_full["meta_tokens"], x[0]], axis=0)
    s0 = {"h": h}
    fwd_in(s0, wl[0], "l0_")
    zones = landed(g_mix0, s0["p"], "gather_mix0", groups[1], wl[0])
    g_in1 = gather(1, groups[0], zones[0], "gather_in1")
    wl[0]["b_gk"] = tie(wl[0]["b_gk"], g_in1)
    fwd_mixer(s0, wl[0], "l0_")
    zones = landed(g_ffn0, s0["h2"], "gather_ffn0", groups[2], wl[0])
    g_mix1 = gather(1, groups[1], zones[0], "gather_mix1")
    g_ffn1 = gather(1, groups[2], g_mix1["token"], "gather_ffn1")
    wl[0]["norm2_g"] = tie(wl[0]["norm2_g"], g_ffn1)
    fwd_ffn(s0, wl[0], "l0_")
    landed(g_in1, s0["h3"], "gather_in1", groups[0], wl[1])
    s1 = {"h": s0["h3"]}
    fwd_in(s1, wl[1], "l1_")
    landed(g_mix1, s1["p"], "gather_mix1", groups[1], wl[1])
    fwd_mixer(s1, wl[1], "l1_")
    landed(g_ffn1, s1["h2"], "gather_ffn1", groups[2], wl[1])
    fwd_ffn(s1, wl[1], "l1_")
    dh, dh_b, dgf, loss_rows = loss_head(s1["h3"], final_norm_g[None], loss_target[0], name="loss_head")
    loss_part = 0.5 * jnp.sum(loss_rows) / D

    def blocks(n, gw):
        if n == "w_in":
            return jnp.moveaxis(_unpermute_in(gw).reshape(D, N_DEV, IN_WIDTH // N_DEV), 1, 0)
        if n == "w_up":
            return jnp.moveaxis(gw.reshape(D, N_DEV, F2 // N_DEV), 1, 0)
        if n == "w_pool_grp":
            gw = gw.astype(BF16).reshape(GROUPS, N_DEV, GDIM // N_DEV, GDIM)
            return jnp.moveaxis(gw, 1, 0).reshape(N_DEV, GROUPS * GDIM // N_DEV, GDIM)
        return gw.reshape(N_DEV, gw.shape[0] // N_DEV, gw.shape[1])

    def scatter(g, names, after, name):
        return exchange_start([blocks(n, g[key.get(n, n)]) for n in names], ["scatter"] * len(names), after,
                              name=name + "_start")

    g1, g0 = {}, {}
    dh2, dh2_b = bwd_ffn(dh, dh_b, s1, wl[1], g1, "l1_")
    s_ffn1 = scatter(g1, groups[2], dh2, "scatter_ffn1")
    dp = bwd_mixer(dh2_b, s1, wl[1], g1, "l1_", after=s_ffn1["token"])
    g1["w_in"] = bwd_in_w(dp, s1, "l1_")
    s_rest1 = scatter(g1, rest, s_ffn1["token"], "scatter_rest1")
    dh, dh_b = bwd_in_x(dp, dh2, s1, wl[1], g1, "l1_", after=s_rest1["token"])
    dh2, dh2_b = bwd_ffn(dh, dh_b, s0, wl[0], g0, "l0_")
    r_ffn1 = exchange_wait(s_ffn1, dh2, name="scatter_ffn1_wait")
    s_ffn0 = scatter(g0, groups[2], r_ffn1[0], "scatter_ffn0")
    dp = bwd_mixer(dh2_b, s0, wl[0], g0, "l0_", after=s_ffn0["token"])
    r_rest1 = exchange_wait(s_rest1, dp, name="scatter_rest1_wait")
    g0["w_in"] = bwd_in_w(dp, s0, "l0_")
    s_rest0 = scatter(g0, rest, r_rest1[0], "scatter_rest0")
    dh, _ = bwd_in_x(dp, dh2, s0, wl[0], g0, "l0_", after=s_rest0["token"])
    r_ffn0 = exchange_wait(s_ffn0, dh, name="scatter_ffn0_wait")
    r_rest0 = exchange_wait(s_rest0, r_ffn0[0], name="scatter_rest0_wait")
    grad_x = dh[X0:]
    recv = [dict(zip(groups[2] + rest, list(r_ffn0) + list(r_rest0))),
            dict(zip(groups[2] + rest, list(r_ffn1) + list(r_rest1)))]

    grads, delta, new_m, new_v = {}, {}, {}, {}
    for n in BIG:
        w3, m3, v3 = shard3(wts[n]), shard3(mom[n]), shard3(var[n])
        first = reduce_adam_layer(recv[1][n], w3, m3, v3, 1, None, name="adam_l1_" + n)
        outs = reduce_adam_layer(recv[0][n], w3, m3, v3, 0, first, name="adam_l0_" + n)
        grads[n], delta[n], new_m[n], new_v[n] = [o.reshape(wts[n].shape) for o in outs]

    g_full = {n: jnp.stack([g0[n], g1[n]])[:, 0] for n in rows}
    g_full["final_norm_g"] = dgf[0]
    g_full["meta_tokens"] = dh[PAD:X0]
    g_full["w_gk"] = jnp.stack([g0["w_gk"], g1["w_gk"]])[:, :RANK]
    g_full["conv_w"] = jnp.stack([g0["conv_w"], g1["conv_w"]])
    rep_shapes = [wts[n].shape for n in REPLICATED] + [(1,)]
    rep_rows = _rows_for(rep_shapes)
    small_blocks = jnp.stack([
        _pack([_to_slots(g_full[n], CUT_AXIS[n])[i] for n in SHARDED_SMALL], small_rows) for i in range(N_DEV)])
    rep_pack = _pack([g_full[n] for n in REPLICATED] + [loss_part.reshape(1)], rep_rows)
    received = exchange([small_blocks, rep_pack], ["scatter", "gather"], name="exchange_small")
    outs = reduce_adam(received[-2], _pack([wts[n] for n in SHARDED_SMALL], small_rows),
                       _pack([mom[n] for n in SHARDED_SMALL], small_rows),
                       _pack([var[n] for n in SHARDED_SMALL], small_rows), name="adam_small")
    for d, o in zip((grads, delta, new_m, new_v), outs):
        for n, a in zip(SHARDED_SMALL, _unpack(o, small_shapes)):
            d[n] = a
    one = [jnp.zeros((1,), F32)]
    outs = reduce_adam(received[-1], _pack([wts[n] for n in REPLICATED] + one, rep_rows),
                       _pack([mom[n] for n in REPLICATED] + one, rep_rows),
                       _pack([var[n] for n in REPLICATED] + one, rep_rows), name="adam_replicated")
    for d, o in zip((grads, delta, new_m, new_v), outs):
        for n, a in zip(REPLICATED + ("loss",), _unpack(o, rep_shapes)):
            d[n] = a
    loss = grads["loss"][0]
    return (loss, grad_x[None], *[grads[n] for n in WEIGHTS], *[delta[n] for n in WEIGHTS],
            *[new_m[n] for n in WEIGHTS], *[new_v[n] for n in WEIGHTS])
```

```python
import functools

import jax
import jax.numpy as jnp
from jax import lax
from jax.experimental import pallas as pl
from jax.experimental.pallas import tpu as pltpu

F32 = jnp.float32
BF16 = jnp.bfloat16

D = 1024
DEPTH = 2
N_META = 16
HEADS = 4
DK = 512
DV = 1024
HK = 128
HV = 256
RANK = 16
TAU = 16.0
CHUNK = 64
POOL_WINDOWS = (2, 4, 8, 16)
GROUPS = 4
GDIM = 256
D_FF = 2816
F2 = 2 * D_FF
EPS = 1e-6
IN_WIDTH = 6160
LR, B1, B2, ADAM_EPS, WD, STEP = 0.001, 0.9, 0.999, 1e-8, 0.01, 10

N_DEV = 8
PAD = CHUNK - N_META
X0 = CHUNK
IN_R = 6272
C_Q, C_K, C_V, C_R, C_U, C_GA, C_GB, C_GLR = 0, 512, 1024, 2048, 3072, 4096, 5120, 6144
VMEM_LIMIT = 56 * 1024 * 1024
LANES = 128


def _params(sem=None):
    return pltpu.CompilerParams(dimension_semantics=sem, vmem_limit_bytes=VMEM_LIMIT)


def _pick(n, prefs):
    for t in prefs:
        if n % t == 0:
            return t
    raise ValueError(f"no tile for {n} in {prefs}")


MM_VMEM_BUDGET = 44 * 1024 * 1024
MM_TILES = {"l0_": (2752, 1376, 688, 192, 128, 64), "l1_": (2752, 1376, 688, 192, 128, 64)}
EW_TILES = {"l0_": (688, 192, 128, 64), "l1_": (688, 192, 128, 64)}


def _row_tile(lp, name="", row_bytes=0, fixed_bytes=0):
    for t in MM_TILES.get(name[:3], MM_TILES["l0_"]):
        if lp % t == 0 and (t * row_bytes + fixed_bytes <= MM_VMEM_BUDGET or t <= 688):
            return t
    raise ValueError(f"no row tile for {lp}")


def _ew_tile(lp, name="", cap=None):
    return _pick(lp, [t for t in EW_TILES.get(name[:3], EW_TILES["l0_"]) if cap is None or t <= cap])


def _sigmoid(x):
    return 1.0 / (1.0 + jnp.exp(-x))


def _dot(a, b, dims):
    return lax.dot_general(a, b, (dims, ((), ())), preferred_element_type=F32)


def _nn(a, b):
    return _dot(a, b, ((1,), (0,)))


def _nt(a, b):
    return _dot(a, b, ((1,), (1,)))


def _tn(a, b):
    return _dot(a, b, ((0,), (0,)))


def mm_nn(a, b, *, out_dtype=BF16, tn=None, res=None, name):
    m, k = a.shape
    n = b.shape[1]
    tn = tn or n
    has_res = res is not None
    out_bytes = jnp.dtype(out_dtype).itemsize
    tm = _row_tile(m, name, 4 * k + 2 * tn * out_bytes + (8 * tn if has_res else 0), 4 * k * tn)

    def body(*refs):
        if has_res:
            a_ref, b_ref, r_ref, o_ref = refs
        else:
            a_ref, b_ref, o_ref = refs
        acc = _nn(a_ref[...], b_ref[...])
        if has_res:
            row = pl.program_id(1) * tm + lax.broadcasted_iota(jnp.int32, (tm, 1), 0)
            acc = jnp.where(row >= PAD, acc + r_ref[...], 0.0)
        o_ref[...] = acc.astype(o_ref.dtype)

    in_specs = [pl.BlockSpec((tm, k), lambda j, i: (i, 0)),
                pl.BlockSpec((k, tn), lambda j, i: (0, j))]
    args = [a, b]
    if has_res:
        in_specs.append(pl.BlockSpec((tm, tn), lambda j, i: (i, j)))
        args.append(res)
    return pl.pallas_call(
        body, name=name, grid=(n // tn, m // tm), in_specs=in_specs,
        out_specs=pl.BlockSpec((tm, tn), lambda j, i: (i, j)),
        out_shape=jax.ShapeDtypeStruct((m, n), out_dtype),
        compiler_params=_params(("parallel", "parallel")))(*args)


def mm_nt(a, b, *, out_dtype=BF16, tn=None, tk=None, after=None, halves=False, name):
    m, k = (a.shape[1], 2 * a.shape[2]) if halves else a.shape
    n = b.shape[0]
    tn = tn or n
    tk = tk or k
    nk = k // tk
    tm = _row_tile(m, name, 4 * tk + 2 * tn * jnp.dtype(out_dtype).itemsize + 4 * tn, 4 * tn * tk)
    extra = [] if after is None else [after]
    if halves:
        per = nk // 2
        a_spec = pl.BlockSpec((None, tm, tk), lambda j, i, kk: (kk // per, i, kk % per))
    else:
        a_spec = pl.BlockSpec((tm, tk), lambda j, i, kk: (i, kk))

    def body(a_ref, b_ref, *rest):
        o_ref, acc_ref = rest[-2:]
        kk = pl.program_id(2)
        part = _nt(a_ref[...], b_ref[...])

        @pl.when(kk == 0)
        def _():
            acc_ref[...] = part

        @pl.when(kk > 0)
        def _():
            acc_ref[...] += part

        @pl.when(kk == nk - 1)
        def _():
            o_ref[...] = acc_ref[...].astype(o_ref.dtype)

    return pl.pallas_call(
        body, name=name, grid=(n // tn, m // tm, nk),
        in_specs=[a_spec, pl.BlockSpec((tn, tk), lambda j, i, kk: (j, kk))]
                 + [pl.BlockSpec(memory_space=pl.ANY)] * len(extra),
        out_specs=pl.BlockSpec((tm, tn), lambda j, i, kk: (i, j)),
        out_shape=jax.ShapeDtypeStruct((m, n), out_dtype),
        scratch_shapes=[pltpu.VMEM((tm, tn), F32)],
        compiler_params=_params(("parallel", "parallel", "arbitrary")))(a, b, *extra)


def mm_tn(a, b, *, tk1=None, tn=None, out_dtype=F32, after=None, halves=False, name):
    m, k1 = a.shape
    n = 2 * b.shape[2] if halves else b.shape[1]
    tk1 = tk1 or k1
    tn = tn or n
    tm = _row_tile(m, name, 4 * tk1 + 4 * tn, tk1 * tn * (4 + 2 * jnp.dtype(out_dtype).itemsize))
    nm = m // tm
    extra = [] if after is None else [after]
    if halves:
        per = n // tn // 2
        b_spec = pl.BlockSpec((None, tm, tn), lambda p, j, i: (j // per, i, j % per))
    else:
        b_spec = pl.BlockSpec((tm, tn), lambda p, j, i: (i, j))

    def body(a_ref, b_ref, *rest):
        o_ref, acc_ref = rest[-2:]
        i = pl.program_id(2)
        part = _tn(a_ref[...], b_ref[...])

        @pl.when(i == 0)
        def _():
            acc_ref[...] = part

        @pl.when(i > 0)
        def _():
            acc_ref[...] += part

        @pl.when(i == nm - 1)
        def _():
            o_ref[...] = acc_ref[...].astype(o_ref.dtype)

    return pl.pallas_call(
        body, name=name, grid=(k1 // tk1, n // tn, nm),
        in_specs=[pl.BlockSpec((tm, tk1), lambda p, j, i: (i, p)), b_spec]
                 + [pl.BlockSpec(memory_space=pl.ANY)] * len(extra),
        out_specs=pl.BlockSpec((tk1, tn), lambda p, j, i: (p, j)),
        out_shape=jax.ShapeDtypeStruct((k1, n), out_dtype),
        scratch_shapes=[pltpu.VMEM((tk1, tn), F32)],
        compiler_params=_params(("parallel", "parallel", "arbitrary")))(a, b, *extra)


def pool_mm_fwd(pooled, wp, scale, *, name):
    m = pooled.shape[0]
    tm = _row_tile(m, name)

    def body(a_ref, w_ref, s_ref, y0_ref, y1_ref):
        acc = _nn(a_ref[...], w_ref[...])
        y0_ref[...] = acc.astype(BF16)
        y1_ref[...] = (acc * s_ref[...]).astype(BF16)

    blk = pl.BlockSpec((tm, GDIM), lambda g, i: (i, g))
    return pl.pallas_call(
        body, name=name, grid=(GROUPS, m // tm),
        in_specs=[blk, pl.BlockSpec((None, GDIM, GDIM), lambda g, i: (g, 0, 0)),
                  pl.BlockSpec((1, GDIM), lambda g, i: (0, g))],
        out_specs=[blk, blk],
        out_shape=[jax.ShapeDtypeStruct((m, D), BF16)] * 2,
        compiler_params=_params(("parallel", "parallel")))(pooled, wp, scale)


def pool_mm_bwd_x(dy0, wp, *, name):
    m = dy0.shape[0]
    tm = _row_tile(m, name)

    def body(a_ref, w_ref, o_ref):
        o_ref[...] = _nt(a_ref[...], w_ref[...]).astype(BF16)

    blk = pl.BlockSpec((tm, GDIM), lambda g, i: (i, g))
    return pl.pallas_call(
        body, name=name, grid=(GROUPS, m // tm),
        in_specs=[blk, pl.BlockSpec((None, GDIM, GDIM), lambda g, i: (g, 0, 0))],
        out_specs=blk, out_shape=jax.ShapeDtypeStruct((m, D), BF16),
        compiler_params=_params(("parallel", "parallel")))(dy0, wp)


def pool_mm_bwd_w(pooled, dy0, *, name):
    m = pooled.shape[0]
    tm = _row_tile(m, name)

    def body(a_ref, b_ref, o_ref):
        part = _tn(a_ref[...], b_ref[...])

        @pl.when(pl.program_id(1) == 0)
        def _():
            o_ref[...] = part

        @pl.when(pl.program_id(1) > 0)
        def _():
            o_ref[...] += part

    blk = pl.BlockSpec((tm, GDIM), lambda g, i: (i, g))
    return pl.pallas_call(
        body, name=name, grid=(GROUPS, m // tm), in_specs=[blk, blk],
        out_specs=pl.BlockSpec((None, GDIM, GDIM), lambda g, i: (g, 0, 0)),
        out_shape=jax.ShapeDtypeStruct((GROUPS, GDIM, GDIM), F32),
        compiler_params=_params(("parallel", "arbitrary")))(pooled, dy0)


def rmsnorm_fwd(x, g, *, name):
    m = x.shape[0]
    tm = _ew_tile(m, name)

    def body(x_ref, g_ref, o_ref):
        xv = x_ref[...]
        r = lax.rsqrt(jnp.mean(xv * xv, axis=-1, keepdims=True) + EPS)
        o_ref[...] = (xv * r * g_ref[...]).astype(BF16)

    return pl.pallas_call(
        body, name=name, grid=(m // tm,),
        in_specs=[pl.BlockSpec((tm, D), lambda i: (i, 0)), pl.BlockSpec((1, D), lambda i: (0, 0))],
        out_specs=pl.BlockSpec((tm, D), lambda i: (i, 0)),
        out_shape=jax.ShapeDtypeStruct((m, D), BF16),
        compiler_params=_params(("parallel",)))(x, g)


def rmsnorm_bwd(dy, x, g, dres, *, name):
    m = x.shape[0]
    tm = _ew_tile(m, name)

    def body(dy_ref, x_ref, g_ref, r_ref, dx_ref, dxb_ref, dg_ref):
        i = pl.program_id(0)
        xv = x_ref[...]
        dyv = dy_ref[...].astype(F32)
        r = lax.rsqrt(jnp.mean(xv * xv, axis=-1, keepdims=True) + EPS)
        xh = xv * r
        dxh = dyv * g_ref[...]
        dx = r * (dxh - xh * jnp.mean(dxh * xh, axis=-1, keepdims=True))
        row = i * tm + lax.broadcasted_iota(jnp.int32, (tm, 1), 0)
        dx = jnp.where(row >= PAD, dx + r_ref[...], 0.0)
        dx_ref[...] = dx
        dxb_ref[...] = dx.astype(BF16)

        @pl.when(i == 0)
        def _():
            dg_ref[...] = jnp.zeros_like(dg_ref)

        dg_ref[...] += jnp.sum(dyv * xh, axis=0, keepdims=True)

    blk = pl.BlockSpec((tm, D), lambda i: (i, 0))
    vec = pl.BlockSpec((1, D), lambda i: (0, 0))
    return pl.pallas_call(
        body, name=name, grid=(m // tm,), in_specs=[blk, blk, vec, blk],
        out_specs=[blk, blk, vec],
        out_shape=[jax.ShapeDtypeStruct((m, D), F32), jax.ShapeDtypeStruct((m, D), BF16),
                   jax.ShapeDtypeStruct((1, D), F32)],
        compiler_params=_params(("arbitrary",)))(dy, x, g, dres)


def loss_head(h, gf, target, *, name):
    m = h.shape[0]
    t = _ew_tile(m, name)
    inv_d = 1.0 / D

    def body(h_ref, g_ref, t_ref, dh_ref, dhb_ref, dg_ref, ls_ref):
        i = pl.program_id(0)

        @pl.when(i == 0)
        def _():
            dg_ref[...] = jnp.zeros_like(dg_ref)
            ls_ref[...] = jnp.zeros_like(ls_ref)

        real = i * t + lax.broadcasted_iota(jnp.int32, (t, 1), 0) >= X0
        xv = h_ref[...]
        r = lax.rsqrt(jnp.mean(xv * xv, axis=-1, keepdims=True) + EPS)
        xh = xv * r
        err = jnp.where(real, xh * g_ref[...] - t_ref[...], 0.0)
        ls_ref[...] += jnp.sum(err * err, axis=0, keepdims=True)
        dy = err * inv_d
        dg_ref[...] += jnp.sum(dy * xh, axis=0, keepdims=True)
        dxh = dy * g_ref[...]
        dh = r * (dxh - xh * jnp.mean(dxh * xh, axis=-1, keepdims=True))
        dh_ref[...] = dh
        dhb_ref[...] = dh.astype(BF16)

    blk = pl.BlockSpec((t, D), lambda i: (i, 0))
    vec = pl.BlockSpec((1, D), lambda i: (0, 0))
    return pl.pallas_call(
        body, name=name, grid=(m // t,),
        in_specs=[blk, vec, blk],
        out_specs=[blk, blk, vec, vec],
        out_shape=[jax.ShapeDtypeStruct((m, D), F32), jax.ShapeDtypeStruct((m, D), BF16),
                   jax.ShapeDtypeStruct((1, D), F32), jax.ShapeDtypeStruct((1, D), F32)],
        compiler_params=_params(("arbitrary",)))(h, gf, target)


def _split3(x):
    x1 = x.astype(BF16)
    r1 = x - x1.astype(F32)
    x2 = r1.astype(BF16)
    x3 = (r1 - x2.astype(F32)).astype(BF16)
    return x1, x2, x3


def _tri_mm(tri, x):
    x1, x2, x3 = _split3(x)
    return _nn(tri, x1) + _nn(tri, x2) + _nn(tri, x3)


def _log_decay(glr, wgk, bgk, row0, rows):
    z = _nn(glr, wgk) + bgk
    la = (jnp.minimum(z, 0.0) - jnp.log(1.0 + jnp.exp(-jnp.abs(z)))) * (1.0 / TAU)
    row = row0 + lax.broadcasted_iota(jnp.int32, (rows, 1), 0)
    return z, jnp.where(row >= PAD, la, 0.0)


def _chunk_group(n_chunks):
    return _pick(n_chunks, (3, 2, 1))


def gla_fwd(p, wgk, bgk, *, name):
    m = p.shape[0]
    n_chunks = m // CHUNK
    cg = _chunk_group(n_chunks)
    t = cg * CHUNK
    scale = HK ** -0.5

    def body(q_ref, k_ref, v_ref, glr_ref, wgk_ref, bgk_ref, o_ref, st_ref, state):
        i = pl.program_id(0)

        @pl.when(i == 0)
        def _():
            state[...] = jnp.zeros_like(state)

        _, la = _log_decay(glr_ref[...], wgk_ref[...], bgk_ref[...], i * t, t)
        ri = lax.broadcasted_iota(jnp.int32, (CHUNK, CHUNK), 0)
        ci = lax.broadcasted_iota(jnp.int32, (CHUNK, CHUNK), 1)
        causal = ri >= ci
        tri = causal.astype(BF16)
        for c in range(cg):
            rows = pl.ds(c * CHUNK, CHUNK)
            b = _tri_mm(tri, la[c * CHUNK:(c + 1) * CHUNK])
            bl = b[CHUNK - 1:CHUNK, :]
            q = q_ref[rows, :].astype(F32) * scale
            k = k_ref[rows, :].astype(F32)
            qd = (q * jnp.exp(b)).astype(BF16)
            ki = (k * jnp.exp(-b)).astype(BF16)
            ke = (k * jnp.exp(bl - b)).astype(BF16)
            dec = jnp.exp(bl)
            for h in range(HEADS):
                ks = slice(h * HK, (h + 1) * HK)
                vs = pl.ds(h * HV, HV)
                vh = v_ref[rows, vs]
                s_t = state[h]
                st_ref[c, h] = s_t
                att = jnp.where(causal, _nt(qd[:, ks], ki[:, ks]), 0.0).astype(BF16)
                o_ref[rows, vs] = _nn(att, vh) + _nt(qd[:, ks], s_t.astype(BF16))
                state[h] = s_t * dec[:, ks] + _tn(vh, ke[:, ks])

    return pl.pallas_call(
        body, name=name, grid=(n_chunks // cg,),
        in_specs=[pl.BlockSpec((t, DK), lambda i: (i, C_Q // DK)),
                  pl.BlockSpec((t, DK), lambda i: (i, C_K // DK)),
                  pl.BlockSpec((t, DV), lambda i: (i, C_V // DV)),
                  pl.BlockSpec((t, LANES), lambda i: (i, C_GLR // LANES)),
                  pl.BlockSpec((LANES, DK), lambda i: (0, 0)),
                  pl.BlockSpec((1, DK), lambda i: (0, 0))],
        out_specs=[pl.BlockSpec((t, DV), lambda i: (i, 0)),
                   pl.BlockSpec((cg, HEADS, HV, HK), lambda i: (i, 0, 0, 0))],
        out_shape=[jax.ShapeDtypeStruct((m, DV), F32),
                   jax.ShapeDtypeStruct((n_chunks, HEADS, HV, HK), F32)],
        scratch_shapes=[pltpu.VMEM((HEADS, HV, HK), F32)],
        compiler_params=_params(("arbitrary",)))(p, p, p, p, wgk, bgk)


def gla_bwd(p, wgk, bgk, st, do, dp, *, name):
    m = p.shape[0]
    n_chunks = m // CHUNK
    cg = _chunk_group(n_chunks)
    t = cg * CHUNK
    ns = n_chunks // cg
    scale = HK ** -0.5

    def body(q_ref, k_ref, v_ref, glr_ref, wgk_ref, bgk_ref, st_ref, do_ref, dp_in,
             dqkv_ref, dglr_ref, dwgk_ref, dbgk_ref, dstate, dz_buf):
        i = pl.program_id(0)
        blk = ns - 1 - i

        @pl.when(i == 0)
        def _():
            dstate[...] = jnp.zeros_like(dstate)
            dwgk_ref[...] = jnp.zeros_like(dwgk_ref)
            dbgk_ref[...] = jnp.zeros_like(dbgk_ref)

        z, la = _log_decay(glr_ref[...], wgk_ref[...], bgk_ref[...], blk * t, t)
        ri = lax.broadcasted_iota(jnp.int32, (CHUNK, CHUNK), 0)
        ci = lax.broadcasted_iota(jnp.int32, (CHUNK, CHUNK), 1)
        causal = ri >= ci
        tri = causal.astype(BF16)
        tri_u = (ri <= ci).astype(BF16)
        for c in reversed(range(cg)):
            rows = pl.ds(c * CHUNK, CHUNK)
            b = _tri_mm(tri, la[c * CHUNK:(c + 1) * CHUNK])
            bl = b[CHUNK - 1:CHUNK, :]
            eb = jnp.exp(b)
            enb = jnp.exp(-b)
            ebl = jnp.exp(bl - b)
            dec = jnp.exp(bl)
            q = q_ref[rows, :].astype(F32) * scale
            k = k_ref[rows, :].astype(F32)
            qd32 = q * eb
            ki32 = k * enb
            ke32 = k * ebl
            qd = qd32.astype(BF16)
            ki = ki32.astype(BF16)
            ke = ke32.astype(BF16)
            dqd_parts, dki_parts, dke_parts, ddec_parts = [], [], [], []
            for h in range(HEADS):
                ks = slice(h * HK, (h + 1) * HK)
                vs = pl.ds(h * HV, HV)
                vh = v_ref[rows, vs]
                doh = do_ref[rows, vs].astype(BF16)
                s_t = st_ref[c, h]
                ds_t = dstate[h]
                ds_b = ds_t.astype(BF16)
                att = jnp.where(causal, _nt(qd[:, ks], ki[:, ks]), 0.0).astype(BF16)
                datt = jnp.where(causal, _nt(doh, vh), 0.0).astype(BF16)
                dvh = _tn(att, doh) + _nt(ke[:, ks], ds_b)
                dqkv_ref[rows, pl.ds(2 * DK + h * HV, HV)] = dvh.astype(BF16)
                dqd_parts.append(_nn(datt, ki[:, ks]) + _nn(doh, s_t.astype(BF16)))
                dki_parts.append(_tn(datt, qd[:, ks]))
                dke_parts.append(_nn(vh, ds_b))
                ddec_parts.append(jnp.sum(s_t * ds_t, axis=0, keepdims=True))
                dstate[h] = _tn(doh, qd[:, ks]) + ds_t * dec[:, ks]
            dqd = jnp.concatenate(dqd_parts, axis=1)
            dki = jnp.concatenate(dki_parts, axis=1)
            dke = jnp.concatenate(dke_parts, axis=1)
            ddec = jnp.concatenate(ddec_parts, axis=1)
            dqkv_ref[rows, pl.ds(0, DK)] = (dqd * eb * scale).astype(BF16)
            dqkv_ref[rows, pl.ds(DK, DK)] = (dki * enb + dke * ebl).astype(BF16)
            dke_ke = dke * ke32
            db = dqd * qd32 - dki * ki32 - dke_ke
            dbl = jnp.sum(dke_ke, axis=0, keepdims=True) + ddec * dec
            dg = _tri_mm(tri_u, db) + dbl
            row = blk * t + c * CHUNK + lax.broadcasted_iota(jnp.int32, (CHUNK, 1), 0)
            zc = z[c * CHUNK:(c + 1) * CHUNK]
            dz = jnp.where(row >= PAD, dg * (1.0 / TAU) * _sigmoid(-zc), 0.0)
            dz_buf[rows, :] = dz
        dz_all = dz_buf[...]
        dz_b = dz_all.astype(BF16)
        dbgk_ref[...] += jnp.sum(dz_all, axis=0, keepdims=True)
        dglr_ref[...] = _nt(dz_b, wgk_ref[...]).astype(BF16)
        dwgk_ref[...] += _tn(glr_ref[...], dz_b)

    rev = lambda i: ns - 1 - i
    return pl.pallas_call(
        body, name=name, grid=(ns,),
        in_specs=[pl.BlockSpec((t, DK), lambda i: (rev(i), C_Q // DK)),
                  pl.BlockSpec((t, DK), lambda i: (rev(i), C_K // DK)),
                  pl.BlockSpec((t, DV), lambda i: (rev(i), C_V // DV)),
                  pl.BlockSpec((t, LANES), lambda i: (rev(i), C_GLR // LANES)),
                  pl.BlockSpec((LANES, DK), lambda i: (0, 0)),
                  pl.BlockSpec((1, DK), lambda i: (0, 0)),
                  pl.BlockSpec((cg, HEADS, HV, HK), lambda i: (rev(i), 0, 0, 0)),
                  pl.BlockSpec((t, DV), lambda i: (rev(i), 0)), pl.BlockSpec(memory_space=pl.ANY)],
        out_specs=[pl.BlockSpec((t, 2 * DK + DV), lambda i: (rev(i), 0)),
                   pl.BlockSpec((t, LANES), lambda i: (rev(i), 0)),
                   pl.BlockSpec((LANES, DK), lambda i: (0, 0)),
                   pl.BlockSpec((1, DK), lambda i: (0, 0))],
        out_shape=[jax.ShapeDtypeStruct((m, IN_R), BF16),
                   jax.ShapeDtypeStruct((m, LANES), BF16),
                   jax.ShapeDtypeStruct((LANES, DK), F32),
                   jax.ShapeDtypeStruct((1, DK), F32)],
        input_output_aliases={8: 0},
        scratch_shapes=[pltpu.VMEM((HEADS, HV, HK), F32), pltpu.VMEM((t, DK), F32)],
        compiler_params=_params(("arbitrary",)))(p, p, p, p, wgk, bgk, st, do, dp)


def place_glr(dp, dglr, *, name):
    m = dp.shape[0]
    tm = _ew_tile(m, name)

    def body(dp_in, g_ref, o_ref):
        o_ref[...] = g_ref[...]

    return pl.pallas_call(
        body, name=name, grid=(m // tm,),
        in_specs=[pl.BlockSpec(memory_space=pl.ANY), pl.BlockSpec((tm, LANES), lambda i: (i, 0))],
        out_specs=pl.BlockSpec((tm, LANES), lambda i: (i, C_GLR // LANES)),
        out_shape=jax.ShapeDtypeStruct((m, IN_R), BF16), input_output_aliases={0: 0},
        compiler_params=_params(("parallel",)))(dp, dglr)


HALO = 16


def _shift_down(xx, s):
    return pltpu.roll(xx, s, 0)


def _shift_up(xx, s):
    return pltpu.roll(xx, xx.shape[0] - s, 0)


def mix_pre(o, p, gn, *, name):
    m = o.shape[0]
    tm = _ew_tile(m, name)

    def body(o_ref, r_ref, u_ref, gn_ref, ya_ref, pooled_ref, halo):
        i = pl.program_id(0)

        @pl.when(i == 0)
        def _():
            halo[...] = jnp.zeros_like(halo)

        rv = r_ref[...].astype(F32)
        silu_r = rv * _sigmoid(rv)
        for h in range(HEADS):
            cs = pl.ds(h * HV, HV)
            ov = o_ref[:, cs]
            rs = lax.rsqrt(jnp.mean(ov * ov, axis=-1, keepdims=True) + EPS)
            ya_ref[:, cs] = (ov * rs * gn_ref[...] * silu_r[:, h * HV:(h + 1) * HV]).astype(BF16)

        row = i * tm + lax.broadcasted_iota(jnp.int32, (tm, 1), 0)
        pos1 = jnp.maximum(row - PAD + 1, 1).astype(F32)
        for g, w in enumerate(POOL_WINDOWS):
            cs = pl.ds(g * GDIM, GDIM)
            uv = u_ref[:, cs].astype(F32)
            xx = jnp.concatenate([halo[:, cs], uv], axis=0)
            s = xx
            span = 1
            while span < w:
                s = s + _shift_down(s, span)
                span *= 2
            inv = 1.0 / jnp.minimum(pos1, float(w))
            pooled_ref[:, cs] = (s[HALO:] * inv - uv).astype(BF16)
            halo[:, cs] = uv[tm - HALO:]

    blk = pl.BlockSpec((tm, D), lambda i: (i, 0))
    return pl.pallas_call(
        body, name=name, grid=(m // tm,),
        in_specs=[blk, pl.BlockSpec((tm, D), lambda i: (i, C_R // D)),
                  pl.BlockSpec((tm, D), lambda i: (i, C_U // D)),
                  pl.BlockSpec((1, HV), lambda i: (0, 0))],
        out_specs=[blk, blk],
        out_shape=[jax.ShapeDtypeStruct((m, D), BF16)] * 2,
        scratch_shapes=[pltpu.VMEM((HALO, D), F32)],
        compiler_params=_params(("arbitrary",)))(o, p, p, gn)


def mix_pre_bwd(dya, dpooled, o, p, gn, dp, *, name):
    m = o.shape[0]
    tm = _ew_tile(m, name)
    nt = m // tm

    def body(dya_ref, dpl_ref, o_ref, r_ref, gn_ref, dp_in, do_ref, dp_ref, dgn_ref, halo):
        i = pl.program_id(0)
        blk_i = nt - 1 - i

        @pl.when(i == 0)
        def _():
            halo[...] = jnp.zeros_like(halo)
            dgn_ref[...] = jnp.zeros_like(dgn_ref)

        rv = r_ref[...].astype(F32)
        sg = _sigmoid(rv)
        silu_r = rv * sg
        dsilu = sg * (1.0 + rv * (1.0 - sg))
        dgn = jnp.zeros((1, HV), F32)
        for h in range(HEADS):
            cs = pl.ds(h * HV, HV)
            hs = slice(h * HV, (h + 1) * HV)
            ov = o_ref[:, cs]
            dy = dya_ref[:, cs].astype(F32)
            rs = lax.rsqrt(jnp.mean(ov * ov, axis=-1, keepdims=True) + EPS)
            xh = ov * rs
            on = xh * gn_ref[...]
            don = dy * silu_r[:, hs]
            dp_ref[:, cs] = (dy * on * dsilu[:, hs]).astype(BF16)
            dxh = don * gn_ref[...]
            do_ref[:, cs] = rs * (dxh - xh * jnp.mean(dxh * xh, axis=-1, keepdims=True))
            dgn = dgn + jnp.sum(don * xh, axis=0, keepdims=True)
        dgn_ref[...] += dgn

        row = blk_i * tm + lax.broadcasted_iota(jnp.int32, (tm, 1), 0)
        pos1 = jnp.maximum(row - PAD + 1, 1).astype(F32)
        for g, w in enumerate(POOL_WINDOWS):
            cs = pl.ds(g * GDIM, GDIM)
            dpv = dpl_ref[:, cs].astype(F32)
            e = dpv * (1.0 / jnp.minimum(pos1, float(w)))
            xx = jnp.concatenate([e, halo[:, cs]], axis=0)
            s = xx
            span = 1
            while span < w:
                s = s + _shift_up(s, span)
                span *= 2
            dp_ref[:, pl.ds(D + g * GDIM, GDIM)] = (s[:tm] - dpv).astype(BF16)
            halo[:, cs] = e[:HALO]

    rev = lambda i: nt - 1 - i
    blk = pl.BlockSpec((tm, D), lambda i: (rev(i), 0))
    return pl.pallas_call(
        body, name=name, grid=(nt,),
        in_specs=[blk, blk, blk, pl.BlockSpec((tm, D), lambda i: (rev(i), C_R // D)),
                  pl.BlockSpec((1, HV), lambda i: (0, 0)), pl.BlockSpec(memory_space=pl.ANY)],
        out_specs=[blk, pl.BlockSpec((tm, 2 * D), lambda i: (rev(i), C_R // (2 * D))),
                   pl.BlockSpec((1, HV), lambda i: (0, 0))],
        out_shape=[jax.ShapeDtypeStruct((m, D), F32), jax.ShapeDtypeStruct((m, IN_R), BF16),
                   jax.ShapeDtypeStruct((1, HV), F32)],
        input_output_aliases={5: 1},
        scratch_shapes=[pltpu.VMEM((HALO, D), F32)],
        compiler_params=_params(("arbitrary",)))(dya, dpooled, o, p, gn, dp)


def merge_fwd(p, ya, yb, bg, *, name):
    m = ya.shape[0]
    tm = _ew_tile(m, name)

    def body(ga_ref, gb_ref, ya_ref, yb_ref, ba_ref, bb_ref, o_ref):
        gate_a = _sigmoid(ga_ref[...].astype(F32) + ba_ref[...])
        gate_b = _sigmoid(gb_ref[...].astype(F32) + bb_ref[...])
        o_ref[...] = (gate_a * ya_ref[...].astype(F32) + gate_b * yb_ref[...].astype(F32)).astype(BF16)

    blk = pl.BlockSpec((tm, D), lambda i: (i, 0))
    return pl.pallas_call(
        body, name=name, grid=(m // tm,),
        in_specs=[pl.BlockSpec((tm, D), lambda i: (i, C_GA // D)),
                  pl.BlockSpec((tm, D), lambda i: (i, C_GB // D)), blk, blk,
                  pl.BlockSpec((1, D), lambda i: (0, 0)), pl.BlockSpec((1, D), lambda i: (0, 1))],
        out_specs=blk, out_shape=jax.ShapeDtypeStruct((m, D), BF16),
        compiler_params=_params(("parallel",)))(p, p, ya, yb, bg, bg)


def merge_bwd(dmrg, p, ya, yb, bg, *, name):
    m = ya.shape[0]
    tm = _ew_tile(m, name)

    def body(dm_ref, ga_ref, gb_ref, ya_ref, yb_ref, ba_ref, bb_ref,
             dya_ref, dyb_ref, dp_ref, dbg_ref):
        @pl.when(pl.program_id(0) == 0)
        def _():
            dbg_ref[...] = jnp.zeros_like(dbg_ref)

        dm = dm_ref[...].astype(F32)
        gate_a = _sigmoid(ga_ref[...].astype(F32) + ba_ref[...])
        gate_b = _sigmoid(gb_ref[...].astype(F32) + bb_ref[...])
        dya_ref[...] = (dm * gate_a).astype(BF16)
        dyb_ref[...] = (dm * gate_b).astype(BF16)
        dga = dm * ya_ref[...].astype(F32) * gate_a * (1.0 - gate_a)
        dgb = dm * yb_ref[...].astype(F32) * gate_b * (1.0 - gate_b)
        dp_ref[:, pl.ds(0, D)] = dga.astype(BF16)
        dp_ref[:, pl.ds(D, D)] = dgb.astype(BF16)
        dbg_ref[:, pl.ds(0, D)] += jnp.sum(dga, axis=0, keepdims=True)
        dbg_ref[:, pl.ds(D, D)] += jnp.sum(dgb, axis=0, keepdims=True)

    blk = pl.BlockSpec((tm, D), lambda i: (i, 0))
    return pl.pallas_call(
        body, name=name, grid=(m // tm,),
        in_specs=[blk, pl.BlockSpec((tm, D), lambda i: (i, C_GA // D)),
                  pl.BlockSpec((tm, D), lambda i: (i, C_GB // D)), blk, blk,
                  pl.BlockSpec((1, D), lambda i: (0, 0)), pl.BlockSpec((1, D), lambda i: (0, 1))],
        out_specs=[blk, blk, pl.BlockSpec((tm, 2 * D), lambda i: (i, C_GA // (2 * D))),
                   pl.BlockSpec((1, 2 * D), lambda i: (0, 0))],
        out_shape=[jax.ShapeDtypeStruct((m, D), BF16)] * 2 + [jax.ShapeDtypeStruct((m, IN_R), BF16),
                                                              jax.ShapeDtypeStruct((1, 2 * D), F32)],
        compiler_params=_params(("arbitrary",)))(dmrg, p, p, ya, yb, bg, bg)


def scale_bwd(dy1, y0, scale, *, name):
    m = y0.shape[0]
    tm = _ew_tile(m, name)

    def body(dy_ref, y0_ref, s_ref, o_ref, ds_ref):
        @pl.when(pl.program_id(0) == 0)
        def _():
            ds_ref[...] = jnp.zeros_like(ds_ref)

        dy = dy_ref[...].astype(F32)
        o_ref[...] = (dy * s_ref[...]).astype(BF16)
        ds_ref[...] += jnp.sum(dy * y0_ref[...].astype(F32), axis=0, keepdims=True)

    blk = pl.BlockSpec((tm, D), lambda i: (i, 0))
    vec = pl.BlockSpec((1, D), lambda i: (0, 0))
    return pl.pallas_call(
        body, name=name, grid=(m // tm,), in_specs=[blk, blk, vec], out_specs=[blk, vec],
        out_shape=[jax.ShapeDtypeStruct((m, D), BF16), jax.ShapeDtypeStruct((1, D), F32)],
        compiler_params=_params(("arbitrary",)))(dy1, y0, scale)


CONV_BLK = 1408
CONV_ROWS = 688
N_CONV_BLK = D_FF // CONV_BLK


def conv_act_fwd(up, cw, cb, *, name):
    m = up.shape[0]
    tm = _ew_tile(m, name, cap=CONV_ROWS)

    def conv(x_ref, halo, w_ref, b_ref):
        xv = x_ref[...].astype(F32)
        xx = jnp.concatenate([halo[...], xv], axis=0)
        y = (w_ref[2:3, :] * xx + w_ref[1:2, :] * _shift_down(xx, 1)
             + w_ref[0:1, :] * _shift_down(xx, 2))[HALO:] + b_ref[...]
        halo[...] = xv[tm - HALO:]
        return y

    def body(xa_ref, xb_ref, wa_ref, wb_ref, ba_ref, bb_ref, upc_a_ref, upc_b_ref, act_ref, halo_a, halo_b):
        @pl.when(pl.program_id(1) == 0)
        def _():
            halo_a[...] = jnp.zeros_like(halo_a)
            halo_b[...] = jnp.zeros_like(halo_b)

        a = conv(xa_ref, halo_a, wa_ref, ba_ref)
        bv = conv(xb_ref, halo_b, wb_ref, bb_ref)
        upc_a_ref[...] = a.astype(BF16)
        upc_b_ref[...] = bv.astype(BF16)
        act_ref[...] = (a * _sigmoid(a) * bv).astype(BF16)

    nb = N_CONV_BLK
    xa = pl.BlockSpec((tm, CONV_BLK), lambda j, i: (i, j))
    xb = pl.BlockSpec((tm, CONV_BLK), lambda j, i: (i, j + nb))
    return pl.pallas_call(
        body, name=name, grid=(nb, m // tm),
        in_specs=[xa, xb,
                  pl.BlockSpec((3, CONV_BLK), lambda j, i: (0, j)),
                  pl.BlockSpec((3, CONV_BLK), lambda j, i: (0, j + nb)),
                  pl.BlockSpec((1, CONV_BLK), lambda j, i: (0, j)),
                  pl.BlockSpec((1, CONV_BLK), lambda j, i: (0, j + nb))],
        out_specs=[xa, xa, xa],
        out_shape=[jax.ShapeDtypeStruct((m, D_FF), BF16)] * 3,
        scratch_shapes=[pltpu.VMEM((HALO, CONV_BLK), F32)] * 2,
        compiler_params=_params(("parallel", "arbitrary")))(up, up, cw, cw, cb, cb)


def conv_act_bwd(dact, upc_a, upc_b, up, cw, *, name):
    m = up.shape[0]
    tm = _ew_tile(m, name, cap=CONV_ROWS)
    nt = m // tm

    def conv_t(d, halo, x_ref, w_ref, dup_ref, half, dw_ref, db_ref):
        xx = jnp.concatenate([d, halo[...]], axis=0)
        d1 = _shift_up(xx, 1)[:tm]
        d2 = _shift_up(xx, 2)[:tm]
        dup_ref[half] = (w_ref[2:3, :] * d + w_ref[1:2, :] * d1 + w_ref[0:1, :] * d2).astype(BF16)
        xv = x_ref[...].astype(F32)
        dw_ref[2:3, :] += jnp.sum(xv * d, axis=0, keepdims=True)
        dw_ref[1:2, :] += jnp.sum(xv * d1, axis=0, keepdims=True)
        dw_ref[0:1, :] += jnp.sum(xv * d2, axis=0, keepdims=True)
        db_ref[...] += jnp.sum(d, axis=0, keepdims=True)
        halo[...] = d[:HALO]

    def body(da_ref, a_ref, b_ref, xa_ref, xb_ref, wa_ref, wb_ref,
             dup_ref, dwa_ref, dwb_ref, dba_ref, dbb_ref, halo_a, halo_b):
        @pl.when(pl.program_id(1) == 0)
        def _():
            for r in (halo_a, halo_b, dwa_ref, dwb_ref, dba_ref, dbb_ref):
                r[...] = jnp.zeros_like(r)

        dact_v = da_ref[...].astype(F32)
        a = a_ref[...].astype(F32)
        bv = b_ref[...].astype(F32)
        sg = _sigmoid(a)
        d_a = dact_v * bv * sg * (1.0 + a * (1.0 - sg))
        d_b = dact_v * a * sg
        conv_t(d_a, halo_a, xa_ref, wa_ref, dup_ref, 0, dwa_ref, dba_ref)
        conv_t(d_b, halo_b, xb_ref, wb_ref, dup_ref, 1, dwb_ref, dbb_ref)

    nb = N_CONV_BLK
    rev = lambda i: nt - 1 - i
    half = pl.BlockSpec((tm, CONV_BLK), lambda j, i: (rev(i), j))
    xa = half
    xb = pl.BlockSpec((tm, CONV_BLK), lambda j, i: (rev(i), j + nb))
    wa = pl.BlockSpec((3, CONV_BLK), lambda j, i: (0, j))
    wb = pl.BlockSpec((3, CONV_BLK), lambda j, i: (0, j + nb))
    va = pl.BlockSpec((1, CONV_BLK), lambda j, i: (0, j))
    outs = pl.pallas_call(
        body, name=name, grid=(nb, nt),
        in_specs=[half, half, half, xa, xb, wa, wb],
        out_specs=[pl.BlockSpec((2, tm, CONV_BLK), lambda j, i: (0, rev(i), j)), wa, wa, va, va],
        out_shape=[jax.ShapeDtypeStruct((2, m, D_FF), BF16)]
                  + [jax.ShapeDtypeStruct((3, D_FF), F32)] * 2
                  + [jax.ShapeDtypeStruct((1, D_FF), F32)] * 2,
        scratch_shapes=[pltpu.VMEM((HALO, CONV_BLK), F32)] * 2,
        compiler_params=_params(("parallel", "arbitrary")))(dact, upc_a, upc_b, up, up, cw, cw)
    return outs


def local_step(x, target, w):
    seq = x.shape[0]
    h = jnp.concatenate([jnp.zeros((PAD, D), F32), w["meta"], x], axis=0)
    saved = []
    for l in range(DEPTH):
        wl = {k: (v[l:l + 1] if k in ROW_PARAMS else v[l]) for k, v in w.items() if k not in ("meta", "final_norm_g")}
        s = {"h": h}
        fwd_in(s, wl, f"l{l}_")
        fwd_mixer(s, wl, f"l{l}_")
        fwd_ffn(s, wl, f"l{l}_")
        saved.append(s)
        h = s["h3"]

    dh, dh_b, dgf, loss_rows = loss_head(h, w["final_norm_g"], jnp.pad(target, ((X0, 0), (0, 0))), name="loss_head")
    g = {"final_norm_g": dgf}
    per_layer = []
    for l in reversed(range(DEPTH)):
        wl = {k: (v[l:l + 1] if k in ROW_PARAMS else v[l]) for k, v in w.items() if k not in ("meta", "final_norm_g")}
        s = saved[l]
        gl = {}
        dh2, dh2_b = bwd_ffn(dh, dh_b, s, wl, gl, f"l{l}_")
        dp = bwd_mixer(dh2_b, s, wl, gl, f"l{l}_")
        gl["w_in"] = bwd_in_w(dp, s, f"l{l}_")
        dh, dh_b = bwd_in_x(dp, dh2, s, wl, gl, f"l{l}_")
        per_layer.append(gl)
    per_layer.reverse()
    for k in per_layer[0]:
        g[k] = jnp.stack([per_layer[l][k].astype(F32) for l in range(DEPTH)])
    g["meta"] = dh[PAD:X0]
    return loss_rows, dh[X0:X0 + seq], g


ROW_PARAMS = ("norm1_g", "b_gk", "gla_norm_g", "pool_scale", "b_gates", "norm2_g", "conv_b")


def fwd_in(s, w, ln):
    s["hn1"] = rmsnorm_fwd(s["h"], w["norm1_g"], name=ln + "norm1")
    s["p"] = mm_nn(s["hn1"], w["w_in"], tn=896, name=ln + "in_proj")


def fwd_mixer(s, w, ln):
    p = s["p"]
    s["o"], s["st"] = gla_fwd(p, w["w_gk"], w["b_gk"], name=ln + "gla_fwd")
    s["ya_in"], s["pooled"] = mix_pre(s["o"], p, w["gla_norm_g"], name=ln + "mix_pre")
    s["ya"] = mm_nn(s["ya_in"], w["w_a"], name=ln + "proj_a")
    s["yb0"], s["yb1"] = pool_mm_fwd(s["pooled"], w["w_pool"], w["pool_scale"], name=ln + "pool_mm")
    s["yb"] = mm_nn(s["yb1"], w["w_b"], name=ln + "proj_b")
    s["mrg"] = merge_fwd(p, s["ya"], s["yb"], w["b_gates"], name=ln + "merge")
    s["h2"] = mm_nn(s["mrg"], w["w_o"], out_dtype=F32, res=s["h"], name=ln + "proj_o")


def fwd_ffn(s, w, ln):
    s["hn2"] = rmsnorm_fwd(s["h2"], w["norm2_g"], name=ln + "norm2")
    s["up"] = mm_nn(s["hn2"], w["w_up"], tn=1408, name=ln + "up_proj")
    s["upc_a"], s["upc_b"], s["act"] = conv_act_fwd(s["up"], w["conv_w"], w["conv_b"], name=ln + "conv_act")
    s["h3"] = mm_nn(s["act"], w["w_down"], out_dtype=F32, res=s["h2"], name=ln + "down_proj")


def bwd_ffn(dh, dh_b, s, w, g, ln, after=None):
    dact = mm_nt(dh_b, w["w_down"], tn=1408, after=after, name=ln + "d_act")
    g["w_down"] = mm_tn(s["act"], dh_b, tk1=1408, out_dtype=BF16, after=after, name=ln + "dw_down")
    dup, dcw_a, dcw_b, dcb_a, dcb_b = conv_act_bwd(
        dact, s["upc_a"], s["upc_b"], s["up"], w["conv_w"], name=ln + "conv_act_bwd")
    dhn2 = mm_nt(dup, w["w_up"], out_dtype=F32, tk=1408, halves=True, name=ln + "d_hn2")
    g["w_up"] = mm_tn(s["hn2"], dup, tn=1408, out_dtype=BF16, halves=True, name=ln + "dw_up")
    dh2, dh2_b, g["norm2_g"] = rmsnorm_bwd(dhn2, s["h2"], w["norm2_g"], dh, name=ln + "norm2_bwd")
    g["conv_w"] = jnp.concatenate([dcw_a, dcw_b], axis=1)
    g["conv_b"] = jnp.concatenate([dcb_a, dcb_b], axis=1)
    return dh2, dh2_b


def bwd_mixer(dh2_b, s, w, g, ln, after=None):
    dmrg = mm_nt(dh2_b, w["w_o"], after=after, name=ln + "d_mrg")
    g["w_o"] = mm_tn(s["mrg"], dh2_b, out_dtype=BF16, after=after, name=ln + "dw_o")
    dya, dyb, dp, g["b_gates"] = merge_bwd(dmrg, s["p"], s["ya"], s["yb"], w["b_gates"], name=ln + "merge_bwd")
    dya_in = mm_nt(dya, w["w_a"], name=ln + "d_ya_in")
    g["w_a"] = mm_tn(s["ya_in"], dya, out_dtype=BF16, name=ln + "dw_a")
    dyb1 = mm_nt(dyb, w["w_b"], name=ln + "d_yb1")
    g["w_b"] = mm_tn(s["yb1"], dyb, out_dtype=BF16, name=ln + "dw_b")
    dyb0, g["pool_scale"] = scale_bwd(dyb1, s["yb0"], w["pool_scale"], name=ln + "scale_bwd")
    dpooled = pool_mm_bwd_x(dyb0, w["w_pool"], name=ln + "d_pooled")
    g["w_pool"] = pool_mm_bwd_w(s["pooled"], dyb0, name=ln + "dw_pool")
    do, dp, g["gla_norm_g"] = mix_pre_bwd(dya_in, dpooled, s["o"], s["p"], w["gla_norm_g"], dp,
                                          name=ln + "mix_pre_bwd")
    dp, dglr, g["w_gk"], g["b_gk"] = gla_bwd(s["p"], w["w_gk"], w["b_gk"], s["st"], do, dp, name=ln + "gla_bwd")
    return place_glr(dp, dglr, name=ln + "place_glr")


def bwd_in_w(dp, s, ln):
    return mm_tn(s["hn1"], dp, tn=896, out_dtype=BF16, name=ln + "dw_in")


def bwd_in_x(dp, dh2, s, w, g, ln, after=None):
    dhn1 = mm_nt(dp, w["w_in"], out_dtype=F32, tk=896, after=after, name=ln + "d_hn1")
    dh, dh_b, g["norm1_g"] = rmsnorm_bwd(dhn1, s["h"], w["norm1_g"], dh2, name=ln + "norm1_bwd")
    return dh, dh_b


def _my_place():
    return lax.axis_index("x"), lax.axis_index("y"), lax.axis_index("c")


def _peer(place, k):
    x, y, c = place
    return (1 - x if k & 4 else x, 1 - y if k & 2 else y, 1 - c if k & 1 else c)


def _index(place):
    x, y, c = place
    return 4 * x + 2 * y + c


def exchange(arrays, kinds, *, name):
    n = len(arrays)

    def body(*refs):
        ins, outs = refs[:n], refs[n:2 * n]
        send_sems, recv_sems, local_sems = refs[2 * n:]
        place = _my_place()
        me = _index(place)

        def src(a, dest):
            return ins[a] if kinds[a] == "gather" else ins[a].at[dest]

        def remote(a, k):
            peer = _peer(place, k)
            return pltpu.make_async_remote_copy(
                src_ref=src(a, _index(peer)), dst_ref=outs[a].at[me],
                send_sem=send_sems.at[a, k - 1], recv_sem=recv_sems.at[a, k - 1],
                device_id=peer, device_id_type=pl.DeviceIdType.MESH)

        def arrival(a, k):
            peer = _peer(place, k)
            return pltpu.make_async_remote_copy(
                src_ref=src(a, me), dst_ref=outs[a].at[_index(peer)],
                send_sem=send_sems.at[a, k - 1], recv_sem=recv_sems.at[a, k - 1],
                device_id=peer, device_id_type=pl.DeviceIdType.MESH)

        own = [pltpu.make_async_copy(src(a, me), outs[a].at[me], local_sems.at[a]) for a in range(n)]
        sends = [remote(a, k) for k in range(1, N_DEV) for a in range(n)]
        for cp in sends:
            cp.start()
        for cp in own:
            cp.start()
        for k in range(1, N_DEV):
            for a in range(n):
                arrival(a, k).wait_recv()
        for cp in sends:
            cp.wait_send()
        for cp in own:
            cp.wait()

    any_spec = pl.BlockSpec(memory_space=pl.ANY)
    out_shape = []
    for arr, kind in zip(arrays, kinds):
        shape = arr.shape if kind == "gather" else arr.shape[1:]
        out_shape.append(jax.ShapeDtypeStruct((N_DEV,) + tuple(shape), arr.dtype))
    return pl.pallas_call(
        body, name=name, in_specs=[any_spec] * n, out_specs=[any_spec] * n, out_shape=out_shape,
        scratch_shapes=[pltpu.SemaphoreType.DMA((n, N_DEV - 1)), pltpu.SemaphoreType.DMA((n, N_DEV - 1)),
                        pltpu.SemaphoreType.DMA((n,))],
    )(*arrays)


def _sem_slot(a, k):
    return a * (N_DEV - 1) + k - 1


_HBM = pl.BlockSpec(memory_space=pltpu.HBM)
_SEM = pl.BlockSpec(memory_space=pltpu.SEMAPHORE)
_DATAFLOW = pltpu.SideEffectType.DATAFLOW_SIDE_EFFECTING


def exchange_start(arrays, kinds, after, *, name):
    n = len(arrays)
    zones = []
    for arr, kind in zip(arrays, kinds):
        shape = arr.shape if kind == "gather" else arr.shape[1:]
        zones.append(lax.empty((N_DEV,) + tuple(shape), arr.dtype))

    def body(*refs):
        ins, lands = refs[:n], refs[n:2 * n]
        send_sems, recv_sems = refs[2 * n + 1], refs[2 * n + 2]
        token = refs[4 * n + 3]
        place = _my_place()
        me = _index(place)
        for a in range(n):
            for k in range(1, N_DEV):
                peer = _peer(place, k)
                pltpu.make_async_remote_copy(
                    src_ref=ins[a] if kinds[a] == "gather" else ins[a].at[_index(peer)], dst_ref=lands[a].at[me],
                    send_sem=send_sems.at[_sem_slot(a, k)], recv_sem=recv_sems.at[_sem_slot(a, k)],
                    device_id=peer, device_id_type=pl.DeviceIdType.MESH).start()
        token[...] = jnp.zeros_like(token)

    sems = pltpu.SemaphoreType.DMA((n * (N_DEV - 1),))
    hbm = lambda a: pltpu.HBM(a.shape, a.dtype)
    outs = pl.pallas_call(
        body, name=name,
        out_shape=(sems, sems, *[hbm(a) for a in arrays], *[hbm(z) for z in zones],
                   jax.ShapeDtypeStruct((8, LANES), F32)),
        in_specs=[_HBM] * (2 * n) + [pl.BlockSpec(memory_space=pl.ANY)],
        out_specs=(_SEM, _SEM, *[_HBM] * (2 * n), pl.BlockSpec(memory_space=pltpu.VMEM)),
        input_output_aliases={i: 2 + i for i in range(2 * n)},
        compiler_params=pltpu.CompilerParams(has_side_effects=_DATAFLOW),
    )(*[pltpu.with_memory_space_constraint(a, pltpu.HBM) for a in arrays],
      *[pltpu.with_memory_space_constraint(z, pltpu.HBM) for z in zones], after)
    return dict(send=outs[0], recv=outs[1], srcs=outs[2:2 + n], zones=outs[2 + n:2 + 2 * n],
                token=outs[2 + 2 * n], kinds=kinds)


def exchange_wait(handle, after, *, name):
    kinds = handle["kinds"]
    n = len(kinds)

    def body(*refs):
        ins, lands = refs[:n], refs[n:2 * n]
        send_sems, recv_sems = refs[2 * n], refs[2 * n + 1]
        place = _my_place()
        me = _index(place)
        for a in range(n):
            for k in range(1, N_DEV):
                peer = _peer(place, k)
                src = ins[a] if kinds[a] == "gather" else ins[a].at[_index(peer)]
                copy = pltpu.make_async_remote_copy(
                    src_ref=src, dst_ref=lands[a].at[_index(peer)],
                    send_sem=send_sems.at[_sem_slot(a, k)], recv_sem=recv_sems.at[_sem_slot(a, k)],
                    device_id=peer, device_id_type=pl.DeviceIdType.MESH)
                copy.wait_send()
                copy.wait_recv()

    srcs, zones = handle["srcs"], handle["zones"]
    after = after if isinstance(after, tuple) else (after,)
    hbm = lambda a: pltpu.HBM(a.shape, a.dtype)
    outs = pl.pallas_call(
        body, name=name,
        out_shape=(*[hbm(a) for a in srcs], *[hbm(z) for z in zones]),
        in_specs=[_HBM] * (2 * n) + [_SEM, _SEM] + [pl.BlockSpec(memory_space=pl.ANY)] * len(after),
        out_specs=[_HBM] * (2 * n),
        input_output_aliases={i: i for i in range(2 * n)},
        compiler_params=pltpu.CompilerParams(has_side_effects=_DATAFLOW),
    )(*srcs, *zones, handle["send"], handle["recv"], *after)
    return _fill_own(outs[:n], outs[n:], kinds)


def _fill_own(srcs, zones, kinds):
    me = _index(_my_place())
    filled = []
    for src, zone, kind in zip(srcs, zones, kinds):
        mine = src if kind == "gather" else lax.dynamic_index_in_dim(src, me, 0, keepdims=False)
        filled.append(lax.dynamic_update_index_in_dim(zone, mine, me, 0))
    return filled


def reduce_adam_layer(parts, w, m, v, layer, prev, *, name):
    _, r, c = w.shape
    tr = _pick(r, (256, 352, 128))

    def body(*refs):
        p_ref, w_ref, m_ref, v_ref = refs[:4]
        g_ref, d_ref, m2_ref, v2_ref = refs[-4:]
        g = p_ref[0].astype(F32)
        for i in range(1, N_DEV):
            g = g + p_ref[i].astype(F32)
        m2 = B1 * m_ref[...] + (1.0 - B1) * g
        v2 = B2 * v_ref[...] + (1.0 - B2) * (g * g)
        m_hat = m2 / (1.0 - B1 ** STEP)
        v_hat = v2 / (1.0 - B2 ** STEP)
        g_ref[...] = g
        d_ref[...] = -LR * (m_hat / (jnp.sqrt(v_hat) + ADAM_EPS) + WD * w_ref[...])
        m2_ref[...] = m2
        v2_ref[...] = v2

    blk = pl.BlockSpec((None, tr, c), lambda i: (layer, i, 0))
    in_specs = [pl.BlockSpec((N_DEV, tr, c), lambda i: (0, i, 0)), blk, blk, blk]
    args = [parts, w, m, v]
    aliases = {}
    if prev is not None:
        in_specs += [pl.BlockSpec(memory_space=pl.ANY)] * 4
        args += list(prev)
        aliases = {4 + j: j for j in range(4)}
    return pl.pallas_call(
        body, name=name, grid=(r // tr,), in_specs=in_specs, out_specs=[blk] * 4,
        out_shape=[jax.ShapeDtypeStruct(w.shape, F32)] * 4, input_output_aliases=aliases,
        compiler_params=_params(("parallel",)))(*args)


def reduce_adam(parts, w, m, v, *, name):
    r, c = w.shape
    tr = _pick(r, (256, 352, 192, 128, 72, 64, 32, 16, 8))

    def body(p_ref, w_ref, m_ref, v_ref, g_ref, d_ref, m2_ref, v2_ref):
        g = p_ref[0].astype(F32)
        for i in range(1, N_DEV):
            g = g + p_ref[i].astype(F32)
        wv = w_ref[...]
        m2 = B1 * m_ref[...] + (1.0 - B1) * g
        v2 = B2 * v_ref[...] + (1.0 - B2) * (g * g)
        m_hat = m2 / (1.0 - B1 ** STEP)
        v_hat = v2 / (1.0 - B2 ** STEP)
        g_ref[...] = g
        d_ref[...] = -LR * (m_hat / (jnp.sqrt(v_hat) + ADAM_EPS) + WD * wv)
        m2_ref[...] = m2
        v2_ref[...] = v2

    blk = pl.BlockSpec((tr, c), lambda i: (i, 0))
    return pl.pallas_call(
        body, name=name, grid=(r // tr,),
        in_specs=[pl.BlockSpec((N_DEV, tr, c), lambda i: (0, i, 0)), blk, blk, blk],
        out_specs=[blk] * 4, out_shape=[jax.ShapeDtypeStruct((r, c), F32)] * 4,
        compiler_params=_params(("parallel",)))(parts, w, m, v)


BIG = ("w_in", "w_a", "w_pool_grp", "w_b", "w_o", "w_up", "w_down")
SHARDED_SMALL = ("meta_tokens", "w_gk", "conv_w")
REPLICATED = ("norm1_g", "b_gk", "gla_norm_g", "pool_scale", "b_gates", "norm2_g", "conv_b", "final_norm_g")
CUT_AXIS = {"w_in": 2, "w_a": 1, "w_pool_grp": 2, "w_b": 1, "w_o": 1, "w_up": 2, "w_down": 1,
            "meta_tokens": 1, "w_gk": 2, "conv_w": 2}
WEIGHTS = ("meta_tokens", "norm1_g", "w_in", "w_gk", "b_gk", "gla_norm_g", "w_a", "w_pool_grp", "pool_scale",
           "w_b", "b_gates", "w_o", "norm2_g", "w_up", "conv_w", "conv_b", "w_down", "final_norm_g")


def _as_2d(a):
    return a.reshape(-1, a.shape[-1])


def _from_slots(slots, axis):
    full = jnp.moveaxis(slots, 0, axis)
    shape = list(full.shape)
    shape[axis:axis + 2] = [shape[axis] * shape[axis + 1]]
    return full.reshape(shape)


def _to_slots(full, axis):
    shape = list(full.shape)
    shape[axis:axis + 1] = [N_DEV, shape[axis] // N_DEV]
    return jnp.moveaxis(full.reshape(shape), axis, 0)


def _pack(vectors, rows):
    flat = jnp.concatenate([v.reshape(-1).astype(F32) for v in vectors])
    return jnp.pad(flat, (0, rows * LANES - flat.shape[0])).reshape(rows, LANES)


def _unpack(packed, shapes):
    flat = packed.reshape(-1)
    out, off = [], 0
    for s in shapes:
        size = 1
        for d in s:
            size *= d
        out.append(flat[off:off + size].reshape(s))
        off += size
    return out


def _rows_for(shapes, mult=8):
    total = 0
    for s in shapes:
        size = 1
        for d in s:
            size *= d
        total += size
    rows = -(-total // LANES)
    return -(-rows // mult) * mult


def _permute_in(w_in):
    pad = jnp.zeros(w_in.shape[:-1] + (IN_R - IN_WIDTH,), w_in.dtype)
    return jnp.concatenate([w_in[..., :2048], w_in[..., 2064:], w_in[..., 2048:2064], pad], axis=-1)


def _unpermute_in(w_r):
    return jnp.concatenate([w_r[..., :2048], w_r[..., C_GLR:C_GLR + RANK], w_r[..., 2048:C_GLR]], axis=-1)


def kernel(x, meta_tokens, norm1_g, w_in, w_gk, b_gk, gla_norm_g, w_a, w_pool_grp, pool_scale, w_b, b_gates, w_o, norm2_g, w_up, conv_w, conv_b, w_down, final_norm_g, loss_target, m_meta_tokens, m_norm1_g, m_w_in, m_w_gk, m_b_gk, m_gla_norm_g, m_w_a, m_w_pool_grp, m_pool_scale, m_w_b, m_b_gates, m_w_o, m_norm2_g, m_w_up, m_conv_w, m_conv_b, m_w_down, m_final_norm_g, v_meta_tokens, v_norm1_g, v_w_in, v_w_gk, v_b_gk, v_gla_norm_g, v_w_a, v_w_pool_grp, v_pool_scale, v_w_b, v_b_gates, v_w_o, v_norm2_g, v_w_up, v_conv_w, v_conv_b, v_w_down, v_final_norm_g):
    wts = dict(meta_tokens=meta_tokens, norm1_g=norm1_g, w_in=w_in, w_gk=w_gk, b_gk=b_gk, gla_norm_g=gla_norm_g,
               w_a=w_a, w_pool_grp=w_pool_grp, pool_scale=pool_scale, w_b=w_b, b_gates=b_gates, w_o=w_o,
               norm2_g=norm2_g, w_up=w_up, conv_w=conv_w, conv_b=conv_b, w_down=w_down, final_norm_g=final_norm_g)
    mom = dict(meta_tokens=m_meta_tokens, norm1_g=m_norm1_g, w_in=m_w_in, w_gk=m_w_gk, b_gk=m_b_gk,
               gla_norm_g=m_gla_norm_g, w_a=m_w_a, w_pool_grp=m_w_pool_grp, pool_scale=m_pool_scale, w_b=m_w_b,
               b_gates=m_b_gates, w_o=m_w_o, norm2_g=m_norm2_g, w_up=m_w_up, conv_w=m_conv_w, conv_b=m_conv_b,
               w_down=m_w_down, final_norm_g=m_final_norm_g)
    var = dict(meta_tokens=v_meta_tokens, norm1_g=v_norm1_g, w_in=v_w_in, w_gk=v_w_gk, b_gk=v_b_gk,
               gla_norm_g=v_gla_norm_g, w_a=v_w_a, w_pool_grp=v_w_pool_grp, pool_scale=v_pool_scale, w_b=v_w_b,
               b_gates=v_b_gates, w_o=v_w_o, norm2_g=v_norm2_g, w_up=v_w_up, conv_w=v_conv_w, conv_b=v_conv_b,
               w_down=v_w_down, final_norm_g=v_final_norm_g)

    small_shapes = [wts[n].shape for n in SHARDED_SMALL]
    small_rows = _rows_for(small_shapes)

    def shard3(a):
        return a.reshape(DEPTH, -1, a.shape[-1])

    def layer_shards(l, names):
        return [shard3(wts[n])[l].astype(BF16) for n in names]

    def tie(row, handle):
        return row + handle["token"][0:1, 0:1]

    def full_weight(n, zone):
        if n == "w_in":
            return _permute_in(jnp.moveaxis(zone, 0, 1).reshape(D, IN_WIDTH))
        if n == "w_up":
            return jnp.moveaxis(zone, 0, 1).reshape(D, F2)
        if n == "w_pool_grp":
            return jnp.moveaxis(zone.reshape(N_DEV, GROUPS, GDIM // N_DEV, GDIM), 0, 1).reshape(GROUPS, GDIM, GDIM)
        return zone.reshape(-1, zone.shape[-1])

    groups = [("w_in",), ("w_a", "w_pool_grp", "w_b", "w_o"), ("w_up", "w_down")]
    rest = groups[0] + groups[1]
    key = {"w_pool_grp": "w_pool"}
    rows = dict(norm1_g=norm1_g, b_gk=b_gk, gla_norm_g=gla_norm_g, pool_scale=pool_scale, b_gates=b_gates,
                norm2_g=norm2_g, conv_b=conv_b)

    def gather(l, names, after, name, head=()):
        return exchange_start(list(head) + layer_shards(l, names), ["gather"] * (len(head) + len(names)), after,
                              name=name + "_start")

    def landed(handle, after, name, names, w_layer):
        zones = exchange_wait(handle, after, name=name + "_wait")
        for n, z in zip(names, zones[len(zones) - len(names):]):
            w_layer[key.get(n, n)] = full_weight(n, z)
        return zones

    wl = [{n: v[l:l + 1] for n, v in rows.items()} for l in range(DEPTH)]
    g_in0 = gather(0, groups[0], x, "gather_in0", head=[_pack([wts[n] for n in SHARDED_SMALL], small_rows)])
    target = jnp.pad(loss_target[0], ((X0, 0), (0, 0)))
    zones = landed(g_in0, (g_in0["token"], target), "gather_in0", groups[0], wl[0])
    small_slots = [jnp.stack(parts) for parts in zip(*[_unpack(zones[0][i], small_shapes) for i in range(N_DEV)])]
    small_full = {n: _from_slots(slots, CUT_AXIS[n]) for n, slots in zip(SHARDED_SMALL, small_slots)}
    w_gk_pad = jnp.pad(small_full["w_gk"], ((0, 0), (0, LANES - RANK), (0, 0))).astype(BF16)
    for l in range(DEPTH):
        wl[l]["w_gk"] = w_gk_pad[l]
        wl[l]["conv_w"] = small_full["conv_w"][l]
    g_mix0 = gather(0, groups[1], zones[1], "gather_mix0")
    g_ffn0 = gather(0, groups[2], g_mix0["token"], "gather_ffn0")
    wl[0]["norm1_g"] = tie(wl[0]["norm1_g"], g_ffn0)

    h = jnp.concatenate([jnp.zeros((PAD, D), F32), small_full["meta_tokens"], x[0]], axis=0)
    s0 = {"h": h}
    fwd_in(s0, wl[0], "l0_")
    zones = landed(g_mix0, s0["p"], "gather_mix0", groups[1], wl[0])
    g_in1 = gather(1, groups[0], zones[0], "gather_in1")
    wl[0]["b_gk"] = tie(wl[0]["b_gk"], g_in1)
    fwd_mixer(s0, wl[0], "l0_")
    zones = landed(g_ffn0, s0["h2"], "gather_ffn0", groups[2], wl[0])
    g_mix1 = gather(1, groups[1], zones[0], "gather_mix1")
    g_ffn1 = gather(1, groups[2], g_mix1["token"], "gather_ffn1")
    wl[0]["norm2_g"] = tie(wl[0]["norm2_g"], g_ffn1)
    fwd_ffn(s0, wl[0], "l0_")
    landed(g_in1, s0["h3"], "gather_in1", groups[0], wl[1])
    s1 = {"h": s0["h3"]}
    fwd_in(s1, wl[1], "l1_")
    landed(g_mix1, s1["p"], "gather_mix1", groups[1], wl[1])
    fwd_mixer(s1, wl[1], "l1_")
    landed(g_ffn1, s1["h2"], "gather_ffn1", groups[2], wl[1])
    fwd_ffn(s1, wl[1], "l1_")
    dh, dh_b, dgf, loss_rows = loss_head(s1["h3"], final_norm_g[None], target, name="loss_head")
    loss_part = 0.5 * jnp.sum(loss_rows) / D

    def blocks(n, gw):
        if n == "w_in":
            return jnp.moveaxis(_unpermute_in(gw).reshape(D, N_DEV, IN_WIDTH // N_DEV), 1, 0)
        if n == "w_up":
            return jnp.moveaxis(gw.reshape(D, N_DEV, F2 // N_DEV), 1, 0)
        if n == "w_pool_grp":
            gw = gw.astype(BF16).reshape(GROUPS, N_DEV, GDIM // N_DEV, GDIM)
            return jnp.moveaxis(gw, 1, 0).reshape(N_DEV, GROUPS * GDIM // N_DEV, GDIM)
        return gw.reshape(N_DEV, gw.shape[0] // N_DEV, gw.shape[1])

    def scatter(g, names, after, name):
        return exchange_start([blocks(n, g[key.get(n, n)]) for n in names], ["scatter"] * len(names), after,
                              name=name + "_start")

    g1, g0 = {}, {}
    dh2, dh2_b = bwd_ffn(dh, dh_b, s1, wl[1], g1, "l1_")
    s_ffn1 = scatter(g1, groups[2], dh2, "scatter_ffn1")
    dp = bwd_mixer(dh2_b, s1, wl[1], g1, "l1_", after=s_ffn1["token"])
    g1["w_in"] = bwd_in_w(dp, s1, "l1_")
    s_rest1 = scatter(g1, rest, s_ffn1["token"], "scatter_rest1")
    dh, dh_b = bwd_in_x(dp, dh2, s1, wl[1], g1, "l1_", after=s_rest1["token"])
    dh2, dh2_b = bwd_ffn(dh, dh_b, s0, wl[0], g0, "l0_")
    r_ffn1 = exchange_wait(s_ffn1, dh2, name="scatter_ffn1_wait")
    s_ffn0 = scatter(g0, groups[2], r_ffn1[0], "scatter_ffn0")
    dp = bwd_mixer(dh2_b, s0, wl[0], g0, "l0_", after=s_ffn0["token"])
    r_rest1 = exchange_wait(s_rest1, dp, name="scatter_rest1_wait")
    g0["w_in"] = bwd_in_w(dp, s0, "l0_")
    s_rest0 = scatter(g0, rest, r_rest1[0], "scatter_rest0")
    dh, _ = bwd_in_x(dp, dh2, s0, wl[0], g0, "l0_", after=s_rest0["token"])
    r_ffn0 = exchange_wait(s_ffn0, dh, name="scatter_ffn0_wait")
    r_rest0 = exchange_wait(s_rest0, r_ffn0[0], name="scatter_rest0_wait")
    grad_x = dh[X0:]
    recv = [dict(zip(groups[2] + rest, list(r_ffn0) + list(r_rest0))),
            dict(zip(groups[2] + rest, list(r_ffn1) + list(r_rest1)))]

    grads, delta, new_m, new_v = {}, {}, {}, {}
    for n in BIG:
        w3, m3, v3 = shard3(wts[n]), shard3(mom[n]), shard3(var[n])
        first = reduce_adam_layer(recv[1][n], w3, m3, v3, 1, None, name="adam_l1_" + n)
        outs = reduce_adam_layer(recv[0][n], w3, m3, v3, 0, first, name="adam_l0_" + n)
        grads[n], delta[n], new_m[n], new_v[n] = [o.reshape(wts[n].shape) for o in outs]

    g_full = {n: jnp.stack([g0[n], g1[n]])[:, 0] for n in rows}
    g_full["final_norm_g"] = dgf[0]
    g_full["meta_tokens"] = dh[PAD:X0]
    g_full["w_gk"] = jnp.stack([g0["w_gk"], g1["w_gk"]])[:, :RANK]
    g_full["conv_w"] = jnp.stack([g0["conv_w"], g1["conv_w"]])
    rep_shapes = [wts[n].shape for n in REPLICATED] + [(1,)]
    rep_rows = _rows_for(rep_shapes)
    small_blocks = jnp.stack([
        _pack([_to_slots(g_full[n], CUT_AXIS[n])[i] for n in SHARDED_SMALL], small_rows) for i in range(N_DEV)])
    rep_pack = _pack([g_full[n] for n in REPLICATED] + [loss_part.reshape(1)], rep_rows)
    received = exchange([small_blocks, rep_pack], ["scatter", "gather"], name="exchange_small")
    outs = reduce_adam(received[-2], _pack([wts[n] for n in SHARDED_SMALL], small_rows),
                       _pack([mom[n] for n in SHARDED_SMALL], small_rows),
                       _pack([var[n] for n in SHARDED_SMALL], small_rows), name="adam_small")
    for d, o in zip((grads, delta, new_m, new_v), outs):
        for n, a in zip(SHARDED_SMALL, _unpack(o, small_shapes)):
            d[n] = a
    one = [jnp.zeros((1,), F32)]
    outs = reduce_adam(received[-1], _pack([wts[n] for n in REPLICATED] + one, rep_rows),
                       _pack([mom[n] for n in REPLICATED] + one, rep_rows),
                       _pack([var[n] for n in REPLICATED] + one, rep_rows), name="adam_replicated")
    for d, o in zip((grads, delta, new_m, new_v), outs):
        for n, a in zip(REPLICATED + ("loss",), _unpack(o, rep_shapes)):
            d[n] = a
    loss = grads["loss"][0]
    return (loss, grad_x[None], *[grads[n] for n in WEIGHTS], *[delta[n] for n in WEIGHTS],
            *[new_m[n] for n in WEIGHTS], *[new_v[n] for n in WEIGHTS])
```

```python
import functools

import jax
import jax.numpy as jnp
from jax import lax
from jax.experimental import pallas as pl
from jax.experimental.pallas import tpu as pltpu

F32 = jnp.float32
BF16 = jnp.bfloat16

D = 1024
DEPTH = 2
N_META = 16
HEADS = 4
DK = 512
DV = 1024
HK = 128
HV = 256
RANK = 16
TAU = 16.0
CHUNK = 64
POOL_WINDOWS = (2, 4, 8, 16)
GROUPS = 4
GDIM = 256
D_FF = 2816
F2 = 2 * D_FF
EPS = 1e-6
IN_WIDTH = 6160
LR, B1, B2, ADAM_EPS, WD, STEP = 0.001, 0.9, 0.999, 1e-8, 0.01, 10

N_DEV = 8
PAD = CHUNK - N_META
X0 = CHUNK
IN_R = 6272
C_Q, C_K, C_V, C_R, C_U, C_GA, C_GB, C_GLR = 0, 512, 1024, 2048, 3072, 4096, 5120, 6144
VMEM_LIMIT = 56 * 1024 * 1024
LANES = 128


def _params(sem=None):
    return pltpu.CompilerParams(dimension_semantics=sem, vmem_limit_bytes=VMEM_LIMIT)


def _pick(n, prefs):
    for t in prefs:
        if n % t == 0:
            return t
    raise ValueError(f"no tile for {n} in {prefs}")


MM_VMEM_BUDGET = 44 * 1024 * 1024
MM_TILES = {"l0_": (2752, 1376, 688, 192, 128, 64), "l1_": (2752, 1376, 688, 192, 128, 64)}
EW_TILES = {"l0_": (688, 192, 128, 64), "l1_": (688, 192, 128, 64)}


def _row_tile(lp, name="", row_bytes=0, fixed_bytes=0):
    for t in MM_TILES.get(name[:3], MM_TILES["l0_"]):
        if lp % t == 0 and (t * row_bytes + fixed_bytes <= MM_VMEM_BUDGET or t <= 688):
            return t
    raise ValueError(f"no row tile for {lp}")


def _ew_tile(lp, name="", cap=None):
    return _pick(lp, [t for t in EW_TILES.get(name[:3], EW_TILES["l0_"]) if cap is None or t <= cap])


def _sigmoid(x):
    return 1.0 / (1.0 + jnp.exp(-x))


def _dot(a, b, dims):
    return lax.dot_general(a, b, (dims, ((), ())), preferred_element_type=F32)


def _nn(a, b):
    return _dot(a, b, ((1,), (0,)))


def _nt(a, b):
    return _dot(a, b, ((1,), (1,)))


def _tn(a, b):
    return _dot(a, b, ((0,), (0,)))


def mm_nn(a, b, *, out_dtype=BF16, tn=None, res=None, after=None, halves=False, name):
    m, k = (a.shape[1], 2 * a.shape[2]) if halves else a.shape
    n = b.shape[1]
    tn = tn or n
    has_res = res is not None
    out_bytes = jnp.dtype(out_dtype).itemsize
    tm = _row_tile(m, name, 4 * k + 2 * tn * out_bytes + (8 * tn if has_res else 0), 4 * k * tn)
    extra = [] if after is None else [after]

    def body(*refs):
        a_ref, b_ref = refs[:2]
        o_ref = refs[-1]
        if has_res:
            r_ref = refs[2]
        if halves:
            acc = _nn(a_ref[0], b_ref[pl.ds(0, k // 2), :]) + _nn(a_ref[1], b_ref[pl.ds(k // 2, k // 2), :])
        else:
            acc = _nn(a_ref[...], b_ref[...])
        if has_res:
            row = pl.program_id(1) * tm + lax.broadcasted_iota(jnp.int32, (tm, 1), 0)
            acc = jnp.where(row >= PAD, acc + r_ref[...], 0.0)
        o_ref[...] = acc.astype(o_ref.dtype)

    a_spec = (pl.BlockSpec((2, tm, k // 2), lambda j, i: (0, i, 0)) if halves
              else pl.BlockSpec((tm, k), lambda j, i: (i, 0)))
    in_specs = [a_spec, pl.BlockSpec((k, tn), lambda j, i: (0, j))]
    args = [a, b]
    if has_res:
        in_specs.append(pl.BlockSpec((tm, tn), lambda j, i: (i, j)))
        args.append(res)
    in_specs += [pl.BlockSpec(memory_space=pl.ANY)] * len(extra)
    args += extra
    return pl.pallas_call(
        body, name=name, grid=(n // tn, m // tm), in_specs=in_specs,
        out_specs=pl.BlockSpec((tm, tn), lambda j, i: (i, j)),
        out_shape=jax.ShapeDtypeStruct((m, n), out_dtype),
        compiler_params=_params(("parallel", "parallel")))(*args)


def mm_nt(a, b, *, out_dtype=BF16, tn=None, tk=None, after=None, halves=False, name):
    m, k = (a.shape[1], 2 * a.shape[2]) if halves else a.shape
    n = b.shape[0]
    tn = tn or n
    tk = tk or k
    nk = k // tk
    tm = _row_tile(m, name, 4 * tk + 2 * tn * jnp.dtype(out_dtype).itemsize + (4 * tn if nk > 1 else 0), 4 * tn * tk)
    extra = [] if after is None else [after]
    if halves:
        per = nk // 2
        a_spec = pl.BlockSpec((None, tm, tk), lambda j, i, kk: (kk // per, i, kk % per))
    else:
        a_spec = pl.BlockSpec((tm, tk), lambda j, i, kk: (i, kk))

    def body(a_ref, b_ref, *rest):
        o_ref, acc_ref = rest[-2:]
        kk = pl.program_id(2)
        part = _nt(a_ref[...], b_ref[...])
        if nk == 1:
            o_ref[...] = part.astype(o_ref.dtype)
            return

        @pl.when(kk == 0)
        def _():
            acc_ref[...] = part

        @pl.when(kk > 0)
        def _():
            acc_ref[...] += part

        @pl.when(kk == nk - 1)
        def _():
            o_ref[...] = acc_ref[...].astype(o_ref.dtype)

    return pl.pallas_call(
        body, name=name, grid=(n // tn, m // tm, nk),
        in_specs=[a_spec, pl.BlockSpec((tn, tk), lambda j, i, kk: (j, kk))]
                 + [pl.BlockSpec(memory_space=pl.ANY)] * len(extra),
        out_specs=pl.BlockSpec((tm, tn), lambda j, i, kk: (i, j)),
        out_shape=jax.ShapeDtypeStruct((m, n), out_dtype),
        scratch_shapes=[pltpu.VMEM((tm, tn) if nk > 1 else (8, LANES), F32)],
        compiler_params=_params(("parallel", "parallel", "arbitrary")))(a, b, *extra)


def mm_tn(a, b, *, tk1=None, tn=None, out_dtype=F32, after=None, halves=False, name):
    m, k1 = (a.shape[1], 2 * a.shape[2]) if halves else a.shape
    n = b.shape[1]
    tk1 = tk1 or k1
    tn = tn or n
    tm = _row_tile(m, name, 4 * tk1 + 4 * tn, tk1 * tn * (4 + 2 * jnp.dtype(out_dtype).itemsize))
    nm = m // tm
    extra = [] if after is None else [after]
    if halves:
        per = k1 // tk1 // 2
        a_spec = pl.BlockSpec((None, tm, tk1), lambda p, j, i: (p // per, i, p % per))
    else:
        a_spec = pl.BlockSpec((tm, tk1), lambda p, j, i: (i, p))

    def body(a_ref, b_ref, *rest):
        o_ref, acc_ref = rest[-2:]
        i = pl.program_id(2)
        part = _tn(a_ref[...], b_ref[...])

        @pl.when(i == 0)
        def _():
            acc_ref[...] = part

        @pl.when(i > 0)
        def _():
            acc_ref[...] += part

        @pl.when(i == nm - 1)
        def _():
            o_ref[...] = acc_ref[...].astype(o_ref.dtype)

    return pl.pallas_call(
        body, name=name, grid=(k1 // tk1, n // tn, nm),
        in_specs=[a_spec, pl.BlockSpec((tm, tn), lambda p, j, i: (i, j))]
                 + [pl.BlockSpec(memory_space=pl.ANY)] * len(extra),
        out_specs=pl.BlockSpec((tk1, tn), lambda p, j, i: (p, j)),
        out_shape=jax.ShapeDtypeStruct((k1, n), out_dtype),
        scratch_shapes=[pltpu.VMEM((tk1, tn), F32)],
        compiler_params=_params(("parallel", "parallel", "arbitrary")))(a, b, *extra)


def pool_mm_fwd(pooled, wp, scale, *, name):
    m = pooled.shape[0]
    tm = _row_tile(m, name)

    def body(a_ref, w_ref, s_ref, y0_ref, y1_ref):
        acc = _nn(a_ref[...], w_ref[...])
        y0_ref[...] = acc.astype(BF16)
        y1_ref[...] = (acc * s_ref[...]).astype(BF16)

    blk = pl.BlockSpec((tm, GDIM), lambda g, i: (i, g))
    return pl.pallas_call(
        body, name=name, grid=(GROUPS, m // tm),
        in_specs=[blk, pl.BlockSpec((None, GDIM, GDIM), lambda g, i: (g, 0, 0)),
                  pl.BlockSpec((1, GDIM), lambda g, i: (0, g))],
        out_specs=[blk, blk],
        out_shape=[jax.ShapeDtypeStruct((m, D), BF16)] * 2,
        compiler_params=_params(("parallel", "parallel")))(pooled, wp, scale)


def pool_mm_bwd_x(dy0, wp, *, name):
    m = dy0.shape[0]
    tm = _row_tile(m, name)

    def body(a_ref, w_ref, o_ref):
        o_ref[...] = _nt(a_ref[...], w_ref[...]).astype(BF16)

    blk = pl.BlockSpec((tm, GDIM), lambda g, i: (i, g))
    return pl.pallas_call(
        body, name=name, grid=(GROUPS, m // tm),
        in_specs=[blk, pl.BlockSpec((None, GDIM, GDIM), lambda g, i: (g, 0, 0))],
        out_specs=blk, out_shape=jax.ShapeDtypeStruct((m, D), BF16),
        compiler_params=_params(("parallel", "parallel")))(dy0, wp)


def pool_mm_bwd_w(pooled, dy0, *, name):
    m = pooled.shape[0]
    tm = _row_tile(m, name)

    def body(a_ref, b_ref, o_ref):
        part = _tn(a_ref[...], b_ref[...])

        @pl.when(pl.program_id(1) == 0)
        def _():
            o_ref[...] = part

        @pl.when(pl.program_id(1) > 0)
        def _():
            o_ref[...] += part

    blk = pl.BlockSpec((tm, GDIM), lambda g, i: (i, g))
    return pl.pallas_call(
        body, name=name, grid=(GROUPS, m // tm), in_specs=[blk, blk],
        out_specs=pl.BlockSpec((None, GDIM, GDIM), lambda g, i: (g, 0, 0)),
        out_shape=jax.ShapeDtypeStruct((GROUPS, GDIM, GDIM), F32),
        compiler_params=_params(("parallel", "arbitrary")))(pooled, dy0)


def rmsnorm_fwd(x, g, *, name):
    m = x.shape[0]
    tm = _ew_tile(m, name)

    def body(x_ref, g_ref, o_ref):
        xv = x_ref[...]
        r = lax.rsqrt(jnp.mean(xv * xv, axis=-1, keepdims=True) + EPS)
        o_ref[...] = (xv * r * g_ref[...]).astype(BF16)

    return pl.pallas_call(
        body, name=name, grid=(m // tm,),
        in_specs=[pl.BlockSpec((tm, D), lambda i: (i, 0)), pl.BlockSpec((1, D), lambda i: (0, 0))],
        out_specs=pl.BlockSpec((tm, D), lambda i: (i, 0)),
        out_shape=jax.ShapeDtypeStruct((m, D), BF16),
        compiler_params=_params(("parallel",)))(x, g)


def rmsnorm_bwd(dy, x, g, dres, *, name):
    m = x.shape[0]
    tm = _ew_tile(m, name)

    def body(dy_ref, x_ref, g_ref, r_ref, dx_ref, dxb_ref, dg_ref):
        i = pl.program_id(0)
        xv = x_ref[...]
        dyv = dy_ref[...].astype(F32)
        r = lax.rsqrt(jnp.mean(xv * xv, axis=-1, keepdims=True) + EPS)
        xh = xv * r
        dxh = dyv * g_ref[...]
        dx = r * (dxh - xh * jnp.mean(dxh * xh, axis=-1, keepdims=True))
        row = i * tm + lax.broadcasted_iota(jnp.int32, (tm, 1), 0)
        dx = jnp.where(row >= PAD, dx + r_ref[...], 0.0)
        dx_ref[...] = dx
        dxb_ref[...] = dx.astype(BF16)

        @pl.when(i == 0)
        def _():
            dg_ref[...] = jnp.zeros_like(dg_ref)

        dg_ref[...] += jnp.sum(dyv * xh, axis=0, keepdims=True)

    blk = pl.BlockSpec((tm, D), lambda i: (i, 0))
    vec = pl.BlockSpec((1, D), lambda i: (0, 0))
    return pl.pallas_call(
        body, name=name, grid=(m // tm,), in_specs=[blk, blk, vec, blk],
        out_specs=[blk, blk, vec],
        out_shape=[jax.ShapeDtypeStruct((m, D), F32), jax.ShapeDtypeStruct((m, D), BF16),
                   jax.ShapeDtypeStruct((1, D), F32)],
        compiler_params=_params(("arbitrary",)))(dy, x, g, dres)


def loss_head(h, gf, target, *, name):
    m = h.shape[0]
    t = _ew_tile(m, name)
    inv_d = 1.0 / D

    def body(h_ref, g_ref, t_ref, dh_ref, dhb_ref, dg_ref, ls_ref):
        i = pl.program_id(0)

        @pl.when(i == 0)
        def _():
            dg_ref[...] = jnp.zeros_like(dg_ref)
            ls_ref[...] = jnp.zeros_like(ls_ref)

        real = i * t + lax.broadcasted_iota(jnp.int32, (t, 1), 0) >= X0
        xv = h_ref[...]
        r = lax.rsqrt(jnp.mean(xv * xv, axis=-1, keepdims=True) + EPS)
        xh = xv * r
        err = jnp.where(real, xh * g_ref[...] - t_ref[...], 0.0)
        ls_ref[...] += jnp.sum(err * err, axis=0, keepdims=True)
        dy = err * inv_d
        dg_ref[...] += jnp.sum(dy * xh, axis=0, keepdims=True)
        dxh = dy * g_ref[...]
        dh = r * (dxh - xh * jnp.mean(dxh * xh, axis=-1, keepdims=True))
        dh_ref[...] = dh
        dhb_ref[...] = dh.astype(BF16)

    blk = pl.BlockSpec((t, D), lambda i: (i, 0))
    vec = pl.BlockSpec((1, D), lambda i: (0, 0))
    return pl.pallas_call(
        body, name=name, grid=(m // t,),
        in_specs=[blk, vec, blk],
        out_specs=[blk, blk, vec, vec],
        out_shape=[jax.ShapeDtypeStruct((m, D), F32), jax.ShapeDtypeStruct((m, D), BF16),
                   jax.ShapeDtypeStruct((1, D), F32), jax.ShapeDtypeStruct((1, D), F32)],
        compiler_params=_params(("arbitrary",)))(h, gf, target)


def _split3(x):
    x1 = x.astype(BF16)
    r1 = x - x1.astype(F32)
    x2 = r1.astype(BF16)
    x3 = (r1 - x2.astype(F32)).astype(BF16)
    return x1, x2, x3


def _tri_mm(tri, x):
    x1, x2, x3 = _split3(x)
    return _nn(tri, x1) + _nn(tri, x2) + _nn(tri, x3)


def _log_decay(glr, wgk, bgk, row0, rows):
    z = _nn(glr, wgk) + bgk
    la = (jnp.minimum(z, 0.0) - jnp.log(1.0 + jnp.exp(-jnp.abs(z)))) * (1.0 / TAU)
    row = row0 + lax.broadcasted_iota(jnp.int32, (rows, 1), 0)
    return z, jnp.where(row >= PAD, la, 0.0)


def _chunk_group(n_chunks):
    return _pick(n_chunks, (3, 2, 1))


def gla_fwd(p, wgk, bgk, *, name):
    m = p.shape[0]
    n_chunks = m // CHUNK
    cg = _chunk_group(n_chunks)
    t = cg * CHUNK
    scale = HK ** -0.5

    def body(q_ref, k_ref, v_ref, glr_ref, wgk_ref, bgk_ref, o_ref, st_ref, state):
        i = pl.program_id(0)

        @pl.when(i == 0)
        def _():
            state[...] = jnp.zeros_like(state)

        _, la = _log_decay(glr_ref[...], wgk_ref[...], bgk_ref[...], i * t, t)
        ri = lax.broadcasted_iota(jnp.int32, (CHUNK, CHUNK), 0)
        ci = lax.broadcasted_iota(jnp.int32, (CHUNK, CHUNK), 1)
        causal = ri >= ci
        tri = causal.astype(BF16)
        for c in range(cg):
            rows = pl.ds(c * CHUNK, CHUNK)
            b = _tri_mm(tri, la[c * CHUNK:(c + 1) * CHUNK])
            bl = b[CHUNK - 1:CHUNK, :]
            q = q_ref[rows, :].astype(F32) * scale
            k = k_ref[rows, :].astype(F32)
            qd = (q * jnp.exp(b)).astype(BF16)
            ki = (k * jnp.exp(-b)).astype(BF16)
            ke = (k * jnp.exp(bl - b)).astype(BF16)
            dec = jnp.exp(bl)
            for h in range(HEADS):
                ks = slice(h * HK, (h + 1) * HK)
                vs = pl.ds(h * HV, HV)
                vh = v_ref[rows, vs]
                s_t = state[h]
                st_ref[c, h] = s_t
                att = jnp.where(causal, _nt(qd[:, ks], ki[:, ks]), 0.0).astype(BF16)
                o_ref[rows, vs] = _nn(att, vh) + _nt(qd[:, ks], s_t.astype(BF16))
                state[h] = s_t * dec[:, ks] + _tn(vh, ke[:, ks])

    return pl.pallas_call(
        body, name=name, grid=(n_chunks // cg,),
        in_specs=[pl.BlockSpec((t, DK), lambda i: (i, C_Q // DK)),
                  pl.BlockSpec((t, DK), lambda i: (i, C_K // DK)),
                  pl.BlockSpec((t, DV), lambda i: (i, C_V // DV)),
                  pl.BlockSpec((t, LANES), lambda i: (i, C_GLR // LANES)),
                  pl.BlockSpec((LANES, DK), lambda i: (0, 0)),
                  pl.BlockSpec((1, DK), lambda i: (0, 0))],
        out_specs=[pl.BlockSpec((t, DV), lambda i: (i, 0)),
                   pl.BlockSpec((cg, HEADS, HV, HK), lambda i: (i, 0, 0, 0))],
        out_shape=[jax.ShapeDtypeStruct((m, DV), F32),
                   jax.ShapeDtypeStruct((n_chunks, HEADS, HV, HK), F32)],
        scratch_shapes=[pltpu.VMEM((HEADS, HV, HK), F32)],
        compiler_params=_params(("arbitrary",)))(p, p, p, p, wgk, bgk)


def gla_bwd(p, wgk, bgk, st, do, dp, *, name):
    m = p.shape[0]
    n_chunks = m // CHUNK
    cg = _chunk_group(n_chunks)
    t = cg * CHUNK
    ns = n_chunks // cg
    scale = HK ** -0.5

    def body(q_ref, k_ref, v_ref, glr_ref, wgk_ref, bgk_ref, st_ref, do_ref, dp_in,
             dqkv_ref, dglr_ref, dwgk_ref, dbgk_ref, dstate, dz_buf):
        i = pl.program_id(0)
        blk = ns - 1 - i

        @pl.when(i == 0)
        def _():
            dstate[...] = jnp.zeros_like(dstate)
            dwgk_ref[...] = jnp.zeros_like(dwgk_ref)
            dbgk_ref[...] = jnp.zeros_like(dbgk_ref)

        z, la = _log_decay(glr_ref[...], wgk_ref[...], bgk_ref[...], blk * t, t)
        ri = lax.broadcasted_iota(jnp.int32, (CHUNK, CHUNK), 0)
        ci = lax.broadcasted_iota(jnp.int32, (CHUNK, CHUNK), 1)
        causal = ri >= ci
        tri = causal.astype(BF16)
        tri_u = (ri <= ci).astype(BF16)
        for c in reversed(range(cg)):
            rows = pl.ds(c * CHUNK, CHUNK)
            b = _tri_mm(tri, la[c * CHUNK:(c + 1) * CHUNK])
            bl = b[CHUNK - 1:CHUNK, :]
            eb = jnp.exp(b)
            enb = jnp.exp(-b)
            ebl = jnp.exp(bl - b)
            dec = jnp.exp(bl)
            q = q_ref[rows, :].astype(F32) * scale
            k = k_ref[rows, :].astype(F32)
            qd32 = q * eb
            ki32 = k * enb
            ke32 = k * ebl
            qd = qd32.astype(BF16)
            ki = ki32.astype(BF16)
            ke = ke32.astype(BF16)
            dqd_parts, dki_parts, dke_parts, ddec_parts = [], [], [], []
            for h in range(HEADS):
                ks = slice(h * HK, (h + 1) * HK)
                vs = pl.ds(h * HV, HV)
                vh = v_ref[rows, vs]
                doh = do_ref[rows, vs].astype(BF16)
                s_t = st_ref[c, h]
                ds_t = dstate[h]
                ds_b = ds_t.astype(BF16)
                att = jnp.where(causal, _nt(qd[:, ks], ki[:, ks]), 0.0).astype(BF16)
                datt = jnp.where(causal, _nt(doh, vh), 0.0).astype(BF16)
                dvh = _tn(att, doh) + _nt(ke[:, ks], ds_b)
                dqkv_ref[rows, pl.ds(2 * DK + h * HV, HV)] = dvh.astype(BF16)
                dqd_parts.append(_nn(datt, ki[:, ks]) + _nn(doh, s_t.astype(BF16)))
                dki_parts.append(_tn(datt, qd[:, ks]))
                dke_parts.append(_nn(vh, ds_b))
                ddec_parts.append(jnp.sum(s_t * ds_t, axis=0, keepdims=True))
                dstate[h] = _tn(doh, qd[:, ks]) + ds_t * dec[:, ks]
            dqd = jnp.concatenate(dqd_parts, axis=1)
            dki = jnp.concatenate(dki_parts, axis=1)
            dke = jnp.concatenate(dke_parts, axis=1)
            ddec = jnp.concatenate(ddec_parts, axis=1)
            dqkv_ref[rows, pl.ds(0, DK)] = (dqd * eb * scale).astype(BF16)
            dqkv_ref[rows, pl.ds(DK, DK)] = (dki * enb + dke * ebl).astype(BF16)
            dke_ke = dke * ke32
            db = dqd * qd32 - dki * ki32 - dke_ke
            dbl = jnp.sum(dke_ke, axis=0, keepdims=True) + ddec * dec
            dg = _tri_mm(tri_u, db) + dbl
            row = blk * t + c * CHUNK + lax.broadcasted_iota(jnp.int32, (CHUNK, 1), 0)
            zc = z[c * CHUNK:(c + 1) * CHUNK]
            dz = jnp.where(row >= PAD, dg * (1.0 / TAU) * _sigmoid(-zc), 0.0)
            dz_buf[rows, :] = dz
        dz_all = dz_buf[...]
        dz_b = dz_all.astype(BF16)
        dbgk_ref[...] += jnp.sum(dz_all, axis=0, keepdims=True)
        dglr_ref[...] = _nt(dz_b, wgk_ref[...]).astype(BF16)
        dwgk_ref[...] += _tn(glr_ref[...], dz_b)

    rev = lambda i: ns - 1 - i
    return pl.pallas_call(
        body, name=name, grid=(ns,),
        in_specs=[pl.BlockSpec((t, DK), lambda i: (rev(i), C_Q // DK)),
                  pl.BlockSpec((t, DK), lambda i: (rev(i), C_K // DK)),
                  pl.BlockSpec((t, DV), lambda i: (rev(i), C_V // DV)),
                  pl.BlockSpec((t, LANES), lambda i: (rev(i), C_GLR // LANES)),
                  pl.BlockSpec((LANES, DK), lambda i: (0, 0)),
                  pl.BlockSpec((1, DK), lambda i: (0, 0)),
                  pl.BlockSpec((cg, HEADS, HV, HK), lambda i: (rev(i), 0, 0, 0)),
                  pl.BlockSpec((t, DV), lambda i: (rev(i), 0)), pl.BlockSpec(memory_space=pl.ANY)],
        out_specs=[pl.BlockSpec((t, 2 * DK + DV), lambda i: (rev(i), 0)),
                   pl.BlockSpec((t, LANES), lambda i: (rev(i), 0)),
                   pl.BlockSpec((LANES, DK), lambda i: (0, 0)),
                   pl.BlockSpec((1, DK), lambda i: (0, 0))],
        out_shape=[jax.ShapeDtypeStruct((m, IN_R), BF16),
                   jax.ShapeDtypeStruct((m, LANES), BF16),
                   jax.ShapeDtypeStruct((LANES, DK), F32),
                   jax.ShapeDtypeStruct((1, DK), F32)],
        input_output_aliases={8: 0},
        scratch_shapes=[pltpu.VMEM((HEADS, HV, HK), F32), pltpu.VMEM((t, DK), F32)],
        compiler_params=_params(("arbitrary",)))(p, p, p, p, wgk, bgk, st, do, dp)


def place_glr(dp, dglr, *, name):
    m = dp.shape[0]
    tm = _ew_tile(m, name)

    def body(dp_in, g_ref, o_ref):
        o_ref[...] = g_ref[...]

    return pl.pallas_call(
        body, name=name, grid=(m // tm,),
        in_specs=[pl.BlockSpec(memory_space=pl.ANY), pl.BlockSpec((tm, LANES), lambda i: (i, 0))],
        out_specs=pl.BlockSpec((tm, LANES), lambda i: (i, C_GLR // LANES)),
        out_shape=jax.ShapeDtypeStruct((m, IN_R), BF16), input_output_aliases={0: 0},
        compiler_params=_params(("parallel",)))(dp, dglr)


HALO = 16


def _shift_down(xx, s):
    return pltpu.roll(xx, s, 0)


def _shift_up(xx, s):
    return pltpu.roll(xx, xx.shape[0] - s, 0)


def mix_pre(o, p, gn, *, name):
    m = o.shape[0]
    tm = _ew_tile(m, name)

    def body(o_ref, r_ref, u_ref, gn_ref, ya_ref, pooled_ref, halo):
        i = pl.program_id(0)

        @pl.when(i == 0)
        def _():
            halo[...] = jnp.zeros_like(halo)

        rv = r_ref[...].astype(F32)
        silu_r = rv * _sigmoid(rv)
        for h in range(HEADS):
            cs = pl.ds(h * HV, HV)
            ov = o_ref[:, cs]
            rs = lax.rsqrt(jnp.mean(ov * ov, axis=-1, keepdims=True) + EPS)
            ya_ref[:, cs] = (ov * rs * gn_ref[...] * silu_r[:, h * HV:(h + 1) * HV]).astype(BF16)

        row = i * tm + lax.broadcasted_iota(jnp.int32, (tm, 1), 0)
        pos1 = jnp.maximum(row - PAD + 1, 1).astype(F32)
        for g, w in enumerate(POOL_WINDOWS):
            cs = pl.ds(g * GDIM, GDIM)
            uv = u_ref[:, cs].astype(F32)
            xx = jnp.concatenate([halo[:, cs], uv], axis=0)
            s = xx
            span = 1
            while span < w:
                s = s + _shift_down(s, span)
                span *= 2
            inv = 1.0 / jnp.minimum(pos1, float(w))
            pooled_ref[:, cs] = (s[HALO:] * inv - uv).astype(BF16)
            halo[:, cs] = uv[tm - HALO:]

    blk = pl.BlockSpec((tm, D), lambda i: (i, 0))
    return pl.pallas_call(
        body, name=name, grid=(m // tm,),
        in_specs=[blk, pl.BlockSpec((tm, D), lambda i: (i, C_R // D)),
                  pl.BlockSpec((tm, D), lambda i: (i, C_U // D)),
                  pl.BlockSpec((1, HV), lambda i: (0, 0))],
        out_specs=[blk, blk],
        out_shape=[jax.ShapeDtypeStruct((m, D), BF16)] * 2,
        scratch_shapes=[pltpu.VMEM((HALO, D), F32)],
        compiler_params=_params(("arbitrary",)))(o, p, p, gn)


def mix_pre_bwd(dya, dpooled, o, p, gn, dp, *, name):
    m = o.shape[0]
    tm = _ew_tile(m, name)
    nt = m // tm

    def body(dya_ref, dpl_ref, o_ref, r_ref, gn_ref, dp_in, do_ref, dp_ref, dgn_ref, halo):
        i = pl.program_id(0)
        blk_i = nt - 1 - i

        @pl.when(i == 0)
        def _():
            halo[...] = jnp.zeros_like(halo)
            dgn_ref[...] = jnp.zeros_like(dgn_ref)

        rv = r_ref[...].astype(F32)
        sg = _sigmoid(rv)
        silu_r = rv * sg
        dsilu = sg * (1.0 + rv * (1.0 - sg))
        dgn = jnp.zeros((1, HV), F32)
        for h in range(HEADS):
            cs = pl.ds(h * HV, HV)
            hs = slice(h * HV, (h + 1) * HV)
            ov = o_ref[:, cs]
            dy = dya_ref[:, cs].astype(F32)
            rs = lax.rsqrt(jnp.mean(ov * ov, axis=-1, keepdims=True) + EPS)
            xh = ov * rs
            on = xh * gn_ref[...]
            don = dy * silu_r[:, hs]
            dp_ref[:, cs] = (dy * on * dsilu[:, hs]).astype(BF16)
            dxh = don * gn_ref[...]
            do_ref[:, cs] = rs * (dxh - xh * jnp.mean(dxh * xh, axis=-1, keepdims=True))
            dgn = dgn + jnp.sum(don * xh, axis=0, keepdims=True)
        dgn_ref[...] += dgn

        row = blk_i * tm + lax.broadcasted_iota(jnp.int32, (tm, 1), 0)
        pos1 = jnp.maximum(row - PAD + 1, 1).astype(F32)
        for g, w in enumerate(POOL_WINDOWS):
            cs = pl.ds(g * GDIM, GDIM)
            dpv = dpl_ref[:, cs].astype(F32)
            e = dpv * (1.0 / jnp.minimum(pos1, float(w)))
            xx = jnp.concatenate([e, halo[:, cs]], axis=0)
            s = xx
            span = 1
            while span < w:
                s = s + _shift_up(s, span)
                span *= 2
            dp_ref[:, pl.ds(D + g * GDIM, GDIM)] = (s[:tm] - dpv).astype(BF16)
            halo[:, cs] = e[:HALO]

    rev = lambda i: nt - 1 - i
    blk = pl.BlockSpec((tm, D), lambda i: (rev(i), 0))
    return pl.pallas_call(
        body, name=name, grid=(nt,),
        in_specs=[blk, blk, blk, pl.BlockSpec((tm, D), lambda i: (rev(i), C_R // D)),
                  pl.BlockSpec((1, HV), lambda i: (0, 0)), pl.BlockSpec(memory_space=pl.ANY)],
        out_specs=[blk, pl.BlockSpec((tm, 2 * D), lambda i: (rev(i), C_R // (2 * D))),
                   pl.BlockSpec((1, HV), lambda i: (0, 0))],
        out_shape=[jax.ShapeDtypeStruct((m, D), F32), jax.ShapeDtypeStruct((m, IN_R), BF16),
                   jax.ShapeDtypeStruct((1, HV), F32)],
        input_output_aliases={5: 1},
        scratch_shapes=[pltpu.VMEM((HALO, D), F32)],
        compiler_params=_params(("arbitrary",)))(dya, dpooled, o, p, gn, dp)


def merge_fwd(p, ya, yb, bg, *, name):
    m = ya.shape[0]
    tm = _ew_tile(m, name)

    def body(ga_ref, gb_ref, ya_ref, yb_ref, ba_ref, bb_ref, o_ref):
        gate_a = _sigmoid(ga_ref[...].astype(F32) + ba_ref[...])
        gate_b = _sigmoid(gb_ref[...].astype(F32) + bb_ref[...])
        o_ref[...] = (gate_a * ya_ref[...].astype(F32) + gate_b * yb_ref[...].astype(F32)).astype(BF16)

    blk = pl.BlockSpec((tm, D), lambda i: (i, 0))
    return pl.pallas_call(
        body, name=name, grid=(m // tm,),
        in_specs=[pl.BlockSpec((tm, D), lambda i: (i, C_GA // D)),
                  pl.BlockSpec((tm, D), lambda i: (i, C_GB // D)), blk, blk,
                  pl.BlockSpec((1, D), lambda i: (0, 0)), pl.BlockSpec((1, D), lambda i: (0, 1))],
        out_specs=blk, out_shape=jax.ShapeDtypeStruct((m, D), BF16),
        compiler_params=_params(("parallel",)))(p, p, ya, yb, bg, bg)


def merge_bwd(dmrg, p, ya, yb, bg, *, name):
    m = ya.shape[0]
    tm = _ew_tile(m, name)

    def body(dm_ref, ga_ref, gb_ref, ya_ref, yb_ref, ba_ref, bb_ref,
             dya_ref, dyb_ref, dp_ref, dbg_ref):
        @pl.when(pl.program_id(0) == 0)
        def _():
            dbg_ref[...] = jnp.zeros_like(dbg_ref)

        dm = dm_ref[...].astype(F32)
        gate_a = _sigmoid(ga_ref[...].astype(F32) + ba_ref[...])
        gate_b = _sigmoid(gb_ref[...].astype(F32) + bb_ref[...])
        dya_ref[...] = (dm * gate_a).astype(BF16)
        dyb_ref[...] = (dm * gate_b).astype(BF16)
        dga = dm * ya_ref[...].astype(F32) * gate_a * (1.0 - gate_a)
        dgb = dm * yb_ref[...].astype(F32) * gate_b * (1.0 - gate_b)
        dp_ref[:, pl.ds(0, D)] = dga.astype(BF16)
        dp_ref[:, pl.ds(D, D)] = dgb.astype(BF16)
        dbg_ref[:, pl.ds(0, D)] += jnp.sum(dga, axis=0, keepdims=True)
        dbg_ref[:, pl.ds(D, D)] += jnp.sum(dgb, axis=0, keepdims=True)

    blk = pl.BlockSpec((tm, D), lambda i: (i, 0))
    return pl.pallas_call(
        body, name=name, grid=(m // tm,),
        in_specs=[blk, pl.BlockSpec((tm, D), lambda i: (i, C_GA // D)),
                  pl.BlockSpec((tm, D), lambda i: (i, C_GB // D)), blk, blk,
                  pl.BlockSpec((1, D), lambda i: (0, 0)), pl.BlockSpec((1, D), lambda i: (0, 1))],
        out_specs=[blk, blk, pl.BlockSpec((tm, 2 * D), lambda i: (i, C_GA // (2 * D))),
                   pl.BlockSpec((1, 2 * D), lambda i: (0, 0))],
        out_shape=[jax.ShapeDtypeStruct((m, D), BF16)] * 2 + [jax.ShapeDtypeStruct((m, IN_R), BF16),
                                                              jax.ShapeDtypeStruct((1, 2 * D), F32)],
        compiler_params=_params(("arbitrary",)))(dmrg, p, p, ya, yb, bg, bg)


def scale_bwd(dy1, y0, scale, *, name):
    m = y0.shape[0]
    tm = _ew_tile(m, name)

    def body(dy_ref, y0_ref, s_ref, o_ref, ds_ref):
        @pl.when(pl.program_id(0) == 0)
        def _():
            ds_ref[...] = jnp.zeros_like(ds_ref)

        dy = dy_ref[...].astype(F32)
        o_ref[...] = (dy * s_ref[...]).astype(BF16)
        ds_ref[...] += jnp.sum(dy * y0_ref[...].astype(F32), axis=0, keepdims=True)

    blk = pl.BlockSpec((tm, D), lambda i: (i, 0))
    vec = pl.BlockSpec((1, D), lambda i: (0, 0))
    return pl.pallas_call(
        body, name=name, grid=(m // tm,), in_specs=[blk, blk, vec], out_specs=[blk, vec],
        out_shape=[jax.ShapeDtypeStruct((m, D), BF16), jax.ShapeDtypeStruct((1, D), F32)],
        compiler_params=_params(("arbitrary",)))(dy1, y0, scale)


CONV_BLK = 1408
CONV_ROWS = 688
N_CONV_BLK = D_FF // CONV_BLK


def conv_act_fwd(up, cw, cb, *, name):
    m = up.shape[0]
    tm = _ew_tile(m, name, cap=CONV_ROWS)

    def conv(x_ref, halo, w_ref, b_ref):
        xv = x_ref[...].astype(F32)
        xx = jnp.concatenate([halo[...], xv], axis=0)
        y = (w_ref[2:3, :] * xx + w_ref[1:2, :] * _shift_down(xx, 1)
             + w_ref[0:1, :] * _shift_down(xx, 2))[HALO:] + b_ref[...]
        halo[...] = xv[tm - HALO:]
        return y

    def body(xa_ref, xb_ref, wa_ref, wb_ref, ba_ref, bb_ref, upc_a_ref, upc_b_ref, act_ref, halo_a, halo_b):
        @pl.when(pl.program_id(1) == 0)
        def _():
            halo_a[...] = jnp.zeros_like(halo_a)
            halo_b[...] = jnp.zeros_like(halo_b)

        a = conv(xa_ref, halo_a, wa_ref, ba_ref)
        bv = conv(xb_ref, halo_b, wb_ref, bb_ref)
        upc_a_ref[...] = a.astype(BF16)
        upc_b_ref[...] = bv.astype(BF16)
        act_ref[...] = (a * _sigmoid(a) * bv).astype(BF16)

    nb = N_CONV_BLK
    xa = pl.BlockSpec((tm, CONV_BLK), lambda j, i: (i, j))
    xb = pl.BlockSpec((tm, CONV_BLK), lambda j, i: (i, j + nb))
    return pl.pallas_call(
        body, name=name, grid=(nb, m // tm),
        in_specs=[xa, xb,
                  pl.BlockSpec((3, CONV_BLK), lambda j, i: (0, j)),
                  pl.BlockSpec((3, CONV_BLK), lambda j, i: (0, j + nb)),
                  pl.BlockSpec((1, CONV_BLK), lambda j, i: (0, j)),
                  pl.BlockSpec((1, CONV_BLK), lambda j, i: (0, j + nb))],
        out_specs=[xa, xa, xa],
        out_shape=[jax.ShapeDtypeStruct((m, D_FF), BF16)] * 3,
        scratch_shapes=[pltpu.VMEM((HALO, CONV_BLK), F32)] * 2,
        compiler_params=_params(("parallel", "arbitrary")))(up, up, cw, cw, cb, cb)


def conv_act_bwd(dact, upc_a, upc_b, up, cw, *, name):
    m = up.shape[0]
    tm = _ew_tile(m, name, cap=CONV_ROWS)
    nt = m // tm

    def conv_t(d, halo, x_ref, w_ref, dup_ref, half, dw_ref, db_ref):
        xx = jnp.concatenate([d, halo[...]], axis=0)
        d1 = _shift_up(xx, 1)[:tm]
        d2 = _shift_up(xx, 2)[:tm]
        dup_ref[half] = (w_ref[2:3, :] * d + w_ref[1:2, :] * d1 + w_ref[0:1, :] * d2).astype(BF16)
        xv = x_ref[...].astype(F32)
        dw_ref[2:3, :] += jnp.sum(xv * d, axis=0, keepdims=True)
        dw_ref[1:2, :] += jnp.sum(xv * d1, axis=0, keepdims=True)
        dw_ref[0:1, :] += jnp.sum(xv * d2, axis=0, keepdims=True)
        db_ref[...] += jnp.sum(d, axis=0, keepdims=True)
        halo[...] = d[:HALO]

    def body(da_ref, a_ref, b_ref, xa_ref, xb_ref, wa_ref, wb_ref,
             dup_ref, dwa_ref, dwb_ref, dba_ref, dbb_ref, halo_a, halo_b):
        @pl.when(pl.program_id(1) == 0)
        def _():
            for r in (halo_a, halo_b, dwa_ref, dwb_ref, dba_ref, dbb_ref):
                r[...] = jnp.zeros_like(r)

        dact_v = da_ref[...].astype(F32)
        a = a_ref[...].astype(F32)
        bv = b_ref[...].astype(F32)
        sg = _sigmoid(a)
        d_a = dact_v * bv * sg * (1.0 + a * (1.0 - sg))
        d_b = dact_v * a * sg
        conv_t(d_a, halo_a, xa_ref, wa_ref, dup_ref, 0, dwa_ref, dba_ref)
        conv_t(d_b, halo_b, xb_ref, wb_ref, dup_ref, 1, dwb_ref, dbb_ref)

    nb = N_CONV_BLK
    rev = lambda i: nt - 1 - i
    half = pl.BlockSpec((tm, CONV_BLK), lambda j, i: (rev(i), j))
    xa = half
    xb = pl.BlockSpec((tm, CONV_BLK), lambda j, i: (rev(i), j + nb))
    wa = pl.BlockSpec((3, CONV_BLK), lambda j, i: (0, j))
    wb = pl.BlockSpec((3, CONV_BLK), lambda j, i: (0, j + nb))
    va = pl.BlockSpec((1, CONV_BLK), lambda j, i: (0, j))
    outs = pl.pallas_call(
        body, name=name, grid=(nb, nt),
        in_specs=[half, half, half, xa, xb, wa, wb],
        out_specs=[pl.BlockSpec((2, tm, CONV_BLK), lambda j, i: (0, rev(i), j)), wa, wa, va, va],
        out_shape=[jax.ShapeDtypeStruct((2, m, D_FF), BF16)]
                  + [jax.ShapeDtypeStruct((3, D_FF), F32)] * 2
                  + [jax.ShapeDtypeStruct((1, D_FF), F32)] * 2,
        scratch_shapes=[pltpu.VMEM((HALO, CONV_BLK), F32)] * 2,
        compiler_params=_params(("parallel", "arbitrary")))(dact, upc_a, upc_b, up, up, cw, cw)
    return outs


def local_step(x, target, w):
    seq = x.shape[0]
    h = jnp.concatenate([jnp.zeros((PAD, D), F32), w["meta"], x], axis=0)
    saved = []
    for l in range(DEPTH):
        wl = {k: (v[l:l + 1] if k in ROW_PARAMS else v[l]) for k, v in w.items() if k not in ("meta", "final_norm_g")}
        s = {"h": h}
        fwd_in(s, wl, f"l{l}_")
        fwd_mixer(s, wl, f"l{l}_")
        fwd_ffn(s, wl, f"l{l}_")
        saved.append(s)
        h = s["h3"]

    dh, dh_b, dgf, loss_rows = loss_head(h, w["final_norm_g"], jnp.pad(target, ((X0, 0), (0, 0))), name="loss_head")
    g = {"final_norm_g": dgf}
    per_layer = []
    for l in reversed(range(DEPTH)):
        wl = {k: (v[l:l + 1] if k in ROW_PARAMS else v[l]) for k, v in w.items() if k not in ("meta", "final_norm_g")}
        s = saved[l]
        gl = {}
        dh2, dh2_b = bwd_ffn(dh, dh_b, s, wl, gl, f"l{l}_")
        dp = bwd_mixer(dh2_b, s, wl, gl, f"l{l}_")
        gl["w_in"] = bwd_in_w(dp, s, f"l{l}_")
        dh, dh_b = bwd_in_x(dp, dh2, s, wl, gl, f"l{l}_")
        per_layer.append(gl)
    per_layer.reverse()
    for k in per_layer[0]:
        g[k] = jnp.stack([per_layer[l][k].astype(F32) for l in range(DEPTH)])
    g["meta"] = dh[PAD:X0]
    return loss_rows, dh[X0:X0 + seq], g


ROW_PARAMS = ("norm1_g", "b_gk", "gla_norm_g", "pool_scale", "b_gates", "norm2_g", "conv_b")


def fwd_in(s, w, ln):
    s["hn1"] = rmsnorm_fwd(s["h"], w["norm1_g"], name=ln + "norm1")
    s["p"] = mm_nt(s["hn1"], w["w_in"], tn=896, name=ln + "in_proj")


def fwd_mixer(s, w, ln):
    p = s["p"]
    s["o"], s["st"] = gla_fwd(p, w["w_gk"], w["b_gk"], name=ln + "gla_fwd")
    s["ya_in"], s["pooled"] = mix_pre(s["o"], p, w["gla_norm_g"], name=ln + "mix_pre")
    s["ya"] = mm_nn(s["ya_in"], w["w_a"], name=ln + "proj_a")
    s["yb0"], s["yb1"] = pool_mm_fwd(s["pooled"], w["w_pool"], w["pool_scale"], name=ln + "pool_mm")
    s["yb"] = mm_nn(s["yb1"], w["w_b"], name=ln + "proj_b")
    s["mrg"] = merge_fwd(p, s["ya"], s["yb"], w["b_gates"], name=ln + "merge")
    s["h2"] = mm_nn(s["mrg"], w["w_o"], out_dtype=F32, res=s["h"], name=ln + "proj_o")


def fwd_ffn(s, w, ln):
    s["hn2"] = rmsnorm_fwd(s["h2"], w["norm2_g"], name=ln + "norm2")
    s["up"] = mm_nt(s["hn2"], w["w_up"], tn=1408, name=ln + "up_proj")
    s["upc_a"], s["upc_b"], s["act"] = conv_act_fwd(s["up"], w["conv_w"], w["conv_b"], name=ln + "conv_act")
    s["h3"] = mm_nn(s["act"], w["w_down"], out_dtype=F32, res=s["h2"], name=ln + "down_proj")


def bwd_ffn(dh, dh_b, s, w, g, ln, after=None):
    dact = mm_nt(dh_b, w["w_down"], tn=1408, after=after, name=ln + "d_act")
    g["w_down"] = mm_tn(s["act"], dh_b, tk1=1408, out_dtype=BF16, after=after, name=ln + "dw_down")
    dup, dcw_a, dcw_b, dcb_a, dcb_b = conv_act_bwd(
        dact, s["upc_a"], s["upc_b"], s["up"], w["conv_w"], name=ln + "conv_act_bwd")
    dhn2 = mm_nn(dup, w["w_up"], out_dtype=F32, tn=512, halves=True, name=ln + "d_hn2")
    g["w_up"] = mm_tn(dup, s["hn2"], tk1=1408, out_dtype=BF16, halves=True, name=ln + "dw_up")
    dh2, dh2_b, g["norm2_g"] = rmsnorm_bwd(dhn2, s["h2"], w["norm2_g"], dh, name=ln + "norm2_bwd")
    g["conv_w"] = jnp.concatenate([dcw_a, dcw_b], axis=1)
    g["conv_b"] = jnp.concatenate([dcb_a, dcb_b], axis=1)
    return dh2, dh2_b


def bwd_mixer(dh2_b, s, w, g, ln, after=None):
    dmrg = mm_nt(dh2_b, w["w_o"], after=after, name=ln + "d_mrg")
    g["w_o"] = mm_tn(s["mrg"], dh2_b, out_dtype=BF16, after=after, name=ln + "dw_o")
    dya, dyb, dp, g["b_gates"] = merge_bwd(dmrg, s["p"], s["ya"], s["yb"], w["b_gates"], name=ln + "merge_bwd")
    dya_in = mm_nt(dya, w["w_a"], name=ln + "d_ya_in")
    g["w_a"] = mm_tn(s["ya_in"], dya, out_dtype=BF16, name=ln + "dw_a")
    dyb1 = mm_nt(dyb, w["w_b"], name=ln + "d_yb1")
    g["w_b"] = mm_tn(s["yb1"], dyb, out_dtype=BF16, name=ln + "dw_b")
    dyb0, g["pool_scale"] = scale_bwd(dyb1, s["yb0"], w["pool_scale"], name=ln + "scale_bwd")
    dpooled = pool_mm_bwd_x(dyb0, w["w_pool"], name=ln + "d_pooled")
    g["w_pool"] = pool_mm_bwd_w(s["pooled"], dyb0, name=ln + "dw_pool")
    do, dp, g["gla_norm_g"] = mix_pre_bwd(dya_in, dpooled, s["o"], s["p"], w["gla_norm_g"], dp,
                                          name=ln + "mix_pre_bwd")
    dp, dglr, g["w_gk"], g["b_gk"] = gla_bwd(s["p"], w["w_gk"], w["b_gk"], s["st"], do, dp, name=ln + "gla_bwd")
    return place_glr(dp, dglr, name=ln + "place_glr")


def bwd_in_w(dp, s, ln):
    return mm_tn(dp, s["hn1"], tk1=896, out_dtype=BF16, name=ln + "dw_in")


def bwd_in_x(dp, dh2, s, w, g, ln, after=None):
    dhn1 = mm_nn(dp, w["w_in"], out_dtype=F32, tn=512, after=after, name=ln + "d_hn1")
    dh, dh_b, g["norm1_g"] = rmsnorm_bwd(dhn1, s["h"], w["norm1_g"], dh2, name=ln + "norm1_bwd")
    return dh, dh_b


def _my_place():
    return lax.axis_index("x"), lax.axis_index("y"), lax.axis_index("c")


def _peer(place, k):
    x, y, c = place
    return (1 - x if k & 4 else x, 1 - y if k & 2 else y, 1 - c if k & 1 else c)


def _index(place):
    x, y, c = place
    return 4 * x + 2 * y + c


def exchange(arrays, kinds, *, name):
    n = len(arrays)

    def body(*refs):
        ins, outs = refs[:n], refs[n:2 * n]
        send_sems, recv_sems, local_sems = refs[2 * n:]
        place = _my_place()
        me = _index(place)

        def src(a, dest):
            return ins[a] if kinds[a] == "gather" else ins[a].at[dest]

        def remote(a, k):
            peer = _peer(place, k)
            return pltpu.make_async_remote_copy(
                src_ref=src(a, _index(peer)), dst_ref=outs[a].at[me],
                send_sem=send_sems.at[a, k - 1], recv_sem=recv_sems.at[a, k - 1],
                device_id=peer, device_id_type=pl.DeviceIdType.MESH)

        def arrival(a, k):
            peer = _peer(place, k)
            return pltpu.make_async_remote_copy(
                src_ref=src(a, me), dst_ref=outs[a].at[_index(peer)],
                send_sem=send_sems.at[a, k - 1], recv_sem=recv_sems.at[a, k - 1],
                device_id=peer, device_id_type=pl.DeviceIdType.MESH)

        own = [pltpu.make_async_copy(src(a, me), outs[a].at[me], local_sems.at[a]) for a in range(n)]
        sends = [remote(a, k) for k in range(1, N_DEV) for a in range(n)]
        for cp in sends:
            cp.start()
        for cp in own:
            cp.start()
        for k in range(1, N_DEV):
            for a in range(n):
                arrival(a, k).wait_recv()
        for cp in sends:
            cp.wait_send()
        for cp in own:
            cp.wait()

    any_spec = pl.BlockSpec(memory_space=pl.ANY)
    out_shape = []
    for arr, kind in zip(arrays, kinds):
        shape = arr.shape if kind == "gather" else arr.shape[1:]
        out_shape.append(jax.ShapeDtypeStruct((N_DEV,) + tuple(shape), arr.dtype))
    return pl.pallas_call(
        body, name=name, in_specs=[any_spec] * n, out_specs=[any_spec] * n, out_shape=out_shape,
        scratch_shapes=[pltpu.SemaphoreType.DMA((n, N_DEV - 1)), pltpu.SemaphoreType.DMA((n, N_DEV - 1)),
                        pltpu.SemaphoreType.DMA((n,))],
    )(*arrays)


def _sem_slot(a, k):
    return a * (N_DEV - 1) + k - 1


_HBM = pl.BlockSpec(memory_space=pltpu.HBM)
_SEM = pl.BlockSpec(memory_space=pltpu.SEMAPHORE)
_DATAFLOW = pltpu.SideEffectType.DATAFLOW_SIDE_EFFECTING


def exchange_start(arrays, kinds, after, *, name):
    n = len(arrays)
    zones = []
    for arr, kind in zip(arrays, kinds):
        shape = arr.shape if kind == "gather" else arr.shape[1:]
        zones.append(lax.empty((N_DEV,) + tuple(shape), arr.dtype))

    def body(*refs):
        ins, lands = refs[:n], refs[n:2 * n]
        send_sems, recv_sems = refs[2 * n + 1], refs[2 * n + 2]
        token = refs[4 * n + 3]
        place = _my_place()
        me = _index(place)
        for a in range(n):
            for k in range(1, N_DEV):
                peer = _peer(place, k)
                pltpu.make_async_remote_copy(
                    src_ref=ins[a] if kinds[a] == "gather" else ins[a].at[_index(peer)], dst_ref=lands[a].at[me],
                    send_sem=send_sems.at[_sem_slot(a, k)], recv_sem=recv_sems.at[_sem_slot(a, k)],
                    device_id=peer, device_id_type=pl.DeviceIdType.MESH).start()
        token[...] = jnp.zeros_like(token)

    sems = pltpu.SemaphoreType.DMA((n * (N_DEV - 1),))
    hbm = lambda a: pltpu.HBM(a.shape, a.dtype)
    outs = pl.pallas_call(
        body, name=name,
        out_shape=(sems, sems, *[hbm(a) for a in arrays], *[hbm(z) for z in zones],
                   jax.ShapeDtypeStruct((8, LANES), F32)),
        in_specs=[_HBM] * (2 * n) + [pl.BlockSpec(memory_space=pl.ANY)],
        out_specs=(_SEM, _SEM, *[_HBM] * (2 * n), pl.BlockSpec(memory_space=pltpu.VMEM)),
        input_output_aliases={i: 2 + i for i in range(2 * n)},
        compiler_params=pltpu.CompilerParams(has_side_effects=_DATAFLOW),
    )(*[pltpu.with_memory_space_constraint(a, pltpu.HBM) for a in arrays],
      *[pltpu.with_memory_space_constraint(z, pltpu.HBM) for z in zones], after)
    return dict(send=outs[0], recv=outs[1], srcs=outs[2:2 + n], zones=outs[2 + n:2 + 2 * n],
                token=outs[2 + 2 * n], kinds=kinds)


def exchange_wait(handle, after, *, name):
    kinds = handle["kinds"]
    n = len(kinds)

    def body(*refs):
        ins, lands = refs[:n], refs[n:2 * n]
        send_sems, recv_sems = refs[2 * n], refs[2 * n + 1]
        place = _my_place()
        me = _index(place)
        for a in range(n):
            for k in range(1, N_DEV):
                peer = _peer(place, k)
                src = ins[a] if kinds[a] == "gather" else ins[a].at[_index(peer)]
                copy = pltpu.make_async_remote_copy(
                    src_ref=src, dst_ref=lands[a].at[_index(peer)],
                    send_sem=send_sems.at[_sem_slot(a, k)], recv_sem=recv_sems.at[_sem_slot(a, k)],
                    device_id=peer, device_id_type=pl.DeviceIdType.MESH)
                copy.wait_send()
                copy.wait_recv()

    srcs, zones = handle["srcs"], handle["zones"]
    after = after if isinstance(after, tuple) else (after,)
    hbm = lambda a: pltpu.HBM(a.shape, a.dtype)
    outs = pl.pallas_call(
        body, name=name,
        out_shape=(*[hbm(a) for a in srcs], *[hbm(z) for z in zones]),
        in_specs=[_HBM] * (2 * n) + [_SEM, _SEM] + [pl.BlockSpec(memory_space=pl.ANY)] * len(after),
        out_specs=[_HBM] * (2 * n),
        input_output_aliases={i: i for i in range(2 * n)},
        compiler_params=pltpu.CompilerParams(has_side_effects=_DATAFLOW),
    )(*srcs, *zones, handle["send"], handle["recv"], *after)
    return _fill_own(outs[:n], outs[n:], kinds)


def _fill_own(srcs, zones, kinds):
    me = _index(_my_place())
    filled = []
    for src, zone, kind in zip(srcs, zones, kinds):
        mine = src if kind == "gather" else lax.dynamic_index_in_dim(src, me, 0, keepdims=False)
        filled.append(lax.dynamic_update_index_in_dim(zone, mine, me, 0))
    return filled


ADAM_COLS = 256


def reduce_adam_layer(parts, w, m, v, layer, prev, *, name):
    _, r, c = w.shape
    tc = ADAM_COLS

    def body(*refs):
        p_ref, w_ref, m_ref, v_ref = refs[:4]
        g_ref, d_ref, m2_ref, v2_ref = refs[-4:]
        g = p_ref[0].astype(F32)
        for i in range(1, N_DEV):
            g = g + p_ref[i].astype(F32)
        m2 = B1 * m_ref[...] + (1.0 - B1) * g
        v2 = B2 * v_ref[...] + (1.0 - B2) * (g * g)
        m_hat = m2 / (1.0 - B1 ** STEP)
        v_hat = v2 / (1.0 - B2 ** STEP)
        g_ref[...] = g
        d_ref[...] = -LR * (m_hat / (jnp.sqrt(v_hat) + ADAM_EPS) + WD * w_ref[...])
        m2_ref[...] = m2
        v2_ref[...] = v2

    blk = pl.BlockSpec((None, r, tc), lambda i: (layer, 0, i))
    in_specs = [pl.BlockSpec((N_DEV, r, tc), lambda i: (0, 0, i)), blk, blk, blk]
    args = [parts, w, m, v]
    aliases = {}
    if prev is not None:
        in_specs += [pl.BlockSpec(memory_space=pl.ANY)] * 4
        args += list(prev)
        aliases = {4 + j: j for j in range(4)}
    return pl.pallas_call(
        body, name=name, grid=(c // tc,), in_specs=in_specs, out_specs=[blk] * 4,
        out_shape=[jax.ShapeDtypeStruct(w.shape, F32)] * 4, input_output_aliases=aliases,
        compiler_params=_params(("parallel",)))(*args)


def reduce_adam(parts, w, m, v, *, name):
    r, c = w.shape
    tr = _pick(r, (256, 352, 192, 128, 72, 64, 32, 16, 8))

    def body(p_ref, w_ref, m_ref, v_ref, g_ref, d_ref, m2_ref, v2_ref):
        g = p_ref[0].astype(F32)
        for i in range(1, N_DEV):
            g = g + p_ref[i].astype(F32)
        wv = w_ref[...]
        m2 = B1 * m_ref[...] + (1.0 - B1) * g
        v2 = B2 * v_ref[...] + (1.0 - B2) * (g * g)
        m_hat = m2 / (1.0 - B1 ** STEP)
        v_hat = v2 / (1.0 - B2 ** STEP)
        g_ref[...] = g
        d_ref[...] = -LR * (m_hat / (jnp.sqrt(v_hat) + ADAM_EPS) + WD * wv)
        m2_ref[...] = m2
        v2_ref[...] = v2

    blk = pl.BlockSpec((tr, c), lambda i: (i, 0))
    return pl.pallas_call(
        body, name=name, grid=(r // tr,),
        in_specs=[pl.BlockSpec((N_DEV, tr, c), lambda i: (0, i, 0)), blk, blk, blk],
        out_specs=[blk] * 4, out_shape=[jax.ShapeDtypeStruct((r, c), F32)] * 4,
        compiler_params=_params(("parallel",)))(parts, w, m, v)


BIG = ("w_in", "w_a", "w_pool_grp", "w_b", "w_o", "w_up", "w_down")
SHARDED_SMALL = ("meta_tokens", "w_gk", "conv_w")
REPLICATED = ("norm1_g", "b_gk", "gla_norm_g", "pool_scale", "b_gates", "norm2_g", "conv_b", "final_norm_g")
CUT_AXIS = {"w_in": 2, "w_a": 1, "w_pool_grp": 2, "w_b": 1, "w_o": 1, "w_up": 2, "w_down": 1,
            "meta_tokens": 1, "w_gk": 2, "conv_w": 2}
WEIGHTS = ("meta_tokens", "norm1_g", "w_in", "w_gk", "b_gk", "gla_norm_g", "w_a", "w_pool_grp", "pool_scale",
           "w_b", "b_gates", "w_o", "norm2_g", "w_up", "conv_w", "conv_b", "w_down", "final_norm_g")


def _as_2d(a):
    return a.reshape(-1, a.shape[-1])


def _from_slots(slots, axis):
    full = jnp.moveaxis(slots, 0, axis)
    shape = list(full.shape)
    shape[axis:axis + 2] = [shape[axis] * shape[axis + 1]]
    return full.reshape(shape)


def _to_slots(full, axis):
    shape = list(full.shape)
    shape[axis:axis + 1] = [N_DEV, shape[axis] // N_DEV]
    return jnp.moveaxis(full.reshape(shape), axis, 0)


def _pack(vectors, rows):
    flat = jnp.concatenate([v.reshape(-1).astype(F32) for v in vectors])
    return jnp.pad(flat, (0, rows * LANES - flat.shape[0])).reshape(rows, LANES)


def _unpack(packed, shapes):
    flat = packed.reshape(-1)
    out, off = [], 0
    for s in shapes:
        size = 1
        for d in s:
            size *= d
        out.append(flat[off:off + size].reshape(s))
        off += size
    return out


def _rows_for(shapes, mult=8):
    total = 0
    for s in shapes:
        size = 1
        for d in s:
            size *= d
        total += size
    rows = -(-total // LANES)
    return -(-rows // mult) * mult


def _permute_rows(w_t):
    pad = jnp.zeros((IN_R - IN_WIDTH,) + w_t.shape[1:], w_t.dtype)
    return jnp.concatenate([w_t[:2048], w_t[2064:], w_t[2048:2064], pad], axis=0)


def _unpermute_rows(w_r):
    return jnp.concatenate([w_r[:2048], w_r[C_GLR:C_GLR + RANK], w_r[2048:C_GLR]], axis=0)


def kernel(x, meta_tokens, norm1_g, w_in, w_gk, b_gk, gla_norm_g, w_a, w_pool_grp, pool_scale, w_b, b_gates, w_o, norm2_g, w_up, conv_w, conv_b, w_down, final_norm_g, loss_target, m_meta_tokens, m_norm1_g, m_w_in, m_w_gk, m_b_gk, m_gla_norm_g, m_w_a, m_w_pool_grp, m_pool_scale, m_w_b, m_b_gates, m_w_o, m_norm2_g, m_w_up, m_conv_w, m_conv_b, m_w_down, m_final_norm_g, v_meta_tokens, v_norm1_g, v_w_in, v_w_gk, v_b_gk, v_gla_norm_g, v_w_a, v_w_pool_grp, v_pool_scale, v_w_b, v_b_gates, v_w_o, v_norm2_g, v_w_up, v_conv_w, v_conv_b, v_w_down, v_final_norm_g):
    wts = dict(meta_tokens=meta_tokens, norm1_g=norm1_g, w_in=w_in, w_gk=w_gk, b_gk=b_gk, gla_norm_g=gla_norm_g,
               w_a=w_a, w_pool_grp=w_pool_grp, pool_scale=pool_scale, w_b=w_b, b_gates=b_gates, w_o=w_o,
               norm2_g=norm2_g, w_up=w_up, conv_w=conv_w, conv_b=conv_b, w_down=w_down, final_norm_g=final_norm_g)
    mom = dict(meta_tokens=m_meta_tokens, norm1_g=m_norm1_g, w_in=m_w_in, w_gk=m_w_gk, b_gk=m_b_gk,
               gla_norm_g=m_gla_norm_g, w_a=m_w_a, w_pool_grp=m_w_pool_grp, pool_scale=m_pool_scale, w_b=m_w_b,
               b_gates=m_b_gates, w_o=m_w_o, norm2_g=m_norm2_g, w_up=m_w_up, conv_w=m_conv_w, conv_b=m_conv_b,
               w_down=m_w_down, final_norm_g=m_final_norm_g)
    var = dict(meta_tokens=v_meta_tokens, norm1_g=v_norm1_g, w_in=v_w_in, w_gk=v_w_gk, b_gk=v_b_gk,
               gla_norm_g=v_gla_norm_g, w_a=v_w_a, w_pool_grp=v_w_pool_grp, pool_scale=v_pool_scale, w_b=v_w_b,
               b_gates=v_b_gates, w_o=v_w_o, norm2_g=v_norm2_g, w_up=v_w_up, conv_w=v_conv_w, conv_b=v_conv_b,
               w_down=v_w_down, final_norm_g=v_final_norm_g)

    small_shapes = [wts[n].shape for n in SHARDED_SMALL]
    small_rows = _rows_for(small_shapes)

    transposed = ("w_in", "w_up")

    def shard3(n, a):
        if n in transposed:
            a = jnp.swapaxes(a, 1, 2)
        return a.reshape(DEPTH, -1, a.shape[-1])

    def unshard3(n, a):
        a = jnp.swapaxes(a, 1, 2) if n in transposed else a
        return a.reshape(wts[n].shape)

    def layer_shards(l, names):
        return [shard3(n, wts[n])[l].astype(BF16) for n in names]

    def tie(row, handle):
        return row + handle["token"][0:1, 0:1]

    def full_weight(n, zone):
        if n == "w_pool_grp":
            return jnp.moveaxis(zone.reshape(N_DEV, GROUPS, GDIM // N_DEV, GDIM), 0, 1).reshape(GROUPS, GDIM, GDIM)
        full = zone.reshape(-1, zone.shape[-1])
        return _permute_rows(full) if n == "w_in" else full

    groups = [("w_in",), ("w_a", "w_pool_grp", "w_b", "w_o"), ("w_up", "w_down")]
    rest = groups[0] + groups[1]
    key = {"w_pool_grp": "w_pool"}
    rows = dict(norm1_g=norm1_g, b_gk=b_gk, gla_norm_g=gla_norm_g, pool_scale=pool_scale, b_gates=b_gates,
                norm2_g=norm2_g, conv_b=conv_b)

    def gather(l, names, after, name, head=()):
        return exchange_start(list(head) + layer_shards(l, names), ["gather"] * (len(head) + len(names)), after,
                              name=name + "_start")

    def landed(handle, after, name, names, w_layer):
        zones = exchange_wait(handle, after, name=name + "_wait")
        for n, z in zip(names, zones[len(zones) - len(names):]):
            w_layer[key.get(n, n)] = full_weight(n, z)
        return zones

    wl = [{n: v[l:l + 1] for n, v in rows.items()} for l in range(DEPTH)]
    g_in0 = gather(0, groups[0], x, "gather_in0", head=[_pack([wts[n] for n in SHARDED_SMALL], small_rows)])
    target = jnp.pad(loss_target[0], ((X0, 0), (0, 0)))
    zones = landed(g_in0, (g_in0["token"], target), "gather_in0", groups[0], wl[0])
    small_slots = [jnp.stack(parts) for parts in zip(*[_unpack(zones[0][i], small_shapes) for i in range(N_DEV)])]
    small_full = {n: _from_slots(slots, CUT_AXIS[n]) for n, slots in zip(SHARDED_SMALL, small_slots)}
    w_gk_pad = jnp.pad(small_full["w_gk"], ((0, 0), (0, LANES - RANK), (0, 0))).astype(BF16)
    for l in range(DEPTH):
        wl[l]["w_gk"] = w_gk_pad[l]
        wl[l]["conv_w"] = small_full["conv_w"][l]
    g_mix0 = gather(0, groups[1], zones[1], "gather_mix0")
    g_ffn0 = gather(0, groups[2], g_mix0["token"], "gather_ffn0")
    wl[0]["norm1_g"] = tie(wl[0]["norm1_g"], g_ffn0)

    h = jnp.concatenate([jnp.zeros((PAD, D), F32), small_full["meta_tokens"], x[0]], axis=0)
    s0 = {"h": h}
    fwd_in(s0, wl[0], "l0_")
    zones = landed(g_mix0, s0["p"], "gather_mix0", groups[1], wl[0])
    g_in1 = gather(1, groups[0], zones[0], "gather_in1")
    wl[0]["b_gk"] = tie(wl[0]["b_gk"], g_in1)
    fwd_mixer(s0, wl[0], "l0_")
    zones = landed(g_ffn0, s0["h2"], "gather_ffn0", groups[2], wl[0])
    g_mix1 = gather(1, groups[1], zones[0], "gather_mix1")
    g_ffn1 = gather(1, groups[2], g_mix1["token"], "gather_ffn1")
    wl[0]["norm2_g"] = tie(wl[0]["norm2_g"], g_ffn1)
    fwd_ffn(s0, wl[0], "l0_")
    landed(g_in1, s0["h3"], "gather_in1", groups[0], wl[1])
    s1 = {"h": s0["h3"]}
    fwd_in(s1, wl[1], "l1_")
    landed(g_mix1, s1["p"], "gather_mix1", groups[1], wl[1])
    fwd_mixer(s1, wl[1], "l1_")
    landed(g_ffn1, s1["h2"], "gather_ffn1", groups[2], wl[1])
    fwd_ffn(s1, wl[1], "l1_")
    dh, dh_b, dgf, loss_rows = loss_head(s1["h3"], final_norm_g[None], target, name="loss_head")
    loss_part = 0.5 * jnp.sum(loss_rows) / D

    def blocks(n, gw):
        if n == "w_in":
            gw = _unpermute_rows(gw)
        if n == "w_pool_grp":
            gw = gw.astype(BF16).reshape(GROUPS, N_DEV, GDIM // N_DEV, GDIM)
            return jnp.moveaxis(gw, 1, 0).reshape(N_DEV, GROUPS * GDIM // N_DEV, GDIM)
        return gw.reshape(N_DEV, gw.shape[0] // N_DEV, gw.shape[1])

    def scatter(g, names, after, name):
        return exchange_start([blocks(n, g[key.get(n, n)]) for n in names], ["scatter"] * len(names), after,
                              name=name + "_start")

    g1, g0 = {}, {}
    dh2, dh2_b = bwd_ffn(dh, dh_b, s1, wl[1], g1, "l1_")
    s_ffn1 = scatter(g1, groups[2], dh2, "scatter_ffn1")
    dp = bwd_mixer(dh2_b, s1, wl[1], g1, "l1_", after=s_ffn1["token"])
    g1["w_in"] = bwd_in_w(dp, s1, "l1_")
    s_rest1 = scatter(g1, rest, s_ffn1["token"], "scatter_rest1")
    dh, dh_b = bwd_in_x(dp, dh2, s1, wl[1], g1, "l1_", after=s_rest1["token"])
    dh2, dh2_b = bwd_ffn(dh, dh_b, s0, wl[0], g0, "l0_")
    r_ffn1 = exchange_wait(s_ffn1, dh2, name="scatter_ffn1_wait")
    s_ffn0 = scatter(g0, groups[2], r_ffn1[0], "scatter_ffn0")
    dp = bwd_mixer(dh2_b, s0, wl[0], g0, "l0_", after=s_ffn0["token"])
    r_rest1 = exchange_wait(s_rest1, dp, name="scatter_rest1_wait")
    g0["w_in"] = bwd_in_w(dp, s0, "l0_")
    s_rest0 = scatter(g0, rest, r_rest1[0], "scatter_rest0")
    dh, _ = bwd_in_x(dp, dh2, s0, wl[0], g0, "l0_", after=s_rest0["token"])
    r_ffn0 = exchange_wait(s_ffn0, dh, name="scatter_ffn0_wait")
    r_rest0 = exchange_wait(s_rest0, r_ffn0[0], name="scatter_rest0_wait")
    grad_x = dh[X0:]
    recv = [dict(zip(groups[2] + rest, list(r_ffn0) + list(r_rest0))),
            dict(zip(groups[2] + rest, list(r_ffn1) + list(r_rest1)))]

    grads, delta, new_m, new_v = {}, {}, {}, {}
    for n in BIG:
        w3, m3, v3 = shard3(n, wts[n]), shard3(n, mom[n]), shard3(n, var[n])
        first = reduce_adam_layer(recv[1][n], w3, m3, v3, 1, None, name="adam_l1_" + n)
        outs = reduce_adam_layer(recv[0][n], w3, m3, v3, 0, first, name="adam_l0_" + n)
        grads[n], delta[n], new_m[n], new_v[n] = [unshard3(n, o) for o in outs]

    g_full = {n: jnp.stack([g0[n], g1[n]])[:, 0] for n in rows}
    g_full["final_norm_g"] = dgf[0]
    g_full["meta_tokens"] = dh[PAD:X0]
    g_full["w_gk"] = jnp.stack([g0["w_gk"], g1["w_gk"]])[:, :RANK]
    g_full["conv_w"] = jnp.stack([g0["conv_w"], g1["conv_w"]])
    rep_shapes = [wts[n].shape for n in REPLICATED] + [(1,)]
    rep_rows = _rows_for(rep_shapes)
    small_blocks = jnp.stack([
        _pack([_to_slots(g_full[n], CUT_AXIS[n])[i] for n in SHARDED_SMALL], small_rows) for i in range(N_DEV)])
    rep_pack = _pack([g_full[n] for n in REPLICATED] + [loss_part.reshape(1)], rep_rows)
    received = exchange([small_blocks, rep_pack], ["scatter", "gather"], name="exchange_small")
    outs = reduce_adam(received[-2], _pack([wts[n] for n in SHARDED_SMALL], small_rows),
                       _pack([mom[n] for n in SHARDED_SMALL], small_rows),
                       _pack([var[n] for n in SHARDED_SMALL], small_rows), name="adam_small")
    for d, o in zip((grads, delta, new_m, new_v), outs):
        for n, a in zip(SHARDED_SMALL, _unpack(o, small_shapes)):
            d[n] = a
    one = [jnp.zeros((1,), F32)]
    outs = reduce_adam(received[-1], _pack([wts[n] for n in REPLICATED] + one, rep_rows),
                       _pack([mom[n] for n in REPLICATED] + one, rep_rows),
                       _pack([var[n] for n in REPLICATED] + one, rep_rows), name="adam_replicated")
    for d, o in zip((grads, delta, new_m, new_v), outs):
        for n, a in zip(REPLICATED + ("loss",), _unpack(o, rep_shapes)):
            d[n] = a
    loss = grads["loss"][0]
    return (loss, grad_x[None], *[grads[n] for n in WEIGHTS], *[delta[n] for n in WEIGHTS],
            *[new_m[n] for n in WEIGHTS], *[new_v[n] for n in WEIGHTS])
```

```python
import functools

import jax
import jax.numpy as jnp
from jax import lax
from jax.experimental import pallas as pl
from jax.experimental.pallas import tpu as pltpu

F32 = jnp.float32
BF16 = jnp.bfloat16

D = 1024
DEPTH = 2
N_META = 16
HEADS = 4
DK = 512
DV = 1024
HK = 128
HV = 256
RANK = 16
TAU = 16.0
CHUNK = 64
POOL_WINDOWS = (2, 4, 8, 16)
GROUPS = 4
GDIM = 256
D_FF = 2816
F2 = 2 * D_FF
EPS = 1e-6
IN_WIDTH = 6160
LR, B1, B2, ADAM_EPS, WD, STEP = 0.001, 0.9, 0.999, 1e-8, 0.01, 10

N_DEV = 8
PAD = CHUNK - N_META
X0 = CHUNK
IN_R = 6272
C_Q, C_K, C_V, C_R, C_U, C_GA, C_GB, C_GLR = 0, 512, 1024, 2048, 3072, 4096, 5120, 6144
VMEM_LIMIT = 56 * 1024 * 1024
LANES = 128


def _params(sem=None):
    return pltpu.CompilerParams(dimension_semantics=sem, vmem_limit_bytes=VMEM_LIMIT)


def _pick(n, prefs):
    for t in prefs:
        if n % t == 0:
            return t
    raise ValueError(f"no tile for {n} in {prefs}")


MM_VMEM_BUDGET = 44 * 1024 * 1024
MM_TILES = {"l0_": (2752, 1376, 688, 192, 128, 64), "l1_": (2752, 1376, 688, 192, 128, 64)}
EW_TILES = {"l0_": (688, 192, 128, 64), "l1_": (688, 192, 128, 64)}


def _row_tile(lp, name="", row_bytes=0, fixed_bytes=0):
    for t in MM_TILES.get(name[:3], MM_TILES["l0_"]):
        if lp % t == 0 and (t * row_bytes + fixed_bytes <= MM_VMEM_BUDGET or t <= 688):
            return t
    raise ValueError(f"no row tile for {lp}")


def _ew_tile(lp, name="", cap=None):
    return _pick(lp, [t for t in EW_TILES.get(name[:3], EW_TILES["l0_"]) if cap is None or t <= cap])


def _sigmoid(x):
    return 1.0 / (1.0 + jnp.exp(-x))


def _dot(a, b, dims):
    return lax.dot_general(a, b, (dims, ((), ())), preferred_element_type=F32)


def _nn(a, b):
    return _dot(a, b, ((1,), (0,)))


def _nt(a, b):
    return _dot(a, b, ((1,), (1,)))


def _tn(a, b):
    return _dot(a, b, ((0,), (0,)))


def mm_nn(a, b, *, out_dtype=BF16, tn=None, res=None, after=None, halves=False, name):
    m, k = (a.shape[1], 2 * a.shape[2]) if halves else a.shape
    n = b.shape[1]
    tn = tn or n
    has_res = res is not None
    out_bytes = jnp.dtype(out_dtype).itemsize
    tm = _row_tile(m, name, 4 * k + 2 * tn * out_bytes + (8 * tn if has_res else 0), 4 * k * tn)
    extra = [] if after is None else [after]

    def body(*refs):
        a_ref, b_ref = refs[:2]
        o_ref = refs[-1]
        if has_res:
            r_ref = refs[2]
        if halves:
            acc = _nn(a_ref[0], b_ref[pl.ds(0, k // 2), :]) + _nn(a_ref[1], b_ref[pl.ds(k // 2, k // 2), :])
        else:
            acc = _nn(a_ref[...], b_ref[...])
        if has_res:
            row = pl.program_id(1) * tm + lax.broadcasted_iota(jnp.int32, (tm, 1), 0)
            acc = jnp.where(row >= PAD, acc + r_ref[...], 0.0)
        o_ref[...] = acc.astype(o_ref.dtype)

    a_spec = (pl.BlockSpec((2, tm, k // 2), lambda j, i: (0, i, 0)) if halves
              else pl.BlockSpec((tm, k), lambda j, i: (i, 0)))
    in_specs = [a_spec, pl.BlockSpec((k, tn), lambda j, i: (0, j))]
    args = [a, b]
    if has_res:
        in_specs.append(pl.BlockSpec((tm, tn), lambda j, i: (i, j)))
        args.append(res)
    in_specs += [pl.BlockSpec(memory_space=pl.ANY)] * len(extra)
    args += extra
    return pl.pallas_call(
        body, name=name, grid=(n // tn, m // tm), in_specs=in_specs,
        out_specs=pl.BlockSpec((tm, tn), lambda j, i: (i, j)),
        out_shape=jax.ShapeDtypeStruct((m, n), out_dtype),
        compiler_params=_params(("parallel", "parallel")))(*args)


def mm_nt(a, b, *, out_dtype=BF16, tn=None, tk=None, after=None, halves=False, name):
    m, k = (a.shape[1], 2 * a.shape[2]) if halves else a.shape
    n = b.shape[0]
    tn = tn or n
    tk = tk or k
    nk = k // tk
    tm = _row_tile(m, name, 4 * tk + 2 * tn * jnp.dtype(out_dtype).itemsize + (4 * tn if nk > 1 else 0), 4 * tn * tk)
    extra = [] if after is None else [after]
    if halves:
        per = nk // 2
        a_spec = pl.BlockSpec((None, tm, tk), lambda j, i, kk: (kk // per, i, kk % per))
    else:
        a_spec = pl.BlockSpec((tm, tk), lambda j, i, kk: (i, kk))

    def body(a_ref, b_ref, *rest):
        o_ref, acc_ref = rest[-2:]
        kk = pl.program_id(2)
        part = _nt(a_ref[...], b_ref[...])
        if nk == 1:
            o_ref[...] = part.astype(o_ref.dtype)
            return

        @pl.when(kk == 0)
        def _():
            acc_ref[...] = part

        @pl.when(kk > 0)
        def _():
            acc_ref[...] += part

        @pl.when(kk == nk - 1)
        def _():
            o_ref[...] = acc_ref[...].astype(o_ref.dtype)

    return pl.pallas_call(
        body, name=name, grid=(n // tn, m // tm, nk),
        in_specs=[a_spec, pl.BlockSpec((tn, tk), lambda j, i, kk: (j, kk))]
                 + [pl.BlockSpec(memory_space=pl.ANY)] * len(extra),
        out_specs=pl.BlockSpec((tm, tn), lambda j, i, kk: (i, j)),
        out_shape=jax.ShapeDtypeStruct((m, n), out_dtype),
        scratch_shapes=[pltpu.VMEM((tm, tn) if nk > 1 else (8, LANES), F32)],
        compiler_params=_params(("parallel", "parallel", "arbitrary")))(a, b, *extra)


def mm_tn(a, b, *, tk1=None, tn=None, out_dtype=F32, after=None, halves=False, name):
    m, k1 = (a.shape[1], 2 * a.shape[2]) if halves else a.shape
    n = b.shape[1]
    tk1 = tk1 or k1
    tn = tn or n
    tm = _row_tile(m, name, 4 * tk1 + 4 * tn, tk1 * tn * (4 + 2 * jnp.dtype(out_dtype).itemsize))
    nm = m // tm
    extra = [] if after is None else [after]
    if halves:
        per = k1 // tk1 // 2
        a_spec = pl.BlockSpec((None, tm, tk1), lambda p, j, i: (p // per, i, p % per))
    else:
        a_spec = pl.BlockSpec((tm, tk1), lambda p, j, i: (i, p))

    def body(a_ref, b_ref, *rest):
        o_ref, acc_ref = rest[-2:]
        i = pl.program_id(2)
        part = _tn(a_ref[...], b_ref[...])

        @pl.when(i == 0)
        def _():
            acc_ref[...] = part

        @pl.when(i > 0)
        def _():
            acc_ref[...] += part

        @pl.when(i == nm - 1)
        def _():
            o_ref[...] = acc_ref[...].astype(o_ref.dtype)

    return pl.pallas_call(
        body, name=name, grid=(k1 // tk1, n // tn, nm),
        in_specs=[a_spec, pl.BlockSpec((tm, tn), lambda p, j, i: (i, j))]
                 + [pl.BlockSpec(memory_space=pl.ANY)] * len(extra),
        out_specs=pl.BlockSpec((tk1, tn), lambda p, j, i: (p, j)),
        out_shape=jax.ShapeDtypeStruct((k1, n), out_dtype),
        scratch_shapes=[pltpu.VMEM((tk1, tn), F32)],
        compiler_params=_params(("parallel", "parallel", "arbitrary")))(a, b, *extra)


def pool_mm_fwd(pooled, wp, scale, *, name):
    m = pooled.shape[0]
    tm = _row_tile(m, name)

    def body(a_ref, w_ref, s_ref, y0_ref, y1_ref):
        acc = _nn(a_ref[...], w_ref[...])
        y0_ref[...] = acc.astype(BF16)
        y1_ref[...] = (acc * s_ref[...]).astype(BF16)

    blk = pl.BlockSpec((tm, GDIM), lambda g, i: (i, g))
    return pl.pallas_call(
        body, name=name, grid=(GROUPS, m // tm),
        in_specs=[blk, pl.BlockSpec((None, GDIM, GDIM), lambda g, i: (g, 0, 0)),
                  pl.BlockSpec((1, GDIM), lambda g, i: (0, g))],
        out_specs=[blk, blk],
        out_shape=[jax.ShapeDtypeStruct((m, D), BF16)] * 2,
        compiler_params=_params(("parallel", "parallel")))(pooled, wp, scale)


def pool_mm_bwd_x(dy0, wp, *, name):
    m = dy0.shape[0]
    tm = _row_tile(m, name)

    def body(a_ref, w_ref, o_ref):
        o_ref[...] = _nt(a_ref[...], w_ref[...]).astype(BF16)

    blk = pl.BlockSpec((tm, GDIM), lambda g, i: (i, g))
    return pl.pallas_call(
        body, name=name, grid=(GROUPS, m // tm),
        in_specs=[blk, pl.BlockSpec((None, GDIM, GDIM), lambda g, i: (g, 0, 0))],
        out_specs=blk, out_shape=jax.ShapeDtypeStruct((m, D), BF16),
        compiler_params=_params(("parallel", "parallel")))(dy0, wp)


def pool_mm_bwd_w(pooled, dy0, *, name):
    m = pooled.shape[0]
    tm = _row_tile(m, name)

    def body(a_ref, b_ref, o_ref):
        part = _tn(a_ref[...], b_ref[...])

        @pl.when(pl.program_id(1) == 0)
        def _():
            o_ref[...] = part

        @pl.when(pl.program_id(1) > 0)
        def _():
            o_ref[...] += part

    blk = pl.BlockSpec((tm, GDIM), lambda g, i: (i, g))
    return pl.pallas_call(
        body, name=name, grid=(GROUPS, m // tm), in_specs=[blk, blk],
        out_specs=pl.BlockSpec((None, GDIM, GDIM), lambda g, i: (g, 0, 0)),
        out_shape=jax.ShapeDtypeStruct((GROUPS, GDIM, GDIM), F32),
        compiler_params=_params(("parallel", "arbitrary")))(pooled, dy0)


def rmsnorm_fwd(x, g, *, name):
    m = x.shape[0]
    tm = _ew_tile(m, name)

    def body(x_ref, g_ref, o_ref):
        xv = x_ref[...]
        r = lax.rsqrt(jnp.mean(xv * xv, axis=-1, keepdims=True) + EPS)
        o_ref[...] = (xv * r * g_ref[...]).astype(BF16)

    return pl.pallas_call(
        body, name=name, grid=(m // tm,),
        in_specs=[pl.BlockSpec((tm, D), lambda i: (i, 0)), pl.BlockSpec((1, D), lambda i: (0, 0))],
        out_specs=pl.BlockSpec((tm, D), lambda i: (i, 0)),
        out_shape=jax.ShapeDtypeStruct((m, D), BF16),
        compiler_params=_params(("parallel",)))(x, g)


def rmsnorm_bwd(dy, x, g, dres, *, name):
    m = x.shape[0]
    tm = _ew_tile(m, name)

    def body(dy_ref, x_ref, g_ref, r_ref, dx_ref, dxb_ref, dg_ref):
        i = pl.program_id(0)
        xv = x_ref[...]
        dyv = dy_ref[...].astype(F32)
        r = lax.rsqrt(jnp.mean(xv * xv, axis=-1, keepdims=True) + EPS)
        xh = xv * r
        dxh = dyv * g_ref[...]
        dx = r * (dxh - xh * jnp.mean(dxh * xh, axis=-1, keepdims=True))
        row = i * tm + lax.broadcasted_iota(jnp.int32, (tm, 1), 0)
        dx = jnp.where(row >= PAD, dx + r_ref[...], 0.0)
        dx_ref[...] = dx
        dxb_ref[...] = dx.astype(BF16)

        @pl.when(i == 0)
        def _():
            dg_ref[...] = jnp.zeros_like(dg_ref)

        dg_ref[...] += jnp.sum(dyv * xh, axis=0, keepdims=True)

    blk = pl.BlockSpec((tm, D), lambda i: (i, 0))
    vec = pl.BlockSpec((1, D), lambda i: (0, 0))
    return pl.pallas_call(
        body, name=name, grid=(m // tm,), in_specs=[blk, blk, vec, blk],
        out_specs=[blk, blk, vec],
        out_shape=[jax.ShapeDtypeStruct((m, D), F32), jax.ShapeDtypeStruct((m, D), BF16),
                   jax.ShapeDtypeStruct((1, D), F32)],
        compiler_params=_params(("arbitrary",)))(dy, x, g, dres)


def loss_head(h, gf, target, *, name):
    m = h.shape[0]
    t = _ew_tile(m, name)
    inv_d = 1.0 / D

    def body(h_ref, g_ref, t_ref, dh_ref, dhb_ref, dg_ref, ls_ref):
        i = pl.program_id(0)

        @pl.when(i == 0)
        def _():
            dg_ref[...] = jnp.zeros_like(dg_ref)
            ls_ref[...] = jnp.zeros_like(ls_ref)

        real = i * t + lax.broadcasted_iota(jnp.int32, (t, 1), 0) >= X0
        xv = h_ref[...]
        r = lax.rsqrt(jnp.mean(xv * xv, axis=-1, keepdims=True) + EPS)
        xh = xv * r
        err = jnp.where(real, xh * g_ref[...] - t_ref[...], 0.0)
        ls_ref[...] += jnp.sum(err * err, axis=0, keepdims=True)
        dy = err * inv_d
        dg_ref[...] += jnp.sum(dy * xh, axis=0, keepdims=True)
        dxh = dy * g_ref[...]
        dh = r * (dxh - xh * jnp.mean(dxh * xh, axis=-1, keepdims=True))
        dh_ref[...] = dh
        dhb_ref[...] = dh.astype(BF16)

    blk = pl.BlockSpec((t, D), lambda i: (i, 0))
    vec = pl.BlockSpec((1, D), lambda i: (0, 0))
    return pl.pallas_call(
        body, name=name, grid=(m // t,),
        in_specs=[blk, vec, blk],
        out_specs=[blk, blk, vec, vec],
        out_shape=[jax.ShapeDtypeStruct((m, D), F32), jax.ShapeDtypeStruct((m, D), BF16),
                   jax.ShapeDtypeStruct((1, D), F32), jax.ShapeDtypeStruct((1, D), F32)],
        compiler_params=_params(("arbitrary",)))(h, gf, target)


def _split3(x):
    x1 = x.astype(BF16)
    r1 = x - x1.astype(F32)
    x2 = r1.astype(BF16)
    x3 = (r1 - x2.astype(F32)).astype(BF16)
    return x1, x2, x3


def _tri_mm(tri, x):
    x1, x2, x3 = _split3(x)
    return _nn(tri, x1) + _nn(tri, x2) + _nn(tri, x3)


def _log_decay(glr, wgk, bgk, row0, rows):
    z = _nn(glr, wgk) + bgk
    la = (jnp.minimum(z, 0.0) - jnp.log(1.0 + jnp.exp(-jnp.abs(z)))) * (1.0 / TAU)
    row = row0 + lax.broadcasted_iota(jnp.int32, (rows, 1), 0)
    return z, jnp.where(row >= PAD, la, 0.0)


def _chunk_group(n_chunks):
    return _pick(n_chunks, (3, 2, 1))


def gla_fwd(p, wgk, bgk, *, name):
    m = p.shape[0]
    n_chunks = m // CHUNK
    cg = _chunk_group(n_chunks)
    t = cg * CHUNK
    scale = HK ** -0.5

    def body(q_ref, k_ref, v_ref, glr_ref, wgk_ref, bgk_ref, o_ref, st_ref, state):
        i = pl.program_id(0)

        @pl.when(i == 0)
        def _():
            state[...] = jnp.zeros_like(state)

        _, la = _log_decay(glr_ref[...], wgk_ref[...], bgk_ref[...], i * t, t)
        ri = lax.broadcasted_iota(jnp.int32, (CHUNK, CHUNK), 0)
        ci = lax.broadcasted_iota(jnp.int32, (CHUNK, CHUNK), 1)
        causal = ri >= ci
        tri = causal.astype(BF16)
        for c in range(cg):
            rows = pl.ds(c * CHUNK, CHUNK)
            b = _tri_mm(tri, la[c * CHUNK:(c + 1) * CHUNK])
            bl = b[CHUNK - 1:CHUNK, :]
            q = q_ref[rows, :].astype(F32) * scale
            k = k_ref[rows, :].astype(F32)
            qd = (q * jnp.exp(b)).astype(BF16)
            ki = (k * jnp.exp(-b)).astype(BF16)
            ke = (k * jnp.exp(bl - b)).astype(BF16)
            dec = jnp.exp(bl)
            for h in range(HEADS):
                ks = slice(h * HK, (h + 1) * HK)
                vs = pl.ds(h * HV, HV)
                vh = v_ref[rows, vs]
                s_t = state[h]
                st_ref[c, h] = s_t
                att = jnp.where(causal, _nt(qd[:, ks], ki[:, ks]), 0.0).astype(BF16)
                o_ref[rows, vs] = _nn(att, vh) + _nt(qd[:, ks], s_t.astype(BF16))
                state[h] = s_t * dec[:, ks] + _tn(vh, ke[:, ks])

    return pl.pallas_call(
        body, name=name, grid=(n_chunks // cg,),
        in_specs=[pl.BlockSpec((t, DK), lambda i: (i, C_Q // DK)),
                  pl.BlockSpec((t, DK), lambda i: (i, C_K // DK)),
                  pl.BlockSpec((t, DV), lambda i: (i, C_V // DV)),
                  pl.BlockSpec((t, LANES), lambda i: (i, C_GLR // LANES)),
                  pl.BlockSpec((LANES, DK), lambda i: (0, 0)),
                  pl.BlockSpec((1, DK), lambda i: (0, 0))],
        out_specs=[pl.BlockSpec((t, DV), lambda i: (i, 0)),
                   pl.BlockSpec((cg, HEADS, HV, HK), lambda i: (i, 0, 0, 0))],
        out_shape=[jax.ShapeDtypeStruct((m, DV), F32),
                   jax.ShapeDtypeStruct((n_chunks, HEADS, HV, HK), F32)],
        scratch_shapes=[pltpu.VMEM((HEADS, HV, HK), F32)],
        compiler_params=_params(("arbitrary",)))(p, p, p, p, wgk, bgk)


def gla_bwd(p, wgk, bgk, st, do, dp, *, name):
    m = p.shape[0]
    n_chunks = m // CHUNK
    cg = _chunk_group(n_chunks)
    t = cg * CHUNK
    ns = n_chunks // cg
    scale = HK ** -0.5

    def body(q_ref, k_ref, v_ref, glr_ref, wgk_ref, bgk_ref, st_ref, do_ref, dp_in,
             dqkv_ref, dglr_ref, dwgk_ref, dbgk_ref, dstate, dz_buf):
        i = pl.program_id(0)
        blk = ns - 1 - i

        @pl.when(i == 0)
        def _():
            dstate[...] = jnp.zeros_like(dstate)
            dwgk_ref[...] = jnp.zeros_like(dwgk_ref)
            dbgk_ref[...] = jnp.zeros_like(dbgk_ref)

        z, la = _log_decay(glr_ref[...], wgk_ref[...], bgk_ref[...], blk * t, t)
        ri = lax.broadcasted_iota(jnp.int32, (CHUNK, CHUNK), 0)
        ci = lax.broadcasted_iota(jnp.int32, (CHUNK, CHUNK), 1)
        causal = ri >= ci
        tri = causal.astype(BF16)
        tri_u = (ri <= ci).astype(BF16)
        for c in reversed(range(cg)):
            rows = pl.ds(c * CHUNK, CHUNK)
            b = _tri_mm(tri, la[c * CHUNK:(c + 1) * CHUNK])
            bl = b[CHUNK - 1:CHUNK, :]
            eb = jnp.exp(b)
            enb = jnp.exp(-b)
            ebl = jnp.exp(bl - b)
            dec = jnp.exp(bl)
            q = q_ref[rows, :].astype(F32) * scale
            k = k_ref[rows, :].astype(F32)
            qd32 = q * eb
            ki32 = k * enb
            ke32 = k * ebl
            qd = qd32.astype(BF16)
            ki = ki32.astype(BF16)
            ke = ke32.astype(BF16)
            dqd_parts, dki_parts, dke_parts, ddec_parts = [], [], [], []
            for h in range(HEADS):
                ks = slice(h * HK, (h + 1) * HK)
                vs = pl.ds(h * HV, HV)
                vh = v_ref[rows, vs]
                doh = do_ref[rows, vs].astype(BF16)
                s_t = st_ref[c, h]
                ds_t = dstate[h]
                ds_b = ds_t.astype(BF16)
                att = jnp.where(causal, _nt(qd[:, ks], ki[:, ks]), 0.0).astype(BF16)
                datt = jnp.where(causal, _nt(doh, vh), 0.0).astype(BF16)
                dvh = _tn(att, doh) + _nt(ke[:, ks], ds_b)
                dqkv_ref[rows, pl.ds(2 * DK + h * HV, HV)] = dvh.astype(BF16)
                dqd_parts.append(_nn(datt, ki[:, ks]) + _nn(doh, s_t.astype(BF16)))
                dki_parts.append(_tn(datt, qd[:, ks]))
                dke_parts.append(_nn(vh, ds_b))
                ddec_parts.append(jnp.sum(s_t * ds_t, axis=0, keepdims=True))
                dstate[h] = _tn(doh, qd[:, ks]) + ds_t * dec[:, ks]
            dqd = jnp.concatenate(dqd_parts, axis=1)
            dki = jnp.concatenate(dki_parts, axis=1)
            dke = jnp.concatenate(dke_parts, axis=1)
            ddec = jnp.concatenate(ddec_parts, axis=1)
            dqkv_ref[rows, pl.ds(0, DK)] = (dqd * eb * scale).astype(BF16)
            dqkv_ref[rows, pl.ds(DK, DK)] = (dki * enb + dke * ebl).astype(BF16)
            dke_ke = dke * ke32
            db = dqd * qd32 - dki * ki32 - dke_ke
            dbl = jnp.sum(dke_ke, axis=0, keepdims=True) + ddec * dec
            dg = _tri_mm(tri_u, db) + dbl
            row = blk * t + c * CHUNK + lax.broadcasted_iota(jnp.int32, (CHUNK, 1), 0)
            zc = z[c * CHUNK:(c + 1) * CHUNK]
            dz = jnp.where(row >= PAD, dg * (1.0 / TAU) * _sigmoid(-zc), 0.0)
            dz_buf[rows, :] = dz
        dz_all = dz_buf[...]
        dz_b = dz_all.astype(BF16)
        dbgk_ref[...] += jnp.sum(dz_all, axis=0, keepdims=True)
        dglr_ref[...] = _nt(dz_b, wgk_ref[...]).astype(BF16)
        dwgk_ref[...] += _tn(glr_ref[...], dz_b)

    rev = lambda i: ns - 1 - i
    return pl.pallas_call(
        body, name=name, grid=(ns,),
        in_specs=[pl.BlockSpec((t, DK), lambda i: (rev(i), C_Q // DK)),
                  pl.BlockSpec((t, DK), lambda i: (rev(i), C_K // DK)),
                  pl.BlockSpec((t, DV), lambda i: (rev(i), C_V // DV)),
                  pl.BlockSpec((t, LANES), lambda i: (rev(i), C_GLR // LANES)),
                  pl.BlockSpec((LANES, DK), lambda i: (0, 0)),
                  pl.BlockSpec((1, DK), lambda i: (0, 0)),
                  pl.BlockSpec((cg, HEADS, HV, HK), lambda i: (rev(i), 0, 0, 0)),
                  pl.BlockSpec((t, DV), lambda i: (rev(i), 0)), pl.BlockSpec(memory_space=pl.ANY)],
        out_specs=[pl.BlockSpec((t, 2 * DK + DV), lambda i: (rev(i), 0)),
                   pl.BlockSpec((t, LANES), lambda i: (rev(i), 0)),
                   pl.BlockSpec((LANES, DK), lambda i: (0, 0)),
                   pl.BlockSpec((1, DK), lambda i: (0, 0))],
        out_shape=[jax.ShapeDtypeStruct((m, IN_R), BF16),
                   jax.ShapeDtypeStruct((m, LANES), BF16),
                   jax.ShapeDtypeStruct((LANES, DK), F32),
                   jax.ShapeDtypeStruct((1, DK), F32)],
        input_output_aliases={8: 0},
        scratch_shapes=[pltpu.VMEM((HEADS, HV, HK), F32), pltpu.VMEM((t, DK), F32)],
        compiler_params=_params(("arbitrary",)))(p, p, p, p, wgk, bgk, st, do, dp)


def place_glr(dp, dglr, *, name):
    m = dp.shape[0]
    tm = _ew_tile(m, name)

    def body(dp_in, g_ref, o_ref):
        o_ref[...] = g_ref[...]

    return pl.pallas_call(
        body, name=name, grid=(m // tm,),
        in_specs=[pl.BlockSpec(memory_space=pl.ANY), pl.BlockSpec((tm, LANES), lambda i: (i, 0))],
        out_specs=pl.BlockSpec((tm, LANES), lambda i: (i, C_GLR // LANES)),
        out_shape=jax.ShapeDtypeStruct((m, IN_R), BF16), input_output_aliases={0: 0},
        compiler_params=_params(("parallel",)))(dp, dglr)


HALO = 16


def _shift_down(xx, s):
    return pltpu.roll(xx, s, 0)


def _shift_up(xx, s):
    return pltpu.roll(xx, xx.shape[0] - s, 0)


def mix_pre(o, p, gn, *, name):
    m = o.shape[0]
    tm = _ew_tile(m, name)

    def body(o_ref, r_ref, u_ref, gn_ref, ya_ref, pooled_ref, halo):
        i = pl.program_id(0)

        @pl.when(i == 0)
        def _():
            halo[...] = jnp.zeros_like(halo)

        rv = r_ref[...].astype(F32)
        silu_r = rv * _sigmoid(rv)
        for h in range(HEADS):
            cs = pl.ds(h * HV, HV)
            ov = o_ref[:, cs]
            rs = lax.rsqrt(jnp.mean(ov * ov, axis=-1, keepdims=True) + EPS)
            ya_ref[:, cs] = (ov * rs * gn_ref[...] * silu_r[:, h * HV:(h + 1) * HV]).astype(BF16)

        row = i * tm + lax.broadcasted_iota(jnp.int32, (tm, 1), 0)
        pos1 = jnp.maximum(row - PAD + 1, 1).astype(F32)
        for g, w in enumerate(POOL_WINDOWS):
            cs = pl.ds(g * GDIM, GDIM)
            uv = u_ref[:, cs].astype(F32)
            xx = jnp.concatenate([halo[:, cs], uv], axis=0)
            s = xx
            span = 1
            while span < w:
                s = s + _shift_down(s, span)
                span *= 2
            inv = 1.0 / jnp.minimum(pos1, float(w))
            pooled_ref[:, cs] = (s[HALO:] * inv - uv).astype(BF16)
            halo[:, cs] = uv[tm - HALO:]

    blk = pl.BlockSpec((tm, D), lambda i: (i, 0))
    return pl.pallas_call(
        body, name=name, grid=(m // tm,),
        in_specs=[blk, pl.BlockSpec((tm, D), lambda i: (i, C_R // D)),
                  pl.BlockSpec((tm, D), lambda i: (i, C_U // D)),
                  pl.BlockSpec((1, HV), lambda i: (0, 0))],
        out_specs=[blk, blk],
        out_shape=[jax.ShapeDtypeStruct((m, D), BF16)] * 2,
        scratch_shapes=[pltpu.VMEM((HALO, D), F32)],
        compiler_params=_params(("arbitrary",)))(o, p, p, gn)


def mix_pre_bwd(dya, dpooled, o, p, gn, dp, *, name):
    m = o.shape[0]
    tm = _ew_tile(m, name)
    nt = m // tm

    def body(dya_ref, dpl_ref, o_ref, r_ref, gn_ref, dp_in, do_ref, dp_ref, dgn_ref, halo):
        i = pl.program_id(0)
        blk_i = nt - 1 - i

        @pl.when(i == 0)
        def _():
            halo[...] = jnp.zeros_like(halo)
            dgn_ref[...] = jnp.zeros_like(dgn_ref)

        rv = r_ref[...].astype(F32)
        sg = _sigmoid(rv)
        silu_r = rv * sg
        dsilu = sg * (1.0 + rv * (1.0 - sg))
        dgn = jnp.zeros((1, HV), F32)
        for h in range(HEADS):
            cs = pl.ds(h * HV, HV)
            hs = slice(h * HV, (h + 1) * HV)
            ov = o_ref[:, cs]
            dy = dya_ref[:, cs].astype(F32)
            rs = lax.rsqrt(jnp.mean(ov * ov, axis=-1, keepdims=True) + EPS)
            xh = ov * rs
            on = xh * gn_ref[...]
            don = dy * silu_r[:, hs]
            dp_ref[:, cs] = (dy * on * dsilu[:, hs]).astype(BF16)
            dxh = don * gn_ref[...]
            do_ref[:, cs] = rs * (dxh - xh * jnp.mean(dxh * xh, axis=-1, keepdims=True))
            dgn = dgn + jnp.sum(don * xh, axis=0, keepdims=True)
        dgn_ref[...] += dgn

        row = blk_i * tm + lax.broadcasted_iota(jnp.int32, (tm, 1), 0)
        pos1 = jnp.maximum(row - PAD + 1, 1).astype(F32)
        for g, w in enumerate(POOL_WINDOWS):
            cs = pl.ds(g * GDIM, GDIM)
            dpv = dpl_ref[:, cs].astype(F32)
            e = dpv * (1.0 / jnp.minimum(pos1, float(w)))
            xx = jnp.concatenate([e, halo[:, cs]], axis=0)
            s = xx
            span = 1
            while span < w:
                s = s + _shift_up(s, span)
                span *= 2
            dp_ref[:, pl.ds(D + g * GDIM, GDIM)] = (s[:tm] - dpv).astype(BF16)
            halo[:, cs] = e[:HALO]

    rev = lambda i: nt - 1 - i
    blk = pl.BlockSpec((tm, D), lambda i: (rev(i), 0))
    return pl.pallas_call(
        body, name=name, grid=(nt,),
        in_specs=[blk, blk, blk, pl.BlockSpec((tm, D), lambda i: (rev(i), C_R // D)),
                  pl.BlockSpec((1, HV), lambda i: (0, 0)), pl.BlockSpec(memory_space=pl.ANY)],
        out_specs=[blk, pl.BlockSpec((tm, 2 * D), lambda i: (rev(i), C_R // (2 * D))),
                   pl.BlockSpec((1, HV), lambda i: (0, 0))],
        out_shape=[jax.ShapeDtypeStruct((m, D), F32), jax.ShapeDtypeStruct((m, IN_R), BF16),
                   jax.ShapeDtypeStruct((1, HV), F32)],
        input_output_aliases={5: 1},
        scratch_shapes=[pltpu.VMEM((HALO, D), F32)],
        compiler_params=_params(("arbitrary",)))(dya, dpooled, o, p, gn, dp)


def merge_fwd(p, ya, yb, bg, *, name):
    m = ya.shape[0]
    tm = _ew_tile(m, name)

    def body(ga_ref, gb_ref, ya_ref, yb_ref, ba_ref, bb_ref, o_ref):
        gate_a = _sigmoid(ga_ref[...].astype(F32) + ba_ref[...])
        gate_b = _sigmoid(gb_ref[...].astype(F32) + bb_ref[...])
        o_ref[...] = (gate_a * ya_ref[...].astype(F32) + gate_b * yb_ref[...].astype(F32)).astype(BF16)

    blk = pl.BlockSpec((tm, D), lambda i: (i, 0))
    return pl.pallas_call(
        body, name=name, grid=(m // tm,),
        in_specs=[pl.BlockSpec((tm, D), lambda i: (i, C_GA // D)),
                  pl.BlockSpec((tm, D), lambda i: (i, C_GB // D)), blk, blk,
                  pl.BlockSpec((1, D), lambda i: (0, 0)), pl.BlockSpec((1, D), lambda i: (0, 1))],
        out_specs=blk, out_shape=jax.ShapeDtypeStruct((m, D), BF16),
        compiler_params=_params(("parallel",)))(p, p, ya, yb, bg, bg)


def merge_bwd(dmrg, p, ya, yb, bg, *, name):
    m = ya.shape[0]
    tm = _ew_tile(m, name)

    def body(dm_ref, ga_ref, gb_ref, ya_ref, yb_ref, ba_ref, bb_ref,
             dya_ref, dyb_ref, dp_ref, dbg_ref):
        @pl.when(pl.program_id(0) == 0)
        def _():
            dbg_ref[...] = jnp.zeros_like(dbg_ref)

        dm = dm_ref[...].astype(F32)
        gate_a = _sigmoid(ga_ref[...].astype(F32) + ba_ref[...])
        gate_b = _sigmoid(gb_ref[...].astype(F32) + bb_ref[...])
        dya_ref[...] = (dm * gate_a).astype(BF16)
        dyb_ref[...] = (dm * gate_b).astype(BF16)
        dga = dm * ya_ref[...].astype(F32) * gate_a * (1.0 - gate_a)
        dgb = dm * yb_ref[...].astype(F32) * gate_b * (1.0 - gate_b)
        dp_ref[:, pl.ds(0, D)] = dga.astype(BF16)
        dp_ref[:, pl.ds(D, D)] = dgb.astype(BF16)
        dbg_ref[:, pl.ds(0, D)] += jnp.sum(dga, axis=0, keepdims=True)
        dbg_ref[:, pl.ds(D, D)] += jnp.sum(dgb, axis=0, keepdims=True)

    blk = pl.BlockSpec((tm, D), lambda i: (i, 0))
    return pl.pallas_call(
        body, name=name, grid=(m // tm,),
        in_specs=[blk, pl.BlockSpec((tm, D), lambda i: (i, C_GA // D)),
                  pl.BlockSpec((tm, D), lambda i: (i, C_GB // D)), blk, blk,
                  pl.BlockSpec((1, D), lambda i: (0, 0)), pl.BlockSpec((1, D), lambda i: (0, 1))],
        out_specs=[blk, blk, pl.BlockSpec((tm, 2 * D), lambda i: (i, C_GA // (2 * D))),
                   pl.BlockSpec((1, 2 * D), lambda i: (0, 0))],
        out_shape=[jax.ShapeDtypeStruct((m, D), BF16)] * 2 + [jax.ShapeDtypeStruct((m, IN_R), BF16),
                                                              jax.ShapeDtypeStruct((1, 2 * D), F32)],
        compiler_params=_params(("arbitrary",)))(dmrg, p, p, ya, yb, bg, bg)


def scale_bwd(dy1, y0, scale, *, name):
    m = y0.shape[0]
    tm = _ew_tile(m, name)

    def body(dy_ref, y0_ref, s_ref, o_ref, ds_ref):
        @pl.when(pl.program_id(0) == 0)
        def _():
            ds_ref[...] = jnp.zeros_like(ds_ref)

        dy = dy_ref[...].astype(F32)
        o_ref[...] = (dy * s_ref[...]).astype(BF16)
        ds_ref[...] += jnp.sum(dy * y0_ref[...].astype(F32), axis=0, keepdims=True)

    blk = pl.BlockSpec((tm, D), lambda i: (i, 0))
    vec = pl.BlockSpec((1, D), lambda i: (0, 0))
    return pl.pallas_call(
        body, name=name, grid=(m // tm,), in_specs=[blk, blk, vec], out_specs=[blk, vec],
        out_shape=[jax.ShapeDtypeStruct((m, D), BF16), jax.ShapeDtypeStruct((1, D), F32)],
        compiler_params=_params(("arbitrary",)))(dy1, y0, scale)


CONV_BLK = 1408
CONV_ROWS = 688
N_CONV_BLK = D_FF // CONV_BLK


def conv_act_fwd(up, cw, cb, *, name):
    m = up.shape[0]
    tm = _ew_tile(m, name, cap=CONV_ROWS)

    def conv(x_ref, halo, w_ref, b_ref):
        xv = x_ref[...].astype(F32)
        xx = jnp.concatenate([halo[...], xv], axis=0)
        y = (w_ref[2:3, :] * xx + w_ref[1:2, :] * _shift_down(xx, 1)
             + w_ref[0:1, :] * _shift_down(xx, 2))[HALO:] + b_ref[...]
        halo[...] = xv[tm - HALO:]
        return y

    def body(xa_ref, xb_ref, wa_ref, wb_ref, ba_ref, bb_ref, upc_a_ref, upc_b_ref, act_ref, halo_a, halo_b):
        @pl.when(pl.program_id(1) == 0)
        def _():
            halo_a[...] = jnp.zeros_like(halo_a)
            halo_b[...] = jnp.zeros_like(halo_b)

        a = conv(xa_ref, halo_a, wa_ref, ba_ref)
        bv = conv(xb_ref, halo_b, wb_ref, bb_ref)
        upc_a_ref[...] = a.astype(BF16)
        upc_b_ref[...] = bv.astype(BF16)
        act_ref[...] = (a * _sigmoid(a) * bv).astype(BF16)

    nb = N_CONV_BLK
    xa = pl.BlockSpec((tm, CONV_BLK), lambda j, i: (i, j))
    xb = pl.BlockSpec((tm, CONV_BLK), lambda j, i: (i, j + nb))
    return pl.pallas_call(
        body, name=name, grid=(nb, m // tm),
        in_specs=[xa, xb,
                  pl.BlockSpec((3, CONV_BLK), lambda j, i: (0, j)),
                  pl.BlockSpec((3, CONV_BLK), lambda j, i: (0, j + nb)),
                  pl.BlockSpec((1, CONV_BLK), lambda j, i: (0, j)),
                  pl.BlockSpec((1, CONV_BLK), lambda j, i: (0, j + nb))],
        out_specs=[xa, xa, xa],
        out_shape=[jax.ShapeDtypeStruct((m, D_FF), BF16)] * 3,
        scratch_shapes=[pltpu.VMEM((HALO, CONV_BLK), F32)] * 2,
        compiler_params=_params(("parallel", "arbitrary")))(up, up, cw, cw, cb, cb)


def conv_act_bwd(dact, upc_a, upc_b, up, cw, *, name):
    m = up.shape[0]
    tm = _ew_tile(m, name, cap=CONV_ROWS)
    nt = m // tm

    def conv_t(d, halo, x_ref, w_ref, dup_ref, half, dw_ref, db_ref):
        xx = jnp.concatenate([d, halo[...]], axis=0)
        d1 = _shift_up(xx, 1)[:tm]
        d2 = _shift_up(xx, 2)[:tm]
        dup_ref[half] = (w_ref[2:3, :] * d + w_ref[1:2, :] * d1 + w_ref[0:1, :] * d2).astype(BF16)
        xv = x_ref[...].astype(F32)
        dw_ref[2:3, :] += jnp.sum(xv * d, axis=0, keepdims=True)
        dw_ref[1:2, :] += jnp.sum(xv * d1, axis=0, keepdims=True)
        dw_ref[0:1, :] += jnp.sum(xv * d2, axis=0, keepdims=True)
        db_ref[...] += jnp.sum(d, axis=0, keepdims=True)
        halo[...] = d[:HALO]

    def body(da_ref, a_ref, b_ref, xa_ref, xb_ref, wa_ref, wb_ref,
             dup_ref, dwa_ref, dwb_ref, dba_ref, dbb_ref, halo_a, halo_b):
        @pl.when(pl.program_id(1) == 0)
        def _():
            for r in (halo_a, halo_b, dwa_ref, dwb_ref, dba_ref, dbb_ref):
                r[...] = jnp.zeros_like(r)

        dact_v = da_ref[...].astype(F32)
        a = a_ref[...].astype(F32)
        bv = b_ref[...].astype(F32)
        sg = _sigmoid(a)
        d_a = dact_v * bv * sg * (1.0 + a * (1.0 - sg))
        d_b = dact_v * a * sg
        conv_t(d_a, halo_a, xa_ref, wa_ref, dup_ref, 0, dwa_ref, dba_ref)
        conv_t(d_b, halo_b, xb_ref, wb_ref, dup_ref, 1, dwb_ref, dbb_ref)

    nb = N_CONV_BLK
    rev = lambda i: nt - 1 - i
    half = pl.BlockSpec((tm, CONV_BLK), lambda j, i: (rev(i), j))
    xa = half
    xb = pl.BlockSpec((tm, CONV_BLK), lambda j, i: (rev(i), j + nb))
    wa = pl.BlockSpec((3, CONV_BLK), lambda j, i: (0, j))
    wb = pl.BlockSpec((3, CONV_BLK), lambda j, i: (0, j + nb))
    va = pl.BlockSpec((1, CONV_BLK), lambda j, i: (0, j))
    outs = pl.pallas_call(
        body, name=name, grid=(nb, nt),
        in_specs=[half, half, half, xa, xb, wa, wb],
        out_specs=[pl.BlockSpec((2, tm, CONV_BLK), lambda j, i: (0, rev(i), j)), wa, wa, va, va],
        out_shape=[jax.ShapeDtypeStruct((2, m, D_FF), BF16)]
                  + [jax.ShapeDtypeStruct((3, D_FF), F32)] * 2
                  + [jax.ShapeDtypeStruct((1, D_FF), F32)] * 2,
        scratch_shapes=[pltpu.VMEM((HALO, CONV_BLK), F32)] * 2,
        compiler_params=_params(("parallel", "arbitrary")))(dact, upc_a, upc_b, up, up, cw, cw)
    return outs


def local_step(x, target, w):
    seq = x.shape[0]
    h = jnp.concatenate([jnp.zeros((PAD, D), F32), w["meta"], x], axis=0)
    saved = []
    for l in range(DEPTH):
        wl = {k: (v[l:l + 1] if k in ROW_PARAMS else v[l]) for k, v in w.items() if k not in ("meta", "final_norm_g")}
        s = {"h": h}
        fwd_in(s, wl, f"l{l}_")
        fwd_mixer(s, wl, f"l{l}_")
        fwd_ffn(s, wl, f"l{l}_")
        saved.append(s)
        h = s["h3"]

    dh, dh_b, dgf, loss_rows = loss_head(h, w["final_norm_g"], jnp.pad(target, ((X0, 0), (0, 0))), name="loss_head")
    g = {"final_norm_g": dgf}
    per_layer = []
    for l in reversed(range(DEPTH)):
        wl = {k: (v[l:l + 1] if k in ROW_PARAMS else v[l]) for k, v in w.items() if k not in ("meta", "final_norm_g")}
        s = saved[l]
        gl = {}
        dh2, dh2_b = bwd_ffn(dh, dh_b, s, wl, gl, f"l{l}_")
        dp = bwd_mixer(dh2_b, s, wl, gl, f"l{l}_")
        gl["w_in"] = bwd_in_w(dp, s, f"l{l}_")
        dh, dh_b = bwd_in_x(dp, dh2, s, wl, gl, f"l{l}_")
        per_layer.append(gl)
    per_layer.reverse()
    for k in per_layer[0]:
        g[k] = jnp.stack([per_layer[l][k].astype(F32) for l in range(DEPTH)])
    g["meta"] = dh[PAD:X0]
    return loss_rows, dh[X0:X0 + seq], g


ROW_PARAMS = ("norm1_g", "b_gk", "gla_norm_g", "pool_scale", "b_gates", "norm2_g", "conv_b")


def fwd_in(s, w, ln):
    s["hn1"] = rmsnorm_fwd(s["h"], w["norm1_g"], name=ln + "norm1")
    s["p"] = mm_nt(s["hn1"], w["w_in"], name=ln + "in_proj")


def fwd_mixer(s, w, ln):
    p = s["p"]
    s["o"], s["st"] = gla_fwd(p, w["w_gk"], w["b_gk"], name=ln + "gla_fwd")
    s["ya_in"], s["pooled"] = mix_pre(s["o"], p, w["gla_norm_g"], name=ln + "mix_pre")
    s["ya"] = mm_nn(s["ya_in"], w["w_a"], name=ln + "proj_a")
    s["yb0"], s["yb1"] = pool_mm_fwd(s["pooled"], w["w_pool"], w["pool_scale"], name=ln + "pool_mm")
    s["yb"] = mm_nn(s["yb1"], w["w_b"], name=ln + "proj_b")
    s["mrg"] = merge_fwd(p, s["ya"], s["yb"], w["b_gates"], name=ln + "merge")
    s["h2"] = mm_nn(s["mrg"], w["w_o"], out_dtype=F32, res=s["h"], name=ln + "proj_o")


def fwd_ffn(s, w, ln):
    s["hn2"] = rmsnorm_fwd(s["h2"], w["norm2_g"], name=ln + "norm2")
    s["up"] = mm_nt(s["hn2"], w["w_up"], tn=D_FF, name=ln + "up_proj")
    s["upc_a"], s["upc_b"], s["act"] = conv_act_fwd(s["up"], w["conv_w"], w["conv_b"], name=ln + "conv_act")
    s["h3"] = mm_nn(s["act"], w["w_down"], out_dtype=F32, res=s["h2"], name=ln + "down_proj")


def bwd_ffn(dh, dh_b, s, w, g, ln, after=None):
    dact = mm_nt(dh_b, w["w_down"], after=after, name=ln + "d_act")
    g["w_down"] = mm_tn(s["act"], dh_b, tk1=1408, out_dtype=BF16, after=after, name=ln + "dw_down")
    dup, dcw_a, dcw_b, dcb_a, dcb_b = conv_act_bwd(
        dact, s["upc_a"], s["upc_b"], s["up"], w["conv_w"], name=ln + "conv_act_bwd")
    dhn2 = mm_nn(dup, w["w_up"], out_dtype=F32, tn=512, halves=True, name=ln + "d_hn2")
    g["w_up"] = mm_tn(dup, s["hn2"], tk1=1408, out_dtype=BF16, halves=True, name=ln + "dw_up")
    dh2, dh2_b, g["norm2_g"] = rmsnorm_bwd(dhn2, s["h2"], w["norm2_g"], dh, name=ln + "norm2_bwd")
    g["conv_w"] = jnp.concatenate([dcw_a, dcw_b], axis=1)
    g["conv_b"] = jnp.concatenate([dcb_a, dcb_b], axis=1)
    return dh2, dh2_b


def bwd_mixer(dh2_b, s, w, g, ln, after=None):
    dmrg = mm_nt(dh2_b, w["w_o"], after=after, name=ln + "d_mrg")
    g["w_o"] = mm_tn(s["mrg"], dh2_b, out_dtype=BF16, after=after, name=ln + "dw_o")
    dya, dyb, dp, g["b_gates"] = merge_bwd(dmrg, s["p"], s["ya"], s["yb"], w["b_gates"], name=ln + "merge_bwd")
    dya_in = mm_nt(dya, w["w_a"], name=ln + "d_ya_in")
    g["w_a"] = mm_tn(s["ya_in"], dya, out_dtype=BF16, name=ln + "dw_a")
    dyb1 = mm_nt(dyb, w["w_b"], name=ln + "d_yb1")
    g["w_b"] = mm_tn(s["yb1"], dyb, out_dtype=BF16, name=ln + "dw_b")
    dyb0, g["pool_scale"] = scale_bwd(dyb1, s["yb0"], w["pool_scale"], name=ln + "scale_bwd")
    dpooled = pool_mm_bwd_x(dyb0, w["w_pool"], name=ln + "d_pooled")
    g["w_pool"] = pool_mm_bwd_w(s["pooled"], dyb0, name=ln + "dw_pool")
    do, dp, g["gla_norm_g"] = mix_pre_bwd(dya_in, dpooled, s["o"], s["p"], w["gla_norm_g"], dp,
                                          name=ln + "mix_pre_bwd")
    dp, dglr, g["w_gk"], g["b_gk"] = gla_bwd(s["p"], w["w_gk"], w["b_gk"], s["st"], do, dp, name=ln + "gla_bwd")
    return place_glr(dp, dglr, name=ln + "place_glr")


def bwd_in_w(dp, s, ln):
    return mm_tn(dp, s["hn1"], tk1=896, out_dtype=BF16, name=ln + "dw_in")


def bwd_in_x(dp, dh2, s, w, g, ln, after=None):
    dhn1 = mm_nn(dp, w["w_in"], out_dtype=F32, tn=512, after=after, name=ln + "d_hn1")
    dh, dh_b, g["norm1_g"] = rmsnorm_bwd(dhn1, s["h"], w["norm1_g"], dh2, name=ln + "norm1_bwd")
    return dh, dh_b


def _my_place():
    return lax.axis_index("x"), lax.axis_index("y"), lax.axis_index("c")


def _peer(place, k):
    x, y, c = place
    return (1 - x if k & 4 else x, 1 - y if k & 2 else y, 1 - c if k & 1 else c)


def _index(place):
    x, y, c = place
    return 4 * x + 2 * y + c


def exchange(arrays, kinds, *, name):
    n = len(arrays)

    def body(*refs):
        ins, outs = refs[:n], refs[n:2 * n]
        send_sems, recv_sems, local_sems = refs[2 * n:]
        place = _my_place()
        me = _index(place)

        def src(a, dest):
            return ins[a] if kinds[a] == "gather" else ins[a].at[dest]

        def remote(a, k):
            peer = _peer(place, k)
            return pltpu.make_async_remote_copy(
                src_ref=src(a, _index(peer)), dst_ref=outs[a].at[me],
                send_sem=send_sems.at[a, k - 1], recv_sem=recv_sems.at[a, k - 1],
                device_id=peer, device_id_type=pl.DeviceIdType.MESH)

        def arrival(a, k):
            peer = _peer(place, k)
            return pltpu.make_async_remote_copy(
                src_ref=src(a, me), dst_ref=outs[a].at[_index(peer)],
                send_sem=send_sems.at[a, k - 1], recv_sem=recv_sems.at[a, k - 1],
                device_id=peer, device_id_type=pl.DeviceIdType.MESH)

        own = [pltpu.make_async_copy(src(a, me), outs[a].at[me], local_sems.at[a]) for a in range(n)]
        sends = [remote(a, k) for k in range(1, N_DEV) for a in range(n)]
        for cp in sends:
            cp.start()
        for cp in own:
            cp.start()
        for k in range(1, N_DEV):
            for a in range(n):
                arrival(a, k).wait_recv()
        for cp in sends:
            cp.wait_send()
        for cp in own:
            cp.wait()

    any_spec = pl.BlockSpec(memory_space=pl.ANY)
    out_shape = []
    for arr, kind in zip(arrays, kinds):
        shape = arr.shape if kind == "gather" else arr.shape[1:]
        out_shape.append(jax.ShapeDtypeStruct((N_DEV,) + tuple(shape), arr.dtype))
    return pl.pallas_call(
        body, name=name, in_specs=[any_spec] * n, out_specs=[any_spec] * n, out_shape=out_shape,
        scratch_shapes=[pltpu.SemaphoreType.DMA((n, N_DEV - 1)), pltpu.SemaphoreType.DMA((n, N_DEV - 1)),
                        pltpu.SemaphoreType.DMA((n,))],
    )(*arrays)


def _sem_slot(a, k):
    return a * (N_DEV - 1) + k - 1


_HBM = pl.BlockSpec(memory_space=pltpu.HBM)
_SEM = pl.BlockSpec(memory_space=pltpu.SEMAPHORE)
_DATAFLOW = pltpu.SideEffectType.DATAFLOW_SIDE_EFFECTING


def exchange_start(arrays, kinds, after, *, name):
    n = len(arrays)
    zones = []
    for arr, kind in zip(arrays, kinds):
        shape = arr.shape if kind == "gather" else arr.shape[1:]
        zones.append(lax.empty((N_DEV,) + tuple(shape), arr.dtype))

    def body(*refs):
        ins, lands = refs[:n], refs[n:2 * n]
        send_sems, recv_sems = refs[2 * n + 1], refs[2 * n + 2]
        token = refs[4 * n + 3]
        place = _my_place()
        me = _index(place)
        for a in range(n):
            for k in range(1, N_DEV):
                peer = _peer(place, k)
                pltpu.make_async_remote_copy(
                    src_ref=ins[a] if kinds[a] == "gather" else ins[a].at[_index(peer)], dst_ref=lands[a].at[me],
                    send_sem=send_sems.at[_sem_slot(a, k)], recv_sem=recv_sems.at[_sem_slot(a, k)],
                    device_id=peer, device_id_type=pl.DeviceIdType.MESH).start()
        token[...] = jnp.zeros_like(token)

    sems = pltpu.SemaphoreType.DMA((n * (N_DEV - 1),))
    hbm = lambda a: pltpu.HBM(a.shape, a.dtype)
    outs = pl.pallas_call(
        body, name=name,
        out_shape=(sems, sems, *[hbm(a) for a in arrays], *[hbm(z) for z in zones],
                   jax.ShapeDtypeStruct((8, LANES), F32)),
        in_specs=[_HBM] * (2 * n) + [pl.BlockSpec(memory_space=pl.ANY)],
        out_specs=(_SEM, _SEM, *[_HBM] * (2 * n), pl.BlockSpec(memory_space=pltpu.VMEM)),
        input_output_aliases={i: 2 + i for i in range(2 * n)},
        compiler_params=pltpu.CompilerParams(has_side_effects=_DATAFLOW),
    )(*[pltpu.with_memory_space_constraint(a, pltpu.HBM) for a in arrays],
      *[pltpu.with_memory_space_constraint(z, pltpu.HBM) for z in zones], after)
    return dict(send=outs[0], recv=outs[1], srcs=outs[2:2 + n], zones=outs[2 + n:2 + 2 * n],
                token=outs[2 + 2 * n], kinds=kinds)


def exchange_wait(handle, after, *, name):
    kinds = handle["kinds"]
    n = len(kinds)

    def body(*refs):
        ins, lands = refs[:n], refs[n:2 * n]
        send_sems, recv_sems = refs[2 * n], refs[2 * n + 1]
        place = _my_place()
        me = _index(place)
        for a in range(n):
            for k in range(1, N_DEV):
                peer = _peer(place, k)
                src = ins[a] if kinds[a] == "gather" else ins[a].at[_index(peer)]
                copy = pltpu.make_async_remote_copy(
                    src_ref=src, dst_ref=lands[a].at[_index(peer)],
                    send_sem=send_sems.at[_sem_slot(a, k)], recv_sem=recv_sems.at[_sem_slot(a, k)],
                    device_id=peer, device_id_type=pl.DeviceIdType.MESH)
                copy.wait_send()
                copy.wait_recv()

    srcs, zones = handle["srcs"], handle["zones"]
    after = after if isinstance(after, tuple) else (after,)
    hbm = lambda a: pltpu.HBM(a.shape, a.dtype)
    outs = pl.pallas_call(
        body, name=name,
        out_shape=(*[hbm(a) for a in srcs], *[hbm(z) for z in zones]),
        in_specs=[_HBM] * (2 * n) + [_SEM, _SEM] + [pl.BlockSpec(memory_space=pl.ANY)] * len(after),
        out_specs=[_HBM] * (2 * n),
        input_output_aliases={i: i for i in range(2 * n)},
        compiler_params=pltpu.CompilerParams(has_side_effects=_DATAFLOW),
    )(*srcs, *zones, handle["send"], handle["recv"], *after)
    return _fill_own(outs[:n], outs[n:], kinds)


def _fill_own(srcs, zones, kinds):
    me = _index(_my_place())
    filled = []
    for src, zone, kind in zip(srcs, zones, kinds):
        mine = src if kind == "gather" else lax.dynamic_index_in_dim(src, me, 0, keepdims=False)
        filled.append(lax.dynamic_update_index_in_dim(zone, mine, me, 0))
    return filled


ADAM_COLS = 256


def reduce_adam_layer(parts, w, m, v, layer, prev, *, name):
    _, r, c = w.shape
    tc = ADAM_COLS

    def body(*refs):
        p_ref, w_ref, m_ref, v_ref = refs[:4]
        g_ref, d_ref, m2_ref, v2_ref = refs[-4:]
        g = p_ref[0].astype(F32)
        for i in range(1, N_DEV):
            g = g + p_ref[i].astype(F32)
        m2 = B1 * m_ref[...] + (1.0 - B1) * g
        v2 = B2 * v_ref[...] + (1.0 - B2) * (g * g)
        m_hat = m2 / (1.0 - B1 ** STEP)
        v_hat = v2 / (1.0 - B2 ** STEP)
        g_ref[...] = g
        d_ref[...] = -LR * (m_hat / (jnp.sqrt(v_hat) + ADAM_EPS) + WD * w_ref[...])
        m2_ref[...] = m2
        v2_ref[...] = v2

    blk = pl.BlockSpec((None, r, tc), lambda i: (layer, 0, i))
    in_specs = [pl.BlockSpec((N_DEV, r, tc), lambda i: (0, 0, i)), blk, blk, blk]
    args = [parts, w, m, v]
    aliases = {}
    if prev is not None:
        in_specs += [pl.BlockSpec(memory_space=pl.ANY)] * 4
        args += list(prev)
        aliases = {4 + j: j for j in range(4)}
    return pl.pallas_call(
        body, name=name, grid=(c // tc,), in_specs=in_specs, out_specs=[blk] * 4,
        out_shape=[jax.ShapeDtypeStruct(w.shape, F32)] * 4, input_output_aliases=aliases,
        compiler_params=_params(("parallel",)))(*args)


def reduce_adam(parts, w, m, v, *, name):
    r, c = w.shape
    tr = _pick(r, (256, 352, 192, 128, 72, 64, 32, 16, 8))

    def body(p_ref, w_ref, m_ref, v_ref, g_ref, d_ref, m2_ref, v2_ref):
        g = p_ref[0].astype(F32)
        for i in range(1, N_DEV):
            g = g + p_ref[i].astype(F32)
        wv = w_ref[...]
        m2 = B1 * m_ref[...] + (1.0 - B1) * g
        v2 = B2 * v_ref[...] + (1.0 - B2) * (g * g)
        m_hat = m2 / (1.0 - B1 ** STEP)
        v_hat = v2 / (1.0 - B2 ** STEP)
        g_ref[...] = g
        d_ref[...] = -LR * (m_hat / (jnp.sqrt(v_hat) + ADAM_EPS) + WD * wv)
        m2_ref[...] = m2
        v2_ref[...] = v2

    blk = pl.BlockSpec((tr, c), lambda i: (i, 0))
    return pl.pallas_call(
        body, name=name, grid=(r // tr,),
        in_specs=[pl.BlockSpec((N_DEV, tr, c), lambda i: (0, i, 0)), blk, blk, blk],
        out_specs=[blk] * 4, out_shape=[jax.ShapeDtypeStruct((r, c), F32)] * 4,
        compiler_params=_params(("parallel",)))(parts, w, m, v)


BIG = ("w_in", "w_a", "w_pool_grp", "w_b", "w_o", "w_up", "w_down")
SHARDED_SMALL = ("meta_tokens", "w_gk", "conv_w")
REPLICATED = ("norm1_g", "b_gk", "gla_norm_g", "pool_scale", "b_gates", "norm2_g", "conv_b", "final_norm_g")
CUT_AXIS = {"w_in": 2, "w_a": 1, "w_pool_grp": 2, "w_b": 1, "w_o": 1, "w_up": 2, "w_down": 1,
            "meta_tokens": 1, "w_gk": 2, "conv_w": 2}
WEIGHTS = ("meta_tokens", "norm1_g", "w_in", "w_gk", "b_gk", "gla_norm_g", "w_a", "w_pool_grp", "pool_scale",
           "w_b", "b_gates", "w_o", "norm2_g", "w_up", "conv_w", "conv_b", "w_down", "final_norm_g")


def _as_2d(a):
    return a.reshape(-1, a.shape[-1])


def _from_slots(slots, axis):
    full = jnp.moveaxis(slots, 0, axis)
    shape = list(full.shape)
    shape[axis:axis + 2] = [shape[axis] * shape[axis + 1]]
    return full.reshape(shape)


def _to_slots(full, axis):
    shape = list(full.shape)
    shape[axis:axis + 1] = [N_DEV, shape[axis] // N_DEV]
    return jnp.moveaxis(full.reshape(shape), axis, 0)


def _pack(vectors, rows):
    flat = jnp.concatenate([v.reshape(-1).astype(F32) for v in vectors])
    return jnp.pad(flat, (0, rows * LANES - flat.shape[0])).reshape(rows, LANES)


def _unpack(packed, shapes):
    flat = packed.reshape(-1)
    out, off = [], 0
    for s in shapes:
        size = 1
        for d in s:
            size *= d
        out.append(flat[off:off + size].reshape(s))
        off += size
    return out


def _rows_for(shapes, mult=8):
    total = 0
    for s in shapes:
        size = 1
        for d in s:
            size *= d
        total += size
    rows = -(-total // LANES)
    return -(-rows // mult) * mult


def _permute_rows(w_t):
    pad = jnp.zeros((IN_R - IN_WIDTH,) + w_t.shape[1:], w_t.dtype)
    return jnp.concatenate([w_t[:2048], w_t[2064:], w_t[2048:2064], pad], axis=0)


def _unpermute_rows(w_r):
    return jnp.concatenate([w_r[:2048], w_r[C_GLR:C_GLR + RANK], w_r[2048:C_GLR]], axis=0)


def kernel(x, meta_tokens, norm1_g, w_in, w_gk, b_gk, gla_norm_g, w_a, w_pool_grp, pool_scale, w_b, b_gates, w_o, norm2_g, w_up, conv_w, conv_b, w_down, final_norm_g, loss_target, m_meta_tokens, m_norm1_g, m_w_in, m_w_gk, m_b_gk, m_gla_norm_g, m_w_a, m_w_pool_grp, m_pool_scale, m_w_b, m_b_gates, m_w_o, m_norm2_g, m_w_up, m_conv_w, m_conv_b, m_w_down, m_final_norm_g, v_meta_tokens, v_norm1_g, v_w_in, v_w_gk, v_b_gk, v_gla_norm_g, v_w_a, v_w_pool_grp, v_pool_scale, v_w_b, v_b_gates, v_w_o, v_norm2_g, v_w_up, v_conv_w, v_conv_b, v_w_down, v_final_norm_g):
    wts = dict(meta_tokens=meta_tokens, norm1_g=norm1_g, w_in=w_in, w_gk=w_gk, b_gk=b_gk, gla_norm_g=gla_norm_g,
               w_a=w_a, w_pool_grp=w_pool_grp, pool_scale=pool_scale, w_b=w_b, b_gates=b_gates, w_o=w_o,
               norm2_g=norm2_g, w_up=w_up, conv_w=conv_w, conv_b=conv_b, w_down=w_down, final_norm_g=final_norm_g)
    mom = dict(meta_tokens=m_meta_tokens, norm1_g=m_norm1_g, w_in=m_w_in, w_gk=m_w_gk, b_gk=m_b_gk,
               gla_norm_g=m_gla_norm_g, w_a=m_w_a, w_pool_grp=m_w_pool_grp, pool_scale=m_pool_scale, w_b=m_w_b,
               b_gates=m_b_gates, w_o=m_w_o, norm2_g=m_norm2_g, w_up=m_w_up, conv_w=m_conv_w, conv_b=m_conv_b,
               w_down=m_w_down, final_norm_g=m_final_norm_g)
    var = dict(meta_tokens=v_meta_tokens, norm1_g=v_norm1_g, w_in=v_w_in, w_gk=v_w_gk, b_gk=v_b_gk,
               gla_norm_g=v_gla_norm_g, w_a=v_w_a, w_pool_grp=v_w_pool_grp, pool_scale=v_pool_scale, w_b=v_w_b,
               b_gates=v_b_gates, w_o=v_w_o, norm2_g=v_norm2_g, w_up=v_w_up, conv_w=v_conv_w, conv_b=v_conv_b,
               w_down=v_w_down, final_norm_g=v_final_norm_g)

    small_shapes = [wts[n].shape for n in SHARDED_SMALL]
    small_rows = _rows_for(small_shapes)

    transposed = ("w_in", "w_up")

    def shard3(n, a):
        if n in transposed:
            a = jnp.swapaxes(a, 1, 2)
        return a.reshape(DEPTH, -1, a.shape[-1])

    def unshard3(n, a):
        a = jnp.swapaxes(a, 1, 2) if n in transposed else a
        return a.reshape(wts[n].shape)

    cast = {(0, "w_in"): shard3("w_in", wts["w_in"])[0].astype(BF16)}
    state3 = {}

    def layer_shards(l, names):
        return [cast[l, n] for n in names]

    def tie(row, handle):
        return row + handle["token"][0:1, 0:1]

    def full_weight(n, zone):
        if n == "w_pool_grp":
            return jnp.moveaxis(zone.reshape(N_DEV, GROUPS, GDIM // N_DEV, GDIM), 0, 1).reshape(GROUPS, GDIM, GDIM)
        full = zone.reshape(-1, zone.shape[-1])
        return _permute_rows(full) if n == "w_in" else full

    groups = [("w_in",), ("w_a", "w_pool_grp", "w_b", "w_o"), ("w_up", "w_down")]
    rest = groups[0] + groups[1]
    key = {"w_pool_grp": "w_pool"}
    rows = dict(norm1_g=norm1_g, b_gk=b_gk, gla_norm_g=gla_norm_g, pool_scale=pool_scale, b_gates=b_gates,
                norm2_g=norm2_g, conv_b=conv_b)

    def gather(l, names, after, name, head=()):
        return exchange_start(list(head) + layer_shards(l, names), ["gather"] * (len(head) + len(names)), after,
                              name=name + "_start")

    def landed(handle, after, name, names, w_layer):
        zones = exchange_wait(handle, after, name=name + "_wait")
        for n, z in zip(names, zones[len(zones) - len(names):]):
            w_layer[key.get(n, n)] = full_weight(n, z)
        return zones

    wl = [{n: v[l:l + 1] for n, v in rows.items()} for l in range(DEPTH)]
    g_in0 = gather(0, groups[0], x, "gather_in0", head=[_pack([wts[n] for n in SHARDED_SMALL], small_rows)])
    zero = g_in0["token"][0, 0]
    target = jnp.pad(loss_target[0] + zero, ((X0, 0), (0, 0)))
    for l in range(DEPTH):
        for n in BIG:
            if (l, n) not in cast:
                cast[l, n] = (shard3(n, wts[n])[l] + zero).astype(BF16)
    for n in BIG:
        state3[n] = tuple(shard3(n, a[n]) + zero if n in transposed else shard3(n, a[n]) for a in (wts, mom, var))
    early = [target] + [cast[l, n] for l in range(DEPTH) for n in BIG if (l, n) != (0, "w_in")]
    early += [a for n in transposed for a in state3[n]]
    zones = landed(g_in0, (g_in0["token"], *early), "gather_in0", groups[0], wl[0])
    small_slots = [jnp.stack(parts) for parts in zip(*[_unpack(zones[0][i], small_shapes) for i in range(N_DEV)])]
    small_full = {n: _from_slots(slots, CUT_AXIS[n]) for n, slots in zip(SHARDED_SMALL, small_slots)}
    w_gk_pad = jnp.pad(small_full["w_gk"], ((0, 0), (0, LANES - RANK), (0, 0))).astype(BF16)
    for l in range(DEPTH):
        wl[l]["w_gk"] = w_gk_pad[l]
        wl[l]["conv_w"] = small_full["conv_w"][l]
    g_mix0 = gather(0, groups[1], zones[1], "gather_mix0")
    g_ffn0 = gather(0, groups[2], g_mix0["token"], "gather_ffn0")
    wl[0]["norm1_g"] = tie(wl[0]["norm1_g"], g_ffn0)

    h = jnp.concatenate([jnp.zeros((PAD, D), F32), small_full["meta_tokens"], x[0]], axis=0)
    s0 = {"h": h}
    fwd_in(s0, wl[0], "l0_")
    zones = landed(g_mix0, s0["p"], "gather_mix0", groups[1], wl[0])
    g_in1 = gather(1, groups[0], zones[0], "gather_in1")
    wl[0]["b_gk"] = tie(wl[0]["b_gk"], g_in1)
    fwd_mixer(s0, wl[0], "l0_")
    zones = landed(g_ffn0, s0["h2"], "gather_ffn0", groups[2], wl[0])
    g_mix1 = gather(1, groups[1], zones[0], "gather_mix1")
    g_ffn1 = gather(1, groups[2], g_mix1["token"], "gather_ffn1")
    wl[0]["norm2_g"] = tie(wl[0]["norm2_g"], g_ffn1)
    fwd_ffn(s0, wl[0], "l0_")
    landed(g_in1, s0["h3"], "gather_in1", groups[0], wl[1])
    s1 = {"h": s0["h3"]}
    fwd_in(s1, wl[1], "l1_")
    landed(g_mix1, s1["p"], "gather_mix1", groups[1], wl[1])
    fwd_mixer(s1, wl[1], "l1_")
    landed(g_ffn1, s1["h2"], "gather_ffn1", groups[2], wl[1])
    fwd_ffn(s1, wl[1], "l1_")
    dh, dh_b, dgf, loss_rows = loss_head(s1["h3"], final_norm_g[None], target, name="loss_head")
    loss_part = 0.5 * jnp.sum(loss_rows) / D

    def blocks(n, gw):
        if n == "w_in":
            gw = _unpermute_rows(gw)
        if n == "w_pool_grp":
            gw = gw.astype(BF16).reshape(GROUPS, N_DEV, GDIM // N_DEV, GDIM)
            return jnp.moveaxis(gw, 1, 0).reshape(N_DEV, GROUPS * GDIM // N_DEV, GDIM)
        return gw.reshape(N_DEV, gw.shape[0] // N_DEV, gw.shape[1])

    def scatter(g, names, after, name):
        return exchange_start([blocks(n, g[key.get(n, n)]) for n in names], ["scatter"] * len(names), after,
                              name=name + "_start")

    g1, g0 = {}, {}
    dh2, dh2_b = bwd_ffn(dh, dh_b, s1, wl[1], g1, "l1_")
    s_ffn1 = scatter(g1, groups[2], dh2, "scatter_ffn1")
    dp = bwd_mixer(dh2_b, s1, wl[1], g1, "l1_", after=s_ffn1["token"])
    g1["w_in"] = bwd_in_w(dp, s1, "l1_")
    s_rest1 = scatter(g1, rest, s_ffn1["token"], "scatter_rest1")
    dh, dh_b = bwd_in_x(dp, dh2, s1, wl[1], g1, "l1_", after=s_rest1["token"])
    dh2, dh2_b = bwd_ffn(dh, dh_b, s0, wl[0], g0, "l0_")
    r_ffn1 = exchange_wait(s_ffn1, dh2, name="scatter_ffn1_wait")
    s_ffn0 = scatter(g0, groups[2], r_ffn1[0], "scatter_ffn0")
    dp = bwd_mixer(dh2_b, s0, wl[0], g0, "l0_", after=s_ffn0["token"])
    r_rest1 = exchange_wait(s_rest1, dp, name="scatter_rest1_wait")
    g0["w_in"] = bwd_in_w(dp, s0, "l0_")
    s_rest0 = scatter(g0, rest, r_rest1[0], "scatter_rest0")
    dh, _ = bwd_in_x(dp, dh2, s0, wl[0], g0, "l0_", after=s_rest0["token"])
    r_ffn0 = exchange_wait(s_ffn0, dh, name="scatter_ffn0_wait")
    r_rest0 = exchange_wait(s_rest0, r_ffn0[0], name="scatter_rest0_wait")
    grad_x = dh[X0:]
    recv = [dict(zip(groups[2] + rest, list(r_ffn0) + list(r_rest0))),
            dict(zip(groups[2] + rest, list(r_ffn1) + list(r_rest1)))]

    grads, delta, new_m, new_v = {}, {}, {}, {}
    for n in BIG:
        w3, m3, v3 = state3[n]
        first = reduce_adam_layer(recv[1][n], w3, m3, v3, 1, None, name="adam_l1_" + n)
        outs = reduce_adam_layer(recv[0][n], w3, m3, v3, 0, first, name="adam_l0_" + n)
        grads[n], delta[n], new_m[n], new_v[n] = [unshard3(n, o) for o in outs]

    g_full = {n: jnp.stack([g0[n], g1[n]])[:, 0] for n in rows}
    g_full["final_norm_g"] = dgf[0]
    g_full["meta_tokens"] = dh[PAD:X0]
    g_full["w_gk"] = jnp.stack([g0["w_gk"], g1["w_gk"]])[:, :RANK]
    g_full["conv_w"] = jnp.stack([g0["conv_w"], g1["conv_w"]])
    rep_shapes = [wts[n].shape for n in REPLICATED] + [(1,)]
    rep_rows = _rows_for(rep_shapes)
    small_blocks = jnp.stack([
        _pack([_to_slots(g_full[n], CUT_AXIS[n])[i] for n in SHARDED_SMALL], small_rows) for i in range(N_DEV)])
    rep_pack = _pack([g_full[n] for n in REPLICATED] + [loss_part.reshape(1)], rep_rows)
    received = exchange([small_blocks, rep_pack], ["scatter", "gather"], name="exchange_small")
    outs = reduce_adam(received[-2], _pack([wts[n] for n in SHARDED_SMALL], small_rows),
                       _pack([mom[n] for n in SHARDED_SMALL], small_rows),
                       _pack([var[n] for n in SHARDED_SMALL], small_rows), name="adam_small")
    for d, o in zip((grads, delta, new_m, new_v), outs):
        for n, a in zip(SHARDED_SMALL, _unpack(o, small_shapes)):
            d[n] = a
    one = [jnp.zeros((1,), F32)]
    outs = reduce_adam(received[-1], _pack([wts[n] for n in REPLICATED] + one, rep_rows),
                       _pack([mom[n] for n in REPLICATED] + one, rep_rows),
                       _pack([var[n] for n in REPLICATED] + one, rep_rows), name="adam_replicated")
    for d, o in zip((grads, delta, new_m, new_v), outs):
        for n, a in zip(REPLICATED + ("loss",), _unpack(o, rep_shapes)):
            d[n] = a
    loss = grads["loss"][0]
    return (loss, grad_x[None], *[grads[n] for n in WEIGHTS], *[delta[n] for n in WEIGHTS],
            *[new_m[n] for n in WEIGHTS], *[new_v[n] for n in WEIGHTS])
```

```python
import jax
import jax.numpy as jnp
from jax import lax
from jax.experimental import pallas as pl
from jax.experimental.pallas import tpu as pltpu

F32 = jnp.float32
BF16 = jnp.bfloat16

D = 1024
DEPTH = 2
N_META = 16
HEADS = 4
DK = 512
DV = 1024
HK = 128
HV = 256
RANK = 16
TAU = 16.0
CHUNK = 64
POOL_WINDOWS = (2, 4, 8, 16)
GROUPS = 4
GDIM = 256
D_FF = 2816
F2 = 2 * D_FF
EPS = 1e-6
IN_WIDTH = 6160
LR, B1, B2, ADAM_EPS, WD, STEP = 0.001, 0.9, 0.999, 1e-8, 0.01, 10

N_DEV = 8
PAD = CHUNK - N_META
X0 = CHUNK
IN_R = 6272
C_Q, C_K, C_V, C_R, C_U, C_GA, C_GB, C_GLR = 0, 512, 1024, 2048, 3072, 4096, 5120, 6144
VMEM_LIMIT = 56 * 1024 * 1024
LANES = 128


def _params(sem=None):
    return pltpu.CompilerParams(dimension_semantics=sem, vmem_limit_bytes=VMEM_LIMIT)


def _pick(n, prefs):
    for t in prefs:
        if n % t == 0:
            return t
    raise ValueError(f"no tile for {n} in {prefs}")


MM_VMEM_BUDGET = 44 * 1024 * 1024
MM_TILES = (2752, 1376, 688, 192, 128, 64)
EW_TILES = (688, 192, 128, 64)


def _row_tile(lp, row_bytes=0, fixed_bytes=0):
    for t in MM_TILES:
        if lp % t == 0 and (t * row_bytes + fixed_bytes <= MM_VMEM_BUDGET or t <= EW_TILES[0]):
            return t
    raise ValueError(f"no row tile for {lp}")


def _ew_tile(lp, cap=None):
    return _pick(lp, [t for t in EW_TILES if cap is None or t <= cap])


def _after(after):
    if after is None:
        return []
    return list(after) if isinstance(after, (tuple, list)) else [after]


def _sigmoid(x):
    return 1.0 / (1.0 + jnp.exp(-x))


def _dot(a, b, dims):
    return lax.dot_general(a, b, (dims, ((), ())), preferred_element_type=F32)


def _nn(a, b):
    return _dot(a, b, ((1,), (0,)))


def _nt(a, b):
    return _dot(a, b, ((1,), (1,)))


def _tn(a, b):
    return _dot(a, b, ((0,), (0,)))


def mm_nn(a, b, *, out_dtype=BF16, tn=None, res=None, after=None, halves=False, name):
    m, k = (a.shape[1], 2 * a.shape[2]) if halves else a.shape
    n = b.shape[1]
    tn = tn or n
    has_res = res is not None
    out_bytes = jnp.dtype(out_dtype).itemsize
    tm = _row_tile(m, 4 * k + 2 * tn * out_bytes + (8 * tn if has_res else 0), 4 * k * tn)
    extra = _after(after)

    def body(*refs):
        a_ref, b_ref = refs[:2]
        o_ref = refs[-1]
        if has_res:
            r_ref = refs[2]
        if halves:
            acc = _nn(a_ref[0], b_ref[pl.ds(0, k // 2), :]) + _nn(a_ref[1], b_ref[pl.ds(k // 2, k // 2), :])
        else:
            acc = _nn(a_ref[...], b_ref[...])
        if has_res:
            row = pl.program_id(1) * tm + lax.broadcasted_iota(jnp.int32, (tm, 1), 0)
            acc = jnp.where(row >= PAD, acc + r_ref[...], 0.0)
        o_ref[...] = acc.astype(o_ref.dtype)

    a_spec = (pl.BlockSpec((2, tm, k // 2), lambda j, i: (0, i, 0)) if halves
              else pl.BlockSpec((tm, k), lambda j, i: (i, 0)))
    in_specs = [a_spec, pl.BlockSpec((k, tn), lambda j, i: (0, j))]
    args = [a, b]
    if has_res:
        in_specs.append(pl.BlockSpec((tm, tn), lambda j, i: (i, j)))
        args.append(res)
    in_specs += [pl.BlockSpec(memory_space=pl.ANY)] * len(extra)
    args += extra
    return pl.pallas_call(
        body, name=name, grid=(n // tn, m // tm), in_specs=in_specs,
        out_specs=pl.BlockSpec((tm, tn), lambda j, i: (i, j)),
        out_shape=jax.ShapeDtypeStruct((m, n), out_dtype),
        compiler_params=_params(("parallel", "parallel")))(*args)


def mm_nt(a, b, *, out_dtype=BF16, tn=None, tk=None, after=None, halves=False, name):
    m, k = (a.shape[1], 2 * a.shape[2]) if halves else a.shape
    n = b.shape[0]
    tn = tn or n
    tk = tk or k
    nk = k // tk
    tm = _row_tile(m, 4 * tk + 2 * tn * jnp.dtype(out_dtype).itemsize + (4 * tn if nk > 1 else 0), 4 * tn * tk)
    extra = _after(after)
    if halves:
        per = nk // 2
        a_spec = pl.BlockSpec((None, tm, tk), lambda j, i, kk: (kk // per, i, kk % per))
    else:
        a_spec = pl.BlockSpec((tm, tk), lambda j, i, kk: (i, kk))

    def body(a_ref, b_ref, *rest):
        o_ref, acc_ref = rest[-2:]
        kk = pl.program_id(2)
        part = _nt(a_ref[...], b_ref[...])
        if nk == 1:
            o_ref[...] = part.astype(o_ref.dtype)
            return

        @pl.when(kk == 0)
        def _():
            acc_ref[...] = part

        @pl.when(kk > 0)
        def _():
            acc_ref[...] += part

        @pl.when(kk == nk - 1)
        def _():
            o_ref[...] = acc_ref[...].astype(o_ref.dtype)

    return pl.pallas_call(
        body, name=name, grid=(n // tn, m // tm, nk),
        in_specs=[a_spec, pl.BlockSpec((tn, tk), lambda j, i, kk: (j, kk))]
                 + [pl.BlockSpec(memory_space=pl.ANY)] * len(extra),
        out_specs=pl.BlockSpec((tm, tn), lambda j, i, kk: (i, j)),
        out_shape=jax.ShapeDtypeStruct((m, n), out_dtype),
        scratch_shapes=[pltpu.VMEM((tm, tn) if nk > 1 else (8, LANES), F32)],
        compiler_params=_params(("parallel", "parallel", "arbitrary")))(a, b, *extra)


def mm_tn(a, b, *, tk1=None, tn=None, out_dtype=F32, after=None, halves=False, name):
    m, k1 = (a.shape[1], 2 * a.shape[2]) if halves else a.shape
    n = b.shape[1]
    tk1 = tk1 or k1
    tn = tn or n
    tm = _row_tile(m, 4 * tk1 + 4 * tn, tk1 * tn * (4 + 2 * jnp.dtype(out_dtype).itemsize))
    nm = m // tm
    extra = _after(after)
    if halves:
        per = k1 // tk1 // 2
        a_spec = pl.BlockSpec((None, tm, tk1), lambda p, j, i: (p // per, i, p % per))
    else:
        a_spec = pl.BlockSpec((tm, tk1), lambda p, j, i: (i, p))

    def body(a_ref, b_ref, *rest):
        o_ref, acc_ref = rest[-2:]
        i = pl.program_id(2)
        part = _tn(a_ref[...], b_ref[...])

        @pl.when(i == 0)
        def _():
            acc_ref[...] = part

        @pl.when(i > 0)
        def _():
            acc_ref[...] += part

        @pl.when(i == nm - 1)
        def _():
            o_ref[...] = acc_ref[...].astype(o_ref.dtype)

    return pl.pallas_call(
        body, name=name, grid=(k1 // tk1, n // tn, nm),
        in_specs=[a_spec, pl.BlockSpec((tm, tn), lambda p, j, i: (i, j))]
                 + [pl.BlockSpec(memory_space=pl.ANY)] * len(extra),
        out_specs=pl.BlockSpec((tk1, tn), lambda p, j, i: (p, j)),
        out_shape=jax.ShapeDtypeStruct((k1, n), out_dtype),
        scratch_shapes=[pltpu.VMEM((tk1, tn), F32)],
        compiler_params=_params(("parallel", "parallel", "arbitrary")))(a, b, *extra)


def pool_mm_fwd(pooled, wp, scale, *, name):
    m = pooled.shape[0]
    tm = _row_tile(m)

    def body(a_ref, w_ref, s_ref, y0_ref, y1_ref):
        acc = _nn(a_ref[...], w_ref[...])
        y0_ref[...] = acc.astype(BF16)
        y1_ref[...] = (acc * s_ref[...]).astype(BF16)

    blk = pl.BlockSpec((tm, GDIM), lambda g, i: (i, g))
    return pl.pallas_call(
        body, name=name, grid=(GROUPS, m // tm),
        in_specs=[blk, pl.BlockSpec((None, GDIM, GDIM), lambda g, i: (g, 0, 0)),
                  pl.BlockSpec((1, GDIM), lambda g, i: (0, g))],
        out_specs=[blk, blk],
        out_shape=[jax.ShapeDtypeStruct((m, D), BF16)] * 2,
        compiler_params=_params(("parallel", "parallel")))(pooled, wp, scale)


def pool_mm_bwd_x(dy0, wp, *, name):
    m = dy0.shape[0]
    tm = _row_tile(m)

    def body(a_ref, w_ref, o_ref):
        o_ref[...] = _nt(a_ref[...], w_ref[...]).astype(BF16)

    blk = pl.BlockSpec((tm, GDIM), lambda g, i: (i, g))
    return pl.pallas_call(
        body, name=name, grid=(GROUPS, m // tm),
        in_specs=[blk, pl.BlockSpec((None, GDIM, GDIM), lambda g, i: (g, 0, 0))],
        out_specs=blk, out_shape=jax.ShapeDtypeStruct((m, D), BF16),
        compiler_params=_params(("parallel", "parallel")))(dy0, wp)


def pool_mm_bwd_w(pooled, dy0, *, name):
    m = pooled.shape[0]
    tm = _row_tile(m)

    def body(a_ref, b_ref, o_ref):
        part = _tn(a_ref[...], b_ref[...])

        @pl.when(pl.program_id(1) == 0)
        def _():
            o_ref[...] = part

        @pl.when(pl.program_id(1) > 0)
        def _():
            o_ref[...] += part

    blk = pl.BlockSpec((tm, GDIM), lambda g, i: (i, g))
    return pl.pallas_call(
        body, name=name, grid=(GROUPS, m // tm), in_specs=[blk, blk],
        out_specs=pl.BlockSpec((None, GDIM, GDIM), lambda g, i: (g, 0, 0)),
        out_shape=jax.ShapeDtypeStruct((GROUPS, GDIM, GDIM), F32),
        compiler_params=_params(("parallel", "arbitrary")))(pooled, dy0)


def rmsnorm_fwd(x, g, *, name):
    m = x.shape[0]
    tm = _ew_tile(m)

    def body(x_ref, g_ref, o_ref):
        xv = x_ref[...]
        r = lax.rsqrt(jnp.mean(xv * xv, axis=-1, keepdims=True) + EPS)
        o_ref[...] = (xv * r * g_ref[...]).astype(BF16)

    return pl.pallas_call(
        body, name=name, grid=(m // tm,),
        in_specs=[pl.BlockSpec((tm, D), lambda i: (i, 0)), pl.BlockSpec((1, D), lambda i: (0, 0))],
        out_specs=pl.BlockSpec((tm, D), lambda i: (i, 0)),
        out_shape=jax.ShapeDtypeStruct((m, D), BF16),
        compiler_params=_params(("parallel",)))(x, g)


def rmsnorm_bwd(dy, x, g, dres, *, name):
    m = x.shape[0]
    tm = _ew_tile(m)

    def body(dy_ref, x_ref, g_ref, r_ref, dx_ref, dxb_ref, dg_ref):
        i = pl.program_id(0)
        xv = x_ref[...]
        dyv = dy_ref[...].astype(F32)
        r = lax.rsqrt(jnp.mean(xv * xv, axis=-1, keepdims=True) + EPS)
        xh = xv * r
        dxh = dyv * g_ref[...]
        dx = r * (dxh - xh * jnp.mean(dxh * xh, axis=-1, keepdims=True))
        row = i * tm + lax.broadcasted_iota(jnp.int32, (tm, 1), 0)
        dx = jnp.where(row >= PAD, dx + r_ref[...], 0.0)
        dx_ref[...] = dx
        dxb_ref[...] = dx.astype(BF16)

        @pl.when(i == 0)
        def _():
            dg_ref[...] = jnp.zeros_like(dg_ref)

        dg_ref[...] += jnp.sum(dyv * xh, axis=0, keepdims=True)

    blk = pl.BlockSpec((tm, D), lambda i: (i, 0))
    vec = pl.BlockSpec((1, D), lambda i: (0, 0))
    return pl.pallas_call(
        body, name=name, grid=(m // tm,), in_specs=[blk, blk, vec, blk],
        out_specs=[blk, blk, vec],
        out_shape=[jax.ShapeDtypeStruct((m, D), F32), jax.ShapeDtypeStruct((m, D), BF16),
                   jax.ShapeDtypeStruct((1, D), F32)],
        compiler_params=_params(("arbitrary",)))(dy, x, g, dres)


def loss_head(h, gf, target, *, name):
    m = h.shape[0]
    t = _ew_tile(m)
    inv_d = 1.0 / D

    def body(h_ref, g_ref, t_ref, dh_ref, dhb_ref, dg_ref, ls_ref):
        i = pl.program_id(0)

        @pl.when(i == 0)
        def _():
            dg_ref[...] = jnp.zeros_like(dg_ref)
            ls_ref[...] = jnp.zeros_like(ls_ref)

        real = i * t + lax.broadcasted_iota(jnp.int32, (t, 1), 0) >= X0
        xv = h_ref[...]
        r = lax.rsqrt(jnp.mean(xv * xv, axis=-1, keepdims=True) + EPS)
        xh = xv * r
        err = jnp.where(real, xh * g_ref[...] - t_ref[...], 0.0)
        ls_ref[...] += jnp.sum(err * err, axis=0, keepdims=True)
        dy = err * inv_d
        dg_ref[...] += jnp.sum(dy * xh, axis=0, keepdims=True)
        dxh = dy * g_ref[...]
        dh = r * (dxh - xh * jnp.mean(dxh * xh, axis=-1, keepdims=True))
        dh_ref[...] = dh
        dhb_ref[...] = dh.astype(BF16)

    blk = pl.BlockSpec((t, D), lambda i: (i, 0))
    vec = pl.BlockSpec((1, D), lambda i: (0, 0))
    return pl.pallas_call(
        body, name=name, grid=(m // t,),
        in_specs=[blk, vec, blk],
        out_specs=[blk, blk, vec, vec],
        out_shape=[jax.ShapeDtypeStruct((m, D), F32), jax.ShapeDtypeStruct((m, D), BF16),
                   jax.ShapeDtypeStruct((1, D), F32), jax.ShapeDtypeStruct((1, D), F32)],
        compiler_params=_params(("arbitrary",)))(h, gf, target)


def _split3(x):
    x1 = x.astype(BF16)
    r1 = x - x1.astype(F32)
    x2 = r1.astype(BF16)
    x3 = (r1 - x2.astype(F32)).astype(BF16)
    return x1, x2, x3


def _tri_mm(tri, x):
    x1, x2, x3 = _split3(x)
    return _nn(tri, x1) + _nn(tri, x2) + _nn(tri, x3)


def _log_decay(glr, wgk, bgk, row0, rows):
    z = _nn(glr, wgk) + bgk
    la = (jnp.minimum(z, 0.0) - jnp.log(1.0 + jnp.exp(-jnp.abs(z)))) * (1.0 / TAU)
    row = row0 + lax.broadcasted_iota(jnp.int32, (rows, 1), 0)
    return z, jnp.where(row >= PAD, la, 0.0)


def _chunk_group(n_chunks):
    return _pick(n_chunks, (3, 2, 1))


def gla_fwd(p, wgk, bgk, *, name):
    m = p.shape[0]
    n_chunks = m // CHUNK
    cg = _chunk_group(n_chunks)
    t = cg * CHUNK
    scale = HK ** -0.5

    def body(q_ref, k_ref, v_ref, glr_ref, wgk_ref, bgk_ref, o_ref, st_ref, state):
        i = pl.program_id(0)

        @pl.when(i == 0)
        def _():
            state[...] = jnp.zeros_like(state)

        _, la = _log_decay(glr_ref[...], wgk_ref[...], bgk_ref[...], i * t, t)
        ri = lax.broadcasted_iota(jnp.int32, (CHUNK, CHUNK), 0)
        ci = lax.broadcasted_iota(jnp.int32, (CHUNK, CHUNK), 1)
        causal = ri >= ci
        tri = causal.astype(BF16)
        for c in range(cg):
            rows = pl.ds(c * CHUNK, CHUNK)
            b = _tri_mm(tri, la[c * CHUNK:(c + 1) * CHUNK])
            bl = b[CHUNK - 1:CHUNK, :]
            q = q_ref[rows, :].astype(F32) * scale
            k = k_ref[rows, :].astype(F32)
            qd = (q * jnp.exp(b)).astype(BF16)
            ki = (k * jnp.exp(-b)).astype(BF16)
            ke = (k * jnp.exp(bl - b)).astype(BF16)
            dec = jnp.exp(bl)
            for h in range(HEADS):
                ks = slice(h * HK, (h + 1) * HK)
                vs = pl.ds(h * HV, HV)
                vh = v_ref[rows, vs]
                s_t = state[h]
                st_ref[c, h] = s_t
                att = jnp.where(causal, _nt(qd[:, ks], ki[:, ks]), 0.0).astype(BF16)
                o_ref[rows, vs] = _nn(att, vh) + _nt(qd[:, ks], s_t.astype(BF16))
                state[h] = s_t * dec[:, ks] + _tn(vh, ke[:, ks])

    return pl.pallas_call(
        body, name=name, grid=(n_chunks // cg,),
        in_specs=[pl.BlockSpec((t, DK), lambda i: (i, C_Q // DK)),
                  pl.BlockSpec((t, DK), lambda i: (i, C_K // DK)),
                  pl.BlockSpec((t, DV), lambda i: (i, C_V // DV)),
                  pl.BlockSpec((t, LANES), lambda i: (i, C_GLR // LANES)),
                  pl.BlockSpec((LANES, DK), lambda i: (0, 0)),
                  pl.BlockSpec((1, DK), lambda i: (0, 0))],
        out_specs=[pl.BlockSpec((t, DV), lambda i: (i, 0)),
                   pl.BlockSpec((cg, HEADS, HV, HK), lambda i: (i, 0, 0, 0))],
        out_shape=[jax.ShapeDtypeStruct((m, DV), F32),
                   jax.ShapeDtypeStruct((n_chunks, HEADS, HV, HK), F32)],
        scratch_shapes=[pltpu.VMEM((HEADS, HV, HK), F32)],
        compiler_params=_params(("arbitrary",)))(p, p, p, p, wgk, bgk)


def gla_bwd(p, wgk, bgk, st, do, dp, *, name):
    m = p.shape[0]
    n_chunks = m // CHUNK
    cg = _chunk_group(n_chunks)
    t = cg * CHUNK
    ns = n_chunks // cg
    scale = HK ** -0.5

    def body(q_ref, k_ref, v_ref, glr_ref, wgk_ref, bgk_ref, st_ref, do_ref, dp_in,
             dqkv_ref, dglr_ref, dwgk_ref, dbgk_ref, dstate, dz_buf):
        i = pl.program_id(0)
        blk = ns - 1 - i

        @pl.when(i == 0)
        def _():
            dstate[...] = jnp.zeros_like(dstate)
            dwgk_ref[...] = jnp.zeros_like(dwgk_ref)
            dbgk_ref[...] = jnp.zeros_like(dbgk_ref)

        z, la = _log_decay(glr_ref[...], wgk_ref[...], bgk_ref[...], blk * t, t)
        ri = lax.broadcasted_iota(jnp.int32, (CHUNK, CHUNK), 0)
        ci = lax.broadcasted_iota(jnp.int32, (CHUNK, CHUNK), 1)
        causal = ri >= ci
        tri = causal.astype(BF16)
        tri_u = (ri <= ci).astype(BF16)
        for c in reversed(range(cg)):
            rows = pl.ds(c * CHUNK, CHUNK)
            b = _tri_mm(tri, la[c * CHUNK:(c + 1) * CHUNK])
            bl = b[CHUNK - 1:CHUNK, :]
            eb = jnp.exp(b)
            enb = jnp.exp(-b)
            ebl = jnp.exp(bl - b)
            dec = jnp.exp(bl)
            q = q_ref[rows, :].astype(F32) * scale
            k = k_ref[rows, :].astype(F32)
            qd32 = q * eb
            ki32 = k * enb
            ke32 = k * ebl
            qd = qd32.astype(BF16)
            ki = ki32.astype(BF16)
            ke = ke32.astype(BF16)
            dqd_parts, dki_parts, dke_parts, ddec_parts = [], [], [], []
            for h in range(HEADS):
                ks = slice(h * HK, (h + 1) * HK)
                vs = pl.ds(h * HV, HV)
                vh = v_ref[rows, vs]
                doh = do_ref[rows, vs].astype(BF16)
                s_t = st_ref[c, h]
                ds_t = dstate[h]
                ds_b = ds_t.astype(BF16)
                att = jnp.where(causal, _nt(qd[:, ks], ki[:, ks]), 0.0).astype(BF16)
                datt = jnp.where(causal, _nt(doh, vh), 0.0).astype(BF16)
                dvh = _tn(att, doh) + _nt(ke[:, ks], ds_b)
                dqkv_ref[rows, pl.ds(2 * DK + h * HV, HV)] = dvh.astype(BF16)
                dqd_parts.append(_nn(datt, ki[:, ks]) + _nn(doh, s_t.astype(BF16)))
                dki_parts.append(_tn(datt, qd[:, ks]))
                dke_parts.append(_nn(vh, ds_b))
                ddec_parts.append(jnp.sum(s_t * ds_t, axis=0, keepdims=True))
                dstate[h] = _tn(doh, qd[:, ks]) + ds_t * dec[:, ks]
            dqd = jnp.concatenate(dqd_parts, axis=1)
            dki = jnp.concatenate(dki_parts, axis=1)
            dke = jnp.concatenate(dke_parts, axis=1)
            ddec = jnp.concatenate(ddec_parts, axis=1)
            dqkv_ref[rows, pl.ds(0, DK)] = (dqd * eb * scale).astype(BF16)
            dqkv_ref[rows, pl.ds(DK, DK)] = (dki * enb + dke * ebl).astype(BF16)
            dke_ke = dke * ke32
            db = dqd * qd32 - dki * ki32 - dke_ke
            dbl = jnp.sum(dke_ke, axis=0, keepdims=True) + ddec * dec
            dg = _tri_mm(tri_u, db) + dbl
            row = blk * t + c * CHUNK + lax.broadcasted_iota(jnp.int32, (CHUNK, 1), 0)
            zc = z[c * CHUNK:(c + 1) * CHUNK]
            dz = jnp.where(row >= PAD, dg * (1.0 / TAU) * _sigmoid(-zc), 0.0)
            dz_buf[rows, :] = dz
        dz_all = dz_buf[...]
        dz_b = dz_all.astype(BF16)
        dbgk_ref[...] += jnp.sum(dz_all, axis=0, keepdims=True)
        dglr_ref[...] = _nt(dz_b, wgk_ref[...]).astype(BF16)
        dwgk_ref[...] += _tn(glr_ref[...], dz_b)

    rev = lambda i: ns - 1 - i
    return pl.pallas_call(
        body, name=name, grid=(ns,),
        in_specs=[pl.BlockSpec((t, DK), lambda i: (rev(i), C_Q // DK)),
                  pl.BlockSpec((t, DK), lambda i: (rev(i), C_K // DK)),
                  pl.BlockSpec((t, DV), lambda i: (rev(i), C_V // DV)),
                  pl.BlockSpec((t, LANES), lambda i: (rev(i), C_GLR // LANES)),
                  pl.BlockSpec((LANES, DK), lambda i: (0, 0)),
                  pl.BlockSpec((1, DK), lambda i: (0, 0)),
                  pl.BlockSpec((cg, HEADS, HV, HK), lambda i: (rev(i), 0, 0, 0)),
                  pl.BlockSpec((t, DV), lambda i: (rev(i), 0)), pl.BlockSpec(memory_space=pl.ANY)],
        out_specs=[pl.BlockSpec((t, 2 * DK + DV), lambda i: (rev(i), 0)),
                   pl.BlockSpec((t, LANES), lambda i: (rev(i), 0)),
                   pl.BlockSpec((LANES, DK), lambda i: (0, 0)),
                   pl.BlockSpec((1, DK), lambda i: (0, 0))],
        out_shape=[jax.ShapeDtypeStruct((m, IN_R), BF16),
                   jax.ShapeDtypeStruct((m, LANES), BF16),
                   jax.ShapeDtypeStruct((LANES, DK), F32),
                   jax.ShapeDtypeStruct((1, DK), F32)],
        input_output_aliases={8: 0},
        scratch_shapes=[pltpu.VMEM((HEADS, HV, HK), F32), pltpu.VMEM((t, DK), F32)],
        compiler_params=_params(("arbitrary",)))(p, p, p, p, wgk, bgk, st, do, dp)


def place_glr(dp, dglr, *, name):
    m = dp.shape[0]
    tm = _ew_tile(m)

    def body(dp_in, g_ref, o_ref):
        o_ref[...] = g_ref[...]

    return pl.pallas_call(
        body, name=name, grid=(m // tm,),
        in_specs=[pl.BlockSpec(memory_space=pl.ANY), pl.BlockSpec((tm, LANES), lambda i: (i, 0))],
        out_specs=pl.BlockSpec((tm, LANES), lambda i: (i, C_GLR // LANES)),
        out_shape=jax.ShapeDtypeStruct((m, IN_R), BF16), input_output_aliases={0: 0},
        compiler_params=_params(("parallel",)))(dp, dglr)


HALO = 16


def _shift_down(xx, s):
    return pltpu.roll(xx, s, 0)


def _shift_up(xx, s):
    return pltpu.roll(xx, xx.shape[0] - s, 0)


def mix_pre(o, p, gn, *, name):
    m = o.shape[0]
    tm = _ew_tile(m)

    def body(o_ref, r_ref, u_ref, gn_ref, ya_ref, pooled_ref, halo):
        i = pl.program_id(0)

        @pl.when(i == 0)
        def _():
            halo[...] = jnp.zeros_like(halo)

        rv = r_ref[...].astype(F32)
        silu_r = rv * _sigmoid(rv)
        for h in range(HEADS):
            cs = pl.ds(h * HV, HV)
            ov = o_ref[:, cs]
            rs = lax.rsqrt(jnp.mean(ov * ov, axis=-1, keepdims=True) + EPS)
            ya_ref[:, cs] = (ov * rs * gn_ref[...] * silu_r[:, h * HV:(h + 1) * HV]).astype(BF16)

        row = i * tm + lax.broadcasted_iota(jnp.int32, (tm, 1), 0)
        pos1 = jnp.maximum(row - PAD + 1, 1).astype(F32)
        for g, w in enumerate(POOL_WINDOWS):
            cs = pl.ds(g * GDIM, GDIM)
            uv = u_ref[:, cs].astype(F32)
            xx = jnp.concatenate([halo[:, cs], uv], axis=0)
            s = xx
            span = 1
            while span < w:
                s = s + _shift_down(s, span)
                span *= 2
            inv = 1.0 / jnp.minimum(pos1, float(w))
            pooled_ref[:, cs] = (s[HALO:] * inv - uv).astype(BF16)
            halo[:, cs] = uv[tm - HALO:]

    blk = pl.BlockSpec((tm, D), lambda i: (i, 0))
    return pl.pallas_call(
        body, name=name, grid=(m // tm,),
        in_specs=[blk, pl.BlockSpec((tm, D), lambda i: (i, C_R // D)),
                  pl.BlockSpec((tm, D), lambda i: (i, C_U // D)),
                  pl.BlockSpec((1, HV), lambda i: (0, 0))],
        out_specs=[blk, blk],
        out_shape=[jax.ShapeDtypeStruct((m, D), BF16)] * 2,
        scratch_shapes=[pltpu.VMEM((HALO, D), F32)],
        compiler_params=_params(("arbitrary",)))(o, p, p, gn)


def mix_pre_bwd(dya, dpooled, o, p, gn, dp, *, name):
    m = o.shape[0]
    tm = _ew_tile(m)
    nt = m // tm

    def body(dya_ref, dpl_ref, o_ref, r_ref, gn_ref, dp_in, do_ref, dp_ref, dgn_ref, halo):
        i = pl.program_id(0)
        blk_i = nt - 1 - i

        @pl.when(i == 0)
        def _():
            halo[...] = jnp.zeros_like(halo)
            dgn_ref[...] = jnp.zeros_like(dgn_ref)

        rv = r_ref[...].astype(F32)
        sg = _sigmoid(rv)
        silu_r = rv * sg
        dsilu = sg * (1.0 + rv * (1.0 - sg))
        dgn = jnp.zeros((1, HV), F32)
        for h in range(HEADS):
            cs = pl.ds(h * HV, HV)
            hs = slice(h * HV, (h + 1) * HV)
            ov = o_ref[:, cs]
            dy = dya_ref[:, cs].astype(F32)
            rs = lax.rsqrt(jnp.mean(ov * ov, axis=-1, keepdims=True) + EPS)
            xh = ov * rs
            on = xh * gn_ref[...]
            don = dy * silu_r[:, hs]
            dp_ref[:, cs] = (dy * on * dsilu[:, hs]).astype(BF16)
            dxh = don * gn_ref[...]
            do_ref[:, cs] = rs * (dxh - xh * jnp.mean(dxh * xh, axis=-1, keepdims=True))
            dgn = dgn + jnp.sum(don * xh, axis=0, keepdims=True)
        dgn_ref[...] += dgn

        row = blk_i * tm + lax.broadcasted_iota(jnp.int32, (tm, 1), 0)
        pos1 = jnp.maximum(row - PAD + 1, 1).astype(F32)
        for g, w in enumerate(POOL_WINDOWS):
            cs = pl.ds(g * GDIM, GDIM)
            dpv = dpl_ref[:, cs].astype(F32)
            e = dpv * (1.0 / jnp.minimum(pos1, float(w)))
            xx = jnp.concatenate([e, halo[:, cs]], axis=0)
            s = xx
            span = 1
            while span < w:
                s = s + _shift_up(s, span)
                span *= 2
            dp_ref[:, pl.ds(D + g * GDIM, GDIM)] = (s[:tm] - dpv).astype(BF16)
            halo[:, cs] = e[:HALO]

    rev = lambda i: nt - 1 - i
    blk = pl.BlockSpec((tm, D), lambda i: (rev(i), 0))
    return pl.pallas_call(
        body, name=name, grid=(nt,),
        in_specs=[blk, blk, blk, pl.BlockSpec((tm, D), lambda i: (rev(i), C_R // D)),
                  pl.BlockSpec((1, HV), lambda i: (0, 0)), pl.BlockSpec(memory_space=pl.ANY)],
        out_specs=[blk, pl.BlockSpec((tm, 2 * D), lambda i: (rev(i), C_R // (2 * D))),
                   pl.BlockSpec((1, HV), lambda i: (0, 0))],
        out_shape=[jax.ShapeDtypeStruct((m, D), F32), jax.ShapeDtypeStruct((m, IN_R), BF16),
                   jax.ShapeDtypeStruct((1, HV), F32)],
        input_output_aliases={5: 1},
        scratch_shapes=[pltpu.VMEM((HALO, D), F32)],
        compiler_params=_params(("arbitrary",)))(dya, dpooled, o, p, gn, dp)


def merge_fwd(p, ya, yb, bg, *, name):
    m = ya.shape[0]
    tm = _ew_tile(m)

    def body(ga_ref, gb_ref, ya_ref, yb_ref, ba_ref, bb_ref, o_ref):
        gate_a = _sigmoid(ga_ref[...].astype(F32) + ba_ref[...])
        gate_b = _sigmoid(gb_ref[...].astype(F32) + bb_ref[...])
        o_ref[...] = (gate_a * ya_ref[...].astype(F32) + gate_b * yb_ref[...].astype(F32)).astype(BF16)

    blk = pl.BlockSpec((tm, D), lambda i: (i, 0))
    return pl.pallas_call(
        body, name=name, grid=(m // tm,),
        in_specs=[pl.BlockSpec((tm, D), lambda i: (i, C_GA // D)),
                  pl.BlockSpec((tm, D), lambda i: (i, C_GB // D)), blk, blk,
                  pl.BlockSpec((1, D), lambda i: (0, 0)), pl.BlockSpec((1, D), lambda i: (0, 1))],
        out_specs=blk, out_shape=jax.ShapeDtypeStruct((m, D), BF16),
        compiler_params=_params(("parallel",)))(p, p, ya, yb, bg, bg)


def merge_bwd(dmrg, p, ya, yb, bg, *, name):
    m = ya.shape[0]
    tm = _ew_tile(m)

    def body(dm_ref, ga_ref, gb_ref, ya_ref, yb_ref, ba_ref, bb_ref,
             dya_ref, dyb_ref, dp_ref, dbg_ref):
        @pl.when(pl.program_id(0) == 0)
        def _():
            dbg_ref[...] = jnp.zeros_like(dbg_ref)

        dm = dm_ref[...].astype(F32)
        gate_a = _sigmoid(ga_ref[...].astype(F32) + ba_ref[...])
        gate_b = _sigmoid(gb_ref[...].astype(F32) + bb_ref[...])
        dya_ref[...] = (dm * gate_a).astype(BF16)
        dyb_ref[...] = (dm * gate_b).astype(BF16)
        dga = dm * ya_ref[...].astype(F32) * gate_a * (1.0 - gate_a)
        dgb = dm * yb_ref[...].astype(F32) * gate_b * (1.0 - gate_b)
        dp_ref[:, pl.ds(0, D)] = dga.astype(BF16)
        dp_ref[:, pl.ds(D, D)] = dgb.astype(BF16)
        dbg_ref[:, pl.ds(0, D)] += jnp.sum(dga, axis=0, keepdims=True)
        dbg_ref[:, pl.ds(D, D)] += jnp.sum(dgb, axis=0, keepdims=True)

    blk = pl.BlockSpec((tm, D), lambda i: (i, 0))
    return pl.pallas_call(
        body, name=name, grid=(m // tm,),
        in_specs=[blk, pl.BlockSpec((tm, D), lambda i: (i, C_GA // D)),
                  pl.BlockSpec((tm, D), lambda i: (i, C_GB // D)), blk, blk,
                  pl.BlockSpec((1, D), lambda i: (0, 0)), pl.BlockSpec((1, D), lambda i: (0, 1))],
        out_specs=[blk, blk, pl.BlockSpec((tm, 2 * D), lambda i: (i, C_GA // (2 * D))),
                   pl.BlockSpec((1, 2 * D), lambda i: (0, 0))],
        out_shape=[jax.ShapeDtypeStruct((m, D), BF16)] * 2 + [jax.ShapeDtypeStruct((m, IN_R), BF16),
                                                              jax.ShapeDtypeStruct((1, 2 * D), F32)],
        compiler_params=_params(("arbitrary",)))(dmrg, p, p, ya, yb, bg, bg)


def scale_bwd(dy1, y0, scale, *, name):
    m = y0.shape[0]
    tm = _ew_tile(m)

    def body(dy_ref, y0_ref, s_ref, o_ref, ds_ref):
        @pl.when(pl.program_id(0) == 0)
        def _():
            ds_ref[...] = jnp.zeros_like(ds_ref)

        dy = dy_ref[...].astype(F32)
        o_ref[...] = (dy * s_ref[...]).astype(BF16)
        ds_ref[...] += jnp.sum(dy * y0_ref[...].astype(F32), axis=0, keepdims=True)

    blk = pl.BlockSpec((tm, D), lambda i: (i, 0))
    vec = pl.BlockSpec((1, D), lambda i: (0, 0))
    return pl.pallas_call(
        body, name=name, grid=(m // tm,), in_specs=[blk, blk, vec], out_specs=[blk, vec],
        out_shape=[jax.ShapeDtypeStruct((m, D), BF16), jax.ShapeDtypeStruct((1, D), F32)],
        compiler_params=_params(("arbitrary",)))(dy1, y0, scale)


CONV_BLK = 1408
CONV_ROWS = 688
N_CONV_BLK = D_FF // CONV_BLK


def conv_act_fwd(up, cw, cb, *, name):
    m = up.shape[0]
    tm = _ew_tile(m, cap=CONV_ROWS)

    def conv(x_ref, halo, w_ref, b_ref):
        xv = x_ref[...].astype(F32)
        xx = jnp.concatenate([halo[...], xv], axis=0)
        y = (w_ref[2:3, :] * xx + w_ref[1:2, :] * _shift_down(xx, 1)
             + w_ref[0:1, :] * _shift_down(xx, 2))[HALO:] + b_ref[...]
        halo[...] = xv[tm - HALO:]
        return y

    def body(xa_ref, xb_ref, wa_ref, wb_ref, ba_ref, bb_ref, upc_a_ref, upc_b_ref, act_ref, halo_a, halo_b):
        @pl.when(pl.program_id(1) == 0)
        def _():
            halo_a[...] = jnp.zeros_like(halo_a)
            halo_b[...] = jnp.zeros_like(halo_b)

        a = conv(xa_ref, halo_a, wa_ref, ba_ref)
        bv = conv(xb_ref, halo_b, wb_ref, bb_ref)
        upc_a_ref[...] = a.astype(BF16)
        upc_b_ref[...] = bv.astype(BF16)
        act_ref[...] = (a * _sigmoid(a) * bv).astype(BF16)

    nb = N_CONV_BLK
    xa = pl.BlockSpec((tm, CONV_BLK), lambda j, i: (i, j))
    xb = pl.BlockSpec((tm, CONV_BLK), lambda j, i: (i, j + nb))
    return pl.pallas_call(
        body, name=name, grid=(nb, m // tm),
        in_specs=[xa, xb,
                  pl.BlockSpec((3, CONV_BLK), lambda j, i: (0, j)),
                  pl.BlockSpec((3, CONV_BLK), lambda j, i: (0, j + nb)),
                  pl.BlockSpec((1, CONV_BLK), lambda j, i: (0, j)),
                  pl.BlockSpec((1, CONV_BLK), lambda j, i: (0, j + nb))],
        out_specs=[xa, xa, xa],
        out_shape=[jax.ShapeDtypeStruct((m, D_FF), BF16)] * 3,
        scratch_shapes=[pltpu.VMEM((HALO, CONV_BLK), F32)] * 2,
        compiler_params=_params(("parallel", "arbitrary")))(up, up, cw, cw, cb, cb)


def conv_act_bwd(dact, upc_a, upc_b, up, cw, *, name):
    m = up.shape[0]
    tm = _ew_tile(m, cap=CONV_ROWS)
    nt = m // tm

    def conv_t(d, halo, x_ref, w_ref, dup_ref, half, dw_ref, db_ref):
        xx = jnp.concatenate([d, halo[...]], axis=0)
        d1 = _shift_up(xx, 1)[:tm]
        d2 = _shift_up(xx, 2)[:tm]
        dup_ref[half] = (w_ref[2:3, :] * d + w_ref[1:2, :] * d1 + w_ref[0:1, :] * d2).astype(BF16)
        xv = x_ref[...].astype(F32)
        dw_ref[2:3, :] += jnp.sum(xv * d, axis=0, keepdims=True)
        dw_ref[1:2, :] += jnp.sum(xv * d1, axis=0, keepdims=True)
        dw_ref[0:1, :] += jnp.sum(xv * d2, axis=0, keepdims=True)
        db_ref[...] += jnp.sum(d, axis=0, keepdims=True)
        halo[...] = d[:HALO]

    def body(da_ref, a_ref, b_ref, xa_ref, xb_ref, wa_ref, wb_ref,
             dup_ref, dwa_ref, dwb_ref, dba_ref, dbb_ref, halo_a, halo_b):
        @pl.when(pl.program_id(1) == 0)
        def _():
            for r in (halo_a, halo_b, dwa_ref, dwb_ref, dba_ref, dbb_ref):
                r[...] = jnp.zeros_like(r)

        dact_v = da_ref[...].astype(F32)
        a = a_ref[...].astype(F32)
        bv = b_ref[...].astype(F32)
        sg = _sigmoid(a)
        d_a = dact_v * bv * sg * (1.0 + a * (1.0 - sg))
        d_b = dact_v * a * sg
        conv_t(d_a, halo_a, xa_ref, wa_ref, dup_ref, 0, dwa_ref, dba_ref)
        conv_t(d_b, halo_b, xb_ref, wb_ref, dup_ref, 1, dwb_ref, dbb_ref)

    nb = N_CONV_BLK
    rev = lambda i: nt - 1 - i
    half = pl.BlockSpec((tm, CONV_BLK), lambda j, i: (rev(i), j))
    xa = half
    xb = pl.BlockSpec((tm, CONV_BLK), lambda j, i: (rev(i), j + nb))
    wa = pl.BlockSpec((3, CONV_BLK), lambda j, i: (0, j))
    wb = pl.BlockSpec((3, CONV_BLK), lambda j, i: (0, j + nb))
    va = pl.BlockSpec((1, CONV_BLK), lambda j, i: (0, j))
    outs = pl.pallas_call(
        body, name=name, grid=(nb, nt),
        in_specs=[half, half, half, xa, xb, wa, wb],
        out_specs=[pl.BlockSpec((2, tm, CONV_BLK), lambda j, i: (0, rev(i), j)), wa, wa, va, va],
        out_shape=[jax.ShapeDtypeStruct((2, m, D_FF), BF16)]
                  + [jax.ShapeDtypeStruct((3, D_FF), F32)] * 2
                  + [jax.ShapeDtypeStruct((1, D_FF), F32)] * 2,
        scratch_shapes=[pltpu.VMEM((HALO, CONV_BLK), F32)] * 2,
        compiler_params=_params(("parallel", "arbitrary")))(dact, upc_a, upc_b, up, up, cw, cw)
    return outs


def local_step(x, target, w):
    seq = x.shape[0]
    h = jnp.concatenate([jnp.zeros((PAD, D), F32), w["meta"], x], axis=0)
    saved = []
    for l in range(DEPTH):
        wl = {k: (v[l:l + 1] if k in ROW_PARAMS else v[l]) for k, v in w.items() if k not in ("meta", "final_norm_g")}
        s = {"h": h}
        fwd_in(s, wl, f"l{l}_")
        fwd_mixer(s, wl, f"l{l}_")
        fwd_ffn(s, wl, f"l{l}_")
        saved.append(s)
        h = s["h3"]

    dh, dh_b, dgf, loss_rows = loss_head(h, w["final_norm_g"], jnp.pad(target, ((X0, 0), (0, 0))), name="loss_head")
    g = {"final_norm_g": dgf}
    per_layer = []
    for l in reversed(range(DEPTH)):
        wl = {k: (v[l:l + 1] if k in ROW_PARAMS else v[l]) for k, v in w.items() if k not in ("meta", "final_norm_g")}
        s = saved[l]
        gl = {}
        dh2, dh2_b = bwd_ffn(dh, dh_b, s, wl, gl, f"l{l}_")
        dp = bwd_mixer(dh2_b, s, wl, gl, f"l{l}_")
        gl["w_in"] = bwd_in_w(dp, s, f"l{l}_")
        dh, dh_b = bwd_in_x(dp, dh2, s, wl, gl, f"l{l}_")
        per_layer.append(gl)
    per_layer.reverse()
    for k in per_layer[0]:
        g[k] = jnp.stack([per_layer[l][k].astype(F32) for l in range(DEPTH)])
    g["meta"] = dh[PAD:X0]
    return loss_rows, dh[X0:X0 + seq], g


ROW_PARAMS = ("norm1_g", "b_gk", "gla_norm_g", "pool_scale", "b_gates", "norm2_g", "conv_b")


def fwd_in(s, w, ln):
    s["hn1"] = rmsnorm_fwd(s["h"], w["norm1_g"], name=ln + "norm1")
    s["p"] = mm_nt(s["hn1"], w["w_in"], name=ln + "in_proj")


def fwd_mixer(s, w, ln):
    p = s["p"]
    s["o"], s["st"] = gla_fwd(p, w["w_gk"], w["b_gk"], name=ln + "gla_fwd")
    s["ya_in"], s["pooled"] = mix_pre(s["o"], p, w["gla_norm_g"], name=ln + "mix_pre")
    s["ya"] = mm_nn(s["ya_in"], w["w_a"], name=ln + "proj_a")
    s["yb0"], s["yb1"] = pool_mm_fwd(s["pooled"], w["w_pool"], w["pool_scale"], name=ln + "pool_mm")
    s["yb"] = mm_nn(s["yb1"], w["w_b"], name=ln + "proj_b")
    s["mrg"] = merge_fwd(p, s["ya"], s["yb"], w["b_gates"], name=ln + "merge")
    s["h2"] = mm_nn(s["mrg"], w["w_o"], out_dtype=F32, res=s["h"], name=ln + "proj_o")


def fwd_ffn(s, w, ln):
    s["hn2"] = rmsnorm_fwd(s["h2"], w["norm2_g"], name=ln + "norm2")
    s["up"] = mm_nt(s["hn2"], w["w_up"], tn=D_FF, name=ln + "up_proj")
    s["upc_a"], s["upc_b"], s["act"] = conv_act_fwd(s["up"], w["conv_w"], w["conv_b"], name=ln + "conv_act")
    s["h3"] = mm_nn(s["act"], w["w_down"], out_dtype=F32, res=s["h2"], name=ln + "down_proj")


def bwd_ffn(dh, dh_b, s, w, g, ln, after=None):
    dact = mm_nt(dh_b, w["w_down"], after=after, name=ln + "d_act")
    g["w_down"] = mm_tn(s["act"], dh_b, tk1=1408, out_dtype=BF16, after=after, name=ln + "dw_down")
    dup, dcw_a, dcw_b, dcb_a, dcb_b = conv_act_bwd(
        dact, s["upc_a"], s["upc_b"], s["up"], w["conv_w"], name=ln + "conv_act_bwd")
    dhn2 = mm_nn(dup, w["w_up"], out_dtype=F32, tn=512, halves=True, name=ln + "d_hn2")
    g["w_up"] = mm_tn(dup, s["hn2"], tk1=1408, out_dtype=BF16, halves=True, name=ln + "dw_up")
    dh2, dh2_b, g["norm2_g"] = rmsnorm_bwd(dhn2, s["h2"], w["norm2_g"], dh, name=ln + "norm2_bwd")
    g["conv_w"] = jnp.concatenate([dcw_a, dcw_b], axis=1)
    g["conv_b"] = jnp.concatenate([dcb_a, dcb_b], axis=1)
    return dh2, dh2_b


def bwd_mixer(dh2_b, s, w, g, ln, after=None):
    dmrg = mm_nt(dh2_b, w["w_o"], after=after, name=ln + "d_mrg")
    g["w_o"] = mm_tn(s["mrg"], dh2_b, out_dtype=BF16, after=after, name=ln + "dw_o")
    dya, dyb, dp, g["b_gates"] = merge_bwd(dmrg, s["p"], s["ya"], s["yb"], w["b_gates"], name=ln + "merge_bwd")
    dya_in = mm_nt(dya, w["w_a"], name=ln + "d_ya_in")
    g["w_a"] = mm_tn(s["ya_in"], dya, out_dtype=BF16, name=ln + "dw_a")
    dyb1 = mm_nt(dyb, w["w_b"], name=ln + "d_yb1")
    g["w_b"] = mm_tn(s["yb1"], dyb, out_dtype=BF16, name=ln + "dw_b")
    dyb0, g["pool_scale"] = scale_bwd(dyb1, s["yb0"], w["pool_scale"], name=ln + "scale_bwd")
    dpooled = pool_mm_bwd_x(dyb0, w["w_pool"], name=ln + "d_pooled")
    g["w_pool"] = pool_mm_bwd_w(s["pooled"], dyb0, name=ln + "dw_pool")
    do, dp, g["gla_norm_g"] = mix_pre_bwd(dya_in, dpooled, s["o"], s["p"], w["gla_norm_g"], dp,
                                          name=ln + "mix_pre_bwd")
    dp, dglr, g["w_gk"], g["b_gk"] = gla_bwd(s["p"], w["w_gk"], w["b_gk"], s["st"], do, dp, name=ln + "gla_bwd")
    return place_glr(dp, dglr, name=ln + "place_glr")


def bwd_in_w(dp, s, ln):
    return mm_tn(dp, s["hn1"], tk1=896, out_dtype=BF16, name=ln + "dw_in")


def bwd_in_x(dp, dh2, s, w, g, ln, after=None):
    dhn1 = mm_nn(dp, w["w_in"], out_dtype=F32, tn=512, after=after, name=ln + "d_hn1")
    dh, dh_b, g["norm1_g"] = rmsnorm_bwd(dhn1, s["h"], w["norm1_g"], dh2, name=ln + "norm1_bwd")
    return dh, dh_b


def _my_place():
    return lax.axis_index("x"), lax.axis_index("y"), lax.axis_index("c")


def _peer(place, k):
    x, y, c = place
    return (1 - x if k & 4 else x, 1 - y if k & 2 else y, 1 - c if k & 1 else c)


def _index(place):
    x, y, c = place
    return 4 * x + 2 * y + c


def exchange(arrays, kinds, *, name):
    n = len(arrays)

    def body(*refs):
        ins, outs = refs[:n], refs[n:2 * n]
        send_sems, recv_sems, local_sems = refs[2 * n:]
        place = _my_place()
        me = _index(place)

        def src(a, dest):
            return ins[a] if kinds[a] == "gather" else ins[a].at[dest]

        def remote(a, k):
            peer = _peer(place, k)
            return pltpu.make_async_remote_copy(
                src_ref=src(a, _index(peer)), dst_ref=outs[a].at[me],
                send_sem=send_sems.at[a, k - 1], recv_sem=recv_sems.at[a, k - 1],
                device_id=peer, device_id_type=pl.DeviceIdType.MESH)

        def arrival(a, k):
            peer = _peer(place, k)
            return pltpu.make_async_remote_copy(
                src_ref=src(a, me), dst_ref=outs[a].at[_index(peer)],
                send_sem=send_sems.at[a, k - 1], recv_sem=recv_sems.at[a, k - 1],
                device_id=peer, device_id_type=pl.DeviceIdType.MESH)

        own = [pltpu.make_async_copy(src(a, me), outs[a].at[me], local_sems.at[a]) for a in range(n)]
        sends = [remote(a, k) for k in range(1, N_DEV) for a in range(n)]
        for cp in sends:
            cp.start()
        for cp in own:
            cp.start()
        for k in range(1, N_DEV):
            for a in range(n):
                arrival(a, k).wait_recv()
        for cp in sends:
            cp.wait_send()
        for cp in own:
            cp.wait()

    any_spec = pl.BlockSpec(memory_space=pl.ANY)
    out_shape = []
    for arr, kind in zip(arrays, kinds):
        shape = arr.shape if kind == "gather" else arr.shape[1:]
        out_shape.append(jax.ShapeDtypeStruct((N_DEV,) + tuple(shape), arr.dtype))
    return pl.pallas_call(
        body, name=name, in_specs=[any_spec] * n, out_specs=[any_spec] * n, out_shape=out_shape,
        scratch_shapes=[pltpu.SemaphoreType.DMA((n, N_DEV - 1)), pltpu.SemaphoreType.DMA((n, N_DEV - 1)),
                        pltpu.SemaphoreType.DMA((n,))],
    )(*arrays)


def _sem_slot(a, k):
    return a * (N_DEV - 1) + k - 1


_HBM = pl.BlockSpec(memory_space=pltpu.HBM)
_SEM = pl.BlockSpec(memory_space=pltpu.SEMAPHORE)
_DATAFLOW = pltpu.SideEffectType.DATAFLOW_SIDE_EFFECTING


def exchange_start(arrays, kinds, after, *, name):
    n = len(arrays)
    zones = []
    for arr, kind in zip(arrays, kinds):
        shape = arr.shape if kind == "gather" else arr.shape[1:]
        zones.append(lax.empty((N_DEV,) + tuple(shape), arr.dtype))

    def body(*refs):
        ins, lands = refs[:n], refs[n:2 * n]
        send_sems, recv_sems = refs[2 * n + 1], refs[2 * n + 2]
        token = refs[4 * n + 3]
        place = _my_place()
        me = _index(place)
        for a in range(n):
            for k in range(1, N_DEV):
                peer = _peer(place, k)
                pltpu.make_async_remote_copy(
                    src_ref=ins[a] if kinds[a] == "gather" else ins[a].at[_index(peer)], dst_ref=lands[a].at[me],
                    send_sem=send_sems.at[_sem_slot(a, k)], recv_sem=recv_sems.at[_sem_slot(a, k)],
                    device_id=peer, device_id_type=pl.DeviceIdType.MESH).start()
        token[...] = jnp.zeros_like(token)

    sems = pltpu.SemaphoreType.DMA((n * (N_DEV - 1),))
    hbm = lambda a: pltpu.HBM(a.shape, a.dtype)
    outs = pl.pallas_call(
        body, name=name,
        out_shape=(sems, sems, *[hbm(a) for a in arrays], *[hbm(z) for z in zones],
                   jax.ShapeDtypeStruct((8, LANES), F32)),
        in_specs=[_HBM] * (2 * n) + [pl.BlockSpec(memory_space=pl.ANY)],
        out_specs=(_SEM, _SEM, *[_HBM] * (2 * n), pl.BlockSpec(memory_space=pltpu.VMEM)),
        input_output_aliases={i: 2 + i for i in range(2 * n)},
        compiler_params=pltpu.CompilerParams(has_side_effects=_DATAFLOW),
    )(*[pltpu.with_memory_space_constraint(a, pltpu.HBM) for a in arrays],
      *[pltpu.with_memory_space_constraint(z, pltpu.HBM) for z in zones], after)
    return dict(send=outs[0], recv=outs[1], srcs=outs[2:2 + n], zones=outs[2 + n:2 + 2 * n],
                token=outs[2 + 2 * n], kinds=kinds)


def exchange_wait(handle, after, *, name):
    kinds = handle["kinds"]
    n = len(kinds)

    def body(*refs):
        ins, lands = refs[:n], refs[n:2 * n]
        send_sems, recv_sems = refs[2 * n], refs[2 * n + 1]
        place = _my_place()
        me = _index(place)
        for a in range(n):
            for k in range(1, N_DEV):
                peer = _peer(place, k)
                src = ins[a] if kinds[a] == "gather" else ins[a].at[_index(peer)]
                copy = pltpu.make_async_remote_copy(
                    src_ref=src, dst_ref=lands[a].at[_index(peer)],
                    send_sem=send_sems.at[_sem_slot(a, k)], recv_sem=recv_sems.at[_sem_slot(a, k)],
                    device_id=peer, device_id_type=pl.DeviceIdType.MESH)
                copy.wait_send()
                copy.wait_recv()

    srcs, zones = handle["srcs"], handle["zones"]
    after = after if isinstance(after, tuple) else (after,)
    hbm = lambda a: pltpu.HBM(a.shape, a.dtype)
    outs = pl.pallas_call(
        body, name=name,
        out_shape=(*[hbm(a) for a in srcs], *[hbm(z) for z in zones]),
        in_specs=[_HBM] * (2 * n) + [_SEM, _SEM] + [pl.BlockSpec(memory_space=pl.ANY)] * len(after),
        out_specs=[_HBM] * (2 * n),
        input_output_aliases={i: i for i in range(2 * n)},
        compiler_params=pltpu.CompilerParams(has_side_effects=_DATAFLOW),
    )(*srcs, *zones, handle["send"], handle["recv"], *after)
    return _fill_own(outs[:n], outs[n:], kinds)


def _fill_own(srcs, zones, kinds):
    me = _index(_my_place())
    filled = []
    for src, zone, kind in zip(srcs, zones, kinds):
        mine = src if kind == "gather" else lax.dynamic_index_in_dim(src, me, 0, keepdims=False)
        filled.append(lax.dynamic_update_index_in_dim(zone, mine, me, 0))
    return filled


ADAM_COLS = 256


def reduce_adam_layer(parts, w, m, v, layer, prev, *, name):
    _, r, c = w.shape
    tc = ADAM_COLS

    def body(*refs):
        p_ref, w_ref, m_ref, v_ref = refs[:4]
        g_ref, d_ref, m2_ref, v2_ref = refs[-4:]
        g = p_ref[0].astype(F32)
        for i in range(1, N_DEV):
            g = g + p_ref[i].astype(F32)
        m2 = B1 * m_ref[...] + (1.0 - B1) * g
        v2 = B2 * v_ref[...] + (1.0 - B2) * (g * g)
        m_hat = m2 / (1.0 - B1 ** STEP)
        v_hat = v2 / (1.0 - B2 ** STEP)
        g_ref[...] = g
        d_ref[...] = -LR * (m_hat / (jnp.sqrt(v_hat) + ADAM_EPS) + WD * w_ref[...])
        m2_ref[...] = m2
        v2_ref[...] = v2

    blk = pl.BlockSpec((None, r, tc), lambda i: (layer, 0, i))
    in_specs = [pl.BlockSpec((N_DEV, r, tc), lambda i: (0, 0, i)), blk, blk, blk]
    args = [parts, w, m, v]
    aliases = {}
    if prev is not None:
        in_specs += [pl.BlockSpec(memory_space=pl.ANY)] * 4
        args += list(prev)
        aliases = {4 + j: j for j in range(4)}
    return pl.pallas_call(
        body, name=name, grid=(c // tc,), in_specs=in_specs, out_specs=[blk] * 4,
        out_shape=[jax.ShapeDtypeStruct(w.shape, F32)] * 4, input_output_aliases=aliases,
        compiler_params=_params(("parallel",)))(*args)


def reduce_adam(parts, w, m, v, *, name):
    r, c = w.shape
    tr = _pick(r, (256, 352, 192, 128, 72, 64, 32, 16, 8))

    def body(p_ref, w_ref, m_ref, v_ref, g_ref, d_ref, m2_ref, v2_ref):
        g = p_ref[0].astype(F32)
        for i in range(1, N_DEV):
            g = g + p_ref[i].astype(F32)
        wv = w_ref[...]
        m2 = B1 * m_ref[...] + (1.0 - B1) * g
        v2 = B2 * v_ref[...] + (1.0 - B2) * (g * g)
        m_hat = m2 / (1.0 - B1 ** STEP)
        v_hat = v2 / (1.0 - B2 ** STEP)
        g_ref[...] = g
        d_ref[...] = -LR * (m_hat / (jnp.sqrt(v_hat) + ADAM_EPS) + WD * wv)
        m2_ref[...] = m2
        v2_ref[...] = v2

    blk = pl.BlockSpec((tr, c), lambda i: (i, 0))
    return pl.pallas_call(
        body, name=name, grid=(r // tr,),
        in_specs=[pl.BlockSpec((N_DEV, tr, c), lambda i: (0, i, 0)), blk, blk, blk],
        out_specs=[blk] * 4, out_shape=[jax.ShapeDtypeStruct((r, c), F32)] * 4,
        compiler_params=_params(("parallel",)))(parts, w, m, v)


BIG = ("w_in", "w_a", "w_pool_grp", "w_b", "w_o", "w_up", "w_down")
SHARDED_SMALL = ("meta_tokens", "w_gk", "conv_w")
REPLICATED = ("norm1_g", "b_gk", "gla_norm_g", "pool_scale", "b_gates", "norm2_g", "conv_b", "final_norm_g")
CUT_AXIS = {"w_in": 2, "w_a": 1, "w_pool_grp": 2, "w_b": 1, "w_o": 1, "w_up": 2, "w_down": 1,
            "meta_tokens": 1, "w_gk": 2, "conv_w": 2}
WEIGHTS = ("meta_tokens", "norm1_g", "w_in", "w_gk", "b_gk", "gla_norm_g", "w_a", "w_pool_grp", "pool_scale",
           "w_b", "b_gates", "w_o", "norm2_g", "w_up", "conv_w", "conv_b", "w_down", "final_norm_g")


def _as_2d(a):
    return a.reshape(-1, a.shape[-1])


def _from_slots(slots, axis):
    full = jnp.moveaxis(slots, 0, axis)
    shape = list(full.shape)
    shape[axis:axis + 2] = [shape[axis] * shape[axis + 1]]
    return full.reshape(shape)


def _to_slots(full, axis):
    shape = list(full.shape)
    shape[axis:axis + 1] = [N_DEV, shape[axis] // N_DEV]
    return jnp.moveaxis(full.reshape(shape), axis, 0)


def _pack(vectors, rows):
    flat = jnp.concatenate([v.reshape(-1).astype(F32) for v in vectors])
    return jnp.pad(flat, (0, rows * LANES - flat.shape[0])).reshape(rows, LANES)


def _unpack(packed, shapes):
    flat = packed.reshape(-1)
    out, off = [], 0
    for s in shapes:
        size = 1
        for d in s:
            size *= d
        out.append(flat[off:off + size].reshape(s))
        off += size
    return out


def _rows_for(shapes, mult=8):
    total = 0
    for s in shapes:
        size = 1
        for d in s:
            size *= d
        total += size
    rows = -(-total // LANES)
    return -(-rows // mult) * mult


def _permute_rows(w_t):
    pad = jnp.zeros((IN_R - IN_WIDTH,) + w_t.shape[1:], w_t.dtype)
    return jnp.concatenate([w_t[:2048], w_t[2064:], w_t[2048:2064], pad], axis=0)


def _unpermute_rows(w_r):
    return jnp.concatenate([w_r[:2048], w_r[C_GLR:C_GLR + RANK], w_r[2048:C_GLR]], axis=0)


def kernel(x, meta_tokens, norm1_g, w_in, w_gk, b_gk, gla_norm_g, w_a, w_pool_grp, pool_scale, w_b, b_gates, w_o, norm2_g, w_up, conv_w, conv_b, w_down, final_norm_g, loss_target, m_meta_tokens, m_norm1_g, m_w_in, m_w_gk, m_b_gk, m_gla_norm_g, m_w_a, m_w_pool_grp, m_pool_scale, m_w_b, m_b_gates, m_w_o, m_norm2_g, m_w_up, m_conv_w, m_conv_b, m_w_down, m_final_norm_g, v_meta_tokens, v_norm1_g, v_w_in, v_w_gk, v_b_gk, v_gla_norm_g, v_w_a, v_w_pool_grp, v_pool_scale, v_w_b, v_b_gates, v_w_o, v_norm2_g, v_w_up, v_conv_w, v_conv_b, v_w_down, v_final_norm_g):
    wts = dict(meta_tokens=meta_tokens, norm1_g=norm1_g, w_in=w_in, w_gk=w_gk, b_gk=b_gk, gla_norm_g=gla_norm_g,
               w_a=w_a, w_pool_grp=w_pool_grp, pool_scale=pool_scale, w_b=w_b, b_gates=b_gates, w_o=w_o,
               norm2_g=norm2_g, w_up=w_up, conv_w=conv_w, conv_b=conv_b, w_down=w_down, final_norm_g=final_norm_g)
    mom = dict(meta_tokens=m_meta_tokens, norm1_g=m_norm1_g, w_in=m_w_in, w_gk=m_w_gk, b_gk=m_b_gk,
               gla_norm_g=m_gla_norm_g, w_a=m_w_a, w_pool_grp=m_w_pool_grp, pool_scale=m_pool_scale, w_b=m_w_b,
               b_gates=m_b_gates, w_o=m_w_o, norm2_g=m_norm2_g, w_up=m_w_up, conv_w=m_conv_w, conv_b=m_conv_b,
               w_down=m_w_down, final_norm_g=m_final_norm_g)
    var = dict(meta_tokens=v_meta_tokens, norm1_g=v_norm1_g, w_in=v_w_in, w_gk=v_w_gk, b_gk=v_b_gk,
               gla_norm_g=v_gla_norm_g, w_a=v_w_a, w_pool_grp=v_w_pool_grp, pool_scale=v_pool_scale, w_b=v_w_b,
               b_gates=v_b_gates, w_o=v_w_o, norm2_g=v_norm2_g, w_up=v_w_up, conv_w=v_conv_w, conv_b=v_conv_b,
               w_down=v_w_down, final_norm_g=v_final_norm_g)

    small_shapes = [wts[n].shape for n in SHARDED_SMALL]
    small_rows = _rows_for(small_shapes)

    transposed = ("w_in", "w_up")

    def shard3(n, a):
        if n in transposed:
            a = jnp.swapaxes(a, 1, 2)
        return a.reshape(DEPTH, -1, a.shape[-1])

    def unshard3(n, a):
        a = jnp.swapaxes(a, 1, 2) if n in transposed else a
        return a.reshape(wts[n].shape)

    cast = {(0, "w_in"): shard3("w_in", wts["w_in"])[0].astype(BF16)}
    state3 = {}

    def layer_shards(l, names):
        return [cast[l, n] for n in names]

    def tie(row, handle):
        return row + handle["token"][0:1, 0:1]

    def full_weight(n, zone):
        if n == "w_pool_grp":
            return jnp.moveaxis(zone.reshape(N_DEV, GROUPS, GDIM // N_DEV, GDIM), 0, 1).reshape(GROUPS, GDIM, GDIM)
        full = zone.reshape(-1, zone.shape[-1])
        return _permute_rows(full) if n == "w_in" else full

    groups = [("w_in",), ("w_a", "w_pool_grp", "w_b", "w_o"), ("w_up", "w_down")]
    rest = groups[0] + groups[1]
    key = {"w_pool_grp": "w_pool"}
    rows = dict(norm1_g=norm1_g, b_gk=b_gk, gla_norm_g=gla_norm_g, pool_scale=pool_scale, b_gates=b_gates,
                norm2_g=norm2_g, conv_b=conv_b)

    def gather(l, names, after, name, head=()):
        return exchange_start(list(head) + layer_shards(l, names), ["gather"] * (len(head) + len(names)), after,
                              name=name + "_start")

    def landed(handle, after, name, names, w_layer):
        zones = exchange_wait(handle, after, name=name + "_wait")
        for n, z in zip(names, zones[len(zones) - len(names):]):
            w_layer[key.get(n, n)] = full_weight(n, z)
        return zones

    wl = [{n: v[l:l + 1] for n, v in rows.items()} for l in range(DEPTH)]
    g_in0 = gather(0, groups[0], x, "gather_in0", head=[_pack([wts[n] for n in SHARDED_SMALL], small_rows)])
    zero = g_in0["token"][0, 0]
    target = jnp.pad(loss_target[0] + zero, ((X0, 0), (0, 0)))
    for l in range(DEPTH):
        for n in BIG:
            if (l, n) not in cast:
                cast[l, n] = (shard3(n, wts[n])[l] + zero).astype(BF16)
    for n in BIG:
        state3[n] = tuple(shard3(n, a[n]) + zero if n in transposed else shard3(n, a[n]) for a in (wts, mom, var))
    early = [target] + [cast[l, n] for l in range(DEPTH) for n in BIG if (l, n) != (0, "w_in")]
    early += [a for n in transposed for a in state3[n]]
    zones = landed(g_in0, (g_in0["token"], *early), "gather_in0", groups[0], wl[0])
    small_slots = [jnp.stack(parts) for parts in zip(*[_unpack(zones[0][i], small_shapes) for i in range(N_DEV)])]
    small_full = {n: _from_slots(slots, CUT_AXIS[n]) for n, slots in zip(SHARDED_SMALL, small_slots)}
    w_gk_pad = jnp.pad(small_full["w_gk"], ((0, 0), (0, LANES - RANK), (0, 0))).astype(BF16)
    for l in range(DEPTH):
        wl[l]["w_gk"] = w_gk_pad[l]
        wl[l]["conv_w"] = small_full["conv_w"][l]
    g_mix0 = gather(0, groups[1], zones[1], "gather_mix0")
    g_ffn0 = gather(0, groups[2], g_mix0["token"], "gather_ffn0")
    wl[0]["norm1_g"] = tie(wl[0]["norm1_g"], g_ffn0)

    h = jnp.concatenate([jnp.zeros((PAD, D), F32), small_full["meta_tokens"], x[0]], axis=0)
    s0 = {"h": h}
    fwd_in(s0, wl[0], "l0_")
    zones = landed(g_mix0, s0["p"], "gather_mix0", groups[1], wl[0])
    g_in1 = gather(1, groups[0], zones[0], "gather_in1")
    wl[0]["b_gk"] = tie(wl[0]["b_gk"], g_in1)
    fwd_mixer(s0, wl[0], "l0_")
    zones = landed(g_ffn0, s0["h2"], "gather_ffn0", groups[2], wl[0])
    g_mix1 = gather(1, groups[1], zones[0], "gather_mix1")
    g_ffn1 = gather(1, groups[2], g_mix1["token"], "gather_ffn1")
    wl[0]["norm2_g"] = tie(wl[0]["norm2_g"], g_ffn1)
    fwd_ffn(s0, wl[0], "l0_")
    landed(g_in1, s0["h3"], "gather_in1", groups[0], wl[1])
    s1 = {"h": s0["h3"]}
    fwd_in(s1, wl[1], "l1_")
    landed(g_mix1, s1["p"], "gather_mix1", groups[1], wl[1])
    fwd_mixer(s1, wl[1], "l1_")
    landed(g_ffn1, s1["h2"], "gather_ffn1", groups[2], wl[1])
    fwd_ffn(s1, wl[1], "l1_")
    dh, dh_b, dgf, loss_rows = loss_head(s1["h3"], final_norm_g[None], target, name="loss_head")
    loss_part = 0.5 * jnp.sum(loss_rows) / D

    def blocks(n, gw):
        if n == "w_in":
            gw = _unpermute_rows(gw)
        if n == "w_pool_grp":
            gw = gw.astype(BF16).reshape(GROUPS, N_DEV, GDIM // N_DEV, GDIM)
            return jnp.moveaxis(gw, 1, 0).reshape(N_DEV, GROUPS * GDIM // N_DEV, GDIM)
        return gw.reshape(N_DEV, gw.shape[0] // N_DEV, gw.shape[1])

    def scatter(g, names, after, name):
        return exchange_start([blocks(n, g[key.get(n, n)]) for n in names], ["scatter"] * len(names), after,
                              name=name + "_start")

    g1, g0 = {}, {}
    dh2, dh2_b = bwd_ffn(dh, dh_b, s1, wl[1], g1, "l1_")
    s_ffn1 = scatter(g1, groups[2], dh2, "scatter_ffn1")
    dp = bwd_mixer(dh2_b, s1, wl[1], g1, "l1_", after=s_ffn1["token"])
    g1["w_in"] = bwd_in_w(dp, s1, "l1_")
    s_rest1 = scatter(g1, rest, s_ffn1["token"], "scatter_rest1")
    dh, dh_b = bwd_in_x(dp, dh2, s1, wl[1], g1, "l1_", after=s_rest1["token"])
    dh2, dh2_b = bwd_ffn(dh, dh_b, s0, wl[0], g0, "l0_")
    half, done = {}, {}

    def adam(l, names, received):
        for n, parts in zip(names, received):
            w3, m3, v3 = state3[n]
            if l == 1:
                half[n] = reduce_adam_layer(parts, w3, m3, v3, 1, None, name="adam_l1_" + n)
            else:
                done[n] = reduce_adam_layer(parts, w3, m3, v3, 0, half[n], name="adam_l0_" + n)

    r_ffn1 = exchange_wait(s_ffn1, dh2, name="scatter_ffn1_wait")
    adam(1, groups[2], r_ffn1)
    s_ffn0 = scatter(g0, groups[2], r_ffn1[0], "scatter_ffn0")
    dp = bwd_mixer(dh2_b, s0, wl[0], g0, "l0_", after=s_ffn0["token"])
    r_rest1 = exchange_wait(s_rest1, dp, name="scatter_rest1_wait")
    adam(1, rest, r_rest1)
    r_ffn0 = exchange_wait(s_ffn0, r_rest1[0], name="scatter_ffn0_wait")
    adam(0, groups[2], r_ffn0)
    g0["w_in"] = bwd_in_w(dp, s0, "l0_")
    s_rest0 = scatter(g0, rest, r_ffn0[0], "scatter_rest0")
    updated = [half[n][0] for n in rest] + [done[n][0] for n in groups[2]]
    dh, _ = bwd_in_x(dp, dh2, s0, wl[0], g0, "l0_", after=(s_rest0["token"], *updated))
    r_rest0 = exchange_wait(s_rest0, dh, name="scatter_rest0_wait")
    adam(0, rest, r_rest0)
    grad_x = dh[X0:]
    grads, delta, new_m, new_v = {}, {}, {}, {}
    for n in BIG:
        grads[n], delta[n], new_m[n], new_v[n] = [unshard3(n, o) for o in done[n]]

    g_full = {n: jnp.stack([g0[n], g1[n]])[:, 0] for n in rows}
    g_full["final_norm_g"] = dgf[0]
    g_full["meta_tokens"] = dh[PAD:X0]
    g_full["w_gk"] = jnp.stack([g0["w_gk"], g1["w_gk"]])[:, :RANK]
    g_full["conv_w"] = jnp.stack([g0["conv_w"], g1["conv_w"]])
    rep_shapes = [wts[n].shape for n in REPLICATED] + [(1,)]
    rep_rows = _rows_for(rep_shapes)
    small_blocks = jnp.stack([
        _pack([_to_slots(g_full[n], CUT_AXIS[n])[i] for n in SHARDED_SMALL], small_rows) for i in range(N_DEV)])
    rep_pack = _pack([g_full[n] for n in REPLICATED] + [loss_part.reshape(1)], rep_rows)
    received = exchange([small_blocks, rep_pack], ["scatter", "gather"], name="exchange_small")
    outs = reduce_adam(received[-2], _pack([wts[n] for n in SHARDED_SMALL], small_rows),
                       _pack([mom[n] for n in SHARDED_SMALL], small_rows),
                       _pack([var[n] for n in SHARDED_SMALL], small_rows), name="adam_small")
    for d, o in zip((grads, delta, new_m, new_v), outs):
        for n, a in zip(SHARDED_SMALL, _unpack(o, small_shapes)):
            d[n] = a
    one = [jnp.zeros((1,), F32)]
    outs = reduce_adam(received[-1], _pack([wts[n] for n in REPLICATED] + one, rep_rows),
                       _pack([mom[n] for n in REPLICATED] + one, rep_rows),
                       _pack([var[n] for n in REPLICATED] + one, rep_rows), name="adam_replicated")
    for d, o in zip((grads, delta, new_m, new_v), outs):
        for n, a in zip(REPLICATED + ("loss",), _unpack(o, rep_shapes)):
            d[n] = a
    loss = grads["loss"][0]
    return (loss, grad_x[None], *[grads[n] for n in WEIGHTS], *[delta[n] for n in WEIGHTS],
            *[new_m[n] for n in WEIGHTS], *[new_v[n] for n in WEIGHTS])
```

```python
import jax
import jax.numpy as jnp
from jax import lax
from jax.experimental import pallas as pl
from jax.experimental.pallas import tpu as pltpu

F32 = jnp.float32
BF16 = jnp.bfloat16

D = 1024
DEPTH = 2
N_META = 16
HEADS = 4
DK = 512
DV = 1024
HK = 128
HV = 256
RANK = 16
TAU = 16.0
CHUNK = 64
POOL_WINDOWS = (2, 4, 8, 16)
GROUPS = 4
GDIM = 256
D_FF = 2816
F2 = 2 * D_FF
EPS = 1e-6
IN_WIDTH = 6160
LR, B1, B2, ADAM_EPS, WD, STEP = 0.001, 0.9, 0.999, 1e-8, 0.01, 10

N_DEV = 8
PAD = CHUNK - N_META
X0 = CHUNK
IN_R = 6272
C_Q, C_K, C_V, C_R, C_U, C_GA, C_GB, C_GLR = 0, 512, 1024, 2048, 3072, 4096, 5120, 6144
VMEM_LIMIT = 56 * 1024 * 1024
LANES = 128


def _params(sem=None):
    return pltpu.CompilerParams(dimension_semantics=sem, vmem_limit_bytes=VMEM_LIMIT)


def _pick(n, prefs):
    for t in prefs:
        if n % t == 0:
            return t
    raise ValueError(f"no tile for {n} in {prefs}")


MM_VMEM_BUDGET = 44 * 1024 * 1024
MM_TILES = (2752, 1376, 688, 192, 128, 64)
EW_TILES = (688, 192, 128, 64)


def _row_tile(lp, row_bytes=0, fixed_bytes=0):
    for t in MM_TILES:
        if lp % t == 0 and (t * row_bytes + fixed_bytes <= MM_VMEM_BUDGET or t <= EW_TILES[0]):
            return t
    raise ValueError(f"no row tile for {lp}")


def _ew_tile(lp, cap=None):
    return _pick(lp, [t for t in EW_TILES if cap is None or t <= cap])


def _after(after):
    if after is None:
        return []
    return list(after) if isinstance(after, (tuple, list)) else [after]


def _sigmoid(x):
    return 1.0 / (1.0 + jnp.exp(-x))


def _dot(a, b, dims):
    return lax.dot_general(a, b, (dims, ((), ())), preferred_element_type=F32)


def _nn(a, b):
    return _dot(a, b, ((1,), (0,)))


def _nt(a, b):
    return _dot(a, b, ((1,), (1,)))


def _tn(a, b):
    return _dot(a, b, ((0,), (0,)))


def mm_nn(a, b, *, out_dtype=BF16, tn=None, res=None, after=None, halves=False, name):
    m, k = (a.shape[1], 2 * a.shape[2]) if halves else a.shape
    n = b.shape[1]
    tn = tn or n
    has_res = res is not None
    out_bytes = jnp.dtype(out_dtype).itemsize
    tm = _row_tile(m, 4 * k + 2 * tn * out_bytes + (8 * tn if has_res else 0), 4 * k * tn)
    extra = _after(after)

    def body(*refs):
        a_ref, b_ref = refs[:2]
        o_ref = refs[-1]
        if has_res:
            r_ref = refs[2]
        if halves:
            acc = _nn(a_ref[0], b_ref[pl.ds(0, k // 2), :]) + _nn(a_ref[1], b_ref[pl.ds(k // 2, k // 2), :])
        else:
            acc = _nn(a_ref[...], b_ref[...])
        if has_res:
            row = pl.program_id(1) * tm + lax.broadcasted_iota(jnp.int32, (tm, 1), 0)
            acc = jnp.where(row >= PAD, acc + r_ref[...], 0.0)
        o_ref[...] = acc.astype(o_ref.dtype)

    a_spec = (pl.BlockSpec((2, tm, k // 2), lambda j, i: (0, i, 0)) if halves
              else pl.BlockSpec((tm, k), lambda j, i: (i, 0)))
    in_specs = [a_spec, pl.BlockSpec((k, tn), lambda j, i: (0, j))]
    args = [a, b]
    if has_res:
        in_specs.append(pl.BlockSpec((tm, tn), lambda j, i: (i, j)))
        args.append(res)
    in_specs += [pl.BlockSpec(memory_space=pl.ANY)] * len(extra)
    args += extra
    return pl.pallas_call(
        body, name=name, grid=(n // tn, m // tm), in_specs=in_specs,
        out_specs=pl.BlockSpec((tm, tn), lambda j, i: (i, j)),
        out_shape=jax.ShapeDtypeStruct((m, n), out_dtype),
        compiler_params=_params(("parallel", "parallel")))(*args)


def mm_nt(a, b, *, out_dtype=BF16, tn=None, tk=None, after=None, halves=False, name):
    m, k = (a.shape[1], 2 * a.shape[2]) if halves else a.shape
    n = b.shape[0]
    tn = tn or n
    tk = tk or k
    nk = k // tk
    tm = _row_tile(m, 4 * tk + 2 * tn * jnp.dtype(out_dtype).itemsize + (4 * tn if nk > 1 else 0), 4 * tn * tk)
    extra = _after(after)
    if halves:
        per = nk // 2
        a_spec = pl.BlockSpec((None, tm, tk), lambda j, i, kk: (kk // per, i, kk % per))
    else:
        a_spec = pl.BlockSpec((tm, tk), lambda j, i, kk: (i, kk))

    def body(a_ref, b_ref, *rest):
        o_ref, acc_ref = rest[-2:]
        kk = pl.program_id(2)
        part = _nt(a_ref[...], b_ref[...])
        if nk == 1:
            o_ref[...] = part.astype(o_ref.dtype)
            return

        @pl.when(kk == 0)
        def _():
            acc_ref[...] = part

        @pl.when(kk > 0)
        def _():
            acc_ref[...] += part

        @pl.when(kk == nk - 1)
        def _():
            o_ref[...] = acc_ref[...].astype(o_ref.dtype)

    return pl.pallas_call(
        body, name=name, grid=(n // tn, m // tm, nk),
        in_specs=[a_spec, pl.BlockSpec((tn, tk), lambda j, i, kk: (j, kk))]
                 + [pl.BlockSpec(memory_space=pl.ANY)] * len(extra),
        out_specs=pl.BlockSpec((tm, tn), lambda j, i, kk: (i, j)),
        out_shape=jax.ShapeDtypeStruct((m, n), out_dtype),
        scratch_shapes=[pltpu.VMEM((tm, tn) if nk > 1 else (8, LANES), F32)],
        compiler_params=_params(("parallel", "parallel", "arbitrary")))(a, b, *extra)


def mm_tn(a, b, *, tk1=None, tn=None, out_dtype=F32, after=None, halves=False, name):
    m, k1 = (a.shape[1], 2 * a.shape[2]) if halves else a.shape
    n = b.shape[1]
    tk1 = tk1 or k1
    tn = tn or n
    tm = _row_tile(m, 4 * tk1 + 4 * tn, tk1 * tn * (4 + 2 * jnp.dtype(out_dtype).itemsize))
    nm = m // tm
    extra = _after(after)
    if halves:
        per = k1 // tk1 // 2
        a_spec = pl.BlockSpec((None, tm, tk1), lambda p, j, i: (p // per, i, p % per))
    else:
        a_spec = pl.BlockSpec((tm, tk1), lambda p, j, i: (i, p))

    def body(a_ref, b_ref, *rest):
        o_ref, acc_ref = rest[-2:]
        i = pl.program_id(2)
        part = _tn(a_ref[...], b_ref[...])

        @pl.when(i == 0)
        def _():
            acc_ref[...] = part

        @pl.when(i > 0)
        def _():
            acc_ref[...] += part

        @pl.when(i == nm - 1)
        def _():
            o_ref[...] = acc_ref[...].astype(o_ref.dtype)

    return pl.pallas_call(
        body, name=name, grid=(k1 // tk1, n // tn, nm),
        in_specs=[a_spec, pl.BlockSpec((tm, tn), lambda p, j, i: (i, j))]
                 + [pl.BlockSpec(memory_space=pl.ANY)] * len(extra),
        out_specs=pl.BlockSpec((tk1, tn), lambda p, j, i: (p, j)),
        out_shape=jax.ShapeDtypeStruct((k1, n), out_dtype),
        scratch_shapes=[pltpu.VMEM((tk1, tn), F32)],
        compiler_params=_params(("parallel", "parallel", "arbitrary")))(a, b, *extra)


def pool_mm_fwd(pooled, wp, scale, *, name):
    m = pooled.shape[0]
    tm = _row_tile(m)

    def body(a_ref, w_ref, s_ref, y0_ref, y1_ref):
        acc = _nn(a_ref[...], w_ref[...])
        y0_ref[...] = acc.astype(BF16)
        y1_ref[...] = (acc * s_ref[...]).astype(BF16)

    blk = pl.BlockSpec((tm, GDIM), lambda g, i: (i, g))
    return pl.pallas_call(
        body, name=name, grid=(GROUPS, m // tm),
        in_specs=[blk, pl.BlockSpec((None, GDIM, GDIM), lambda g, i: (g, 0, 0)),
                  pl.BlockSpec((1, GDIM), lambda g, i: (0, g))],
        out_specs=[blk, blk],
        out_shape=[jax.ShapeDtypeStruct((m, D), BF16)] * 2,
        compiler_params=_params(("parallel", "parallel")))(pooled, wp, scale)


def pool_mm_bwd_x(dy0, wp, *, name):
    m = dy0.shape[0]
    tm = _row_tile(m)

    def body(a_ref, w_ref, o_ref):
        o_ref[...] = _nt(a_ref[...], w_ref[...]).astype(BF16)

    blk = pl.BlockSpec((tm, GDIM), lambda g, i: (i, g))
    return pl.pallas_call(
        body, name=name, grid=(GROUPS, m // tm),
        in_specs=[blk, pl.BlockSpec((None, GDIM, GDIM), lambda g, i: (g, 0, 0))],
        out_specs=blk, out_shape=jax.ShapeDtypeStruct((m, D), BF16),
        compiler_params=_params(("parallel", "parallel")))(dy0, wp)


def pool_mm_bwd_w(pooled, dy0, *, name):
    m = pooled.shape[0]
    tm = _row_tile(m)

    def body(a_ref, b_ref, o_ref):
        part = _tn(a_ref[...], b_ref[...])

        @pl.when(pl.program_id(1) == 0)
        def _():
            o_ref[...] = part

        @pl.when(pl.program_id(1) > 0)
        def _():
            o_ref[...] += part

    blk = pl.BlockSpec((tm, GDIM), lambda g, i: (i, g))
    return pl.pallas_call(
        body, name=name, grid=(GROUPS, m // tm), in_specs=[blk, blk],
        out_specs=pl.BlockSpec((None, GDIM, GDIM), lambda g, i: (g, 0, 0)),
        out_shape=jax.ShapeDtypeStruct((GROUPS, GDIM, GDIM), F32),
        compiler_params=_params(("parallel", "arbitrary")))(pooled, dy0)


def rmsnorm_fwd(x, g, *, name):
    m = x.shape[0]
    tm = _ew_tile(m)

    def body(x_ref, g_ref, o_ref):
        xv = x_ref[...]
        r = lax.rsqrt(jnp.mean(xv * xv, axis=-1, keepdims=True) + EPS)
        o_ref[...] = (xv * r * g_ref[...]).astype(BF16)

    return pl.pallas_call(
        body, name=name, grid=(m // tm,),
        in_specs=[pl.BlockSpec((tm, D), lambda i: (i, 0)), pl.BlockSpec((1, D), lambda i: (0, 0))],
        out_specs=pl.BlockSpec((tm, D), lambda i: (i, 0)),
        out_shape=jax.ShapeDtypeStruct((m, D), BF16),
        compiler_params=_params(("parallel",)))(x, g)


def rmsnorm_bwd(dy, x, g, dres, *, name):
    m = x.shape[0]
    tm = _ew_tile(m)

    def body(dy_ref, x_ref, g_ref, r_ref, dx_ref, dxb_ref, dg_ref):
        i = pl.program_id(0)
        xv = x_ref[...]
        dyv = dy_ref[...].astype(F32)
        r = lax.rsqrt(jnp.mean(xv * xv, axis=-1, keepdims=True) + EPS)
        xh = xv * r
        dxh = dyv * g_ref[...]
        dx = r * (dxh - xh * jnp.mean(dxh * xh, axis=-1, keepdims=True))
        row = i * tm + lax.broadcasted_iota(jnp.int32, (tm, 1), 0)
        dx = jnp.where(row >= PAD, dx + r_ref[...], 0.0)
        dx_ref[...] = dx
        dxb_ref[...] = dx.astype(BF16)

        @pl.when(i == 0)
        def _():
            dg_ref[...] = jnp.zeros_like(dg_ref)

        dg_ref[...] += jnp.sum(dyv * xh, axis=0, keepdims=True)

    blk = pl.BlockSpec((tm, D), lambda i: (i, 0))
    vec = pl.BlockSpec((1, D), lambda i: (0, 0))
    return pl.pallas_call(
        body, name=name, grid=(m // tm,), in_specs=[blk, blk, vec, blk],
        out_specs=[blk, blk, vec],
        out_shape=[jax.ShapeDtypeStruct((m, D), F32), jax.ShapeDtypeStruct((m, D), BF16),
                   jax.ShapeDtypeStruct((1, D), F32)],
        compiler_params=_params(("arbitrary",)))(dy, x, g, dres)


def loss_head(h, gf, target, *, name):
    m = h.shape[0]
    t = _ew_tile(m)
    inv_d = 1.0 / D

    def body(h_ref, g_ref, t_ref, dh_ref, dhb_ref, dg_ref, ls_ref):
        i = pl.program_id(0)

        @pl.when(i == 0)
        def _():
            dg_ref[...] = jnp.zeros_like(dg_ref)
            ls_ref[...] = jnp.zeros_like(ls_ref)

        real = i * t + lax.broadcasted_iota(jnp.int32, (t, 1), 0) >= X0
        xv = h_ref[...]
        r = lax.rsqrt(jnp.mean(xv * xv, axis=-1, keepdims=True) + EPS)
        xh = xv * r
        err = jnp.where(real, xh * g_ref[...] - t_ref[...], 0.0)
        ls_ref[...] += jnp.sum(err * err, axis=0, keepdims=True)
        dy = err * inv_d
        dg_ref[...] += jnp.sum(dy * xh, axis=0, keepdims=True)
        dxh = dy * g_ref[...]
        dh = r * (dxh - xh * jnp.mean(dxh * xh, axis=-1, keepdims=True))
        dh_ref[...] = dh
        dhb_ref[...] = dh.astype(BF16)

    blk = pl.BlockSpec((t, D), lambda i: (i, 0))
    vec = pl.BlockSpec((1, D), lambda i: (0, 0))
    return pl.pallas_call(
        body, name=name, grid=(m // t,),
        in_specs=[blk, vec, blk],
        out_specs=[blk, blk, vec, vec],
        out_shape=[jax.ShapeDtypeStruct((m, D), F32), jax.ShapeDtypeStruct((m, D), BF16),
                   jax.ShapeDtypeStruct((1, D), F32), jax.ShapeDtypeStruct((1, D), F32)],
        compiler_params=_params(("arbitrary",)))(h, gf, target)


def _split3(x):
    x1 = x.astype(BF16)
    r1 = x - x1.astype(F32)
    x2 = r1.astype(BF16)
    x3 = (r1 - x2.astype(F32)).astype(BF16)
    return x1, x2, x3


def _tri_mm(tri, x):
    x1, x2, x3 = _split3(x)
    return _nn(tri, x1) + _nn(tri, x2) + _nn(tri, x3)


def _log_decay(glr, wgk, bgk, row0, rows):
    z = _nn(glr, wgk) + bgk
    la = (jnp.minimum(z, 0.0) - jnp.log(1.0 + jnp.exp(-jnp.abs(z)))) * (1.0 / TAU)
    row = row0 + lax.broadcasted_iota(jnp.int32, (rows, 1), 0)
    return z, jnp.where(row >= PAD, la, 0.0)


def _chunk_group(n_chunks):
    return _pick(n_chunks, (3, 2, 1))


def gla_fwd(p, wgk, bgk, *, name):
    m = p.shape[0]
    n_chunks = m // CHUNK
    cg = _chunk_group(n_chunks)
    t = cg * CHUNK
    scale = HK ** -0.5

    def body(q_ref, k_ref, v_ref, glr_ref, wgk_ref, bgk_ref, o_ref, st_ref, state):
        i = pl.program_id(0)

        @pl.when(i == 0)
        def _():
            state[...] = jnp.zeros_like(state)

        _, la = _log_decay(glr_ref[...], wgk_ref[...], bgk_ref[...], i * t, t)
        ri = lax.broadcasted_iota(jnp.int32, (CHUNK, CHUNK), 0)
        ci = lax.broadcasted_iota(jnp.int32, (CHUNK, CHUNK), 1)
        causal = ri >= ci
        tri = causal.astype(BF16)
        carried = [state[h] for h in range(HEADS)]
        for c in range(cg):
            rows = pl.ds(c * CHUNK, CHUNK)
            b = _tri_mm(tri, la[c * CHUNK:(c + 1) * CHUNK])
            bl = b[CHUNK - 1:CHUNK, :]
            q = q_ref[rows, :].astype(F32) * scale
            k = k_ref[rows, :].astype(F32)
            qd = (q * jnp.exp(b)).astype(BF16)
            ki = (k * jnp.exp(-b)).astype(BF16)
            ke = (k * jnp.exp(bl - b)).astype(BF16)
            dec = jnp.exp(bl)
            for h in range(HEADS):
                ks = slice(h * HK, (h + 1) * HK)
                vs = pl.ds(h * HV, HV)
                vh = v_ref[rows, vs]
                s_t = carried[h]
                st_ref[c, h] = s_t
                att = jnp.where(causal, _nt(qd[:, ks], ki[:, ks]), 0.0).astype(BF16)
                o_ref[rows, vs] = _nn(att, vh) + _nt(qd[:, ks], s_t.astype(BF16))
                carried[h] = s_t * dec[:, ks] + _tn(vh, ke[:, ks])
        for h in range(HEADS):
            state[h] = carried[h]

    return pl.pallas_call(
        body, name=name, grid=(n_chunks // cg,),
        in_specs=[pl.BlockSpec((t, DK), lambda i: (i, C_Q // DK)),
                  pl.BlockSpec((t, DK), lambda i: (i, C_K // DK)),
                  pl.BlockSpec((t, DV), lambda i: (i, C_V // DV)),
                  pl.BlockSpec((t, LANES), lambda i: (i, C_GLR // LANES)),
                  pl.BlockSpec((LANES, DK), lambda i: (0, 0)),
                  pl.BlockSpec((1, DK), lambda i: (0, 0))],
        out_specs=[pl.BlockSpec((t, DV), lambda i: (i, 0)),
                   pl.BlockSpec((cg, HEADS, HV, HK), lambda i: (i, 0, 0, 0))],
        out_shape=[jax.ShapeDtypeStruct((m, DV), F32),
                   jax.ShapeDtypeStruct((n_chunks, HEADS, HV, HK), F32)],
        scratch_shapes=[pltpu.VMEM((HEADS, HV, HK), F32)],
        compiler_params=_params(("arbitrary",)))(p, p, p, p, wgk, bgk)


def gla_bwd(p, wgk, bgk, st, do, dp, *, name):
    m = p.shape[0]
    n_chunks = m // CHUNK
    cg = _chunk_group(n_chunks)
    t = cg * CHUNK
    ns = n_chunks // cg
    scale = HK ** -0.5

    def body(q_ref, k_ref, v_ref, glr_ref, wgk_ref, bgk_ref, st_ref, do_ref, dp_in,
             dqkv_ref, dglr_ref, dwgk_ref, dbgk_ref, dstate):
        i = pl.program_id(0)
        blk = ns - 1 - i

        @pl.when(i == 0)
        def _():
            dstate[...] = jnp.zeros_like(dstate)
            dwgk_ref[...] = jnp.zeros_like(dwgk_ref)
            dbgk_ref[...] = jnp.zeros_like(dbgk_ref)

        z, la = _log_decay(glr_ref[...], wgk_ref[...], bgk_ref[...], blk * t, t)
        ri = lax.broadcasted_iota(jnp.int32, (CHUNK, CHUNK), 0)
        ci = lax.broadcasted_iota(jnp.int32, (CHUNK, CHUNK), 1)
        causal = ri >= ci
        tri = causal.astype(BF16)
        tri_u = (ri <= ci).astype(BF16)
        carried = [dstate[h] for h in range(HEADS)]
        dz_parts = [None] * cg
        for c in reversed(range(cg)):
            rows = pl.ds(c * CHUNK, CHUNK)
            b = _tri_mm(tri, la[c * CHUNK:(c + 1) * CHUNK])
            bl = b[CHUNK - 1:CHUNK, :]
            eb = jnp.exp(b)
            enb = jnp.exp(-b)
            ebl = jnp.exp(bl - b)
            dec = jnp.exp(bl)
            q = q_ref[rows, :].astype(F32) * scale
            k = k_ref[rows, :].astype(F32)
            qd32 = q * eb
            ki32 = k * enb
            ke32 = k * ebl
            qd = qd32.astype(BF16)
            ki = ki32.astype(BF16)
            ke = ke32.astype(BF16)
            dqd_parts, dki_parts, dke_parts, ddec_parts = [], [], [], []
            for h in range(HEADS):
                ks = slice(h * HK, (h + 1) * HK)
                vs = pl.ds(h * HV, HV)
                vh = v_ref[rows, vs]
                doh = do_ref[rows, vs].astype(BF16)
                s_t = st_ref[c, h]
                ds_t = carried[h]
                ds_b = ds_t.astype(BF16)
                att = jnp.where(causal, _nt(qd[:, ks], ki[:, ks]), 0.0).astype(BF16)
                datt = jnp.where(causal, _nt(doh, vh), 0.0).astype(BF16)
                dvh = _tn(att, doh) + _nt(ke[:, ks], ds_b)
                dqkv_ref[rows, pl.ds(2 * DK + h * HV, HV)] = dvh.astype(BF16)
                dqd_parts.append(_nn(datt, ki[:, ks]) + _nn(doh, s_t.astype(BF16)))
                dki_parts.append(_tn(datt, qd[:, ks]))
                dke_parts.append(_nn(vh, ds_b))
                ddec_parts.append(jnp.sum(s_t * ds_t, axis=0, keepdims=True))
                carried[h] = _tn(doh, qd[:, ks]) + ds_t * dec[:, ks]
            dqd = jnp.concatenate(dqd_parts, axis=1)
            dki = jnp.concatenate(dki_parts, axis=1)
            dke = jnp.concatenate(dke_parts, axis=1)
            ddec = jnp.concatenate(ddec_parts, axis=1)
            dqkv_ref[rows, pl.ds(0, DK)] = (dqd * eb * scale).astype(BF16)
            dqkv_ref[rows, pl.ds(DK, DK)] = (dki * enb + dke * ebl).astype(BF16)
            dke_ke = dke * ke32
            db = dqd * qd32 - dki * ki32 - dke_ke
            dbl = jnp.sum(dke_ke, axis=0, keepdims=True) + ddec * dec
            dg = _tri_mm(tri_u, db) + dbl
            row = blk * t + c * CHUNK + lax.broadcasted_iota(jnp.int32, (CHUNK, 1), 0)
            zc = z[c * CHUNK:(c + 1) * CHUNK]
            dz = jnp.where(row >= PAD, dg * (1.0 / TAU) * _sigmoid(-zc), 0.0)
            dz_parts[c] = dz
        for h in range(HEADS):
            dstate[h] = carried[h]
        dz_all = jnp.concatenate(dz_parts, axis=0)
        dz_b = dz_all.astype(BF16)
        dbgk_ref[...] += jnp.sum(dz_all, axis=0, keepdims=True)
        dglr_ref[...] = _nt(dz_b, wgk_ref[...]).astype(BF16)
        dwgk_ref[...] += _tn(glr_ref[...], dz_b)

    rev = lambda i: ns - 1 - i
    return pl.pallas_call(
        body, name=name, grid=(ns,),
        in_specs=[pl.BlockSpec((t, DK), lambda i: (rev(i), C_Q // DK)),
                  pl.BlockSpec((t, DK), lambda i: (rev(i), C_K // DK)),
                  pl.BlockSpec((t, DV), lambda i: (rev(i), C_V // DV)),
                  pl.BlockSpec((t, LANES), lambda i: (rev(i), C_GLR // LANES)),
                  pl.BlockSpec((LANES, DK), lambda i: (0, 0)),
                  pl.BlockSpec((1, DK), lambda i: (0, 0)),
                  pl.BlockSpec((cg, HEADS, HV, HK), lambda i: (rev(i), 0, 0, 0)),
                  pl.BlockSpec((t, DV), lambda i: (rev(i), 0)), pl.BlockSpec(memory_space=pl.ANY)],
        out_specs=[pl.BlockSpec((t, 2 * DK + DV), lambda i: (rev(i), 0)),
                   pl.BlockSpec((t, LANES), lambda i: (rev(i), 0)),
                   pl.BlockSpec((LANES, DK), lambda i: (0, 0)),
                   pl.BlockSpec((1, DK), lambda i: (0, 0))],
        out_shape=[jax.ShapeDtypeStruct((m, IN_R), BF16),
                   jax.ShapeDtypeStruct((m, LANES), BF16),
                   jax.ShapeDtypeStruct((LANES, DK), F32),
                   jax.ShapeDtypeStruct((1, DK), F32)],
        input_output_aliases={8: 0},
        scratch_shapes=[pltpu.VMEM((HEADS, HV, HK), F32)],
        compiler_params=_params(("arbitrary",)))(p, p, p, p, wgk, bgk, st, do, dp)


def place_glr(dp, dglr, *, name):
    m = dp.shape[0]
    tm = _ew_tile(m)

    def body(dp_in, g_ref, o_ref):
        o_ref[...] = g_ref[...]

    return pl.pallas_call(
        body, name=name, grid=(m // tm,),
        in_specs=[pl.BlockSpec(memory_space=pl.ANY), pl.BlockSpec((tm, LANES), lambda i: (i, 0))],
        out_specs=pl.BlockSpec((tm, LANES), lambda i: (i, C_GLR // LANES)),
        out_shape=jax.ShapeDtypeStruct((m, IN_R), BF16), input_output_aliases={0: 0},
        compiler_params=_params(("parallel",)))(dp, dglr)


HALO = 16


def _shift_down(xx, s):
    return pltpu.roll(xx, s, 0)


def _shift_up(xx, s):
    return pltpu.roll(xx, xx.shape[0] - s, 0)


def mix_pre(o, p, gn, *, name):
    m = o.shape[0]
    tm = _ew_tile(m)

    def body(o_ref, r_ref, u_ref, gn_ref, ya_ref, pooled_ref, halo):
        i = pl.program_id(0)

        @pl.when(i == 0)
        def _():
            halo[...] = jnp.zeros_like(halo)

        rv = r_ref[...].astype(F32)
        silu_r = rv * _sigmoid(rv)
        for h in range(HEADS):
            cs = pl.ds(h * HV, HV)
            ov = o_ref[:, cs]
            rs = lax.rsqrt(jnp.mean(ov * ov, axis=-1, keepdims=True) + EPS)
            ya_ref[:, cs] = (ov * rs * gn_ref[...] * silu_r[:, h * HV:(h + 1) * HV]).astype(BF16)

        row = i * tm + lax.broadcasted_iota(jnp.int32, (tm, 1), 0)
        pos1 = jnp.maximum(row - PAD + 1, 1).astype(F32)
        for g, w in enumerate(POOL_WINDOWS):
            cs = pl.ds(g * GDIM, GDIM)
            uv = u_ref[:, cs].astype(F32)
            xx = jnp.concatenate([halo[:, cs], uv], axis=0)
            s = xx
            span = 1
            while span < w:
                s = s + _shift_down(s, span)
                span *= 2
            inv = 1.0 / jnp.minimum(pos1, float(w))
            pooled_ref[:, cs] = (s[HALO:] * inv - uv).astype(BF16)
            halo[:, cs] = uv[tm - HALO:]

    blk = pl.BlockSpec((tm, D), lambda i: (i, 0))
    return pl.pallas_call(
        body, name=name, grid=(m // tm,),
        in_specs=[blk, pl.BlockSpec((tm, D), lambda i: (i, C_R // D)),
                  pl.BlockSpec((tm, D), lambda i: (i, C_U // D)),
                  pl.BlockSpec((1, HV), lambda i: (0, 0))],
        out_specs=[blk, blk],
        out_shape=[jax.ShapeDtypeStruct((m, D), BF16)] * 2,
        scratch_shapes=[pltpu.VMEM((HALO, D), F32)],
        compiler_params=_params(("arbitrary",)))(o, p, p, gn)


def mix_pre_bwd(dya, dpooled, o, p, gn, dp, *, name):
    m = o.shape[0]
    tm = _ew_tile(m)
    nt = m // tm

    def body(dya_ref, dpl_ref, o_ref, r_ref, gn_ref, dp_in, do_ref, dp_ref, dgn_ref, halo):
        i = pl.program_id(0)
        blk_i = nt - 1 - i

        @pl.when(i == 0)
        def _():
            halo[...] = jnp.zeros_like(halo)
            dgn_ref[...] = jnp.zeros_like(dgn_ref)

        rv = r_ref[...].astype(F32)
        sg = _sigmoid(rv)
        silu_r = rv * sg
        dsilu = sg * (1.0 + rv * (1.0 - sg))
        dgn = jnp.zeros((1, HV), F32)
        for h in range(HEADS):
            cs = pl.ds(h * HV, HV)
            hs = slice(h * HV, (h + 1) * HV)
            ov = o_ref[:, cs]
            dy = dya_ref[:, cs].astype(F32)
            rs = lax.rsqrt(jnp.mean(ov * ov, axis=-1, keepdims=True) + EPS)
            xh = ov * rs
            on = xh * gn_ref[...]
            don = dy * silu_r[:, hs]
            dp_ref[:, cs] = (dy * on * dsilu[:, hs]).astype(BF16)
            dxh = don * gn_ref[...]
            do_ref[:, cs] = rs * (dxh - xh * jnp.mean(dxh * xh, axis=-1, keepdims=True))
            dgn = dgn + jnp.sum(don * xh, axis=0, keepdims=True)
        dgn_ref[...] += dgn

        row = blk_i * tm + lax.broadcasted_iota(jnp.int32, (tm, 1), 0)
        pos1 = jnp.maximum(row - PAD + 1, 1).astype(F32)
        for g, w in enumerate(POOL_WINDOWS):
            cs = pl.ds(g * GDIM, GDIM)
            dpv = dpl_ref[:, cs].astype(F32)
            e = dpv * (1.0 / jnp.minimum(pos1, float(w)))
            xx = jnp.concatenate([e, halo[:, cs]], axis=0)
            s = xx
            span = 1
            while span < w:
                s = s + _shift_up(s, span)
                span *= 2
            dp_ref[:, pl.ds(D + g * GDIM, GDIM)] = (s[:tm] - dpv).astype(BF16)
            halo[:, cs] = e[:HALO]

    rev = lambda i: nt - 1 - i
    blk = pl.BlockSpec((tm, D), lambda i: (rev(i), 0))
    return pl.pallas_call(
        body, name=name, grid=(nt,),
        in_specs=[blk, blk, blk, pl.BlockSpec((tm, D), lambda i: (rev(i), C_R // D)),
                  pl.BlockSpec((1, HV), lambda i: (0, 0)), pl.BlockSpec(memory_space=pl.ANY)],
        out_specs=[blk, pl.BlockSpec((tm, 2 * D), lambda i: (rev(i), C_R // (2 * D))),
                   pl.BlockSpec((1, HV), lambda i: (0, 0))],
        out_shape=[jax.ShapeDtypeStruct((m, D), F32), jax.ShapeDtypeStruct((m, IN_R), BF16),
                   jax.ShapeDtypeStruct((1, HV), F32)],
        input_output_aliases={5: 1},
        scratch_shapes=[pltpu.VMEM((HALO, D), F32)],
        compiler_params=_params(("arbitrary",)))(dya, dpooled, o, p, gn, dp)


def merge_fwd(p, ya, yb, bg, *, name):
    m = ya.shape[0]
    tm = _ew_tile(m)

    def body(ga_ref, gb_ref, ya_ref, yb_ref, ba_ref, bb_ref, o_ref):
        gate_a = _sigmoid(ga_ref[...].astype(F32) + ba_ref[...])
        gate_b = _sigmoid(gb_ref[...].astype(F32) + bb_ref[...])
        o_ref[...] = (gate_a * ya_ref[...].astype(F32) + gate_b * yb_ref[...].astype(F32)).astype(BF16)

    blk = pl.BlockSpec((tm, D), lambda i: (i, 0))
    return pl.pallas_call(
        body, name=name, grid=(m // tm,),
        in_specs=[pl.BlockSpec((tm, D), lambda i: (i, C_GA // D)),
                  pl.BlockSpec((tm, D), lambda i: (i, C_GB // D)), blk, blk,
                  pl.BlockSpec((1, D), lambda i: (0, 0)), pl.BlockSpec((1, D), lambda i: (0, 1))],
        out_specs=blk, out_shape=jax.ShapeDtypeStruct((m, D), BF16),
        compiler_params=_params(("parallel",)))(p, p, ya, yb, bg, bg)


def merge_bwd(dmrg, p, ya, yb, bg, *, name):
    m = ya.shape[0]
    tm = _ew_tile(m)

    def body(dm_ref, ga_ref, gb_ref, ya_ref, yb_ref, ba_ref, bb_ref,
             dya_ref, dyb_ref, dp_ref, dbg_ref):
        @pl.when(pl.program_id(0) == 0)
        def _():
            dbg_ref[...] = jnp.zeros_like(dbg_ref)

        dm = dm_ref[...].astype(F32)
        gate_a = _sigmoid(ga_ref[...].astype(F32) + ba_ref[...])
        gate_b = _sigmoid(gb_ref[...].astype(F32) + bb_ref[...])
        dya_ref[...] = (dm * gate_a).astype(BF16)
        dyb_ref[...] = (dm * gate_b).astype(BF16)
        dga = dm * ya_ref[...].astype(F32) * gate_a * (1.0 - gate_a)
        dgb = dm * yb_ref[...].astype(F32) * gate_b * (1.0 - gate_b)
        dp_ref[:, pl.ds(0, D)] = dga.astype(BF16)
        dp_ref[:, pl.ds(D, D)] = dgb.astype(BF16)
        dbg_ref[:, pl.ds(0, D)] += jnp.sum(dga, axis=0, keepdims=True)
        dbg_ref[:, pl.ds(D, D)] += jnp.sum(dgb, axis=0, keepdims=True)

    blk = pl.BlockSpec((tm, D), lambda i: (i, 0))
    return pl.pallas_call(
        body, name=name, grid=(m // tm,),
        in_specs=[blk, pl.BlockSpec((tm, D), lambda i: (i, C_GA // D)),
                  pl.BlockSpec((tm, D), lambda i: (i, C_GB // D)), blk, blk,
                  pl.BlockSpec((1, D), lambda i: (0, 0)), pl.BlockSpec((1, D), lambda i: (0, 1))],
        out_specs=[blk, blk, pl.BlockSpec((tm, 2 * D), lambda i: (i, C_GA // (2 * D))),
                   pl.BlockSpec((1, 2 * D), lambda i: (0, 0))],
        out_shape=[jax.ShapeDtypeStruct((m, D), BF16)] * 2 + [jax.ShapeDtypeStruct((m, IN_R), BF16),
                                                              jax.ShapeDtypeStruct((1, 2 * D), F32)],
        compiler_params=_params(("arbitrary",)))(dmrg, p, p, ya, yb, bg, bg)


def scale_bwd(dy1, y0, scale, *, name):
    m = y0.shape[0]
    tm = _ew_tile(m)

    def body(dy_ref, y0_ref, s_ref, o_ref, ds_ref):
        @pl.when(pl.program_id(0) == 0)
        def _():
            ds_ref[...] = jnp.zeros_like(ds_ref)

        dy = dy_ref[...].astype(F32)
        o_ref[...] = (dy * s_ref[...]).astype(BF16)
        ds_ref[...] += jnp.sum(dy * y0_ref[...].astype(F32), axis=0, keepdims=True)

    blk = pl.BlockSpec((tm, D), lambda i: (i, 0))
    vec = pl.BlockSpec((1, D), lambda i: (0, 0))
    return pl.pallas_call(
        body, name=name, grid=(m // tm,), in_specs=[blk, blk, vec], out_specs=[blk, vec],
        out_shape=[jax.ShapeDtypeStruct((m, D), BF16), jax.ShapeDtypeStruct((1, D), F32)],
        compiler_params=_params(("arbitrary",)))(dy1, y0, scale)


CONV_BLK = 1408
CONV_ROWS = 688
N_CONV_BLK = D_FF // CONV_BLK


def conv_act_fwd(up, cw, cb, *, name):
    m = up.shape[0]
    tm = _ew_tile(m, cap=CONV_ROWS)

    def conv(x_ref, halo, w_ref, b_ref):
        xv = x_ref[...].astype(F32)
        xx = jnp.concatenate([halo[...], xv], axis=0)
        y = (w_ref[2:3, :] * xx + w_ref[1:2, :] * _shift_down(xx, 1)
             + w_ref[0:1, :] * _shift_down(xx, 2))[HALO:] + b_ref[...]
        halo[...] = xv[tm - HALO:]
        return y

    def body(xa_ref, xb_ref, wa_ref, wb_ref, ba_ref, bb_ref, upc_a_ref, upc_b_ref, act_ref, halo_a, halo_b):
        @pl.when(pl.program_id(1) == 0)
        def _():
            halo_a[...] = jnp.zeros_like(halo_a)
            halo_b[...] = jnp.zeros_like(halo_b)

        a = conv(xa_ref, halo_a, wa_ref, ba_ref)
        bv = conv(xb_ref, halo_b, wb_ref, bb_ref)
        upc_a_ref[...] = a.astype(BF16)
        upc_b_ref[...] = bv.astype(BF16)
        act_ref[...] = (a * _sigmoid(a) * bv).astype(BF16)

    nb = N_CONV_BLK
    xa = pl.BlockSpec((tm, CONV_BLK), lambda j, i: (i, j))
    xb = pl.BlockSpec((tm, CONV_BLK), lambda j, i: (i, j + nb))
    return pl.pallas_call(
        body, name=name, grid=(nb, m // tm),
        in_specs=[xa, xb,
                  pl.BlockSpec((3, CONV_BLK), lambda j, i: (0, j)),
                  pl.BlockSpec((3, CONV_BLK), lambda j, i: (0, j + nb)),
                  pl.BlockSpec((1, CONV_BLK), lambda j, i: (0, j)),
                  pl.BlockSpec((1, CONV_BLK), lambda j, i: (0, j + nb))],
        out_specs=[xa, xa, xa],
        out_shape=[jax.ShapeDtypeStruct((m, D_FF), BF16)] * 3,
        scratch_shapes=[pltpu.VMEM((HALO, CONV_BLK), F32)] * 2,
        compiler_params=_params(("parallel", "arbitrary")))(up, up, cw, cw, cb, cb)


def conv_act_bwd(dact, upc_a, upc_b, up, cw, *, name):
    m = up.shape[0]
    tm = _ew_tile(m, cap=CONV_ROWS)
    nt = m // tm

    def conv_t(d, halo, x_ref, w_ref, dup_ref, half, dw_ref, db_ref):
        xx = jnp.concatenate([d, halo[...]], axis=0)
        d1 = _shift_up(xx, 1)[:tm]
        d2 = _shift_up(xx, 2)[:tm]
        dup_ref[half] = (w_ref[2:3, :] * d + w_ref[1:2, :] * d1 + w_ref[0:1, :] * d2).astype(BF16)
        xv = x_ref[...].astype(F32)
        dw_ref[2:3, :] += jnp.sum(xv * d, axis=0, keepdims=True)
        dw_ref[1:2, :] += jnp.sum(xv * d1, axis=0, keepdims=True)
        dw_ref[0:1, :] += jnp.sum(xv * d2, axis=0, keepdims=True)
        db_ref[...] += jnp.sum(d, axis=0, keepdims=True)
        halo[...] = d[:HALO]

    def body(da_ref, a_ref, b_ref, xa_ref, xb_ref, wa_ref, wb_ref,
             dup_ref, dwa_ref, dwb_ref, dba_ref, dbb_ref, halo_a, halo_b):
        @pl.when(pl.program_id(1) == 0)
        def _():
            for r in (halo_a, halo_b, dwa_ref, dwb_ref, dba_ref, dbb_ref):
                r[...] = jnp.zeros_like(r)

        dact_v = da_ref[...].astype(F32)
        a = a_ref[...].astype(F32)
        bv = b_ref[...].astype(F32)
        sg = _sigmoid(a)
        d_a = dact_v * bv * sg * (1.0 + a * (1.0 - sg))
        d_b = dact_v * a * sg
        conv_t(d_a, halo_a, xa_ref, wa_ref, dup_ref, 0, dwa_ref, dba_ref)
        conv_t(d_b, halo_b, xb_ref, wb_ref, dup_ref, 1, dwb_ref, dbb_ref)

    nb = N_CONV_BLK
    rev = lambda i: nt - 1 - i
    half = pl.BlockSpec((tm, CONV_BLK), lambda j, i: (rev(i), j))
    xa = half
    xb = pl.BlockSpec((tm, CONV_BLK), lambda j, i: (rev(i), j + nb))
    wa = pl.BlockSpec((3, CONV_BLK), lambda j, i: (0, j))
    wb = pl.BlockSpec((3, CONV_BLK), lambda j, i: (0, j + nb))
    va = pl.BlockSpec((1, CONV_BLK), lambda j, i: (0, j))
    outs = pl.pallas_call(
        body, name=name, grid=(nb, nt),
        in_specs=[half, half, half, xa, xb, wa, wb],
        out_specs=[pl.BlockSpec((2, tm, CONV_BLK), lambda j, i: (0, rev(i), j)), wa, wa, va, va],
        out_shape=[jax.ShapeDtypeStruct((2, m, D_FF), BF16)]
                  + [jax.ShapeDtypeStruct((3, D_FF), F32)] * 2
                  + [jax.ShapeDtypeStruct((1, D_FF), F32)] * 2,
        scratch_shapes=[pltpu.VMEM((HALO, CONV_BLK), F32)] * 2,
        compiler_params=_params(("parallel", "arbitrary")))(dact, upc_a, upc_b, up, up, cw, cw)
    return outs


def local_step(x, target, w):
    seq = x.shape[0]
    h = jnp.concatenate([jnp.zeros((PAD, D), F32), w["meta"], x], axis=0)
    saved = []
    for l in range(DEPTH):
        wl = {k: (v[l:l + 1] if k in ROW_PARAMS else v[l]) for k, v in w.items() if k not in ("meta", "final_norm_g")}
        s = {"h": h}
        fwd_in(s, wl, f"l{l}_")
        fwd_mixer(s, wl, f"l{l}_")
        fwd_ffn(s, wl, f"l{l}_")
        saved.append(s)
        h = s["h3"]

    dh, dh_b, dgf, loss_rows = loss_head(h, w["final_norm_g"], jnp.pad(target, ((X0, 0), (0, 0))), name="loss_head")
    g = {"final_norm_g": dgf}
    per_layer = []
    for l in reversed(range(DEPTH)):
        wl = {k: (v[l:l + 1] if k in ROW_PARAMS else v[l]) for k, v in w.items() if k not in ("meta", "final_norm_g")}
        s = saved[l]
        gl = {}
        dh2, dh2_b = bwd_ffn(dh, dh_b, s, wl, gl, f"l{l}_")
        dp = bwd_mixer(dh2_b, s, wl, gl, f"l{l}_")
        gl["w_in"] = bwd_in_w(dp, s, f"l{l}_")
        dh, dh_b = bwd_in_x(dp, dh2, s, wl, gl, f"l{l}_")
        per_layer.append(gl)
    per_layer.reverse()
    for k in per_layer[0]:
        g[k] = jnp.stack([per_layer[l][k].astype(F32) for l in range(DEPTH)])
    g["meta"] = dh[PAD:X0]
    return loss_rows, dh[X0:X0 + seq], g


ROW_PARAMS = ("norm1_g", "b_gk", "gla_norm_g", "pool_scale", "b_gates", "norm2_g", "conv_b")


def fwd_in(s, w, ln):
    s["hn1"] = rmsnorm_fwd(s["h"], w["norm1_g"], name=ln + "norm1")
    s["p"] = mm_nt(s["hn1"], w["w_in"], name=ln + "in_proj")


def fwd_mixer(s, w, ln):
    p = s["p"]
    s["o"], s["st"] = gla_fwd(p, w["w_gk"], w["b_gk"], name=ln + "gla_fwd")
    s["ya_in"], s["pooled"] = mix_pre(s["o"], p, w["gla_norm_g"], name=ln + "mix_pre")
    s["ya"] = mm_nn(s["ya_in"], w["w_a"], name=ln + "proj_a")
    s["yb0"], s["yb1"] = pool_mm_fwd(s["pooled"], w["w_pool"], w["pool_scale"], name=ln + "pool_mm")
    s["yb"] = mm_nn(s["yb1"], w["w_b"], name=ln + "proj_b")
    s["mrg"] = merge_fwd(p, s["ya"], s["yb"], w["b_gates"], name=ln + "merge")
    s["h2"] = mm_nn(s["mrg"], w["w_o"], out_dtype=F32, res=s["h"], name=ln + "proj_o")


def fwd_ffn(s, w, ln):
    s["hn2"] = rmsnorm_fwd(s["h2"], w["norm2_g"], name=ln + "norm2")
    s["up"] = mm_nt(s["hn2"], w["w_up"], tn=D_FF, name=ln + "up_proj")
    s["upc_a"], s["upc_b"], s["act"] = conv_act_fwd(s["up"], w["conv_w"], w["conv_b"], name=ln + "conv_act")
    s["h3"] = mm_nn(s["act"], w["w_down"], out_dtype=F32, res=s["h2"], name=ln + "down_proj")


def bwd_ffn(dh, dh_b, s, w, g, ln, after=None):
    dact = mm_nt(dh_b, w["w_down"], after=after, name=ln + "d_act")
    g["w_down"] = mm_tn(s["act"], dh_b, tk1=1408, out_dtype=BF16, after=after, name=ln + "dw_down")
    dup, dcw_a, dcw_b, dcb_a, dcb_b = conv_act_bwd(
        dact, s["upc_a"], s["upc_b"], s["up"], w["conv_w"], name=ln + "conv_act_bwd")
    dhn2 = mm_nn(dup, w["w_up"], out_dtype=F32, tn=512, halves=True, name=ln + "d_hn2")
    g["w_up"] = mm_tn(dup, s["hn2"], tk1=1408, out_dtype=BF16, halves=True, name=ln + "dw_up")
    dh2, dh2_b, g["norm2_g"] = rmsnorm_bwd(dhn2, s["h2"], w["norm2_g"], dh, name=ln + "norm2_bwd")
    g["conv_w"] = jnp.concatenate([dcw_a, dcw_b], axis=1)
    g["conv_b"] = jnp.concatenate([dcb_a, dcb_b], axis=1)
    return dh2, dh2_b


def bwd_mixer(dh2_b, s, w, g, ln, after=None):
    dmrg = mm_nt(dh2_b, w["w_o"], after=after, name=ln + "d_mrg")
    g["w_o"] = mm_tn(s["mrg"], dh2_b, out_dtype=BF16, after=after, name=ln + "dw_o")
    dya, dyb, dp, g["b_gates"] = merge_bwd(dmrg, s["p"], s["ya"], s["yb"], w["b_gates"], name=ln + "merge_bwd")
    dya_in = mm_nt(dya, w["w_a"], name=ln + "d_ya_in")
    g["w_a"] = mm_tn(s["ya_in"], dya, out_dtype=BF16, name=ln + "dw_a")
    dyb1 = mm_nt(dyb, w["w_b"], name=ln + "d_yb1")
    g["w_b"] = mm_tn(s["yb1"], dyb, out_dtype=BF16, name=ln + "dw_b")
    dyb0, g["pool_scale"] = scale_bwd(dyb1, s["yb0"], w["pool_scale"], name=ln + "scale_bwd")
    dpooled = pool_mm_bwd_x(dyb0, w["w_pool"], name=ln + "d_pooled")
    g["w_pool"] = pool_mm_bwd_w(s["pooled"], dyb0, name=ln + "dw_pool")
    do, dp, g["gla_norm_g"] = mix_pre_bwd(dya_in, dpooled, s["o"], s["p"], w["gla_norm_g"], dp,
                                          name=ln + "mix_pre_bwd")
    dp, dglr, g["w_gk"], g["b_gk"] = gla_bwd(s["p"], w["w_gk"], w["b_gk"], s["st"], do, dp, name=ln + "gla_bwd")
    return place_glr(dp, dglr, name=ln + "place_glr")


def bwd_in_w(dp, s, ln):
    return mm_tn(dp, s["hn1"], tk1=896, out_dtype=BF16, name=ln + "dw_in")


def bwd_in_x(dp, dh2, s, w, g, ln, after=None):
    dhn1 = mm_nn(dp, w["w_in"], out_dtype=F32, tn=512, after=after, name=ln + "d_hn1")
    dh, dh_b, g["norm1_g"] = rmsnorm_bwd(dhn1, s["h"], w["norm1_g"], dh2, name=ln + "norm1_bwd")
    return dh, dh_b


def _my_place():
    return lax.axis_index("x"), lax.axis_index("y"), lax.axis_index("c")


def _peer(place, k):
    x, y, c = place
    return (1 - x if k & 4 else x, 1 - y if k & 2 else y, 1 - c if k & 1 else c)


def _index(place):
    x, y, c = place
    return 4 * x + 2 * y + c


def exchange(arrays, kinds, *, name):
    n = len(arrays)

    def body(*refs):
        ins, outs = refs[:n], refs[n:2 * n]
        send_sems, recv_sems, local_sems = refs[2 * n:]
        place = _my_place()
        me = _index(place)

        def src(a, dest):
            return ins[a] if kinds[a] == "gather" else ins[a].at[dest]

        def remote(a, k):
            peer = _peer(place, k)
            return pltpu.make_async_remote_copy(
                src_ref=src(a, _index(peer)), dst_ref=outs[a].at[me],
                send_sem=send_sems.at[a, k - 1], recv_sem=recv_sems.at[a, k - 1],
                device_id=peer, device_id_type=pl.DeviceIdType.MESH)

        def arrival(a, k):
            peer = _peer(place, k)
            return pltpu.make_async_remote_copy(
                src_ref=src(a, me), dst_ref=outs[a].at[_index(peer)],
                send_sem=send_sems.at[a, k - 1], recv_sem=recv_sems.at[a, k - 1],
                device_id=peer, device_id_type=pl.DeviceIdType.MESH)

        own = [pltpu.make_async_copy(src(a, me), outs[a].at[me], local_sems.at[a]) for a in range(n)]
        sends = [remote(a, k) for k in range(1, N_DEV) for a in range(n)]
        for cp in sends:
            cp.start()
        for cp in own:
            cp.start()
        for k in range(1, N_DEV):
            for a in range(n):
                arrival(a, k).wait_recv()
        for cp in sends:
            cp.wait_send()
        for cp in own:
            cp.wait()

    any_spec = pl.BlockSpec(memory_space=pl.ANY)
    out_shape = []
    for arr, kind in zip(arrays, kinds):
        shape = arr.shape if kind == "gather" else arr.shape[1:]
        out_shape.append(jax.ShapeDtypeStruct((N_DEV,) + tuple(shape), arr.dtype))
    return pl.pallas_call(
        body, name=name, in_specs=[any_spec] * n, out_specs=[any_spec] * n, out_shape=out_shape,
        scratch_shapes=[pltpu.SemaphoreType.DMA((n, N_DEV - 1)), pltpu.SemaphoreType.DMA((n, N_DEV - 1)),
                        pltpu.SemaphoreType.DMA((n,))],
    )(*arrays)


def _sem_slot(a, k):
    return a * (N_DEV - 1) + k - 1


_HBM = pl.BlockSpec(memory_space=pltpu.HBM)
_SEM = pl.BlockSpec(memory_space=pltpu.SEMAPHORE)
_DATAFLOW = pltpu.SideEffectType.DATAFLOW_SIDE_EFFECTING


def exchange_start(arrays, kinds, after, *, name):
    n = len(arrays)
    zones = []
    for arr, kind in zip(arrays, kinds):
        shape = arr.shape if kind == "gather" else arr.shape[1:]
        zones.append(lax.empty((N_DEV,) + tuple(shape), arr.dtype))

    def body(*refs):
        ins, lands = refs[:n], refs[n:2 * n]
        send_sems, recv_sems = refs[2 * n + 1], refs[2 * n + 2]
        token = refs[4 * n + 3]
        place = _my_place()
        me = _index(place)
        for a in range(n):
            for k in range(1, N_DEV):
                peer = _peer(place, k)
                pltpu.make_async_remote_copy(
                    src_ref=ins[a] if kinds[a] == "gather" else ins[a].at[_index(peer)], dst_ref=lands[a].at[me],
                    send_sem=send_sems.at[_sem_slot(a, k)], recv_sem=recv_sems.at[_sem_slot(a, k)],
                    device_id=peer, device_id_type=pl.DeviceIdType.MESH).start()
        token[...] = jnp.zeros_like(token)

    sems = pltpu.SemaphoreType.DMA((n * (N_DEV - 1),))
    hbm = lambda a: pltpu.HBM(a.shape, a.dtype)
    outs = pl.pallas_call(
        body, name=name,
        out_shape=(sems, sems, *[hbm(a) for a in arrays], *[hbm(z) for z in zones],
                   jax.ShapeDtypeStruct((8, LANES), F32)),
        in_specs=[_HBM] * (2 * n) + [pl.BlockSpec(memory_space=pl.ANY)],
        out_specs=(_SEM, _SEM, *[_HBM] * (2 * n), pl.BlockSpec(memory_space=pltpu.VMEM)),
        input_output_aliases={i: 2 + i for i in range(2 * n)},
        compiler_params=pltpu.CompilerParams(has_side_effects=_DATAFLOW),
    )(*[pltpu.with_memory_space_constraint(a, pltpu.HBM) for a in arrays],
      *[pltpu.with_memory_space_constraint(z, pltpu.HBM) for z in zones], after)
    return dict(send=outs[0], recv=outs[1], srcs=outs[2:2 + n], zones=outs[2 + n:2 + 2 * n],
                token=outs[2 + 2 * n], kinds=kinds)


def exchange_wait(handle, after, *, name):
    kinds = handle["kinds"]
    n = len(kinds)

    def body(*refs):
        ins, lands = refs[:n], refs[n:2 * n]
        send_sems, recv_sems = refs[2 * n], refs[2 * n + 1]
        place = _my_place()
        me = _index(place)
        for a in range(n):
            for k in range(1, N_DEV):
                peer = _peer(place, k)
                src = ins[a] if kinds[a] == "gather" else ins[a].at[_index(peer)]
                copy = pltpu.make_async_remote_copy(
                    src_ref=src, dst_ref=lands[a].at[_index(peer)],
                    send_sem=send_sems.at[_sem_slot(a, k)], recv_sem=recv_sems.at[_sem_slot(a, k)],
                    device_id=peer, device_id_type=pl.DeviceIdType.MESH)
                copy.wait_send()
                copy.wait_recv()

    srcs, zones = handle["srcs"], handle["zones"]
    after = after if isinstance(after, tuple) else (after,)
    hbm = lambda a: pltpu.HBM(a.shape, a.dtype)
    outs = pl.pallas_call(
        body, name=name,
        out_shape=(*[hbm(a) for a in srcs], *[hbm(z) for z in zones]),
        in_specs=[_HBM] * (2 * n) + [_SEM, _SEM] + [pl.BlockSpec(memory_space=pl.ANY)] * len(after),
        out_specs=[_HBM] * (2 * n),
        input_output_aliases={i: i for i in range(2 * n)},
        compiler_params=pltpu.CompilerParams(has_side_effects=_DATAFLOW),
    )(*srcs, *zones, handle["send"], handle["recv"], *after)
    return _fill_own(outs[:n], outs[n:], kinds)


def _fill_own(srcs, zones, kinds):
    me = _index(_my_place())
    filled = []
    for src, zone, kind in zip(srcs, zones, kinds):
        mine = src if kind == "gather" else lax.dynamic_index_in_dim(src, me, 0, keepdims=False)
        filled.append(lax.dynamic_update_index_in_dim(zone, mine, me, 0))
    return filled


ADAM_COLS = 256


def reduce_adam_layer(parts, w, m, v, layer, prev, *, name):
    _, r, c = w.shape
    tc = ADAM_COLS

    def body(*refs):
        p_ref, w_ref, m_ref, v_ref = refs[:4]
        g_ref, d_ref, m2_ref, v2_ref = refs[-4:]
        g = p_ref[0].astype(F32)
        for i in range(1, N_DEV):
            g = g + p_ref[i].astype(F32)
        m2 = B1 * m_ref[...] + (1.0 - B1) * g
        v2 = B2 * v_ref[...] + (1.0 - B2) * (g * g)
        m_hat = m2 / (1.0 - B1 ** STEP)
        v_hat = v2 / (1.0 - B2 ** STEP)
        g_ref[...] = g
        d_ref[...] = -LR * (m_hat / (jnp.sqrt(v_hat) + ADAM_EPS) + WD * w_ref[...])
        m2_ref[...] = m2
        v2_ref[...] = v2

    blk = pl.BlockSpec((None, r, tc), lambda i: (layer, 0, i))
    in_specs = [pl.BlockSpec((N_DEV, r, tc), lambda i: (0, 0, i)), blk, blk, blk]
    args = [parts, w, m, v]
    aliases = {}
    if prev is not None:
        in_specs += [pl.BlockSpec(memory_space=pl.ANY)] * 4
        args += list(prev)
        aliases = {4 + j: j for j in range(4)}
    return pl.pallas_call(
        body, name=name, grid=(c // tc,), in_specs=in_specs, out_specs=[blk] * 4,
        out_shape=[jax.ShapeDtypeStruct(w.shape, F32)] * 4, input_output_aliases=aliases,
        compiler_params=_params(("parallel",)))(*args)


def reduce_adam(parts, w, m, v, *, name):
    r, c = w.shape
    tr = _pick(r, (256, 352, 192, 128, 72, 64, 32, 16, 8))

    def body(p_ref, w_ref, m_ref, v_ref, g_ref, d_ref, m2_ref, v2_ref):
        g = p_ref[0].astype(F32)
        for i in range(1, N_DEV):
            g = g + p_ref[i].astype(F32)
        wv = w_ref[...]
        m2 = B1 * m_ref[...] + (1.0 - B1) * g
        v2 = B2 * v_ref[...] + (1.0 - B2) * (g * g)
        m_hat = m2 / (1.0 - B1 ** STEP)
        v_hat = v2 / (1.0 - B2 ** STEP)
        g_ref[...] = g
        d_ref[...] = -LR * (m_hat / (jnp.sqrt(v_hat) + ADAM_EPS) + WD * wv)
        m2_ref[...] = m2
        v2_ref[...] = v2

    blk = pl.BlockSpec((tr, c), lambda i: (i, 0))
    return pl.pallas_call(
        body, name=name, grid=(r // tr,),
        in_specs=[pl.BlockSpec((N_DEV, tr, c), lambda i: (0, i, 0)), blk, blk, blk],
        out_specs=[blk] * 4, out_shape=[jax.ShapeDtypeStruct((r, c), F32)] * 4,
        compiler_params=_params(("parallel",)))(parts, w, m, v)


BIG = ("w_in", "w_a", "w_pool_grp", "w_b", "w_o", "w_up", "w_down")
SHARDED_SMALL = ("meta_tokens", "w_gk", "conv_w")
REPLICATED = ("norm1_g", "b_gk", "gla_norm_g", "pool_scale", "b_gates", "norm2_g", "conv_b", "final_norm_g")
CUT_AXIS = {"w_in": 2, "w_a": 1, "w_pool_grp": 2, "w_b": 1, "w_o": 1, "w_up": 2, "w_down": 1,
            "meta_tokens": 1, "w_gk": 2, "conv_w": 2}
WEIGHTS = ("meta_tokens", "norm1_g", "w_in", "w_gk", "b_gk", "gla_norm_g", "w_a", "w_pool_grp", "pool_scale",
           "w_b", "b_gates", "w_o", "norm2_g", "w_up", "conv_w", "conv_b", "w_down", "final_norm_g")


def _as_2d(a):
    return a.reshape(-1, a.shape[-1])


def _from_slots(slots, axis):
    full = jnp.moveaxis(slots, 0, axis)
    shape = list(full.shape)
    shape[axis:axis + 2] = [shape[axis] * shape[axis + 1]]
    return full.reshape(shape)


def _to_slots(full, axis):
    shape = list(full.shape)
    shape[axis:axis + 1] = [N_DEV, shape[axis] // N_DEV]
    return jnp.moveaxis(full.reshape(shape), axis, 0)


def _pack(vectors, rows):
    flat = jnp.concatenate([v.reshape(-1).astype(F32) for v in vectors])
    return jnp.pad(flat, (0, rows * LANES - flat.shape[0])).reshape(rows, LANES)


def _unpack(packed, shapes):
    flat = packed.reshape(-1)
    out, off = [], 0
    for s in shapes:
        size = 1
        for d in s:
            size *= d
        out.append(flat[off:off + size].reshape(s))
        off += size
    return out


def _rows_for(shapes, mult=8):
    total = 0
    for s in shapes:
        size = 1
        for d in s:
            size *= d
        total += size
    rows = -(-total // LANES)
    return -(-rows // mult) * mult


def _permute_rows(w_t):
    pad = jnp.zeros((IN_R - IN_WIDTH,) + w_t.shape[1:], w_t.dtype)
    return jnp.concatenate([w_t[:2048], w_t[2064:], w_t[2048:2064], pad], axis=0)


def _unpermute_rows(w_r):
    return jnp.concatenate([w_r[:2048], w_r[C_GLR:C_GLR + RANK], w_r[2048:C_GLR]], axis=0)


def kernel(x, meta_tokens, norm1_g, w_in, w_gk, b_gk, gla_norm_g, w_a, w_pool_grp, pool_scale, w_b, b_gates, w_o, norm2_g, w_up, conv_w, conv_b, w_down, final_norm_g, loss_target, m_meta_tokens, m_norm1_g, m_w_in, m_w_gk, m_b_gk, m_gla_norm_g, m_w_a, m_w_pool_grp, m_pool_scale, m_w_b, m_b_gates, m_w_o, m_norm2_g, m_w_up, m_conv_w, m_conv_b, m_w_down, m_final_norm_g, v_meta_tokens, v_norm1_g, v_w_in, v_w_gk, v_b_gk, v_gla_norm_g, v_w_a, v_w_pool_grp, v_pool_scale, v_w_b, v_b_gates, v_w_o, v_norm2_g, v_w_up, v_conv_w, v_conv_b, v_w_down, v_final_norm_g):
    wts = dict(meta_tokens=meta_tokens, norm1_g=norm1_g, w_in=w_in, w_gk=w_gk, b_gk=b_gk, gla_norm_g=gla_norm_g,
               w_a=w_a, w_pool_grp=w_pool_grp, pool_scale=pool_scale, w_b=w_b, b_gates=b_gates, w_o=w_o,
               norm2_g=norm2_g, w_up=w_up, conv_w=conv_w, conv_b=conv_b, w_down=w_down, final_norm_g=final_norm_g)
    mom = dict(meta_tokens=m_meta_tokens, norm1_g=m_norm1_g, w_in=m_w_in, w_gk=m_w_gk, b_gk=m_b_gk,
               gla_norm_g=m_gla_norm_g, w_a=m_w_a, w_pool_grp=m_w_pool_grp, pool_scale=m_pool_scale, w_b=m_w_b,
               b_gates=m_b_gates, w_o=m_w_o, norm2_g=m_norm2_g, w_up=m_w_up, conv_w=m_conv_w, conv_b=m_conv_b,
               w_down=m_w_down, final_norm_g=m_final_norm_g)
    var = dict(meta_tokens=v_meta_tokens, norm1_g=v_norm1_g, w_in=v_w_in, w_gk=v_w_gk, b_gk=v_b_gk,
               gla_norm_g=v_gla_norm_g, w_a=v_w_a, w_pool_grp=v_w_pool_grp, pool_scale=v_pool_scale, w_b=v_w_b,
               b_gates=v_b_gates, w_o=v_w_o, norm2_g=v_norm2_g, w_up=v_w_up, conv_w=v_conv_w, conv_b=v_conv_b,
               w_down=v_w_down, final_norm_g=v_final_norm_g)

    small_shapes = [wts[n].shape for n in SHARDED_SMALL]
    small_rows = _rows_for(small_shapes)

    transposed = ("w_in", "w_up")

    def shard3(n, a):
        if n in transposed:
            a = jnp.swapaxes(a, 1, 2)
        return a.reshape(DEPTH, -1, a.shape[-1])

    def unshard3(n, a):
        a = jnp.swapaxes(a, 1, 2) if n in transposed else a
        return a.reshape(wts[n].shape)

    cast = {(0, "w_in"): shard3("w_in", wts["w_in"])[0].astype(BF16)}
    state3 = {}

    def layer_shards(l, names):
        return [cast[l, n] for n in names]

    def tie(row, handle):
        return row + handle["token"][0:1, 0:1]

    def full_weight(n, zone):
        if n == "w_pool_grp":
            return jnp.moveaxis(zone.reshape(N_DEV, GROUPS, GDIM // N_DEV, GDIM), 0, 1).reshape(GROUPS, GDIM, GDIM)
        full = zone.reshape(-1, zone.shape[-1])
        return _permute_rows(full) if n == "w_in" else full

    groups = [("w_in",), ("w_a", "w_pool_grp", "w_b", "w_o"), ("w_up", "w_down")]
    rest = groups[0] + groups[1]
    key = {"w_pool_grp": "w_pool"}
    rows = dict(norm1_g=norm1_g, b_gk=b_gk, gla_norm_g=gla_norm_g, pool_scale=pool_scale, b_gates=b_gates,
                norm2_g=norm2_g, conv_b=conv_b)

    def gather(l, names, after, name, head=()):
        return exchange_start(list(head) + layer_shards(l, names), ["gather"] * (len(head) + len(names)), after,
                              name=name + "_start")

    def landed(handle, after, name, names, w_layer):
        zones = exchange_wait(handle, after, name=name + "_wait")
        for n, z in zip(names, zones[len(zones) - len(names):]):
            w_layer[key.get(n, n)] = full_weight(n, z)
        return zones

    wl = [{n: v[l:l + 1] for n, v in rows.items()} for l in range(DEPTH)]
    g_in0 = gather(0, groups[0], x, "gather_in0", head=[_pack([wts[n] for n in SHARDED_SMALL], small_rows)])
    zero = g_in0["token"][0, 0]
    target = jnp.pad(loss_target[0] + zero, ((X0, 0), (0, 0)))
    for l in range(DEPTH):
        for n in BIG:
            if (l, n) not in cast:
                cast[l, n] = (shard3(n, wts[n])[l] + zero).astype(BF16)
    for n in BIG:
        state3[n] = tuple(shard3(n, a[n]) + zero if n in transposed else shard3(n, a[n]) for a in (wts, mom, var))
    frame = jnp.pad(x[0] + zero, ((X0, 0), (0, 0)))
    early = [target, frame] + [cast[l, n] for l in range(DEPTH) for n in BIG if (l, n) != (0, "w_in")]
    early += [a for n in transposed for a in state3[n]]
    zones = landed(g_in0, (g_in0["token"], *early), "gather_in0", groups[0], wl[0])
    small_slots = [jnp.stack(parts) for parts in zip(*[_unpack(zones[0][i], small_shapes) for i in range(N_DEV)])]
    small_full = {n: _from_slots(slots, CUT_AXIS[n]) for n, slots in zip(SHARDED_SMALL, small_slots)}
    w_gk_pad = jnp.pad(small_full["w_gk"], ((0, 0), (0, LANES - RANK), (0, 0))).astype(BF16)
    for l in range(DEPTH):
        wl[l]["w_gk"] = w_gk_pad[l]
        wl[l]["conv_w"] = small_full["conv_w"][l]
    g_mix0 = gather(0, groups[1], zones[1], "gather_mix0")
    g_ffn0 = gather(0, groups[2], g_mix0["token"], "gather_ffn0")
    wl[0]["norm1_g"] = tie(wl[0]["norm1_g"], g_ffn0)

    h = lax.dynamic_update_slice(frame, small_full["meta_tokens"], (PAD, 0))
    s0 = {"h": h}
    fwd_in(s0, wl[0], "l0_")
    zones = landed(g_mix0, s0["p"], "gather_mix0", groups[1], wl[0])
    g_in1 = gather(1, groups[0], zones[0], "gather_in1")
    wl[0]["b_gk"] = tie(wl[0]["b_gk"], g_in1)
    fwd_mixer(s0, wl[0], "l0_")
    zones = landed(g_ffn0, s0["h2"], "gather_ffn0", groups[2], wl[0])
    g_mix1 = gather(1, groups[1], zones[0], "gather_mix1")
    g_ffn1 = gather(1, groups[2], g_mix1["token"], "gather_ffn1")
    wl[0]["norm2_g"] = tie(wl[0]["norm2_g"], g_ffn1)
    fwd_ffn(s0, wl[0], "l0_")
    landed(g_in1, s0["h3"], "gather_in1", groups[0], wl[1])
    s1 = {"h": s0["h3"]}
    fwd_in(s1, wl[1], "l1_")
    landed(g_mix1, s1["p"], "gather_mix1", groups[1], wl[1])
    fwd_mixer(s1, wl[1], "l1_")
    landed(g_ffn1, s1["h2"], "gather_ffn1", groups[2], wl[1])
    fwd_ffn(s1, wl[1], "l1_")
    dh, dh_b, dgf, loss_rows = loss_head(s1["h3"], final_norm_g[None], target, name="loss_head")
    loss_part = 0.5 * jnp.sum(loss_rows) / D

    def blocks(n, gw):
        if n == "w_in":
            gw = _unpermute_rows(gw)
        if n == "w_pool_grp":
            gw = gw.astype(BF16).reshape(GROUPS, N_DEV, GDIM // N_DEV, GDIM)
            return jnp.moveaxis(gw, 1, 0).reshape(N_DEV, GROUPS * GDIM // N_DEV, GDIM)
        return gw.reshape(N_DEV, gw.shape[0] // N_DEV, gw.shape[1])

    def scatter(g, names, after, name):
        return exchange_start([blocks(n, g[key.get(n, n)]) for n in names], ["scatter"] * len(names), after,
                              name=name + "_start")

    g1, g0 = {}, {}
    dh2, dh2_b = bwd_ffn(dh, dh_b, s1, wl[1], g1, "l1_")
    s_ffn1 = scatter(g1, groups[2], dh2, "scatter_ffn1")
    dp = bwd_mixer(dh2_b, s1, wl[1], g1, "l1_", after=s_ffn1["token"])
    g1["w_in"] = bwd_in_w(dp, s1, "l1_")
    s_rest1 = scatter(g1, rest, s_ffn1["token"], "scatter_rest1")
    dh, dh_b = bwd_in_x(dp, dh2, s1, wl[1], g1, "l1_", after=s_rest1["token"])
    dh2, dh2_b = bwd_ffn(dh, dh_b, s0, wl[0], g0, "l0_")
    half, done = {}, {}

    def adam(l, names, received):
        for n, parts in zip(names, received):
            w3, m3, v3 = state3[n]
            if l == 1:
                half[n] = reduce_adam_layer(parts, w3, m3, v3, 1, None, name="adam_l1_" + n)
            else:
                done[n] = reduce_adam_layer(parts, w3, m3, v3, 0, half[n], name="adam_l0_" + n)

    r_ffn1 = exchange_wait(s_ffn1, dh2, name="scatter_ffn1_wait")
    adam(1, groups[2], r_ffn1)
    s_ffn0 = scatter(g0, groups[2], r_ffn1[0], "scatter_ffn0")
    dp = bwd_mixer(dh2_b, s0, wl[0], g0, "l0_", after=s_ffn0["token"])
    r_rest1 = exchange_wait(s_rest1, dp, name="scatter_rest1_wait")
    adam(1, rest, r_rest1)
    r_ffn0 = exchange_wait(s_ffn0, r_rest1[0], name="scatter_ffn0_wait")
    adam(0, groups[2], r_ffn0)
    g0["w_in"] = bwd_in_w(dp, s0, "l0_")
    s_rest0 = scatter(g0, rest, r_ffn0[0], "scatter_rest0")
    updated = [half[n][0] for n in rest] + [done[n][0] for n in groups[2]]
    dh, _ = bwd_in_x(dp, dh2, s0, wl[0], g0, "l0_", after=(s_rest0["token"], *updated))
    grad_x = dh[X0:]

    g_full = {n: jnp.stack([g0[n], g1[n]])[:, 0] for n in rows}
    g_full["final_norm_g"] = dgf[0]
    g_full["meta_tokens"] = dh[PAD:X0]
    g_full["w_gk"] = jnp.stack([g0["w_gk"], g1["w_gk"]])[:, :RANK]
    g_full["conv_w"] = jnp.stack([g0["conv_w"], g1["conv_w"]])
    rep_shapes = [wts[n].shape for n in REPLICATED] + [(1,)]
    rep_rows = _rows_for(rep_shapes)
    small_blocks = jnp.stack([
        _pack([_to_slots(g_full[n], CUT_AXIS[n])[i] for n in SHARDED_SMALL], small_rows) for i in range(N_DEV)])
    rep_pack = _pack([g_full[n] for n in REPLICATED] + [loss_part.reshape(1)], rep_rows)
    r_rest0 = exchange_wait(s_rest0, (grad_x, small_blocks, rep_pack), name="scatter_rest0_wait")
    adam(0, rest, r_rest0)
    grads, delta, new_m, new_v = {}, {}, {}, {}
    for n in BIG:
        grads[n], delta[n], new_m[n], new_v[n] = [unshard3(n, o) for o in done[n]]
    received = exchange([small_blocks, rep_pack], ["scatter", "gather"], name="exchange_small")
    outs = reduce_adam(received[-2], _pack([wts[n] for n in SHARDED_SMALL], small_rows),
                       _pack([mom[n] for n in SHARDED_SMALL], small_rows),
                       _pack([var[n] for n in SHARDED_SMALL], small_rows), name="adam_small")
    for d, o in zip((grads, delta, new_m, new_v), outs):
        for n, a in zip(SHARDED_SMALL, _unpack(o, small_shapes)):
            d[n] = a
    one = [jnp.zeros((1,), F32)]
    outs = reduce_adam(received[-1], _pack([wts[n] for n in REPLICATED] + one, rep_rows),
                       _pack([mom[n] for n in REPLICATED] + one, rep_rows),
                       _pack([var[n] for n in REPLICATED] + one, rep_rows), name="adam_replicated")
    for d, o in zip((grads, delta, new_m, new_v), outs):
        for n, a in zip(REPLICATED + ("loss",), _unpack(o, rep_shapes)):
            d[n] = a
    loss = grads["loss"][0]
    return (loss, grad_x[None], *[grads[n] for n in WEIGHTS], *[delta[n] for n in WEIGHTS],
            *[new_m[n] for n in WEIGHTS], *[new_v[n] for n in WEIGHTS])
```

```python
import jax
import jax.numpy as jnp
from jax import lax
from jax.experimental import pallas as pl
from jax.experimental.pallas import tpu as pltpu

F32 = jnp.float32
BF16 = jnp.bfloat16

D = 1024
DEPTH = 2
N_META = 16
HEADS = 4
DK = 512
DV = 1024
HK = 128
HV = 256
RANK = 16
TAU = 16.0
CHUNK = 64
POOL_WINDOWS = (2, 4, 8, 16)
GROUPS = 4
GDIM = 256
D_FF = 2816
F2 = 2 * D_FF
EPS = 1e-6
IN_WIDTH = 6160
LR, B1, B2, ADAM_EPS, WD, STEP = 0.001, 0.9, 0.999, 1e-8, 0.01, 10

N_DEV = 8
PAD = CHUNK - N_META
X0 = CHUNK
IN_R = 6272
C_Q, C_K, C_V, C_R, C_U, C_GA, C_GB, C_GLR = 0, 512, 1024, 2048, 3072, 4096, 5120, 6144
VMEM_LIMIT = 56 * 1024 * 1024
LANES = 128


def _params(sem=None):
    return pltpu.CompilerParams(dimension_semantics=sem, vmem_limit_bytes=VMEM_LIMIT)


def _pick(n, prefs):
    for t in prefs:
        if n % t == 0:
            return t
    raise ValueError(f"no tile for {n} in {prefs}")


MM_VMEM_BUDGET = 44 * 1024 * 1024
MM_TILES = (2752, 1376, 688, 192, 128, 64)
EW_TILES = (688, 192, 128, 64)


def _row_tile(lp, row_bytes=0, fixed_bytes=0):
    for t in MM_TILES:
        if lp % t == 0 and (t * row_bytes + fixed_bytes <= MM_VMEM_BUDGET or t <= EW_TILES[0]):
            return t
    raise ValueError(f"no row tile for {lp}")


def _ew_tile(lp, cap=None):
    return _pick(lp, [t for t in EW_TILES if cap is None or t <= cap])


def _after(after):
    if after is None:
        return []
    return list(after) if isinstance(after, (tuple, list)) else [after]


def _sigmoid(x):
    return 1.0 / (1.0 + jnp.exp(-x))


def _dot(a, b, dims):
    return lax.dot_general(a, b, (dims, ((), ())), preferred_element_type=F32)


def _nn(a, b):
    return _dot(a, b, ((1,), (0,)))


def _nt(a, b):
    return _dot(a, b, ((1,), (1,)))


def _tn(a, b):
    return _dot(a, b, ((0,), (0,)))


def mm_nn(a, b, *, out_dtype=BF16, tn=None, res=None, after=None, halves=False, name):
    m, k = (a.shape[1], 2 * a.shape[2]) if halves else a.shape
    n = b.shape[1]
    tn = tn or n
    has_res = res is not None
    out_bytes = jnp.dtype(out_dtype).itemsize
    tm = _row_tile(m, 4 * k + 2 * tn * out_bytes + (8 * tn if has_res else 0), 4 * k * tn)
    extra = _after(after)

    def body(*refs):
        a_ref, b_ref = refs[:2]
        o_ref = refs[-1]
        if has_res:
            r_ref = refs[2]
        if halves:
            acc = _nn(a_ref[0], b_ref[pl.ds(0, k // 2), :]) + _nn(a_ref[1], b_ref[pl.ds(k // 2, k // 2), :])
        else:
            acc = _nn(a_ref[...], b_ref[...])
        if has_res:
            row = pl.program_id(1) * tm + lax.broadcasted_iota(jnp.int32, (tm, 1), 0)
            acc = jnp.where(row >= PAD, acc + r_ref[...], 0.0)
        o_ref[...] = acc.astype(o_ref.dtype)

    a_spec = (pl.BlockSpec((2, tm, k // 2), lambda j, i: (0, i, 0)) if halves
              else pl.BlockSpec((tm, k), lambda j, i: (i, 0)))
    in_specs = [a_spec, pl.BlockSpec((k, tn), lambda j, i: (0, j))]
    args = [a, b]
    if has_res:
        in_specs.append(pl.BlockSpec((tm, tn), lambda j, i: (i, j)))
        args.append(res)
    in_specs += [pl.BlockSpec(memory_space=pl.ANY)] * len(extra)
    args += extra
    return pl.pallas_call(
        body, name=name, grid=(n // tn, m // tm), in_specs=in_specs,
        out_specs=pl.BlockSpec((tm, tn), lambda j, i: (i, j)),
        out_shape=jax.ShapeDtypeStruct((m, n), out_dtype),
        compiler_params=_params(("parallel", "parallel")))(*args)


def mm_nt(a, b, *, out_dtype=BF16, tn=None, tk=None, after=None, halves=False, name):
    m, k = (a.shape[1], 2 * a.shape[2]) if halves else a.shape
    n = b.shape[0]
    tn = tn or n
    tk = tk or k
    nk = k // tk
    tm = _row_tile(m, 4 * tk + 2 * tn * jnp.dtype(out_dtype).itemsize + (4 * tn if nk > 1 else 0), 4 * tn * tk)
    extra = _after(after)
    if halves:
        per = nk // 2
        a_spec = pl.BlockSpec((None, tm, tk), lambda j, i, kk: (kk // per, i, kk % per))
    else:
        a_spec = pl.BlockSpec((tm, tk), lambda j, i, kk: (i, kk))

    def body(a_ref, b_ref, *rest):
        o_ref, acc_ref = rest[-2:]
        kk = pl.program_id(2)
        part = _nt(a_ref[...], b_ref[...])
        if nk == 1:
            o_ref[...] = part.astype(o_ref.dtype)
            return

        @pl.when(kk == 0)
        def _():
            acc_ref[...] = part

        @pl.when(kk > 0)
        def _():
            acc_ref[...] += part

        @pl.when(kk == nk - 1)
        def _():
            o_ref[...] = acc_ref[...].astype(o_ref.dtype)

    return pl.pallas_call(
        body, name=name, grid=(n // tn, m // tm, nk),
        in_specs=[a_spec, pl.BlockSpec((tn, tk), lambda j, i, kk: (j, kk))]
                 + [pl.BlockSpec(memory_space=pl.ANY)] * len(extra),
        out_specs=pl.BlockSpec((tm, tn), lambda j, i, kk: (i, j)),
        out_shape=jax.ShapeDtypeStruct((m, n), out_dtype),
        scratch_shapes=[pltpu.VMEM((tm, tn) if nk > 1 else (8, LANES), F32)],
        compiler_params=_params(("parallel", "parallel", "arbitrary")))(a, b, *extra)


def mm_tn(a, b, *, tk1=None, tn=None, out_dtype=F32, after=None, halves=False, name):
    m, k1 = (a.shape[1], 2 * a.shape[2]) if halves else a.shape
    n = b.shape[1]
    tk1 = tk1 or k1
    tn = tn or n
    tm = _row_tile(m, 4 * tk1 + 4 * tn, tk1 * tn * (4 + 2 * jnp.dtype(out_dtype).itemsize))
    nm = m // tm
    extra = _after(after)
    if halves:
        per = k1 // tk1 // 2
        a_spec = pl.BlockSpec((None, tm, tk1), lambda p, j, i: (p // per, i, p % per))
    else:
        a_spec = pl.BlockSpec((tm, tk1), lambda p, j, i: (i, p))

    def body(a_ref, b_ref, *rest):
        o_ref, acc_ref = rest[-2:]
        i = pl.program_id(2)
        part = _tn(a_ref[...], b_ref[...])

        @pl.when(i == 0)
        def _():
            acc_ref[...] = part

        @pl.when(i > 0)
        def _():
            acc_ref[...] += part

        @pl.when(i == nm - 1)
        def _():
            o_ref[...] = acc_ref[...].astype(o_ref.dtype)

    return pl.pallas_call(
        body, name=name, grid=(k1 // tk1, n // tn, nm),
        in_specs=[a_spec, pl.BlockSpec((tm, tn), lambda p, j, i: (i, j))]
                 + [pl.BlockSpec(memory_space=pl.ANY)] * len(extra),
        out_specs=pl.BlockSpec((tk1, tn), lambda p, j, i: (p, j)),
        out_shape=jax.ShapeDtypeStruct((k1, n), out_dtype),
        scratch_shapes=[pltpu.VMEM((tk1, tn), F32)],
        compiler_params=_params(("parallel", "parallel", "arbitrary")))(a, b, *extra)


def pool_mm_fwd(pooled, wp, scale, *, name):
    m = pooled.shape[0]
    tm = _row_tile(m)

    def body(a_ref, w_ref, s_ref, y0_ref, y1_ref):
        acc = _nn(a_ref[...], w_ref[...])
        y0_ref[...] = acc.astype(BF16)
        y1_ref[...] = (acc * s_ref[...]).astype(BF16)

    blk = pl.BlockSpec((tm, GDIM), lambda g, i: (i, g))
    return pl.pallas_call(
        body, name=name, grid=(GROUPS, m // tm),
        in_specs=[blk, pl.BlockSpec((None, GDIM, GDIM), lambda g, i: (g, 0, 0)),
                  pl.BlockSpec((1, GDIM), lambda g, i: (0, g))],
        out_specs=[blk, blk],
        out_shape=[jax.ShapeDtypeStruct((m, D), BF16)] * 2,
        compiler_params=_params(("parallel", "parallel")))(pooled, wp, scale)


def pool_mm_bwd_x(dy0, wp, *, name):
    m = dy0.shape[0]
    tm = _row_tile(m)

    def body(a_ref, w_ref, o_ref):
        o_ref[...] = _nt(a_ref[...], w_ref[...]).astype(BF16)

    blk = pl.BlockSpec((tm, GDIM), lambda g, i: (i, g))
    return pl.pallas_call(
        body, name=name, grid=(GROUPS, m // tm),
        in_specs=[blk, pl.BlockSpec((None, GDIM, GDIM), lambda g, i: (g, 0, 0))],
        out_specs=blk, out_shape=jax.ShapeDtypeStruct((m, D), BF16),
        compiler_params=_params(("parallel", "parallel")))(dy0, wp)


def pool_mm_bwd_w(pooled, dy0, *, name):
    m = pooled.shape[0]
    tm = _row_tile(m)

    def body(a_ref, b_ref, o_ref):
        part = _tn(a_ref[...], b_ref[...])

        @pl.when(pl.program_id(1) == 0)
        def _():
            o_ref[...] = part

        @pl.when(pl.program_id(1) > 0)
        def _():
            o_ref[...] += part

    blk = pl.BlockSpec((tm, GDIM), lambda g, i: (i, g))
    return pl.pallas_call(
        body, name=name, grid=(GROUPS, m // tm), in_specs=[blk, blk],
        out_specs=pl.BlockSpec((None, GDIM, GDIM), lambda g, i: (g, 0, 0)),
        out_shape=jax.ShapeDtypeStruct((GROUPS, GDIM, GDIM), F32),
        compiler_params=_params(("parallel", "arbitrary")))(pooled, dy0)


def rmsnorm_fwd(x, g, *, name):
    m = x.shape[0]
    tm = _ew_tile(m)

    def body(x_ref, g_ref, o_ref):
        xv = x_ref[...]
        r = lax.rsqrt(jnp.mean(xv * xv, axis=-1, keepdims=True) + EPS)
        o_ref[...] = (xv * r * g_ref[...]).astype(BF16)

    return pl.pallas_call(
        body, name=name, grid=(m // tm,),
        in_specs=[pl.BlockSpec((tm, D), lambda i: (i, 0)), pl.BlockSpec((1, D), lambda i: (0, 0))],
        out_specs=pl.BlockSpec((tm, D), lambda i: (i, 0)),
        out_shape=jax.ShapeDtypeStruct((m, D), BF16),
        compiler_params=_params(("parallel",)))(x, g)


def rmsnorm_bwd(dy, x, g, dres, *, name):
    m = x.shape[0]
    tm = _ew_tile(m)

    def body(dy_ref, x_ref, g_ref, r_ref, dx_ref, dxb_ref, dg_ref):
        i = pl.program_id(0)
        xv = x_ref[...]
        dyv = dy_ref[...].astype(F32)
        r = lax.rsqrt(jnp.mean(xv * xv, axis=-1, keepdims=True) + EPS)
        xh = xv * r
        dxh = dyv * g_ref[...]
        dx = r * (dxh - xh * jnp.mean(dxh * xh, axis=-1, keepdims=True))
        row = i * tm + lax.broadcasted_iota(jnp.int32, (tm, 1), 0)
        dx = jnp.where(row >= PAD, dx + r_ref[...], 0.0)
        dx_ref[...] = dx
        dxb_ref[...] = dx.astype(BF16)

        @pl.when(i == 0)
        def _():
            dg_ref[...] = jnp.zeros_like(dg_ref)

        dg_ref[...] += jnp.sum(dyv * xh, axis=0, keepdims=True)

    blk = pl.BlockSpec((tm, D), lambda i: (i, 0))
    vec = pl.BlockSpec((1, D), lambda i: (0, 0))
    return pl.pallas_call(
        body, name=name, grid=(m // tm,), in_specs=[blk, blk, vec, blk],
        out_specs=[blk, blk, vec],
        out_shape=[jax.ShapeDtypeStruct((m, D), F32), jax.ShapeDtypeStruct((m, D), BF16),
                   jax.ShapeDtypeStruct((1, D), F32)],
        compiler_params=_params(("arbitrary",)))(dy, x, g, dres)


def loss_head(h, gf, target, *, name):
    m = h.shape[0]
    t = _ew_tile(m)
    inv_d = 1.0 / D

    def body(h_ref, g_ref, t_ref, dh_ref, dhb_ref, dg_ref, ls_ref):
        i = pl.program_id(0)

        @pl.when(i == 0)
        def _():
            dg_ref[...] = jnp.zeros_like(dg_ref)
            ls_ref[...] = jnp.zeros_like(ls_ref)

        real = i * t + lax.broadcasted_iota(jnp.int32, (t, 1), 0) >= X0
        xv = h_ref[...]
        r = lax.rsqrt(jnp.mean(xv * xv, axis=-1, keepdims=True) + EPS)
        xh = xv * r
        err = jnp.where(real, xh * g_ref[...] - t_ref[...], 0.0)
        ls_ref[...] += jnp.sum(err * err, axis=0, keepdims=True)
        dy = err * inv_d
        dg_ref[...] += jnp.sum(dy * xh, axis=0, keepdims=True)
        dxh = dy * g_ref[...]
        dh = r * (dxh - xh * jnp.mean(dxh * xh, axis=-1, keepdims=True))
        dh_ref[...] = dh
        dhb_ref[...] = dh.astype(BF16)

    blk = pl.BlockSpec((t, D), lambda i: (i, 0))
    vec = pl.BlockSpec((1, D), lambda i: (0, 0))
    return pl.pallas_call(
        body, name=name, grid=(m // t,),
        in_specs=[blk, vec, blk],
        out_specs=[blk, blk, vec, vec],
        out_shape=[jax.ShapeDtypeStruct((m, D), F32), jax.ShapeDtypeStruct((m, D), BF16),
                   jax.ShapeDtypeStruct((1, D), F32), jax.ShapeDtypeStruct((1, D), F32)],
        compiler_params=_params(("arbitrary",)))(h, gf, target)


def _split3(x):
    x1 = x.astype(BF16)
    r1 = x - x1.astype(F32)
    x2 = r1.astype(BF16)
    x3 = (r1 - x2.astype(F32)).astype(BF16)
    return x1, x2, x3


def _tri_mm(tri, x):
    x1, x2, x3 = _split3(x)
    return _nn(tri, x1) + _nn(tri, x2) + _nn(tri, x3)


def _log_decay(glr, wgk, bgk, row0, rows):
    z = _nn(glr, wgk) + bgk
    la = (jnp.minimum(z, 0.0) - jnp.log(1.0 + jnp.exp(-jnp.abs(z)))) * (1.0 / TAU)
    row = row0 + lax.broadcasted_iota(jnp.int32, (rows, 1), 0)
    return z, jnp.where(row >= PAD, la, 0.0)


def _chunk_group(n_chunks):
    return _pick(n_chunks, (3, 2, 1))


def gla_fwd(p, wgk, bgk, *, name):
    m = p.shape[0]
    n_chunks = m // CHUNK
    cg = _chunk_group(n_chunks)
    t = cg * CHUNK
    scale = HK ** -0.5

    def body(q_ref, k_ref, v_ref, glr_ref, wgk_ref, bgk_ref, o_ref, st_ref, state):
        i = pl.program_id(0)

        @pl.when(i == 0)
        def _():
            state[...] = jnp.zeros_like(state)

        _, la = _log_decay(glr_ref[...], wgk_ref[...], bgk_ref[...], i * t, t)
        ri = lax.broadcasted_iota(jnp.int32, (CHUNK, CHUNK), 0)
        ci = lax.broadcasted_iota(jnp.int32, (CHUNK, CHUNK), 1)
        causal = ri >= ci
        tri = causal.astype(BF16)
        carried = [state[h] for h in range(HEADS)]
        for c in range(cg):
            rows = pl.ds(c * CHUNK, CHUNK)
            b = _tri_mm(tri, la[c * CHUNK:(c + 1) * CHUNK])
            bl = b[CHUNK - 1:CHUNK, :]
            q = q_ref[rows, :].astype(F32) * scale
            k = k_ref[rows, :].astype(F32)
            qd = (q * jnp.exp(b)).astype(BF16)
            ki = (k * jnp.exp(-b)).astype(BF16)
            ke = (k * jnp.exp(bl - b)).astype(BF16)
            dec = jnp.exp(bl)
            for h in range(HEADS):
                ks = slice(h * HK, (h + 1) * HK)
                vs = pl.ds(h * HV, HV)
                vh = v_ref[rows, vs]
                s_t = carried[h]
                st_ref[c, h] = s_t
                att = jnp.where(causal, _nt(qd[:, ks], ki[:, ks]), 0.0).astype(BF16)
                o_ref[rows, vs] = _nn(att, vh) + _nt(qd[:, ks], s_t.astype(BF16))
                carried[h] = s_t * dec[:, ks] + _tn(vh, ke[:, ks])
        for h in range(HEADS):
            state[h] = carried[h]

    return pl.pallas_call(
        body, name=name, grid=(n_chunks // cg,),
        in_specs=[pl.BlockSpec((t, DK), lambda i: (i, C_Q // DK)),
                  pl.BlockSpec((t, DK), lambda i: (i, C_K // DK)),
                  pl.BlockSpec((t, DV), lambda i: (i, C_V // DV)),
                  pl.BlockSpec((t, LANES), lambda i: (i, C_GLR // LANES)),
                  pl.BlockSpec((LANES, DK), lambda i: (0, 0)),
                  pl.BlockSpec((1, DK), lambda i: (0, 0))],
        out_specs=[pl.BlockSpec((t, DV), lambda i: (i, 0)),
                   pl.BlockSpec((cg, HEADS, HV, HK), lambda i: (i, 0, 0, 0))],
        out_shape=[jax.ShapeDtypeStruct((m, DV), F32),
                   jax.ShapeDtypeStruct((n_chunks, HEADS, HV, HK), F32)],
        scratch_shapes=[pltpu.VMEM((HEADS, HV, HK), F32)],
        compiler_params=_params(("arbitrary",)))(p, p, p, p, wgk, bgk)


def gla_bwd(p, wgk, bgk, st, do, dp, *, name):
    m = p.shape[0]
    n_chunks = m // CHUNK
    cg = _chunk_group(n_chunks)
    t = cg * CHUNK
    ns = n_chunks // cg
    scale = HK ** -0.5

    def body(q_ref, k_ref, v_ref, glr_ref, wgk_ref, bgk_ref, st_ref, do_ref, dp_in,
             dqkv_ref, dglr_ref, dwgk_ref, dbgk_ref, dstate):
        i = pl.program_id(0)
        blk = ns - 1 - i

        @pl.when(i == 0)
        def _():
            dstate[...] = jnp.zeros_like(dstate)
            dwgk_ref[...] = jnp.zeros_like(dwgk_ref)
            dbgk_ref[...] = jnp.zeros_like(dbgk_ref)

        z, la = _log_decay(glr_ref[...], wgk_ref[...], bgk_ref[...], blk * t, t)
        ri = lax.broadcasted_iota(jnp.int32, (CHUNK, CHUNK), 0)
        ci = lax.broadcasted_iota(jnp.int32, (CHUNK, CHUNK), 1)
        causal = ri >= ci
        tri = causal.astype(BF16)
        tri_u = (ri <= ci).astype(BF16)
        carried = [dstate[h] for h in range(HEADS)]
        dz_parts = [None] * cg
        for c in reversed(range(cg)):
            rows = pl.ds(c * CHUNK, CHUNK)
            b = _tri_mm(tri, la[c * CHUNK:(c + 1) * CHUNK])
            bl = b[CHUNK - 1:CHUNK, :]
            eb = jnp.exp(b)
            enb = jnp.exp(-b)
            ebl = jnp.exp(bl - b)
            dec = jnp.exp(bl)
            q = q_ref[rows, :].astype(F32) * scale
            k = k_ref[rows, :].astype(F32)
            qd32 = q * eb
            ki32 = k * enb
            ke32 = k * ebl
            qd = qd32.astype(BF16)
            ki = ki32.astype(BF16)
            ke = ke32.astype(BF16)
            dqd_parts, dki_parts, dke_parts, ddec_parts = [], [], [], []
            for h in range(HEADS):
                ks = slice(h * HK, (h + 1) * HK)
                vs = pl.ds(h * HV, HV)
                vh = v_ref[rows, vs]
                doh = do_ref[rows, vs].astype(BF16)
                s_t = st_ref[c, h]
                ds_t = carried[h]
                ds_b = ds_t.astype(BF16)
                att = jnp.where(causal, _nt(qd[:, ks], ki[:, ks]), 0.0).astype(BF16)
                datt = jnp.where(causal, _nt(doh, vh), 0.0).astype(BF16)
                dvh = _tn(att, doh) + _nt(ke[:, ks], ds_b)
                dqkv_ref[rows, pl.ds(2 * DK + h * HV, HV)] = dvh.astype(BF16)
                dqd_parts.append(_nn(datt, ki[:, ks]) + _nn(doh, s_t.astype(BF16)))
                dki_parts.append(_tn(datt, qd[:, ks]))
                dke_parts.append(_nn(vh, ds_b))
                ddec_parts.append(jnp.sum(s_t * ds_t, axis=0, keepdims=True))
                carried[h] = _tn(doh, qd[:, ks]) + ds_t * dec[:, ks]
            dqd = jnp.concatenate(dqd_parts, axis=1)
            dki = jnp.concatenate(dki_parts, axis=1)
            dke = jnp.concatenate(dke_parts, axis=1)
            ddec = jnp.concatenate(ddec_parts, axis=1)
            dqkv_ref[rows, pl.ds(0, DK)] = (dqd * eb * scale).astype(BF16)
            dqkv_ref[rows, pl.ds(DK, DK)] = (dki * enb + dke * ebl).astype(BF16)
            dke_ke = dke * ke32
            db = dqd * qd32 - dki * ki32 - dke_ke
            dbl = jnp.sum(dke_ke, axis=0, keepdims=True) + ddec * dec
            dg = _tri_mm(tri_u, db) + dbl
            row = blk * t + c * CHUNK + lax.broadcasted_iota(jnp.int32, (CHUNK, 1), 0)
            zc = z[c * CHUNK:(c + 1) * CHUNK]
            dz = jnp.where(row >= PAD, dg * (1.0 / TAU) * _sigmoid(-zc), 0.0)
            dz_parts[c] = dz
        for h in range(HEADS):
            dstate[h] = carried[h]
        dz_all = jnp.concatenate(dz_parts, axis=0)
        dz_b = dz_all.astype(BF16)
        dbgk_ref[...] += jnp.sum(dz_all, axis=0, keepdims=True)
        dglr_ref[...] = _nt(dz_b, wgk_ref[...]).astype(BF16)
        dwgk_ref[...] += _tn(glr_ref[...], dz_b)

    rev = lambda i: ns - 1 - i
    return pl.pallas_call(
        body, name=name, grid=(ns,),
        in_specs=[pl.BlockSpec((t, DK), lambda i: (rev(i), C_Q // DK)),
                  pl.BlockSpec((t, DK), lambda i: (rev(i), C_K // DK)),
                  pl.BlockSpec((t, DV), lambda i: (rev(i), C_V // DV)),
                  pl.BlockSpec((t, LANES), lambda i: (rev(i), C_GLR // LANES)),
                  pl.BlockSpec((LANES, DK), lambda i: (0, 0)),
                  pl.BlockSpec((1, DK), lambda i: (0, 0)),
                  pl.BlockSpec((cg, HEADS, HV, HK), lambda i: (rev(i), 0, 0, 0)),
                  pl.BlockSpec((t, DV), lambda i: (rev(i), 0)), pl.BlockSpec(memory_space=pl.ANY)],
        out_specs=[pl.BlockSpec((t, 2 * DK + DV), lambda i: (rev(i), 0)),
                   pl.BlockSpec((t, LANES), lambda i: (rev(i), 0)),
                   pl.BlockSpec((LANES, DK), lambda i: (0, 0)),
                   pl.BlockSpec((1, DK), lambda i: (0, 0))],
        out_shape=[jax.ShapeDtypeStruct((m, IN_R), BF16),
                   jax.ShapeDtypeStruct((m, LANES), BF16),
                   jax.ShapeDtypeStruct((LANES, DK), F32),
                   jax.ShapeDtypeStruct((1, DK), F32)],
        input_output_aliases={8: 0},
        scratch_shapes=[pltpu.VMEM((HEADS, HV, HK), F32)],
        compiler_params=_params(("arbitrary",)))(p, p, p, p, wgk, bgk, st, do, dp)


def place_glr(dp, dglr, *, name):
    m = dp.shape[0]
    tm = _ew_tile(m)

    def body(dp_in, g_ref, o_ref):
        o_ref[...] = g_ref[...]

    return pl.pallas_call(
        body, name=name, grid=(m // tm,),
        in_specs=[pl.BlockSpec(memory_space=pl.ANY), pl.BlockSpec((tm, LANES), lambda i: (i, 0))],
        out_specs=pl.BlockSpec((tm, LANES), lambda i: (i, C_GLR // LANES)),
        out_shape=jax.ShapeDtypeStruct((m, IN_R), BF16), input_output_aliases={0: 0},
        compiler_params=_params(("parallel",)))(dp, dglr)


HALO = 16


def _shift_down(xx, s):
    return pltpu.roll(xx, s, 0)


def _shift_up(xx, s):
    return pltpu.roll(xx, xx.shape[0] - s, 0)


def mix_pre(o, p, gn, *, name):
    m = o.shape[0]
    tm = _ew_tile(m)

    def body(o_ref, r_ref, u_ref, gn_ref, ya_ref, pooled_ref, halo):
        i = pl.program_id(0)

        @pl.when(i == 0)
        def _():
            halo[...] = jnp.zeros_like(halo)

        rv = r_ref[...].astype(F32)
        silu_r = rv * _sigmoid(rv)
        for h in range(HEADS):
            cs = pl.ds(h * HV, HV)
            ov = o_ref[:, cs]
            rs = lax.rsqrt(jnp.mean(ov * ov, axis=-1, keepdims=True) + EPS)
            ya_ref[:, cs] = (ov * rs * gn_ref[...] * silu_r[:, h * HV:(h + 1) * HV]).astype(BF16)

        row = i * tm + lax.broadcasted_iota(jnp.int32, (tm, 1), 0)
        pos1 = jnp.maximum(row - PAD + 1, 1).astype(F32)
        for g, w in enumerate(POOL_WINDOWS):
            cs = pl.ds(g * GDIM, GDIM)
            uv = u_ref[:, cs].astype(F32)
            xx = jnp.concatenate([halo[:, cs], uv], axis=0)
            s = xx
            span = 1
            while span < w:
                s = s + _shift_down(s, span)
                span *= 2
            inv = 1.0 / jnp.minimum(pos1, float(w))
            pooled_ref[:, cs] = (s[HALO:] * inv - uv).astype(BF16)
            halo[:, cs] = uv[tm - HALO:]

    blk = pl.BlockSpec((tm, D), lambda i: (i, 0))
    return pl.pallas_call(
        body, name=name, grid=(m // tm,),
        in_specs=[blk, pl.BlockSpec((tm, D), lambda i: (i, C_R // D)),
                  pl.BlockSpec((tm, D), lambda i: (i, C_U // D)),
                  pl.BlockSpec((1, HV), lambda i: (0, 0))],
        out_specs=[blk, blk],
        out_shape=[jax.ShapeDtypeStruct((m, D), BF16)] * 2,
        scratch_shapes=[pltpu.VMEM((HALO, D), F32)],
        compiler_params=_params(("arbitrary",)))(o, p, p, gn)


def mix_pre_bwd(dya, dpooled, o, p, gn, dp, *, name):
    m = o.shape[0]
    tm = _ew_tile(m)
    nt = m // tm

    def body(dya_ref, dpl_ref, o_ref, r_ref, gn_ref, dp_in, do_ref, dp_ref, dgn_ref, halo):
        i = pl.program_id(0)
        blk_i = nt - 1 - i

        @pl.when(i == 0)
        def _():
            halo[...] = jnp.zeros_like(halo)
            dgn_ref[...] = jnp.zeros_like(dgn_ref)

        rv = r_ref[...].astype(F32)
        sg = _sigmoid(rv)
        silu_r = rv * sg
        dsilu = sg * (1.0 + rv * (1.0 - sg))
        dgn = jnp.zeros((1, HV), F32)
        for h in range(HEADS):
            cs = pl.ds(h * HV, HV)
            hs = slice(h * HV, (h + 1) * HV)
            ov = o_ref[:, cs]
            dy = dya_ref[:, cs].astype(F32)
            rs = lax.rsqrt(jnp.mean(ov * ov, axis=-1, keepdims=True) + EPS)
            xh = ov * rs
            on = xh * gn_ref[...]
            don = dy * silu_r[:, hs]
            dp_ref[:, cs] = (dy * on * dsilu[:, hs]).astype(BF16)
            dxh = don * gn_ref[...]
            do_ref[:, cs] = rs * (dxh - xh * jnp.mean(dxh * xh, axis=-1, keepdims=True))
            dgn = dgn + jnp.sum(don * xh, axis=0, keepdims=True)
        dgn_ref[...] += dgn

        row = blk_i * tm + lax.broadcasted_iota(jnp.int32, (tm, 1), 0)
        pos1 = jnp.maximum(row - PAD + 1, 1).astype(F32)
        for g, w in enumerate(POOL_WINDOWS):
            cs = pl.ds(g * GDIM, GDIM)
            dpv = dpl_ref[:, cs].astype(F32)
            e = dpv * (1.0 / jnp.minimum(pos1, float(w)))
            xx = jnp.concatenate([e, halo[:, cs]], axis=0)
            s = xx
            span = 1
            while span < w:
                s = s + _shift_up(s, span)
                span *= 2
            dp_ref[:, pl.ds(D + g * GDIM, GDIM)] = (s[:tm] - dpv).astype(BF16)
            halo[:, cs] = e[:HALO]

    rev = lambda i: nt - 1 - i
    blk = pl.BlockSpec((tm, D), lambda i: (rev(i), 0))
    return pl.pallas_call(
        body, name=name, grid=(nt,),
        in_specs=[blk, blk, blk, pl.BlockSpec((tm, D), lambda i: (rev(i), C_R // D)),
                  pl.BlockSpec((1, HV), lambda i: (0, 0)), pl.BlockSpec(memory_space=pl.ANY)],
        out_specs=[blk, pl.BlockSpec((tm, 2 * D), lambda i: (rev(i), C_R // (2 * D))),
                   pl.BlockSpec((1, HV), lambda i: (0, 0))],
        out_shape=[jax.ShapeDtypeStruct((m, D), F32), jax.ShapeDtypeStruct((m, IN_R), BF16),
                   jax.ShapeDtypeStruct((1, HV), F32)],
        input_output_aliases={5: 1},
        scratch_shapes=[pltpu.VMEM((HALO, D), F32)],
        compiler_params=_params(("arbitrary",)))(dya, dpooled, o, p, gn, dp)


def merge_fwd(p, ya, yb, bg, *, name):
    m = ya.shape[0]
    tm = _ew_tile(m)

    def body(ga_ref, gb_ref, ya_ref, yb_ref, ba_ref, bb_ref, o_ref):
        gate_a = _sigmoid(ga_ref[...].astype(F32) + ba_ref[...])
        gate_b = _sigmoid(gb_ref[...].astype(F32) + bb_ref[...])
        o_ref[...] = (gate_a * ya_ref[...].astype(F32) + gate_b * yb_ref[...].astype(F32)).astype(BF16)

    blk = pl.BlockSpec((tm, D), lambda i: (i, 0))
    return pl.pallas_call(
        body, name=name, grid=(m // tm,),
        in_specs=[pl.BlockSpec((tm, D), lambda i: (i, C_GA // D)),
                  pl.BlockSpec((tm, D), lambda i: (i, C_GB // D)), blk, blk,
                  pl.BlockSpec((1, D), lambda i: (0, 0)), pl.BlockSpec((1, D), lambda i: (0, 1))],
        out_specs=blk, out_shape=jax.ShapeDtypeStruct((m, D), BF16),
        compiler_params=_params(("parallel",)))(p, p, ya, yb, bg, bg)


def merge_proj_o(p, ya, yb, bg, w_o, res, *, name):
    m = ya.shape[0]
    tm = _ew_tile(m)

    def body(ga_ref, gb_ref, ya_ref, yb_ref, ba_ref, bb_ref, w_ref, r_ref, mrg_ref, o_ref):
        gate_a = _sigmoid(ga_ref[...].astype(F32) + ba_ref[...])
        gate_b = _sigmoid(gb_ref[...].astype(F32) + bb_ref[...])
        mrg = (gate_a * ya_ref[...].astype(F32) + gate_b * yb_ref[...].astype(F32)).astype(BF16)
        mrg_ref[...] = mrg
        row = pl.program_id(0) * tm + lax.broadcasted_iota(jnp.int32, (tm, 1), 0)
        o_ref[...] = jnp.where(row >= PAD, _nn(mrg, w_ref[...]) + r_ref[...], 0.0)

    blk = pl.BlockSpec((tm, D), lambda i: (i, 0))
    return pl.pallas_call(
        body, name=name, grid=(m // tm,),
        in_specs=[pl.BlockSpec((tm, D), lambda i: (i, C_GA // D)),
                  pl.BlockSpec((tm, D), lambda i: (i, C_GB // D)), blk, blk,
                  pl.BlockSpec((1, D), lambda i: (0, 0)), pl.BlockSpec((1, D), lambda i: (0, 1)),
                  pl.BlockSpec((D, D), lambda i: (0, 0)), blk],
        out_specs=[blk, blk],
        out_shape=[jax.ShapeDtypeStruct((m, D), BF16), jax.ShapeDtypeStruct((m, D), F32)],
        compiler_params=_params(("parallel",)))(p, p, ya, yb, bg, bg, w_o, res)


def merge_bwd(dmrg, p, ya, yb, bg, *, name):
    m = ya.shape[0]
    tm = _ew_tile(m)

    def body(dm_ref, ga_ref, gb_ref, ya_ref, yb_ref, ba_ref, bb_ref,
             dya_ref, dyb_ref, dp_ref, dbg_ref):
        @pl.when(pl.program_id(0) == 0)
        def _():
            dbg_ref[...] = jnp.zeros_like(dbg_ref)

        dm = dm_ref[...].astype(F32)
        gate_a = _sigmoid(ga_ref[...].astype(F32) + ba_ref[...])
        gate_b = _sigmoid(gb_ref[...].astype(F32) + bb_ref[...])
        dya_ref[...] = (dm * gate_a).astype(BF16)
        dyb_ref[...] = (dm * gate_b).astype(BF16)
        dga = dm * ya_ref[...].astype(F32) * gate_a * (1.0 - gate_a)
        dgb = dm * yb_ref[...].astype(F32) * gate_b * (1.0 - gate_b)
        dp_ref[:, pl.ds(0, D)] = dga.astype(BF16)
        dp_ref[:, pl.ds(D, D)] = dgb.astype(BF16)
        dbg_ref[:, pl.ds(0, D)] += jnp.sum(dga, axis=0, keepdims=True)
        dbg_ref[:, pl.ds(D, D)] += jnp.sum(dgb, axis=0, keepdims=True)

    blk = pl.BlockSpec((tm, D), lambda i: (i, 0))
    return pl.pallas_call(
        body, name=name, grid=(m // tm,),
        in_specs=[blk, pl.BlockSpec((tm, D), lambda i: (i, C_GA // D)),
                  pl.BlockSpec((tm, D), lambda i: (i, C_GB // D)), blk, blk,
                  pl.BlockSpec((1, D), lambda i: (0, 0)), pl.BlockSpec((1, D), lambda i: (0, 1))],
        out_specs=[blk, blk, pl.BlockSpec((tm, 2 * D), lambda i: (i, C_GA // (2 * D))),
                   pl.BlockSpec((1, 2 * D), lambda i: (0, 0))],
        out_shape=[jax.ShapeDtypeStruct((m, D), BF16)] * 2 + [jax.ShapeDtypeStruct((m, IN_R), BF16),
                                                              jax.ShapeDtypeStruct((1, 2 * D), F32)],
        compiler_params=_params(("arbitrary",)))(dmrg, p, p, ya, yb, bg, bg)


def scale_bwd(dy1, y0, scale, *, name):
    m = y0.shape[0]
    tm = _ew_tile(m)

    def body(dy_ref, y0_ref, s_ref, o_ref, ds_ref):
        @pl.when(pl.program_id(0) == 0)
        def _():
            ds_ref[...] = jnp.zeros_like(ds_ref)

        dy = dy_ref[...].astype(F32)
        o_ref[...] = (dy * s_ref[...]).astype(BF16)
        ds_ref[...] += jnp.sum(dy * y0_ref[...].astype(F32), axis=0, keepdims=True)

    blk = pl.BlockSpec((tm, D), lambda i: (i, 0))
    vec = pl.BlockSpec((1, D), lambda i: (0, 0))
    return pl.pallas_call(
        body, name=name, grid=(m // tm,), in_specs=[blk, blk, vec], out_specs=[blk, vec],
        out_shape=[jax.ShapeDtypeStruct((m, D), BF16), jax.ShapeDtypeStruct((1, D), F32)],
        compiler_params=_params(("arbitrary",)))(dy1, y0, scale)


CONV_BLK = 1408
CONV_ROWS = 688
N_CONV_BLK = D_FF // CONV_BLK


def conv_act_fwd(up, cw, cb, *, name):
    m = up.shape[0]
    tm = _ew_tile(m, cap=CONV_ROWS)

    def conv(x_ref, halo, w_ref, b_ref):
        xv = x_ref[...].astype(F32)
        xx = jnp.concatenate([halo[...], xv], axis=0)
        y = (w_ref[2:3, :] * xx + w_ref[1:2, :] * _shift_down(xx, 1)
             + w_ref[0:1, :] * _shift_down(xx, 2))[HALO:] + b_ref[...]
        halo[...] = xv[tm - HALO:]
        return y

    def body(xa_ref, xb_ref, wa_ref, wb_ref, ba_ref, bb_ref, upc_a_ref, upc_b_ref, act_ref, halo_a, halo_b):
        @pl.when(pl.program_id(1) == 0)
        def _():
            halo_a[...] = jnp.zeros_like(halo_a)
            halo_b[...] = jnp.zeros_like(halo_b)

        a = conv(xa_ref, halo_a, wa_ref, ba_ref)
        bv = conv(xb_ref, halo_b, wb_ref, bb_ref)
        upc_a_ref[...] = a.astype(BF16)
        upc_b_ref[...] = bv.astype(BF16)
        act_ref[...] = (a * _sigmoid(a) * bv).astype(BF16)

    nb = N_CONV_BLK
    xa = pl.BlockSpec((tm, CONV_BLK), lambda j, i: (i, j))
    xb = pl.BlockSpec((tm, CONV_BLK), lambda j, i: (i, j + nb))
    return pl.pallas_call(
        body, name=name, grid=(nb, m // tm),
        in_specs=[xa, xb,
                  pl.BlockSpec((3, CONV_BLK), lambda j, i: (0, j)),
                  pl.BlockSpec((3, CONV_BLK), lambda j, i: (0, j + nb)),
                  pl.BlockSpec((1, CONV_BLK), lambda j, i: (0, j)),
                  pl.BlockSpec((1, CONV_BLK), lambda j, i: (0, j + nb))],
        out_specs=[xa, xa, xa],
        out_shape=[jax.ShapeDtypeStruct((m, D_FF), BF16)] * 3,
        scratch_shapes=[pltpu.VMEM((HALO, CONV_BLK), F32)] * 2,
        compiler_params=_params(("parallel", "arbitrary")))(up, up, cw, cw, cb, cb)


def conv_act_bwd(dact, upc_a, upc_b, up, cw, *, name):
    m = up.shape[0]
    tm = _ew_tile(m, cap=CONV_ROWS)
    nt = m // tm

    def conv_t(d, halo, x_ref, w_ref, dup_ref, half, dw_ref, db_ref):
        xx = jnp.concatenate([d, halo[...]], axis=0)
        d1 = _shift_up(xx, 1)[:tm]
        d2 = _shift_up(xx, 2)[:tm]
        dup_ref[half] = (w_ref[2:3, :] * d + w_ref[1:2, :] * d1 + w_ref[0:1, :] * d2).astype(BF16)
        xv = x_ref[...].astype(F32)
        dw_ref[2:3, :] += jnp.sum(xv * d, axis=0, keepdims=True)
        dw_ref[1:2, :] += jnp.sum(xv * d1, axis=0, keepdims=True)
        dw_ref[0:1, :] += jnp.sum(xv * d2, axis=0, keepdims=True)
        db_ref[...] += jnp.sum(d, axis=0, keepdims=True)
        halo[...] = d[:HALO]

    def body(da_ref, a_ref, b_ref, xa_ref, xb_ref, wa_ref, wb_ref,
             dup_ref, dwa_ref, dwb_ref, dba_ref, dbb_ref, halo_a, halo_b):
        @pl.when(pl.program_id(1) == 0)
        def _():
            for r in (halo_a, halo_b, dwa_ref, dwb_ref, dba_ref, dbb_ref):
                r[...] = jnp.zeros_like(r)

        dact_v = da_ref[...].astype(F32)
        a = a_ref[...].astype(F32)
        bv = b_ref[...].astype(F32)
        sg = _sigmoid(a)
        d_a = dact_v * bv * sg * (1.0 + a * (1.0 - sg))
        d_b = dact_v * a * sg
        conv_t(d_a, halo_a, xa_ref, wa_ref, dup_ref, 0, dwa_ref, dba_ref)
        conv_t(d_b, halo_b, xb_ref, wb_ref, dup_ref, 1, dwb_ref, dbb_ref)

    nb = N_CONV_BLK
    rev = lambda i: nt - 1 - i
    half = pl.BlockSpec((tm, CONV_BLK), lambda j, i: (rev(i), j))
    xa = half
    xb = pl.BlockSpec((tm, CONV_BLK), lambda j, i: (rev(i), j + nb))
    wa = pl.BlockSpec((3, CONV_BLK), lambda j, i: (0, j))
    wb = pl.BlockSpec((3, CONV_BLK), lambda j, i: (0, j + nb))
    va = pl.BlockSpec((1, CONV_BLK), lambda j, i: (0, j))
    outs = pl.pallas_call(
        body, name=name, grid=(nb, nt),
        in_specs=[half, half, half, xa, xb, wa, wb],
        out_specs=[pl.BlockSpec((2, tm, CONV_BLK), lambda j, i: (0, rev(i), j)), wa, wa, va, va],
        out_shape=[jax.ShapeDtypeStruct((2, m, D_FF), BF16)]
                  + [jax.ShapeDtypeStruct((3, D_FF), F32)] * 2
                  + [jax.ShapeDtypeStruct((1, D_FF), F32)] * 2,
        scratch_shapes=[pltpu.VMEM((HALO, CONV_BLK), F32)] * 2,
        compiler_params=_params(("parallel", "arbitrary")))(dact, upc_a, upc_b, up, up, cw, cw)
    return outs


def local_step(x, target, w):
    seq = x.shape[0]
    h = jnp.concatenate([jnp.zeros((PAD, D), F32), w["meta"], x], axis=0)
    saved = []
    for l in range(DEPTH):
        wl = {k: (v[l:l + 1] if k in ROW_PARAMS else v[l]) for k, v in w.items() if k not in ("meta", "final_norm_g")}
        s = {"h": h}
        fwd_in(s, wl, f"l{l}_")
        fwd_mixer(s, wl, f"l{l}_")
        fwd_ffn(s, wl, f"l{l}_")
        saved.append(s)
        h = s["h3"]

    dh, dh_b, dgf, loss_rows = loss_head(h, w["final_norm_g"], jnp.pad(target, ((X0, 0), (0, 0))), name="loss_head")
    g = {"final_norm_g": dgf}
    per_layer = []
    for l in reversed(range(DEPTH)):
        wl = {k: (v[l:l + 1] if k in ROW_PARAMS else v[l]) for k, v in w.items() if k not in ("meta", "final_norm_g")}
        s = saved[l]
        gl = {}
        dh2, dh2_b = bwd_ffn(dh, dh_b, s, wl, gl, f"l{l}_")
        dp = bwd_mixer(dh2_b, s, wl, gl, f"l{l}_")
        gl["w_in"] = bwd_in_w(dp, s, f"l{l}_")
        dh, dh_b = bwd_in_x(dp, dh2, s, wl, gl, f"l{l}_")
        per_layer.append(gl)
    per_layer.reverse()
    for k in per_layer[0]:
        g[k] = jnp.stack([per_layer[l][k].astype(F32) for l in range(DEPTH)])
    g["meta"] = dh[PAD:X0]
    return loss_rows, dh[X0:X0 + seq], g


ROW_PARAMS = ("norm1_g", "b_gk", "gla_norm_g", "pool_scale", "b_gates", "norm2_g", "conv_b")


def fwd_in(s, w, ln):
    s["hn1"] = rmsnorm_fwd(s["h"], w["norm1_g"], name=ln + "norm1")
    s["p"] = mm_nt(s["hn1"], w["w_in"], name=ln + "in_proj")


def fwd_mixer(s, w, ln):
    p = s["p"]
    s["o"], s["st"] = gla_fwd(p, w["w_gk"], w["b_gk"], name=ln + "gla_fwd")
    s["ya_in"], s["pooled"] = mix_pre(s["o"], p, w["gla_norm_g"], name=ln + "mix_pre")
    s["ya"] = mm_nn(s["ya_in"], w["w_a"], name=ln + "proj_a")
    s["yb0"], s["yb1"] = pool_mm_fwd(s["pooled"], w["w_pool"], w["pool_scale"], name=ln + "pool_mm")
    s["yb"] = mm_nn(s["yb1"], w["w_b"], name=ln + "proj_b")
    s["mrg"], s["h2"] = merge_proj_o(p, s["ya"], s["yb"], w["b_gates"], w["w_o"], s["h"], name=ln + "proj_o")


def fwd_ffn(s, w, ln):
    s["hn2"] = rmsnorm_fwd(s["h2"], w["norm2_g"], name=ln + "norm2")
    s["up"] = mm_nt(s["hn2"], w["w_up"], tn=D_FF, name=ln + "up_proj")
    s["upc_a"], s["upc_b"], s["act"] = conv_act_fwd(s["up"], w["conv_w"], w["conv_b"], name=ln + "conv_act")
    s["h3"] = mm_nn(s["act"], w["w_down"], out_dtype=F32, res=s["h2"], name=ln + "down_proj")


def bwd_ffn(dh, dh_b, s, w, g, ln, after=None):
    dact = mm_nt(dh_b, w["w_down"], after=after, name=ln + "d_act")
    g["w_down"] = mm_tn(s["act"], dh_b, tk1=1408, out_dtype=BF16, after=after, name=ln + "dw_down")
    dup, dcw_a, dcw_b, dcb_a, dcb_b = conv_act_bwd(
        dact, s["upc_a"], s["upc_b"], s["up"], w["conv_w"], name=ln + "conv_act_bwd")
    dhn2 = mm_nn(dup, w["w_up"], out_dtype=F32, tn=512, halves=True, name=ln + "d_hn2")
    g["w_up"] = mm_tn(dup, s["hn2"], tk1=1408, out_dtype=BF16, halves=True, name=ln + "dw_up")
    dh2, dh2_b, g["norm2_g"] = rmsnorm_bwd(dhn2, s["h2"], w["norm2_g"], dh, name=ln + "norm2_bwd")
    g["conv_w"] = jnp.concatenate([dcw_a, dcw_b], axis=1)
    g["conv_b"] = jnp.concatenate([dcb_a, dcb_b], axis=1)
    return dh2, dh2_b


def bwd_mixer(dh2_b, s, w, g, ln, after=None):
    dmrg = mm_nt(dh2_b, w["w_o"], after=after, name=ln + "d_mrg")
    g["w_o"] = mm_tn(s["mrg"], dh2_b, out_dtype=BF16, after=after, name=ln + "dw_o")
    dya, dyb, dp, g["b_gates"] = merge_bwd(dmrg, s["p"], s["ya"], s["yb"], w["b_gates"], name=ln + "merge_bwd")
    dya_in = mm_nt(dya, w["w_a"], name=ln + "d_ya_in")
    g["w_a"] = mm_tn(s["ya_in"], dya, out_dtype=BF16, name=ln + "dw_a")
    dyb1 = mm_nt(dyb, w["w_b"], name=ln + "d_yb1")
    g["w_b"] = mm_tn(s["yb1"], dyb, out_dtype=BF16, name=ln + "dw_b")
    dyb0, g["pool_scale"] = scale_bwd(dyb1, s["yb0"], w["pool_scale"], name=ln + "scale_bwd")
    dpooled = pool_mm_bwd_x(dyb0, w["w_pool"], name=ln + "d_pooled")
    g["w_pool"] = pool_mm_bwd_w(s["pooled"], dyb0, name=ln + "dw_pool")
    do, dp, g["gla_norm_g"] = mix_pre_bwd(dya_in, dpooled, s["o"], s["p"], w["gla_norm_g"], dp,
                                          name=ln + "mix_pre_bwd")
    dp, dglr, g["w_gk"], g["b_gk"] = gla_bwd(s["p"], w["w_gk"], w["b_gk"], s["st"], do, dp, name=ln + "gla_bwd")
    return place_glr(dp, dglr, name=ln + "place_glr")


def bwd_in_w(dp, s, ln):
    return mm_tn(dp, s["hn1"], tk1=896, out_dtype=BF16, name=ln + "dw_in")


def bwd_in_x(dp, dh2, s, w, g, ln, after=None):
    dhn1 = mm_nn(dp, w["w_in"], out_dtype=F32, tn=512, after=after, name=ln + "d_hn1")
    dh, dh_b, g["norm1_g"] = rmsnorm_bwd(dhn1, s["h"], w["norm1_g"], dh2, name=ln + "norm1_bwd")
    return dh, dh_b


def _my_place():
    return lax.axis_index("x"), lax.axis_index("y"), lax.axis_index("c")


def _peer(place, k):
    x, y, c = place
    return (1 - x if k & 4 else x, 1 - y if k & 2 else y, 1 - c if k & 1 else c)


def _index(place):
    x, y, c = place
    return 4 * x + 2 * y + c


def exchange(arrays, kinds, *, name):
    n = len(arrays)

    def body(*refs):
        ins, outs = refs[:n], refs[n:2 * n]
        send_sems, recv_sems, local_sems = refs[2 * n:]
        place = _my_place()
        me = _index(place)

        def src(a, dest):
            return ins[a] if kinds[a] == "gather" else ins[a].at[dest]

        def remote(a, k):
            peer = _peer(place, k)
            return pltpu.make_async_remote_copy(
                src_ref=src(a, _index(peer)), dst_ref=outs[a].at[me],
                send_sem=send_sems.at[a, k - 1], recv_sem=recv_sems.at[a, k - 1],
                device_id=peer, device_id_type=pl.DeviceIdType.MESH)

        def arrival(a, k):
            peer = _peer(place, k)
            return pltpu.make_async_remote_copy(
                src_ref=src(a, me), dst_ref=outs[a].at[_index(peer)],
                send_sem=send_sems.at[a, k - 1], recv_sem=recv_sems.at[a, k - 1],
                device_id=peer, device_id_type=pl.DeviceIdType.MESH)

        own = [pltpu.make_async_copy(src(a, me), outs[a].at[me], local_sems.at[a]) for a in range(n)]
        sends = [remote(a, k) for k in range(1, N_DEV) for a in range(n)]
        for cp in sends:
            cp.start()
        for cp in own:
            cp.start()
        for k in range(1, N_DEV):
            for a in range(n):
                arrival(a, k).wait_recv()
        for cp in sends:
            cp.wait_send()
        for cp in own:
            cp.wait()

    any_spec = pl.BlockSpec(memory_space=pl.ANY)
    out_shape = []
    for arr, kind in zip(arrays, kinds):
        shape = arr.shape if kind == "gather" else arr.shape[1:]
        out_shape.append(jax.ShapeDtypeStruct((N_DEV,) + tuple(shape), arr.dtype))
    return pl.pallas_call(
        body, name=name, in_specs=[any_spec] * n, out_specs=[any_spec] * n, out_shape=out_shape,
        scratch_shapes=[pltpu.SemaphoreType.DMA((n, N_DEV - 1)), pltpu.SemaphoreType.DMA((n, N_DEV - 1)),
                        pltpu.SemaphoreType.DMA((n,))],
    )(*arrays)


def _sem_slot(a, k):
    return a * (N_DEV - 1) + k - 1


_HBM = pl.BlockSpec(memory_space=pltpu.HBM)
_SEM = pl.BlockSpec(memory_space=pltpu.SEMAPHORE)
_DATAFLOW = pltpu.SideEffectType.DATAFLOW_SIDE_EFFECTING


def exchange_start(arrays, kinds, after, *, name):
    n = len(arrays)
    zones = []
    for arr, kind in zip(arrays, kinds):
        shape = arr.shape if kind == "gather" else arr.shape[1:]
        zones.append(lax.empty((N_DEV,) + tuple(shape), arr.dtype))

    def body(*refs):
        ins, lands = refs[:n], refs[n:2 * n]
        send_sems, recv_sems = refs[2 * n + 1], refs[2 * n + 2]
        token = refs[4 * n + 3]
        place = _my_place()
        me = _index(place)
        for a in range(n):
            for k in range(1, N_DEV):
                peer = _peer(place, k)
                pltpu.make_async_remote_copy(
                    src_ref=ins[a] if kinds[a] == "gather" else ins[a].at[_index(peer)], dst_ref=lands[a].at[me],
                    send_sem=send_sems.at[_sem_slot(a, k)], recv_sem=recv_sems.at[_sem_slot(a, k)],
                    device_id=peer, device_id_type=pl.DeviceIdType.MESH).start()
        token[...] = jnp.zeros_like(token)

    sems = pltpu.SemaphoreType.DMA((n * (N_DEV - 1),))
    hbm = lambda a: pltpu.HBM(a.shape, a.dtype)
    outs = pl.pallas_call(
        body, name=name,
        out_shape=(sems, sems, *[hbm(a) for a in arrays], *[hbm(z) for z in zones],
                   jax.ShapeDtypeStruct((8, LANES), F32)),
        in_specs=[_HBM] * (2 * n) + [pl.BlockSpec(memory_space=pl.ANY)],
        out_specs=(_SEM, _SEM, *[_HBM] * (2 * n), pl.BlockSpec(memory_space=pltpu.VMEM)),
        input_output_aliases={i: 2 + i for i in range(2 * n)},
        compiler_params=pltpu.CompilerParams(has_side_effects=_DATAFLOW),
    )(*[pltpu.with_memory_space_constraint(a, pltpu.HBM) for a in arrays],
      *[pltpu.with_memory_space_constraint(z, pltpu.HBM) for z in zones], after)
    return dict(send=outs[0], recv=outs[1], srcs=outs[2:2 + n], zones=outs[2 + n:2 + 2 * n],
                token=outs[2 + 2 * n], kinds=kinds)


def exchange_wait(handle, after, *, name):
    kinds = handle["kinds"]
    n = len(kinds)

    def body(*refs):
        ins, lands = refs[:n], refs[n:2 * n]
        send_sems, recv_sems = refs[2 * n], refs[2 * n + 1]
        place = _my_place()
        me = _index(place)
        for a in range(n):
            for k in range(1, N_DEV):
                peer = _peer(place, k)
                src = ins[a] if kinds[a] == "gather" else ins[a].at[_index(peer)]
                copy = pltpu.make_async_remote_copy(
                    src_ref=src, dst_ref=lands[a].at[_index(peer)],
                    send_sem=send_sems.at[_sem_slot(a, k)], recv_sem=recv_sems.at[_sem_slot(a, k)],
                    device_id=peer, device_id_type=pl.DeviceIdType.MESH)
                copy.wait_send()
                copy.wait_recv()

    srcs, zones = handle["srcs"], handle["zones"]
    after = after if isinstance(after, tuple) else (after,)
    hbm = lambda a: pltpu.HBM(a.shape, a.dtype)
    outs = pl.pallas_call(
        body, name=name,
        out_shape=(*[hbm(a) for a in srcs], *[hbm(z) for z in zones]),
        in_specs=[_HBM] * (2 * n) + [_SEM, _SEM] + [pl.BlockSpec(memory_space=pl.ANY)] * len(after),
        out_specs=[_HBM] * (2 * n),
        input_output_aliases={i: i for i in range(2 * n)},
        compiler_params=pltpu.CompilerParams(has_side_effects=_DATAFLOW),
    )(*srcs, *zones, handle["send"], handle["recv"], *after)
    return _fill_own(outs[:n], outs[n:], kinds)


def _fill_own(srcs, zones, kinds):
    me = _index(_my_place())
    filled = []
    for src, zone, kind in zip(srcs, zones, kinds):
        mine = src if kind == "gather" else lax.dynamic_index_in_dim(src, me, 0, keepdims=False)
        filled.append(lax.dynamic_update_index_in_dim(zone, mine, me, 0))
    return filled


ADAM_COLS = 256


def reduce_adam_layer(parts, w, m, v, layer, prev, *, name):
    _, r, c = w.shape
    tc = ADAM_COLS

    def body(*refs):
        p_ref, w_ref, m_ref, v_ref = refs[:4]
        g_ref, d_ref, m2_ref, v2_ref = refs[-4:]
        g = p_ref[0].astype(F32)
        for i in range(1, N_DEV):
            g = g + p_ref[i].astype(F32)
        m2 = B1 * m_ref[...] + (1.0 - B1) * g
        v2 = B2 * v_ref[...] + (1.0 - B2) * (g * g)
        m_hat = m2 / (1.0 - B1 ** STEP)
        v_hat = v2 / (1.0 - B2 ** STEP)
        g_ref[...] = g
        d_ref[...] = -LR * (m_hat / (jnp.sqrt(v_hat) + ADAM_EPS) + WD * w_ref[...])
        m2_ref[...] = m2
        v2_ref[...] = v2

    blk = pl.BlockSpec((None, r, tc), lambda i: (layer, 0, i))
    in_specs = [pl.BlockSpec((N_DEV, r, tc), lambda i: (0, 0, i)), blk, blk, blk]
    args = [parts, w, m, v]
    aliases = {}
    if prev is not None:
        in_specs += [pl.BlockSpec(memory_space=pl.ANY)] * 4
        args += list(prev)
        aliases = {4 + j: j for j in range(4)}
    return pl.pallas_call(
        body, name=name, grid=(c // tc,), in_specs=in_specs, out_specs=[blk] * 4,
        out_shape=[jax.ShapeDtypeStruct(w.shape, F32)] * 4, input_output_aliases=aliases,
        compiler_params=_params(("parallel",)))(*args)


def reduce_adam(parts, w, m, v, *, name):
    r, c = w.shape
    tr = _pick(r, (256, 352, 192, 128, 72, 64, 32, 16, 8))

    def body(p_ref, w_ref, m_ref, v_ref, g_ref, d_ref, m2_ref, v2_ref):
        g = p_ref[0].astype(F32)
        for i in range(1, N_DEV):
            g = g + p_ref[i].astype(F32)
        wv = w_ref[...]
        m2 = B1 * m_ref[...] + (1.0 - B1) * g
        v2 = B2 * v_ref[...] + (1.0 - B2) * (g * g)
        m_hat = m2 / (1.0 - B1 ** STEP)
        v_hat = v2 / (1.0 - B2 ** STEP)
        g_ref[...] = g
        d_ref[...] = -LR * (m_hat / (jnp.sqrt(v_hat) + ADAM_EPS) + WD * wv)
        m2_ref[...] = m2
        v2_ref[...] = v2

    blk = pl.BlockSpec((tr, c), lambda i: (i, 0))
    return pl.pallas_call(
        body, name=name, grid=(r // tr,),
        in_specs=[pl.BlockSpec((N_DEV, tr, c), lambda i: (0, i, 0)), blk, blk, blk],
        out_specs=[blk] * 4, out_shape=[jax.ShapeDtypeStruct((r, c), F32)] * 4,
        compiler_params=_params(("parallel",)))(parts, w, m, v)


BIG = ("w_in", "w_a", "w_pool_grp", "w_b", "w_o", "w_up", "w_down")
SHARDED_SMALL = ("meta_tokens", "w_gk", "conv_w")
REPLICATED = ("norm1_g", "b_gk", "gla_norm_g", "pool_scale", "b_gates", "norm2_g", "conv_b", "final_norm_g")
CUT_AXIS = {"w_in": 2, "w_a": 1, "w_pool_grp": 2, "w_b": 1, "w_o": 1, "w_up": 2, "w_down": 1,
            "meta_tokens": 1, "w_gk": 2, "conv_w": 2}
WEIGHTS = ("meta_tokens", "norm1_g", "w_in", "w_gk", "b_gk", "gla_norm_g", "w_a", "w_pool_grp", "pool_scale",
           "w_b", "b_gates", "w_o", "norm2_g", "w_up", "conv_w", "conv_b", "w_down", "final_norm_g")


def _as_2d(a):
    return a.reshape(-1, a.shape[-1])


def _from_slots(slots, axis):
    full = jnp.moveaxis(slots, 0, axis)
    shape = list(full.shape)
    shape[axis:axis + 2] = [shape[axis] * shape[axis + 1]]
    return full.reshape(shape)


def _to_slots(full, axis):
    shape = list(full.shape)
    shape[axis:axis + 1] = [N_DEV, shape[axis] // N_DEV]
    return jnp.moveaxis(full.reshape(shape), axis, 0)


def _pack(vectors, rows):
    flat = jnp.concatenate([v.reshape(-1).astype(F32) for v in vectors])
    return jnp.pad(flat, (0, rows * LANES - flat.shape[0])).reshape(rows, LANES)


def _unpack(packed, shapes):
    flat = packed.reshape(-1)
    out, off = [], 0
    for s in shapes:
        size = 1
        for d in s:
            size *= d
        out.append(flat[off:off + size].reshape(s))
        off += size
    return out


def _rows_for(shapes, mult=8):
    total = 0
    for s in shapes:
        size = 1
        for d in s:
            size *= d
        total += size
    rows = -(-total // LANES)
    return -(-rows // mult) * mult


def _permute_rows(w_t):
    pad = jnp.zeros((IN_R - IN_WIDTH,) + w_t.shape[1:], w_t.dtype)
    return jnp.concatenate([w_t[:2048], w_t[2064:], w_t[2048:2064], pad], axis=0)


def _unpermute_rows(w_r):
    return jnp.concatenate([w_r[:2048], w_r[C_GLR:C_GLR + RANK], w_r[2048:C_GLR]], axis=0)


def kernel(x, meta_tokens, norm1_g, w_in, w_gk, b_gk, gla_norm_g, w_a, w_pool_grp, pool_scale, w_b, b_gates, w_o, norm2_g, w_up, conv_w, conv_b, w_down, final_norm_g, loss_target, m_meta_tokens, m_norm1_g, m_w_in, m_w_gk, m_b_gk, m_gla_norm_g, m_w_a, m_w_pool_grp, m_pool_scale, m_w_b, m_b_gates, m_w_o, m_norm2_g, m_w_up, m_conv_w, m_conv_b, m_w_down, m_final_norm_g, v_meta_tokens, v_norm1_g, v_w_in, v_w_gk, v_b_gk, v_gla_norm_g, v_w_a, v_w_pool_grp, v_pool_scale, v_w_b, v_b_gates, v_w_o, v_norm2_g, v_w_up, v_conv_w, v_conv_b, v_w_down, v_final_norm_g):
    wts = dict(meta_tokens=meta_tokens, norm1_g=norm1_g, w_in=w_in, w_gk=w_gk, b_gk=b_gk, gla_norm_g=gla_norm_g,
               w_a=w_a, w_pool_grp=w_pool_grp, pool_scale=pool_scale, w_b=w_b, b_gates=b_gates, w_o=w_o,
               norm2_g=norm2_g, w_up=w_up, conv_w=conv_w, conv_b=conv_b, w_down=w_down, final_norm_g=final_norm_g)
    mom = dict(meta_tokens=m_meta_tokens, norm1_g=m_norm1_g, w_in=m_w_in, w_gk=m_w_gk, b_gk=m_b_gk,
               gla_norm_g=m_gla_norm_g, w_a=m_w_a, w_pool_grp=m_w_pool_grp, pool_scale=m_pool_scale, w_b=m_w_b,
               b_gates=m_b_gates, w_o=m_w_o, norm2_g=m_norm2_g, w_up=m_w_up, conv_w=m_conv_w, conv_b=m_conv_b,
               w_down=m_w_down, final_norm_g=m_final_norm_g)
    var = dict(meta_tokens=v_meta_tokens, norm1_g=v_norm1_g, w_in=v_w_in, w_gk=v_w_gk, b_gk=v_b_gk,
               gla_norm_g=v_gla_norm_g, w_a=v_w_a, w_pool_grp=v_w_pool_grp, pool_scale=v_pool_scale, w_b=v_w_b,
               b_gates=v_b_gates, w_o=v_w_o, norm2_g=v_norm2_g, w_up=v_w_up, conv_w=v_conv_w, conv_b=v_conv_b,
               w_down=v_w_down, final_norm_g=v_final_norm_g)

    small_shapes = [wts[n].shape for n in SHARDED_SMALL]
    small_rows = _rows_for(small_shapes)

    transposed = ("w_in", "w_up")

    def shard3(n, a):
        if n in transposed:
            a = jnp.swapaxes(a, 1, 2)
        return a.reshape(DEPTH, -1, a.shape[-1])

    def unshard3(n, a):
        a = jnp.swapaxes(a, 1, 2) if n in transposed else a
        return a.reshape(wts[n].shape)

    cast = {(0, "w_in"): shard3("w_in", wts["w_in"])[0].astype(BF16)}
    state3 = {}

    def layer_shards(l, names):
        return [cast[l, n] for n in names]

    def tie(row, handle):
        return row + handle["token"][0:1, 0:1]

    def full_weight(n, zone):
        if n == "w_pool_grp":
            return jnp.moveaxis(zone.reshape(N_DEV, GROUPS, GDIM // N_DEV, GDIM), 0, 1).reshape(GROUPS, GDIM, GDIM)
        full = zone.reshape(-1, zone.shape[-1])
        return _permute_rows(full) if n == "w_in" else full

    groups = [("w_in",), ("w_a", "w_pool_grp", "w_b", "w_o"), ("w_up", "w_down")]
    rest = groups[0] + groups[1]
    key = {"w_pool_grp": "w_pool"}
    rows = dict(norm1_g=norm1_g, b_gk=b_gk, gla_norm_g=gla_norm_g, pool_scale=pool_scale, b_gates=b_gates,
                norm2_g=norm2_g, conv_b=conv_b)

    def gather(l, names, after, name, head=()):
        return exchange_start(list(head) + layer_shards(l, names), ["gather"] * (len(head) + len(names)), after,
                              name=name + "_start")

    def landed(handle, after, name, names, w_layer):
        zones = exchange_wait(handle, after, name=name + "_wait")
        for n, z in zip(names, zones[len(zones) - len(names):]):
            w_layer[key.get(n, n)] = full_weight(n, z)
        return zones

    wl = [{n: v[l:l + 1] for n, v in rows.items()} for l in range(DEPTH)]
    g_in0 = gather(0, groups[0], x, "gather_in0", head=[_pack([wts[n] for n in SHARDED_SMALL], small_rows)])
    zero = g_in0["token"][0, 0]
    target = jnp.pad(loss_target[0] + zero, ((X0, 0), (0, 0)))
    for l in range(DEPTH):
        for n in BIG:
            if (l, n) not in cast:
                cast[l, n] = (shard3(n, wts[n])[l] + zero).astype(BF16)
    for n in BIG:
        state3[n] = tuple(shard3(n, a[n]) + zero if n in transposed else shard3(n, a[n]) for a in (wts, mom, var))
    frame = jnp.pad(x[0] + zero, ((X0, 0), (0, 0)))
    early = [target, frame] + [cast[l, n] for l in range(DEPTH) for n in BIG if (l, n) != (0, "w_in")]
    early += [a for n in transposed for a in state3[n]]
    zones = landed(g_in0, (g_in0["token"], *early), "gather_in0", groups[0], wl[0])
    small_slots = [jnp.stack(parts) for parts in zip(*[_unpack(zones[0][i], small_shapes) for i in range(N_DEV)])]
    small_full = {n: _from_slots(slots, CUT_AXIS[n]) for n, slots in zip(SHARDED_SMALL, small_slots)}
    w_gk_pad = jnp.pad(small_full["w_gk"], ((0, 0), (0, LANES - RANK), (0, 0))).astype(BF16)
    for l in range(DEPTH):
        wl[l]["w_gk"] = w_gk_pad[l]
        wl[l]["conv_w"] = small_full["conv_w"][l]
    g_mix0 = gather(0, groups[1], zones[1], "gather_mix0")
    g_ffn0 = gather(0, groups[2], g_mix0["token"], "gather_ffn0")
    wl[0]["norm1_g"] = tie(wl[0]["norm1_g"], g_ffn0)

    h = lax.dynamic_update_slice(frame, small_full["meta_tokens"], (PAD, 0))
    s0 = {"h": h}
    fwd_in(s0, wl[0], "l0_")
    zones = landed(g_mix0, s0["p"], "gather_mix0", groups[1], wl[0])
    g_in1 = gather(1, groups[0], zones[0], "gather_in1")
    wl[0]["b_gk"] = tie(wl[0]["b_gk"], g_in1)
    fwd_mixer(s0, wl[0], "l0_")
    zones = landed(g_ffn0, s0["h2"], "gather_ffn0", groups[2], wl[0])
    g_mix1 = gather(1, groups[1], zones[0], "gather_mix1")
    g_ffn1 = gather(1, groups[2], g_mix1["token"], "gather_ffn1")
    wl[0]["norm2_g"] = tie(wl[0]["norm2_g"], g_ffn1)
    fwd_ffn(s0, wl[0], "l0_")
    landed(g_in1, s0["h3"], "gather_in1", groups[0], wl[1])
    s1 = {"h": s0["h3"]}
    fwd_in(s1, wl[1], "l1_")
    landed(g_mix1, s1["p"], "gather_mix1", groups[1], wl[1])
    fwd_mixer(s1, wl[1], "l1_")
    landed(g_ffn1, s1["h2"], "gather_ffn1", groups[2], wl[1])
    fwd_ffn(s1, wl[1], "l1_")
    dh, dh_b, dgf, loss_rows = loss_head(s1["h3"], final_norm_g[None], target, name="loss_head")
    loss_part = 0.5 * jnp.sum(loss_rows) / D

    def blocks(n, gw):
        if n == "w_in":
            gw = _unpermute_rows(gw)
        if n == "w_pool_grp":
            gw = gw.astype(BF16).reshape(GROUPS, N_DEV, GDIM // N_DEV, GDIM)
            return jnp.moveaxis(gw, 1, 0).reshape(N_DEV, GROUPS * GDIM // N_DEV, GDIM)
        return gw.reshape(N_DEV, gw.shape[0] // N_DEV, gw.shape[1])

    def scatter(g, names, after, name):
        return exchange_start([blocks(n, g[key.get(n, n)]) for n in names], ["scatter"] * len(names), after,
                              name=name + "_start")

    g1, g0 = {}, {}
    dh2, dh2_b = bwd_ffn(dh, dh_b, s1, wl[1], g1, "l1_")
    s_ffn1 = scatter(g1, groups[2], dh2, "scatter_ffn1")
    dp = bwd_mixer(dh2_b, s1, wl[1], g1, "l1_", after=s_ffn1["token"])
    g1["w_in"] = bwd_in_w(dp, s1, "l1_")
    s_rest1 = scatter(g1, rest, s_ffn1["token"], "scatter_rest1")
    dh, dh_b = bwd_in_x(dp, dh2, s1, wl[1], g1, "l1_", after=s_rest1["token"])
    dh2, dh2_b = bwd_ffn(dh, dh_b, s0, wl[0], g0, "l0_")
    half, done = {}, {}

    def adam(l, names, received):
        for n, parts in zip(names, received):
            w3, m3, v3 = state3[n]
            if l == 1:
                half[n] = reduce_adam_layer(parts, w3, m3, v3, 1, None, name="adam_l1_" + n)
            else:
                done[n] = reduce_adam_layer(parts, w3, m3, v3, 0, half[n], name="adam_l0_" + n)

    r_ffn1 = exchange_wait(s_ffn1, dh2, name="scatter_ffn1_wait")
    adam(1, groups[2], r_ffn1)
    s_ffn0 = scatter(g0, groups[2], r_ffn1[0], "scatter_ffn0")
    dp = bwd_mixer(dh2_b, s0, wl[0], g0, "l0_", after=s_ffn0["token"])
    r_rest1 = exchange_wait(s_rest1, dp, name="scatter_rest1_wait")
    adam(1, rest, r_rest1)
    r_ffn0 = exchange_wait(s_ffn0, r_rest1[0], name="scatter_ffn0_wait")
    adam(0, groups[2], r_ffn0)
    g0["w_in"] = bwd_in_w(dp, s0, "l0_")
    s_rest0 = scatter(g0, rest, r_ffn0[0], "scatter_rest0")
    updated = [half[n][0] for n in rest] + [done[n][0] for n in groups[2]]
    dh, _ = bwd_in_x(dp, dh2, s0, wl[0], g0, "l0_", after=(s_rest0["token"], *updated))
    grad_x = dh[X0:]

    g_full = {n: jnp.stack([g0[n], g1[n]])[:, 0] for n in rows}
    g_full["final_norm_g"] = dgf[0]
    g_full["meta_tokens"] = dh[PAD:X0]
    g_full["w_gk"] = jnp.stack([g0["w_gk"], g1["w_gk"]])[:, :RANK]
    g_full["conv_w"] = jnp.stack([g0["conv_w"], g1["conv_w"]])
    rep_shapes = [wts[n].shape for n in REPLICATED] + [(1,)]
    rep_rows = _rows_for(rep_shapes)
    small_blocks = jnp.stack([
        _pack([_to_slots(g_full[n], CUT_AXIS[n])[i] for n in SHARDED_SMALL], small_rows) for i in range(N_DEV)])
    rep_pack = _pack([g_full[n] for n in REPLICATED] + [loss_part.reshape(1)], rep_rows)
    r_rest0 = exchange_wait(s_rest0, (grad_x, small_blocks, rep_pack), name="scatter_rest0_wait")
    adam(0, rest, r_rest0)
    grads, delta, new_m, new_v = {}, {}, {}, {}
    for n in BIG:
        grads[n], delta[n], new_m[n], new_v[n] = [unshard3(n, o) for o in done[n]]
    received = exchange([small_blocks, rep_pack], ["scatter", "gather"], name="exchange_small")
    outs = reduce_adam(received[-2], _pack([wts[n] for n in SHARDED_SMALL], small_rows),
                       _pack([mom[n] for n in SHARDED_SMALL], small_rows),
                       _pack([var[n] for n in SHARDED_SMALL], small_rows), name="adam_small")
    for d, o in zip((grads, delta, new_m, new_v), outs):
        for n, a in zip(SHARDED_SMALL, _unpack(o, small_shapes)):
            d[n] = a
    one = [jnp.zeros((1,), F32)]
    outs = reduce_adam(received[-1], _pack([wts[n] for n in REPLICATED] + one, rep_rows),
                       _pack([mom[n] for n in REPLICATED] + one, rep_rows),
                       _pack([var[n] for n in REPLICATED] + one, rep_rows), name="adam_replicated")
    for d, o in zip((grads, delta, new_m, new_v), outs):
        for n, a in zip(REPLICATED + ("loss",), _unpack(o, rep_shapes)):
            d[n] = a
    loss = grads["loss"][0]
    return (loss, grad_x[None], *[grads[n] for n in WEIGHTS], *[delta[n] for n in WEIGHTS],
            *[new_m[n] for n in WEIGHTS], *[new_v[n] for n in WEIGHTS])
```
